```python
import jax
import jax.numpy as jnp
from jax import lax
import numpy as np

D_MODEL = 1024
BATCH = 2
SEQ = 16384
DEPTH = 4

GRID_W = 64
CTX_LEN = 256
N_MIXERS = 2
N_LAYERS_A = (DEPTH + N_MIXERS - 1) // N_MIXERS
N_LAYERS_B = DEPTH // N_MIXERS
N_MOD = 6
NORM_EPS = 1e-6
CHUNK = 128
GMLP_WIDTH = 2 * D_MODEL
GMLP_GROUP_CH = 128
GMLP_GROUPS = GMLP_WIDTH // GMLP_GROUP_CH
LN_EPS = 1e-5
RWKV_HEAD = 64
RWKV_HEADS = D_MODEL // RWKV_HEAD
DECAY_LORA = 64
AAA_LORA = 64
GATE_LORA = 160
GN_EPS = 64e-5
N_DIR = 2
N_GROUPS = 4
EXPERTS_PER_GROUP = 8
N_EXPERTS = N_GROUPS * EXPERTS_PER_GROUP
EXPERT_HIDDEN = 512
TOP_K = 2
MOE_BLOCK = 128

kernel_name = 'hybrid_gmlp_rwkv7_hmoe_dit'


def rmsnorm(x, g):
    xf = x.astype(jnp.float32)
    y = xf * lax.rsqrt(jnp.mean(xf * xf, axis=-1, keepdims=True) + NORM_EPS)
    return y.astype(x.dtype) * g


def layernorm(x, g, b, eps):
    xf = x.astype(jnp.float32)
    mu = jnp.mean(xf, axis=-1, keepdims=True)
    var = jnp.mean(jnp.square(xf - mu), axis=-1, keepdims=True)
    return ((xf - mu) * lax.rsqrt(var + eps)).astype(x.dtype) * g + b


def gmlp_mixer(h, w_in, b_in, ln_g, ln_b, w_s, b_s, w_out, b_out):
    bsz, n, _ = h.shape
    z = jax.nn.gelu(h @ w_in + b_in, approximate=False)
    u, v = jnp.split(z, 2, axis=-1)
    v = layernorm(v, ln_g, ln_b, LN_EPS)
    v = v.reshape(bsz, n // CHUNK, CHUNK, GMLP_GROUPS, GMLP_GROUP_CH)
    v = jnp.einsum('gpq,bcqgd->bcpgd', w_s, v) + b_s.T[:, :, None]
    return (u * v.reshape(bsz, n, GMLP_WIDTH)) @ w_out + b_out


def qshift_grid(h, rows):
    bsz, n, d = h.shape
    q = d // 4
    g = h.reshape(bsz, rows, GRID_W, d)
    left = jnp.pad(g[:, :, :-1, :q], ((0, 0), (0, 0), (1, 0), (0, 0)))
    right = jnp.pad(g[:, :, 1:, q:2 * q], ((0, 0), (0, 0), (0, 1), (0, 0)))
    up = jnp.pad(g[:, :-1, :, 2 * q:3 * q], ((0, 0), (1, 0), (0, 0), (0, 0)))
    down = jnp.pad(g[:, 1:, :, 3 * q:], ((0, 0), (0, 1), (0, 0), (0, 0)))
    return jnp.concatenate([left, right, up, down], axis=-1).reshape(bsz, n, d)


def bishift_1d(h):
    half = h.shape[-1] // 2
    prev = jnp.pad(h[:, :-1, :half], ((0, 0), (1, 0), (0, 0)))
    nxt = jnp.pad(h[:, 1:, half:], ((0, 0), (0, 1), (0, 0)))
    return jnp.concatenate([prev, nxt], axis=-1)


def _heads(z):
    return z.reshape(*z.shape[:-1], RWKV_HEADS, RWKV_HEAD)


def _both_dirs(z):
    return jnp.stack([z, z[:, ::-1]])


def _rev_dir1(z):
    return jnp.stack([z[0], z[1][:, ::-1]])


def _scan_layout(z):
    return jnp.moveaxis(_heads(z), 2, 0).astype(jnp.float32)


def rwkv_prepare(h, h_shift, mu, wr, wk, wv, w0, w1, w2, a0, a1, a2, k_k, k_a):
    xx = h_shift - h
    xr, xw, xk, xv, xa = [h + xx * mu[j] for j in range(5)]
    r = xr @ wr
    k = xk @ wk
    v = xv @ wv
    lw = jnp.einsum('zbtr,zrd->zbtd', jnp.tanh(jnp.einsum('btd,zdr->zbtr', xw, w1)), w2)
    w_log = -jax.nn.softplus(-(w0[:, None, None, :] + lw).astype(jnp.float32)) - 0.5
    decay = jnp.exp(-jnp.exp(w_log))
    la = jnp.einsum('zbtr,zrd->zbtd', jnp.einsum('btd,zdr->zbtr', xa, a1), a2)
    a = jax.nn.sigmoid((a0[:, None, None, :] + la).astype(jnp.float32))
    kk = _heads(k * k_k).astype(jnp.float32)
    kk = (kk * lax.rsqrt(jnp.sum(kk * kk, axis=-1, keepdims=True) + 1e-12)).reshape(k.shape)
    k_mod = k[None] * (1 + (a - 1) * k_a)
    scan_in = (_scan_layout(_both_dirs(r)), _scan_layout(_rev_dir1(decay)), _scan_layout(_rev_dir1(k_mod)),
               _scan_layout(_both_dirs(v)), _scan_layout(_both_dirs(-kk)), _scan_layout(_rev_dir1(kk[None] * a)))
    return scan_in, r, k_mod, v, xx


def _wkv_step(S, inp):
    r_t, w_t, k_t, v_t, a_t, b_t = inp
    sa = jnp.einsum('zbhvk,zbhk->zbhv', S, a_t)
    S = S * w_t[..., None, :] + sa[..., :, None] * b_t[..., None, :] + v_t[..., :, None] * k_t[..., None, :]
    return S, jnp.einsum('zbhvk,zbhk->zbhv', S, r_t)


def wkv_bidir(state0, scan_in):
    state, ys = lax.scan(_wkv_step, state0, scan_in)
    ys = jnp.moveaxis(ys, 0, 2)
    return state, ys[0] + ys[1][:, ::-1]


def rwkv_out(y, h, xx, r, k_mod, v, mu, ln_g, ln_b, r_k, g1, g2, wo):
    bsz, n, d = r.shape
    y = layernorm(y, ln_g.reshape(RWKV_HEADS, RWKV_HEAD), ln_b.reshape(RWKV_HEADS, RWKV_HEAD), GN_EPS)
    bonus = jnp.einsum('bthk,zbthk->bth', _heads(r) * r_k, _heads(k_mod))
    y = y.astype(r.dtype) + bonus[..., None] * _heads(v)
    xg = h + xx * mu[5]
    gate = jax.nn.sigmoid(xg @ g1) @ g2
    return (y.reshape(bsz, n, d) * gate) @ wo


def rwkv_mixer(hc, hl, rows, need_ctx_out, mu, wr, wk, wv, wo, w0, w1, w2, a0, a1, a2, g1, g2,
               k_k, k_a, r_k, ln_g, ln_b):
    proj = (mu, wr, wk, wv, w0, w1, w2, a0, a1, a2, k_k, k_a)
    scan_c, r_c, km_c, v_c, xx_c = rwkv_prepare(hc, bishift_1d(hc), *proj)
    scan_l, r_l, km_l, v_l, xx_l = rwkv_prepare(hl, qshift_grid(hl, rows), *proj)
    state0 = jnp.zeros((N_DIR, hc.shape[0], RWKV_HEADS, RWKV_HEAD, RWKV_HEAD), jnp.float32)
    state_c, y_c = wkv_bidir(state0, scan_c)
    _, y_l = wkv_bidir(state_c, scan_l)
    outp = (mu, ln_g, ln_b, r_k, g1, g2, wo)
    out_l = rwkv_out(y_l, hl, xx_l, r_l, km_l, v_l, *outp)
    out_c = rwkv_out(y_c, hc, xx_c, r_c, km_c, v_c, *outp) if need_ctx_out else None
    return out_c, out_l


def hier_moe(h, w_grp, b_grp, w_exp, b_exp, w_gate, w_up, w_down):
    n_tok, d = h.shape
    hf = h.astype(jnp.float32)
    grp_logits = hf @ w_grp.astype(jnp.float32) + b_grp.astype(jnp.float32)
    grp_prob = jax.nn.softmax(grp_logits, axis=-1)
    grp = jnp.argmax(grp_logits, axis=-1)
    p_grp = jnp.take_along_axis(grp_prob, grp[:, None], axis=-1)
    exp_logits = (hf @ w_exp.astype(jnp.float32) + b_exp.astype(jnp.float32)).reshape(n_tok, N_GROUPS, EXPERTS_PER_GROUP)
    exp_logits = jnp.take_along_axis(exp_logits, grp[:, None, None], axis=1)[:, 0]
    top_logit, top_idx = lax.top_k(exp_logits, TOP_K)
    weights = (p_grp * jax.nn.softmax(top_logit, axis=-1)).astype(h.dtype)
    expert_id = (grp[:, None] * EXPERTS_PER_GROUP + top_idx).reshape(-1)
    n_assign = n_tok * TOP_K
    order = jnp.argsort(expert_id)
    sorted_e = expert_id[order]
    token_of = order // TOP_K
    counts = jnp.bincount(expert_id, length=N_EXPERTS)
    padded = ((counts + MOE_BLOCK - 1) // MOE_BLOCK) * MOE_BLOCK
    pad_end = jnp.cumsum(padded)
    pad_start = pad_end - padded
    raw_start = jnp.cumsum(counts) - counts
    dest = pad_start[sorted_e] + (jnp.arange(n_assign) - raw_start[sorted_e])
    n_blocks = (n_assign + MOE_BLOCK - 1) // MOE_BLOCK + N_EXPERTS
    cap = n_blocks * MOE_BLOCK
    buf = jnp.zeros((cap, d), h.dtype).at[dest].set(h[token_of])
    block_expert = jnp.minimum(jnp.searchsorted(pad_end, jnp.arange(n_blocks) * MOE_BLOCK, side='right'),
                               N_EXPERTS - 1)

    def expert_block(args):
        xb, e = args
        return (jax.nn.silu(xb @ w_gate[e]) * (xb @ w_up[e])) @ w_down[e]

    out = lax.map(expert_block, (buf.reshape(n_blocks, MOE_BLOCK, d), block_expert)).reshape(cap, d)
    y = jnp.zeros((n_assign, d), h.dtype).at[order].set(out[dest]).reshape(n_tok, TOP_K, d)
    return jnp.einsum('tk,tkd->td', weights, y)


def setup_inputs(seed: int = 0) -> dict:
    key = jax.random.key(seed)
    keys = iter(jax.random.split(key, 48))
    f32 = jnp.float32

    def nrm(shape, scale):
        return scale * jax.random.normal(next(keys), shape, f32)

    def gain(shape):
        return 1.0 + nrm(shape, 0.1)

    def unif(shape, lo, hi):
        return jax.random.uniform(next(keys), shape, f32, lo, hi)

    D, W, F = D_MODEL, GMLP_WIDTH, EXPERT_HIDDEN
    NA, NB = N_LAYERS_A, N_LAYERS_B
    return {
        'x': nrm((BATCH, SEQ, D), 1.0),
        'c': nrm((BATCH, D), 1.0),
        'ctx': nrm((BATCH, CTX_LEN, D), 1.0),
        'c_ctx': nrm((D,), 1.0),
        'ada_w': nrm((DEPTH, D, N_MOD * D), 0.5 * D ** -0.5),
        'ada_b': nrm((DEPTH, N_MOD * D), 0.05),
        'norm1_g': gain((DEPTH, D)),
        'norm2_g': gain((DEPTH, D)),
        'final_g': gain((D,)),
        'ga_w_in': nrm((NA, D, 2 * W), D ** -0.5),
        'ga_b_in': nrm((NA, 2 * W), 0.02),
        'ga_ln_g': gain((NA, W)),
        'ga_ln_b': nrm((NA, W), 0.02),
        'ga_w_s': nrm((NA, GMLP_GROUPS, CHUNK, CHUNK), 0.5 * CHUNK ** -0.5),
        'ga_b_s': gain((NA, GMLP_GROUPS, CHUNK)),
        'ga_w_out': nrm((NA, W, D), W ** -0.5),
        'ga_b_out': nrm((NA, D), 0.02),
        'rw_mu': unif((NB, N_MOD, D), 0.0, 1.0),
        'rw_wr': nrm((NB, D, D), D ** -0.5),
        'rw_wk': nrm((NB, D, D), D ** -0.5),
        'rw_wv': nrm((NB, D, D), D ** -0.5),
        'rw_wo': nrm((NB, D, D), D ** -0.5),
        'rw_w0': unif((NB, N_DIR, D), -6.0, 0.0),
        'rw_w1': nrm((NB, N_DIR, D, DECAY_LORA), 0.5 * D ** -0.5),
        'rw_w2': nrm((NB, N_DIR, DECAY_LORA, D), 0.5 * DECAY_LORA ** -0.5),
        'rw_a0': nrm((NB, N_DIR, D), 0.5),
        'rw_a1': nrm((NB, N_DIR, D, AAA_LORA), 0.5 * D ** -0.5),
        'rw_a2': nrm((NB, N_DIR, AAA_LORA, D), 0.5 * AAA_LORA ** -0.5),
        'rw_g1': nrm((NB, D, GATE_LORA), D ** -0.5),
        'rw_g2': nrm((NB, GATE_LORA, D), GATE_LORA ** -0.5),
        'rw_k_k': 0.85 + nrm((NB, D), 0.05),
        'rw_k_a': gain((NB, D)),
        'rw_r_k': nrm((NB, RWKV_HEADS, RWKV_HEAD), 0.1),
        'rw_ln_g': gain((NB, D)),
        'rw_ln_b': nrm((NB, D), 0.02),
        'moe_w_grp': nrm((DEPTH, D, N_GROUPS), D ** -0.5),
        'moe_b_grp': nrm((DEPTH, N_GROUPS), 0.01),
        'moe_w_exp': nrm((DEPTH, D, N_EXPERTS), D ** -0.5),
        'moe_b_exp': nrm((DEPTH, N_EXPERTS), 0.01),
        'moe_w_gate': nrm((DEPTH, N_EXPERTS, D, F), D ** -0.5),
        'moe_w_up': nrm((DEPTH, N_EXPERTS, D, F), D ** -0.5),
        'moe_w_down': nrm((DEPTH, N_EXPERTS, F, D), F ** -0.5),
    }


def reference(x, c, ctx, c_ctx, ada_w, ada_b, norm1_g, norm2_g, final_g,
              ga_w_in, ga_b_in, ga_ln_g, ga_ln_b, ga_w_s, ga_b_s, ga_w_out, ga_b_out,
              rw_mu, rw_wr, rw_wk, rw_wv, rw_wo, rw_w0, rw_w1, rw_w2, rw_a0, rw_a1, rw_a2,
              rw_g1, rw_g2, rw_k_k, rw_k_a, rw_r_k, rw_ln_g, rw_ln_b,
              moe_w_grp, moe_b_grp, moe_w_exp, moe_b_exp, moe_w_gate, moe_w_up, moe_w_down):
    bsz, n, d = x.shape
    rows = n // GRID_W
    xc = ctx
    silu_c = jax.nn.silu(c)
    silu_cc = jax.nn.silu(c_ctx)
    for i in range(DEPTH):
        last = i == DEPTH - 1
        j = i // N_MIXERS
        mod_l = silu_c @ ada_w[i] + ada_b[i]
        mod_c = silu_cc @ ada_w[i] + ada_b[i]
        sh1, sc1, gt1, sh2, sc2, gt2 = jnp.split(mod_l[:, None, :], N_MOD, axis=-1)
        csh1, csc1, cgt1, csh2, csc2, cgt2 = jnp.split(mod_c, N_MOD, axis=-1)
        hl = rmsnorm(x, norm1_g[i]) * (1 + sc1) + sh1
        hc = rmsnorm(xc, norm1_g[i]) * (1 + csc1) + csh1
        if i % N_MIXERS == 0:
            gp = (ga_w_in[j], ga_b_in[j], ga_ln_g[j], ga_ln_b[j], ga_w_s[j], ga_b_s[j], ga_w_out[j], ga_b_out[j])
            yl = gmlp_mixer(hl, *gp)
            yc = None if last else gmlp_mixer(hc, *gp)
        else:
            yc, yl = rwkv_mixer(hc, hl, rows, not last, rw_mu[j], rw_wr[j], rw_wk[j], rw_wv[j], rw_wo[j],
                                rw_w0[j], rw_w1[j], rw_w2[j], rw_a0[j], rw_a1[j], rw_a2[j], rw_g1[j], rw_g2[j],
                                rw_k_k[j], rw_k_a[j], rw_r_k[j], rw_ln_g[j], rw_ln_b[j])
        x = x + gt1 * yl
        hl2 = rmsnorm(x, norm2_g[i]) * (1 + sc2) + sh2
        mp = (moe_w_grp[i], moe_b_grp[i], moe_w_exp[i], moe_b_exp[i], moe_w_gate[i], moe_w_up[i], moe_w_down[i])
        if last:
            x = x + gt2 * hier_moe(hl2.reshape(-1, d), *mp).reshape(x.shape)
        else:
            xc = xc + cgt1 * yc
            hc2 = rmsnorm(xc, norm2_g[i]) * (1 + csc2) + csh2
            out = hier_moe(jnp.concatenate([hl2.reshape(-1, d), hc2.reshape(-1, d)], axis=0), *mp)
            x = x + gt2 * out[:bsz * n].reshape(x.shape)
            xc = xc + cgt2 * out[bsz * n:].reshape(xc.shape)
    return rmsnorm(x, final_g)
```

```python
import functools
import math

import jax
import jax.numpy as jnp
from jax import lax
from jax.experimental import pallas as pl
from jax.experimental.pallas import tpu as pltpu

F32 = jnp.float32
BF16 = jnp.bfloat16

N_MOD = 6
NORM_EPS = 1e-6
GRID_W = 64
CHUNK = 128
GMLP_GROUP_CH = 128
LN_EPS = 1e-5
RWKV_HEAD = 64
GN_EPS = 64e-5
N_GROUPS = 4
EXPERTS_PER_GROUP = 8
N_EXPERTS = N_GROUPS * EXPERTS_PER_GROUP
TOP_K = 2

LANES = 128
TM = 256
MOE_ROWS = 256
SCAN_BLOCK = 256
SCAN_CHUNK = 16
SCAN_LANES = 256
VMEM_LIMIT = 56 * 1024 * 1024
NEG_BIG = -3.0e38
SQRT_HALF = 0.7071067811865476


def _dotb(a, b):
    return jnp.dot(a.astype(BF16), b.astype(BF16), preferred_element_type=F32)


def _dot_nt(a, b):
    return lax.dot_general(a.astype(BF16), b.astype(BF16), (((1,), (1,)), ((), ())), preferred_element_type=F32)


def _dot_tn(a, b):
    return lax.dot_general(a.astype(BF16), b.astype(BF16), (((0,), (0,)), ((), ())), preferred_element_type=F32)


def _split2(x):
    hi = x.astype(BF16)
    lo = (x - hi.astype(F32)).astype(BF16)
    return hi, lo


def _split3(x):
    hi = x.astype(BF16)
    r1 = x - hi.astype(F32)
    mid = r1.astype(BF16)
    lo = (r1 - mid.astype(F32)).astype(BF16)
    return hi, mid, lo


def _dot_sel(sel_bf16, x):
    hi, mid, lo = _split3(x)
    d = lambda p: jnp.dot(sel_bf16, p, preferred_element_type=F32)
    return d(hi) + d(mid) + d(lo)


def _dot_x_sel(x, sel_bf16):
    hi, mid, lo = _split3(x)
    d = lambda p: jnp.dot(p, sel_bf16, preferred_element_type=F32)
    return d(hi) + d(mid) + d(lo)


def _dot_hp(a, b):
    ah, al = _split2(a)
    bh, bl = _split2(b)
    d = lambda p, q: jnp.dot(p, q, preferred_element_type=F32)
    return d(ah, bh) + d(ah, bl) + d(al, bh)


def _norm_mod(x, g, sc, sh):
    y = x * lax.rsqrt(jnp.mean(x * x, axis=-1, keepdims=True) + NORM_EPS)
    return (y * g) * (1.0 + sc) + sh


def _mod_parts(mod_ref):
    m = mod_ref[0]
    d = m.shape[-1] // N_MOD
    return [m[:, i * d:(i + 1) * d] for i in range(N_MOD)]


def _mods_kernel(s_ref, w_ref, b_ref, o_ref):
    s = s_ref[...]
    s = s * jax.nn.sigmoid(s)
    o_ref[0] = _dotb(s, w_ref[0]) + b_ref[0]


def _mods(s_rows, ada_w, ada_b):
    depth, d, nd = ada_w.shape
    tn = nd // 4
    return pl.pallas_call(
        _mods_kernel,
        grid=(depth, nd // tn),
        in_specs=[pl.BlockSpec(s_rows.shape, lambda i, j: (0, 0)),
                  pl.BlockSpec((1, d, tn), lambda i, j: (i, 0, j)),
                  pl.BlockSpec((1, 1, tn), lambda i, j: (i, 0, j))],
        out_specs=pl.BlockSpec((1, s_rows.shape[0], tn), lambda i, j: (i, 0, j)),
        out_shape=jax.ShapeDtypeStruct((depth, s_rows.shape[0], nd), F32),
        compiler_params=pltpu.CompilerParams(dimension_semantics=("parallel", "parallel"),
                                             vmem_limit_bytes=VMEM_LIMIT),
        name="mods",
    )(s_rows, ada_w, ada_b.reshape(depth, 1, nd))


def _route(logits):
    lane = lax.broadcasted_iota(jnp.int32, logits.shape, 1)
    lane_f = lane.astype(F32)
    big = jnp.float32(1e9)
    is_g = lane < N_GROUPS
    gl = jnp.where(is_g, logits, NEG_BIG)
    gm = jnp.max(gl, axis=1, keepdims=True)
    grp = jnp.min(jnp.where(is_g & (gl == gm), lane_f, big), axis=1, keepdims=True)
    p_grp = 1.0 / jnp.sum(jnp.where(is_g, jnp.exp(gl - gm), 0.0), axis=1, keepdims=True)
    lo = N_GROUPS + grp * EXPERTS_PER_GROUP
    in_grp = (lane_f >= lo) & (lane_f < lo + EXPERTS_PER_GROUP)
    el = jnp.where(in_grp, logits, NEG_BIG)
    m1 = jnp.max(el, axis=1, keepdims=True)
    i1 = jnp.min(jnp.where(in_grp & (el == m1), lane_f, big), axis=1, keepdims=True)
    rest = in_grp & (lane_f != i1)
    el2 = jnp.where(rest, logits, NEG_BIG)
    m2 = jnp.max(el2, axis=1, keepdims=True)
    i2 = jnp.min(jnp.where(rest & (el2 == m2), lane_f, big), axis=1, keepdims=True)
    e21 = jnp.exp(m2 - m1)
    s0 = 1.0 / (1.0 + e21)
    w0 = p_grp * s0
    w1 = p_grp * (e21 * s0)
    out = jnp.where(lane == 0, i1 - N_GROUPS,
                    jnp.where(lane == 1, i2 - N_GROUPS,
                              jnp.where(lane == 2, w0, jnp.where(lane == 3, w1, 0.0))))
    return out


def _residual_and_route(x, y, mods, g2_ref, wrt_ref, brt_ref, xo_ref, h2_ref, route_ref):
    _, _, gt1, sh2, sc2, _ = mods
    xn = x + gt1 * y
    xo_ref[...] = xn
    h2 = _norm_mod(xn, g2_ref[...], sc2, sh2).astype(BF16)
    h2_ref[...] = h2
    logits = jnp.dot(h2, wrt_ref[...], preferred_element_type=F32) + brt_ref[...]
    route_ref[...] = _route(logits)


def _gmlp_kernel(x_ref, mod_ref, g1_ref, g2_ref, win_ref, bin_ref, lng_ref, lnb_ref, ws_ref, bs_ref,
                 wout_ref, bout_ref, wrt_ref, brt_ref, xo_ref, h2_ref, route_ref, gated_ref):
    x = x_ref[...]
    mods = _mod_parts(mod_ref)
    sh1, sc1 = mods[0], mods[1]
    h = _norm_mod(x, g1_ref[...], sc1, sh1)
    z = _dotb(h, win_ref[...]) + bin_ref[...]
    z = 0.5 * z * (1.0 + lax.erf(z * SQRT_HALF))
    width = z.shape[1] // 2
    u = z[:, :width]
    v = z[:, width:]
    mu = jnp.mean(v, axis=-1, keepdims=True)
    vc = v - mu
    var = jnp.mean(vc * vc, axis=-1, keepdims=True)
    vn = ((vc * lax.rsqrt(var + LN_EPS)) * lng_ref[...] + lnb_ref[...]).astype(BF16)
    n_chunks = x.shape[0] // CHUNK
    for g in range(width // GMLP_GROUP_CH):
        cs = slice(g * GMLP_GROUP_CH, (g + 1) * GMLP_GROUP_CH)
        rhs = jnp.concatenate([vn[c * CHUNK:(c + 1) * CHUNK, cs] for c in range(n_chunks)], axis=1)
        s = jnp.dot(ws_ref[g], rhs, preferred_element_type=F32) + bs_ref[g]
        for c in range(n_chunks):
            rsl = slice(c * CHUNK, (c + 1) * CHUNK)
            gated_ref[rsl, cs] = (u[rsl, cs] * s[:, c * GMLP_GROUP_CH:(c + 1) * GMLP_GROUP_CH]).astype(BF16)
    y = jnp.dot(gated_ref[...], wout_ref[...], preferred_element_type=F32) + bout_ref[...]
    _residual_and_route(x, y, mods, g2_ref, wrt_ref, brt_ref, xo_ref, h2_ref, route_ref)


def _tile_specs(t_tok, d, seg_of):
    x_spec = pl.BlockSpec((TM, d), lambda t: (t, 0))
    return x_spec


def _full_spec(shape):
    nd = len(shape)
    return pl.BlockSpec(shape, lambda *_: (0,) * nd)


def _layer_spec(shape, j):
    nd = len(shape)
    return pl.BlockSpec((None,) + tuple(shape[1:]), lambda *_: (j,) + (0,) * (nd - 1))


def _mixer_out_shapes(t_tok, d):
    return (jax.ShapeDtypeStruct((t_tok, d), F32), jax.ShapeDtypeStruct((t_tok, d), BF16),
            jax.ShapeDtypeStruct((t_tok, LANES), F32))


def _mixer_out_specs(d):
    return (pl.BlockSpec((TM, d), lambda t: (t, 0)), pl.BlockSpec((TM, d), lambda t: (t, 0)),
            pl.BlockSpec((TM, LANES), lambda t: (t, 0)))


def _gmlp_layer(x, mods3, mod_spec, i, j, p):
    t_tok, d = x.shape
    width = p["ga_w_out"].shape[1]
    row = lambda a: a.reshape(a.shape[0], 1, a.shape[1])
    args = (x, mods3, row(p["norm1_g"]), row(p["norm2_g"]), p["ga_w_in"], row(p["ga_b_in"]), row(p["ga_ln_g"]),
            row(p["ga_ln_b"]), p["ga_w_s"], p["ga_b_s"], p["ga_w_out"], row(p["ga_b_out"]), p["wrt"], p["brt"])
    in_specs = [pl.BlockSpec((TM, d), lambda t: (t, 0)), mod_spec,
                _layer_spec(args[2].shape, i), _layer_spec(args[3].shape, i),
                _layer_spec(args[4].shape, j), _layer_spec(args[5].shape, j), _layer_spec(args[6].shape, j),
                _layer_spec(args[7].shape, j), _layer_spec(args[8].shape, j), _layer_spec(args[9].shape, j),
                _layer_spec(args[10].shape, j), _layer_spec(args[11].shape, j),
                _layer_spec(args[12].shape, i), _layer_spec(args[13].shape, i)]
    return pl.pallas_call(
        _gmlp_kernel,
        grid=(t_tok // TM,),
        in_specs=in_specs,
        out_specs=_mixer_out_specs(d),
        out_shape=_mixer_out_shapes(t_tok, d),
        scratch_shapes=[pltpu.VMEM((TM, width), BF16)],
        compiler_params=pltpu.CompilerParams(dimension_semantics=("parallel",), vmem_limit_bytes=VMEM_LIMIT),
        name="gmlp_layer",
    )(*args)


def _normmod_kernel(x_ref, mod_ref, g1_ref, h_ref):
    mods = _mod_parts(mod_ref)
    h_ref[...] = _norm_mod(x_ref[...], g1_ref[...], mods[1], mods[0])


def _normmod(x, mods3, mod_spec, i, p):
    t_tok, d = x.shape
    g1 = p["norm1_g"].reshape(-1, 1, d)
    return pl.pallas_call(
        _normmod_kernel,
        grid=(t_tok // TM,),
        in_specs=[pl.BlockSpec((TM, d), lambda t: (t, 0)), mod_spec, _layer_spec(g1.shape, i)],
        out_specs=pl.BlockSpec((TM, d), lambda t: (t, 0)),
        out_shape=jax.ShapeDtypeStruct((t_tok, d), F32),
        compiler_params=pltpu.CompilerParams(dimension_semantics=("parallel",), vmem_limit_bytes=VMEM_LIMIT),
        name="normmod",
    )(x, mods3, g1)


def _seg_sum(q, e_ref):
    return _dot_x_sel(q, e_ref[...])


def _seg_expand(s, et_ref):
    return _dot_x_sel(s, et_ref[...])


def _rwkv_prep_kernel(h_ref, hs_ref, mu_ref, wr_ref, wk_ref, wv_ref, w1_ref, w2_ref, w0_ref, a1_ref, a2_ref, a0_ref,
                      g1_ref, g2_ref, kk_ref, ka_ref, e_ref, et_ref,
                      r_o, v_o, kkn_o, gate_o, lw_o, km_o, bb_o):
    h = h_ref[...]
    xx = hs_ref[...] - h
    mu = mu_ref[...]
    xr, xw, xk, xv, xa, xg = [h + xx * mu[q:q + 1] for q in range(6)]
    r = _dotb(xr, wr_ref[...])
    k = _dotb(xk, wk_ref[...])
    v = _dotb(xv, wv_ref[...])
    r_o[...] = r
    v_o[...] = v
    kkr = k * kk_ref[...]
    rs = lax.rsqrt(_seg_sum(kkr * kkr, e_ref) + 1e-12)
    kkn = kkr * _seg_expand(rs, et_ref)
    kkn_o[...] = kkn
    gate_o[...] = _dotb(jax.nn.sigmoid(_dotb(xg, g1_ref[...])), g2_ref[...])
    tw = jnp.tanh(_dotb(xw, w1_ref[...]))
    ta = _dotb(xa, a1_ref[...])
    lane = lax.broadcasted_iota(jnp.int32, tw.shape, 1)
    half = tw.shape[1] // 2
    w0 = w0_ref[...]
    a0 = a0_ref[...]
    ka = ka_ref[...]
    for z in range(2):
        zm = (lane >= z * half) & (lane < (z + 1) * half)
        lwz = _dotb(jnp.where(zm, tw, 0.0), w2_ref[...])
        t = -(w0[z:z + 1] + lwz)
        sp = jnp.maximum(t, 0.0) + jnp.log1p(jnp.exp(-jnp.abs(t)))
        lw_o[z] = -jnp.exp(-sp - 0.5)
        la = _dotb(jnp.where(zm, ta, 0.0), a2_ref[...])
        a = jax.nn.sigmoid(a0[z:z + 1] + la)
        km_o[z] = k * (1.0 + (a - 1.0) * ka)
        bb_o[z] = kkn * a


def _rwkv_prep(h, hs, j, p):
    t_tok, d = h.shape
    row = lambda a: a.reshape(a.shape[0], 1, a.shape[1])
    args = (h, hs, p["rw_mu8"], p["rw_wr"], p["rw_wk"], p["rw_wv"], p["rw_w1c"], p["rw_w2c"], p["rw_w0"],
            p["rw_a1c"], p["rw_a2c"], p["rw_a0"], p["rw_g1p"], p["rw_g2p"], row(p["rw_k_k"]), row(p["rw_k_a"]),
            p["head_sel"], p["head_sel_t"])
    tok = pl.BlockSpec((TM, d), lambda t: (t, 0))
    in_specs = [tok, tok] + [_layer_spec(a.shape, j) for a in args[2:16]] + [_full_spec(args[16].shape),
                                                                             _full_spec(args[17].shape)]
    dir_spec = pl.BlockSpec((2, TM, d), lambda t: (0, t, 0))
    tok_shape = jax.ShapeDtypeStruct((t_tok, d), F32)
    dir_shape = jax.ShapeDtypeStruct((2, t_tok, d), F32)
    return pl.pallas_call(
        _rwkv_prep_kernel,
        grid=(t_tok // TM,),
        in_specs=in_specs,
        out_specs=(tok, tok, tok, tok, dir_spec, dir_spec, dir_spec),
        out_shape=(tok_shape, tok_shape, tok_shape, tok_shape, dir_shape, dir_shape, dir_shape),
        compiler_params=pltpu.CompilerParams(dimension_semantics=("parallel",), vmem_limit_bytes=VMEM_LIMIT),
        name="rwkv_prep",
    )(*args)


def _scan_kernel(r_ref, v_ref, kk_ref, lw_ref, km_ref, bb_ref, y_ref, g_scr, *, rev):
    c = SCAN_CHUNK
    nb = SCAN_BLOCK

    @pl.when(pl.program_id(2) == 0)
    def _():
        g_scr[...] = jnp.zeros_like(g_scr)

    r = r_ref[...]
    v = v_ref[...]
    a = -kk_ref[...]
    lw = lw_ref[...]
    km = km_ref[...]
    bb = bb_ref[...]

    row = lax.broadcasted_iota(jnp.int32, (nb, nb), 0)
    col = lax.broadcasted_iota(jnp.int32, (nb, nb), 1)
    shift_c = int(math.log2(c))
    same = (row >> shift_c) == (col >> shift_c)
    if rev:
        incl = same & (col >= row)
        strict = same & (col > row)
    else:
        incl = same & (col <= row)
        strict = same & (col < row)
    incl_b = jnp.where(incl, 1.0, 0.0).astype(BF16)
    same_b = jnp.where(same, 1.0, 0.0).astype(BF16)
    eye = jnp.where(row == col, 1.0, 0.0)
    shift_h = int(math.log2(RWKV_HEAD))
    head_of_lane = col >> shift_h
    blockdiag = (row >> shift_h) == head_of_lane

    cum = _dot_sel(incl_b, lw)
    tot = _dot_sel(same_b, lw)
    e_cum = jnp.exp(cum)
    at = a * jnp.exp(cum - lw)
    rt = r * e_cum
    e_inv = jnp.exp(-cum)
    bt = bb * e_inv
    kt = km * e_inv
    e_rem = jnp.exp(tot - cum)
    bh = bb * e_rem
    kh = km * e_rem
    e_tot = jnp.exp(tot)

    bk = jnp.concatenate([bt, kt], axis=0).astype(BF16)
    ah = jnp.zeros_like(r)
    vh = jnp.zeros_like(r)
    rh = jnp.zeros_like(r)
    yh = jnp.zeros_like(r)
    for hh in range(SCAN_LANES // RWKV_HEAD):
        hm = head_of_lane == hh
        at_h = jnp.where(hm, at, 0.0)
        rt_h = jnp.where(hm, rt, 0.0)
        v_h = jnp.where(hm, v, 0.0)
        s = _dot_nt(jnp.concatenate([at_h, rt_h], axis=0), bk)
        a_ab = jnp.where(strict, s[:nb, :nb], 0.0)
        a_ak = jnp.where(strict, s[:nb, nb:], 0.0)
        m_rb = jnp.where(incl, s[nb:, :nb], 0.0)
        m_rk = jnp.where(incl, s[nb:, nb:], 0.0)
        pw = a_ab
        tinv = eye + a_ab
        n_sq = int(math.log2(c)) - 1
        for _ in range(n_sq):
            pw = _dot_hp(pw, pw)
            tinv = tinv + _dot_hp(tinv, pw)
        ah_h = _dot_hp(tinv, at_h)
        vh_h = _dot_hp(tinv, _dotb(a_ak, v_h))
        ah = ah + ah_h
        vh = vh + vh_h
        rh = rh + rt_h + _dotb(m_rb, ah_h)
        yh = yh + _dotb(m_rk, v_h) + _dotb(m_rb, vh_h)

    g = g_scr[...]
    order = range(nb // c - 1, -1, -1) if rev else range(nb // c)
    for ci in order:
        rs = slice(ci * c, (ci + 1) * c)
        p = _dot_nt(jnp.concatenate([ah[rs], rh[rs]], axis=0), g)
        u = p[:c] + vh[rs]
        y_ref[rs, :] = p[c:] + yh[rs]
        dg = _dot_tn(jnp.concatenate([u, v[rs]], axis=0), jnp.concatenate([bh[rs], kh[rs]], axis=0))
        g = jnp.where(blockdiag, g * e_tot[ci * c:ci * c + 1] + dg, 0.0)
    g_scr[...] = g


def _wkv_scan(r, v, kkn, lw, km, bb, z, bsz, n_lat, n_ctx):
    t_tok, d = r.shape
    lat_blocks = n_lat // SCAN_BLOCK
    ctx_blocks = n_ctx // SCAN_BLOCK
    steps = ctx_blocks + lat_blocks
    ctx_base = bsz * lat_blocks
    rev = z == 1

    def blk(b, s):
        if rev:
            cb = ctx_base + b * ctx_blocks + (ctx_blocks - 1 - s)
            lb = b * lat_blocks + (lat_blocks - 1 - (s - ctx_blocks))
        else:
            cb = ctx_base + b * ctx_blocks + s
            lb = b * lat_blocks + (s - ctx_blocks)
        return jnp.where(s < ctx_blocks, cb, lb)

    tok = pl.BlockSpec((SCAN_BLOCK, SCAN_LANES), lambda b, g, s: (blk(b, s), g))
    dirs = pl.BlockSpec((None, SCAN_BLOCK, SCAN_LANES), lambda b, g, s: (z, blk(b, s), g))
    return pl.pallas_call(
        functools.partial(_scan_kernel, rev=rev),
        grid=(bsz, d // SCAN_LANES, steps),
        in_specs=[tok, tok, tok, dirs, dirs, dirs],
        out_specs=tok,
        out_shape=jax.ShapeDtypeStruct((t_tok, d), F32),
        scratch_shapes=[pltpu.VMEM((SCAN_LANES, SCAN_LANES), F32)],
        compiler_params=pltpu.CompilerParams(dimension_semantics=("parallel", "parallel", "arbitrary"),
                                             vmem_limit_bytes=VMEM_LIMIT),
        name="wkv_scan_rev" if rev else "wkv_scan_fwd",
    )(r, v, kkn, lw, km, bb)


def _rwkv_out_kernel(x_ref, mod_ref, y0_ref, y1_ref, r_ref, v_ref, km_ref, gate_ref, lng_ref, lnb_ref, rk_ref,
                     wo_ref, e_ref, et_ref, g2_ref, wrt_ref, brt_ref, xo_ref, h2_ref, route_ref):
    x = x_ref[...]
    mods = _mod_parts(mod_ref)
    y = y0_ref[...] + y1_ref[...]
    inv_n = 1.0 / RWKV_HEAD
    mu = _seg_expand(_seg_sum(y, e_ref) * inv_n, et_ref)
    yc = y - mu
    var = _seg_sum(yc * yc, e_ref) * inv_n
    yn = (yc * _seg_expand(lax.rsqrt(var + GN_EPS), et_ref)) * lng_ref[...] + lnb_ref[...]
    v = v_ref[...]
    bonus = _seg_sum((r_ref[...] * rk_ref[...]) * (km_ref[0] + km_ref[1]), e_ref)
    yn = yn + _seg_expand(bonus, et_ref) * v
    out = _dotb(yn * gate_ref[...], wo_ref[...])
    _residual_and_route(x, out, mods, g2_ref, wrt_ref, brt_ref, xo_ref, h2_ref, route_ref)


def _rwkv_out(x, mods3, mod_spec, y0, y1, r, v, km, gate, i, j, p):
    t_tok, d = x.shape
    row = lambda a: a.reshape(a.shape[0], 1, a.shape[1])
    tok = pl.BlockSpec((TM, d), lambda t: (t, 0))
    dir_spec = pl.BlockSpec((2, TM, d), lambda t: (0, t, 0))
    args = (x, mods3, y0, y1, r, v, km, gate, row(p["rw_ln_g"]), row(p["rw_ln_b"]), row(p["rw_r_k"].reshape(-1, d)),
            p["rw_wo"], p["head_sel"], p["head_sel_t"], row(p["norm2_g"]), p["wrt"], p["brt"])
    in_specs = [tok, mod_spec, tok, tok, tok, tok, dir_spec, tok,
                _layer_spec(args[8].shape, j), _layer_spec(args[9].shape, j), _layer_spec(args[10].shape, j),
                _layer_spec(args[11].shape, j), _full_spec(args[12].shape), _full_spec(args[13].shape),
                _layer_spec(args[14].shape, i), _layer_spec(args[15].shape, i), _layer_spec(args[16].shape, i)]
    return pl.pallas_call(
        _rwkv_out_kernel,
        grid=(t_tok // TM,),
        in_specs=in_specs,
        out_specs=_mixer_out_specs(d),
        out_shape=_mixer_out_shapes(t_tok, d),
        compiler_params=pltpu.CompilerParams(dimension_semantics=("parallel",), vmem_limit_bytes=VMEM_LIMIT),
        name="rwkv_out",
    )(*args)


def _grid_shift(h, bsz, n_lat, n_ctx):
    d = h.shape[-1]
    q = d // 4
    g = h[:bsz * n_lat].reshape(bsz, n_lat // GRID_W, GRID_W, d)
    left = jnp.pad(g[:, :, :-1, :q], ((0, 0), (0, 0), (1, 0), (0, 0)))
    right = jnp.pad(g[:, :, 1:, q:2 * q], ((0, 0), (0, 0), (0, 1), (0, 0)))
    up = jnp.pad(g[:, :-1, :, 2 * q:3 * q], ((0, 0), (1, 0), (0, 0), (0, 0)))
    down = jnp.pad(g[:, 1:, :, 3 * q:], ((0, 0), (0, 1), (0, 0), (0, 0)))
    lat = jnp.concatenate([left, right, up, down], axis=-1).reshape(bsz * n_lat, d)
    c = h[bsz * n_lat:].reshape(bsz, n_ctx, d)
    half = d // 2
    prev = jnp.pad(c[:, :-1, :half], ((0, 0), (1, 0), (0, 0)))
    nxt = jnp.pad(c[:, 1:, half:], ((0, 0), (0, 1), (0, 0)))
    ctx = jnp.concatenate([prev, nxt], axis=-1).reshape(bsz * n_ctx, d)
    return jnp.concatenate([lat, ctx], axis=0)


def _expert_kernel(be_ref, nu_ref, x_ref, wg_ref, wu_ref, wd_ref, o_ref, wg_s, wu_s, wd_s):
    b = pl.program_id(0)
    e = be_ref[b]
    prev = be_ref[jnp.maximum(b - 1, 0)]
    used = b < nu_ref[0]

    @pl.when(used & ((b == 0) | (e != prev)))
    def _():
        wg_s[...] = wg_ref[...].astype(BF16)
        wu_s[...] = wu_ref[...].astype(BF16)
        wd_s[...] = wd_ref[...].astype(BF16)

    @pl.when(used)
    def _():
        xb = x_ref[...]
        g = jnp.dot(xb, wg_s[...], preferred_element_type=F32)
        u = jnp.dot(xb, wu_s[...], preferred_element_type=F32)
        hmid = (g * jax.nn.sigmoid(g)) * u
        o_ref[...] = jnp.dot(hmid.astype(BF16), wd_s[...], preferred_element_type=F32)

    @pl.when(jnp.logical_not(used))
    def _():
        o_ref[...] = jnp.zeros_like(o_ref)


def _experts(buf, block_expert, n_used, w_gate, w_up, w_down, i):
    cap, d = buf.shape
    f = w_gate.shape[-1]
    n_blocks = cap // MOE_ROWS
    grid_spec = pltpu.PrefetchScalarGridSpec(
        num_scalar_prefetch=2,
        grid=(n_blocks,),
        in_specs=[pl.BlockSpec((MOE_ROWS, d), lambda b, be, nu: (b, 0)),
                  pl.BlockSpec((None, None, d, f), lambda b, be, nu: (i, be[b], 0, 0)),
                  pl.BlockSpec((None, None, d, f), lambda b, be, nu: (i, be[b], 0, 0)),
                  pl.BlockSpec((None, None, f, d), lambda b, be, nu: (i, be[b], 0, 0))],
        out_specs=pl.BlockSpec((MOE_ROWS, d), lambda b, be, nu: (b, 0)),
        scratch_shapes=[pltpu.VMEM((d, f), BF16), pltpu.VMEM((d, f), BF16), pltpu.VMEM((f, d), BF16)],
    )
    return pl.pallas_call(
        _expert_kernel,
        grid_spec=grid_spec,
        out_shape=jax.ShapeDtypeStruct((cap, d), F32),
        compiler_params=pltpu.CompilerParams(dimension_semantics=("arbitrary",), vmem_limit_bytes=VMEM_LIMIT),
        name="experts",
    )(block_expert, n_used, buf, w_gate, w_up, w_down)


def _combine_kernel(x_ref, mod_ref, route_ref, y0_ref, y1_ref, fg_ref, xo_ref, *, final):
    gt2 = _mod_parts(mod_ref)[5]
    route = route_ref[...]
    xn = x_ref[...] + gt2 * (route[:, 2:3] * y0_ref[...] + route[:, 3:4] * y1_ref[...])
    if final:
        xn = (xn * lax.rsqrt(jnp.mean(xn * xn, axis=-1, keepdims=True) + NORM_EPS)) * fg_ref[...]
    xo_ref[...] = xn


def _combine(x, mods3, mod_spec, route, y0, y1, final_g, final):
    t_tok, d = x.shape
    tok = pl.BlockSpec((TM, d), lambda t: (t, 0))
    return pl.pallas_call(
        functools.partial(_combine_kernel, final=final),
        grid=(t_tok // TM,),
        in_specs=[tok, mod_spec, pl.BlockSpec((TM, LANES), lambda t: (t, 0)), tok, tok, _full_spec((1, d))],
        out_specs=tok,
        out_shape=jax.ShapeDtypeStruct((t_tok, d), F32),
        compiler_params=pltpu.CompilerParams(dimension_semantics=("parallel",), vmem_limit_bytes=VMEM_LIMIT),
        name="moe_combine",
    )(x, mods3, route, y0, y1, final_g.reshape(1, d))


def _moe(x, h2, route, mods3, mod_spec, i, p, final):
    t_tok, d = x.shape
    n_assign = t_tok * TOP_K
    expert_id = route[:, :TOP_K].astype(jnp.int32).reshape(-1)
    onehot = (expert_id[:, None] == jnp.arange(N_EXPERTS, dtype=jnp.int32)[None, :]).astype(jnp.int32)
    csum = jnp.cumsum(onehot, axis=0)
    rank = jnp.sum(onehot * csum, axis=1) - 1
    counts = csum[-1]
    padded = ((counts + MOE_ROWS - 1) // MOE_ROWS) * MOE_ROWS
    pad_end = jnp.cumsum(padded)
    pad_start = pad_end - padded
    dest = pad_start[expert_id] + rank
    n_blocks = (n_assign + MOE_ROWS - 1) // MOE_ROWS + N_EXPERTS
    cap = n_blocks * MOE_ROWS
    slot_token = jnp.zeros((cap,), jnp.int32).at[dest].set(jnp.arange(n_assign, dtype=jnp.int32) // TOP_K)
    block_expert = jnp.minimum(
        jnp.searchsorted(pad_end, jnp.arange(n_blocks, dtype=jnp.int32) * MOE_ROWS, side="right"),
        N_EXPERTS - 1).astype(jnp.int32)
    n_used = (pad_end[-1:] // MOE_ROWS).astype(jnp.int32)
    buf = jnp.take(h2, slot_token, axis=0)
    out = _experts(buf, block_expert, n_used, p["moe_w_gate"], p["moe_w_up"], p["moe_w_down"], i)
    dest2 = dest.reshape(t_tok, TOP_K)
    y0 = jnp.take(out, dest2[:, 0], axis=0)
    y1 = jnp.take(out, dest2[:, 1], axis=0)
    return _combine(x, mods3, mod_spec, route, y0, y1, p["final_g"], final)


def kernel(x, c, ctx, c_ctx, ada_w, ada_b, norm1_g, norm2_g, final_g, ga_w_in, ga_b_in, ga_ln_g, ga_ln_b, ga_w_s,
           ga_b_s, ga_w_out, ga_b_out, rw_mu, rw_wr, rw_wk, rw_wv, rw_wo, rw_w0, rw_w1, rw_w2, rw_a0, rw_a1, rw_a2,
           rw_g1, rw_g2, rw_k_k, rw_k_a, rw_r_k, rw_ln_g, rw_ln_b, moe_w_grp, moe_b_grp, moe_w_exp, moe_b_exp,
           moe_w_gate, moe_w_up, moe_w_down):
    bsz, n_lat, d = x.shape
    n_ctx = ctx.shape[1]
    depth = ada_w.shape[0]
    n_mixers = 2
    assert n_lat % TM == 0 and n_ctx % TM == 0 and n_lat % SCAN_BLOCK == 0 and n_ctx % SCAN_BLOCK == 0
    assert bsz + 1 <= 8 and d % SCAN_LANES == 0
    t_lat = bsz * n_lat
    lat_tiles = t_lat // TM
    tiles_per_batch = n_lat // TM
    heads = d // RWKV_HEAD

    s_rows = jnp.concatenate([c, c_ctx[None, :], jnp.zeros((8 - bsz - 1, d), F32)], axis=0)
    mods = _mods(s_rows, ada_w, ada_b)
    mods3 = mods.reshape(depth * 8, 1, N_MOD * d)

    def mod_spec_for(i):
        def imap(t):
            return (i * 8 + jnp.where(t < lat_tiles, t // tiles_per_batch, bsz), 0, 0)
        return pl.BlockSpec((1, 1, N_MOD * d), imap)

    head_sel = (jnp.arange(d)[:, None] // RWKV_HEAD == jnp.arange(LANES)[None, :]).astype(BF16)
    zpad = lambda a, axis, n: jnp.pad(a, [(0, n - a.shape[k]) if k == axis else (0, 0) for k in range(a.ndim)])
    glora = ((rw_g1.shape[-1] + LANES - 1) // LANES) * LANES
    wrt = jnp.concatenate([moe_w_grp, moe_w_exp], axis=-1)
    brt = jnp.concatenate([moe_b_grp, moe_b_exp], axis=-1)
    p = dict(
        norm1_g=norm1_g, norm2_g=norm2_g, final_g=final_g,
        ga_w_in=ga_w_in.astype(BF16), ga_b_in=ga_b_in, ga_ln_g=ga_ln_g, ga_ln_b=ga_ln_b,
        ga_w_s=ga_w_s.astype(BF16), ga_b_s=ga_b_s[..., None], ga_w_out=ga_w_out.astype(BF16), ga_b_out=ga_b_out,
        rw_mu8=zpad(rw_mu, 1, 8),
        rw_wr=rw_wr.astype(BF16), rw_wk=rw_wk.astype(BF16), rw_wv=rw_wv.astype(BF16), rw_wo=rw_wo.astype(BF16),
        rw_w1c=jnp.concatenate([rw_w1[:, 0], rw_w1[:, 1]], axis=-1).astype(BF16),
        rw_w2c=jnp.concatenate([rw_w2[:, 0], rw_w2[:, 1]], axis=-2).astype(BF16),
        rw_a1c=jnp.concatenate([rw_a1[:, 0], rw_a1[:, 1]], axis=-1).astype(BF16),
        rw_a2c=jnp.concatenate([rw_a2[:, 0], rw_a2[:, 1]], axis=-2).astype(BF16),
        rw_w0=rw_w0, rw_a0=rw_a0,
        rw_g1p=zpad(rw_g1, 2, glora).astype(BF16), rw_g2p=zpad(rw_g2, 1, glora).astype(BF16),
        rw_k_k=rw_k_k, rw_k_a=rw_k_a, rw_r_k=rw_r_k, rw_ln_g=rw_ln_g, rw_ln_b=rw_ln_b,
        head_sel=head_sel, head_sel_t=head_sel.T,
        wrt=zpad(wrt, 2, LANES).astype(BF16), brt=zpad(brt, 1, LANES).reshape(depth, 1, LANES),
        moe_w_gate=moe_w_gate, moe_w_up=moe_w_up, moe_w_down=moe_w_down,
    )
    assert heads <= LANES and 2 * rw_w1.shape[-1] == LANES and 2 * rw_a1.shape[-1] == LANES

    xs = jnp.concatenate([x.reshape(t_lat, d), ctx.reshape(bsz * n_ctx, d)], axis=0)
    for i in range(depth):
        j = i // n_mixers
        mod_spec = mod_spec_for(i)
        if i % n_mixers == 0:
            xs, h2, route = _gmlp_layer(xs, mods3, mod_spec, i, j, p)
        else:
            h = _normmod(xs, mods3, mod_spec, i, p)
            hs = _grid_shift(h, bsz, n_lat, n_ctx)
            r, v, kkn, gate, lw, km, bb = _rwkv_prep(h, hs, j, p)
            y0 = _wkv_scan(r, v, kkn, lw, km, bb, 0, bsz, n_lat, n_ctx)
            y1 = _wkv_scan(r, v, kkn, lw, km, bb, 1, bsz, n_lat, n_ctx)
            xs, h2, route = _rwkv_out(xs, mods3, mod_spec, y0, y1, r, v, km, gate, i, j, p)
        xs = _moe(xs, h2, route, mods3, mod_spec, i, p, final=(i == depth - 1))
    return xs[:t_lat].reshape(bsz, n_lat, d)
```

```python
import functools
import math

import jax
import jax.numpy as jnp
from jax import lax
from jax.experimental import pallas as pl
from jax.experimental.pallas import tpu as pltpu

F32 = jnp.float32
BF16 = jnp.bfloat16

N_MOD = 6
NORM_EPS = 1e-6
GRID_W = 64
CHUNK = 128
GMLP_GROUP_CH = 128
LN_EPS = 1e-5
RWKV_HEAD = 64
GN_EPS = 64e-5
N_GROUPS = 4
EXPERTS_PER_GROUP = 8
N_EXPERTS = N_GROUPS * EXPERTS_PER_GROUP
TOP_K = 2

LANES = 128
TM = 256
MOE_ROWS = 256
SCAN_BLOCK = 256
SCAN_CHUNK = 32
SCAN_LANES = 256
SCAN_GROUPS = 1
VMEM_LIMIT = 56 * 1024 * 1024
NEG_BIG = -3.0e38
SQRT_HALF = 0.7071067811865476


def _dotb(a, b):
    return jnp.dot(a.astype(BF16), b.astype(BF16), preferred_element_type=F32)


def _dot_nt(a, b):
    return lax.dot_general(a.astype(BF16), b.astype(BF16), (((1,), (1,)), ((), ())), preferred_element_type=F32)


def _dot_tn(a, b):
    return lax.dot_general(a.astype(BF16), b.astype(BF16), (((0,), (0,)), ((), ())), preferred_element_type=F32)


def _split2(x):
    hi = x.astype(BF16)
    lo = (x - hi.astype(F32)).astype(BF16)
    return hi, lo


def _split3(x):
    hi = x.astype(BF16)
    r1 = x - hi.astype(F32)
    mid = r1.astype(BF16)
    lo = (r1 - mid.astype(F32)).astype(BF16)
    return hi, mid, lo


def _dot_sel(sel_bf16, x):
    hi, mid, lo = _split3(x)
    d = lambda p: jnp.dot(sel_bf16, p, preferred_element_type=F32)
    return d(hi) + d(mid) + d(lo)


def _dot_x_sel(x, sel_bf16):
    hi, mid, lo = _split3(x)
    d = lambda p: jnp.dot(p, sel_bf16, preferred_element_type=F32)
    return d(hi) + d(mid) + d(lo)


def _norm_mod(x, g, sc, sh):
    y = x * lax.rsqrt(jnp.mean(x * x, axis=-1, keepdims=True) + NORM_EPS)
    return (y * g) * (1.0 + sc) + sh


def _mod_parts(mod_ref):
    m = mod_ref[0]
    d = m.shape[-1] // N_MOD
    return [m[:, i * d:(i + 1) * d] for i in range(N_MOD)]


def _mods_kernel(s_ref, w_ref, b_ref, o_ref):
    s = s_ref[...]
    s = s * jax.nn.sigmoid(s)
    o_ref[0] = _dotb(s, w_ref[0]) + b_ref[0]


def _mods(s_rows, ada_w, ada_b):
    depth, d, nd = ada_w.shape
    tn = nd // 4
    return pl.pallas_call(
        _mods_kernel,
        grid=(depth, nd // tn),
        in_specs=[pl.BlockSpec(s_rows.shape, lambda i, j: (0, 0)),
                  pl.BlockSpec((1, d, tn), lambda i, j: (i, 0, j)),
                  pl.BlockSpec((1, 1, tn), lambda i, j: (i, 0, j))],
        out_specs=pl.BlockSpec((1, s_rows.shape[0], tn), lambda i, j: (i, 0, j)),
        out_shape=jax.ShapeDtypeStruct((depth, s_rows.shape[0], nd), F32),
        compiler_params=pltpu.CompilerParams(dimension_semantics=("parallel", "parallel"),
                                             vmem_limit_bytes=VMEM_LIMIT),
        name="mods",
    )(s_rows, ada_w, ada_b.reshape(depth, 1, nd))


def _route(logits):
    lane = lax.broadcasted_iota(jnp.int32, logits.shape, 1)
    lane_f = lane.astype(F32)
    big = jnp.float32(1e9)
    is_g = lane < N_GROUPS
    gl = jnp.where(is_g, logits, NEG_BIG)
    gm = jnp.max(gl, axis=1, keepdims=True)
    grp = jnp.min(jnp.where(is_g & (gl == gm), lane_f, big), axis=1, keepdims=True)
    p_grp = 1.0 / jnp.sum(jnp.where(is_g, jnp.exp(gl - gm), 0.0), axis=1, keepdims=True)
    lo = N_GROUPS + grp * EXPERTS_PER_GROUP
    in_grp = (lane_f >= lo) & (lane_f < lo + EXPERTS_PER_GROUP)
    el = jnp.where(in_grp, logits, NEG_BIG)
    m1 = jnp.max(el, axis=1, keepdims=True)
    i1 = jnp.min(jnp.where(in_grp & (el == m1), lane_f, big), axis=1, keepdims=True)
    rest = in_grp & (lane_f != i1)
    el2 = jnp.where(rest, logits, NEG_BIG)
    m2 = jnp.max(el2, axis=1, keepdims=True)
    i2 = jnp.min(jnp.where(rest & (el2 == m2), lane_f, big), axis=1, keepdims=True)
    e21 = jnp.exp(m2 - m1)
    s0 = 1.0 / (1.0 + e21)
    w0 = p_grp * s0
    w1 = p_grp * (e21 * s0)
    out = jnp.where(lane == 0, i1 - N_GROUPS,
                    jnp.where(lane == 1, i2 - N_GROUPS,
                              jnp.where(lane == 2, w0, jnp.where(lane == 3, w1, 0.0))))
    return out


def _residual_and_route(x, y, mods, g2_ref, wrt_ref, brt_ref, xo_ref, h2_ref, route_ref):
    _, _, gt1, sh2, sc2, _ = mods
    xn = x + gt1 * y
    xo_ref[...] = xn
    h2 = _norm_mod(xn, g2_ref[...], sc2, sh2).astype(BF16)
    h2_ref[...] = h2
    logits = jnp.dot(h2, wrt_ref[...], preferred_element_type=F32) + brt_ref[...]
    route_ref[...] = _route(logits)


def _gmlp_kernel(x_ref, mod_ref, g1_ref, g2_ref, win_ref, bin_ref, lng_ref, lnb_ref, ws_ref, bs_ref,
                 wout_ref, bout_ref, wrt_ref, brt_ref, xo_ref, h2_ref, route_ref, gated_ref):
    x = x_ref[...]
    mods = _mod_parts(mod_ref)
    sh1, sc1 = mods[0], mods[1]
    h = _norm_mod(x, g1_ref[...], sc1, sh1)
    z = _dotb(h, win_ref[...]) + bin_ref[...]
    z = 0.5 * z * (1.0 + lax.erf(z * SQRT_HALF))
    width = z.shape[1] // 2
    u = z[:, :width]
    v = z[:, width:]
    mu = jnp.mean(v, axis=-1, keepdims=True)
    vc = v - mu
    var = jnp.mean(vc * vc, axis=-1, keepdims=True)
    vn = ((vc * lax.rsqrt(var + LN_EPS)) * lng_ref[...] + lnb_ref[...]).astype(BF16)
    n_chunks = x.shape[0] // CHUNK
    for g in range(width // GMLP_GROUP_CH):
        cs = slice(g * GMLP_GROUP_CH, (g + 1) * GMLP_GROUP_CH)
        rhs = jnp.concatenate([vn[c * CHUNK:(c + 1) * CHUNK, cs] for c in range(n_chunks)], axis=1)
        s = jnp.dot(ws_ref[g], rhs, preferred_element_type=F32) + bs_ref[g]
        for c in range(n_chunks):
            rsl = slice(c * CHUNK, (c + 1) * CHUNK)
            gated_ref[rsl, cs] = (u[rsl, cs] * s[:, c * GMLP_GROUP_CH:(c + 1) * GMLP_GROUP_CH]).astype(BF16)
    y = jnp.dot(gated_ref[...], wout_ref[...], preferred_element_type=F32) + bout_ref[...]
    _residual_and_route(x, y, mods, g2_ref, wrt_ref, brt_ref, xo_ref, h2_ref, route_ref)


def _tile_specs(t_tok, d, seg_of):
    x_spec = pl.BlockSpec((TM, d), lambda t: (t, 0))
    return x_spec


def _full_spec(shape):
    nd = len(shape)
    return pl.BlockSpec(shape, lambda *_: (0,) * nd)


def _layer_spec(shape, j):
    nd = len(shape)
    return pl.BlockSpec((None,) + tuple(shape[1:]), lambda *_: (j,) + (0,) * (nd - 1))


def _mixer_out_shapes(t_tok, d):
    return (jax.ShapeDtypeStruct((t_tok, d), F32), jax.ShapeDtypeStruct((t_tok, d), BF16),
            jax.ShapeDtypeStruct((t_tok, LANES), F32))


def _mixer_out_specs(d):
    return (pl.BlockSpec((TM, d), lambda t: (t, 0)), pl.BlockSpec((TM, d), lambda t: (t, 0)),
            pl.BlockSpec((TM, LANES), lambda t: (t, 0)))


def _gmlp_layer(x, mods3, mod_spec, i, j, p):
    t_tok, d = x.shape
    width = p["ga_w_out"].shape[1]
    row = lambda a: a.reshape(a.shape[0], 1, a.shape[1])
    args = (x, mods3, row(p["norm1_g"]), row(p["norm2_g"]), p["ga_w_in"], row(p["ga_b_in"]), row(p["ga_ln_g"]),
            row(p["ga_ln_b"]), p["ga_w_s"], p["ga_b_s"], p["ga_w_out"], row(p["ga_b_out"]), p["wrt"], p["brt"])
    in_specs = [pl.BlockSpec((TM, d), lambda t: (t, 0)), mod_spec,
                _layer_spec(args[2].shape, i), _layer_spec(args[3].shape, i),
                _layer_spec(args[4].shape, j), _layer_spec(args[5].shape, j), _layer_spec(args[6].shape, j),
                _layer_spec(args[7].shape, j), _layer_spec(args[8].shape, j), _layer_spec(args[9].shape, j),
                _layer_spec(args[10].shape, j), _layer_spec(args[11].shape, j),
                _layer_spec(args[12].shape, i), _layer_spec(args[13].shape, i)]
    return pl.pallas_call(
        _gmlp_kernel,
        grid=(t_tok // TM,),
        in_specs=in_specs,
        out_specs=_mixer_out_specs(d),
        out_shape=_mixer_out_shapes(t_tok, d),
        scratch_shapes=[pltpu.VMEM((TM, width), BF16)],
        compiler_params=pltpu.CompilerParams(dimension_semantics=("parallel",), vmem_limit_bytes=VMEM_LIMIT),
        name="gmlp_layer",
    )(*args)


def _normmod_kernel(x_ref, mod_ref, g1_ref, h_ref):
    mods = _mod_parts(mod_ref)
    h_ref[...] = _norm_mod(x_ref[...], g1_ref[...], mods[1], mods[0])


def _normmod(x, mods3, mod_spec, i, p):
    t_tok, d = x.shape
    g1 = p["norm1_g"].reshape(-1, 1, d)
    return pl.pallas_call(
        _normmod_kernel,
        grid=(t_tok // TM,),
        in_specs=[pl.BlockSpec((TM, d), lambda t: (t, 0)), mod_spec, _layer_spec(g1.shape, i)],
        out_specs=pl.BlockSpec((TM, d), lambda t: (t, 0)),
        out_shape=jax.ShapeDtypeStruct((t_tok, d), F32),
        compiler_params=pltpu.CompilerParams(dimension_semantics=("parallel",), vmem_limit_bytes=VMEM_LIMIT),
        name="normmod",
    )(x, mods3, g1)


def _seg_sum(q, e_ref):
    return _dot_x_sel(q, e_ref[...])


def _seg_expand(s, et_ref):
    return _dot_x_sel(s, et_ref[...])


def _rwkv_prep_kernel(h_ref, hs_ref, mu_ref, wr_ref, wk_ref, wv_ref, w1_ref, w2_ref, w0_ref, a1_ref, a2_ref, a0_ref,
                      g1_ref, g2_ref, kk_ref, ka_ref, e_ref, et_ref,
                      r_o, v_o, kkn_o, gate_o, lw_o, km_o, bb_o):
    h = h_ref[...]
    xx = hs_ref[...] - h
    mu = mu_ref[...]
    xr, xw, xk, xv, xa, xg = [h + xx * mu[q:q + 1] for q in range(6)]
    r = _dotb(xr, wr_ref[...])
    k = _dotb(xk, wk_ref[...])
    v = _dotb(xv, wv_ref[...])
    r_o[...] = r
    v_o[...] = v
    kkr = k * kk_ref[...]
    rs = lax.rsqrt(_seg_sum(kkr * kkr, e_ref) + 1e-12)
    kkn = kkr * _seg_expand(rs, et_ref)
    kkn_o[...] = kkn
    gate_o[...] = _dotb(jax.nn.sigmoid(_dotb(xg, g1_ref[...])), g2_ref[...])
    tw = jnp.tanh(_dotb(xw, w1_ref[...]))
    ta = _dotb(xa, a1_ref[...])
    lane = lax.broadcasted_iota(jnp.int32, tw.shape, 1)
    half = tw.shape[1] // 2
    w0 = w0_ref[...]
    a0 = a0_ref[...]
    ka = ka_ref[...]
    for z in range(2):
        zm = (lane >= z * half) & (lane < (z + 1) * half)
        lwz = _dotb(jnp.where(zm, tw, 0.0), w2_ref[...])
        t = -(w0[z:z + 1] + lwz)
        sp = jnp.maximum(t, 0.0) + jnp.log1p(jnp.exp(-jnp.abs(t)))
        lw_o[z] = -jnp.exp(-sp - 0.5)
        la = _dotb(jnp.where(zm, ta, 0.0), a2_ref[...])
        a = jax.nn.sigmoid(a0[z:z + 1] + la)
        km_o[z] = k * (1.0 + (a - 1.0) * ka)
        bb_o[z] = kkn * a


def _rwkv_prep(h, hs, j, p):
    t_tok, d = h.shape
    row = lambda a: a.reshape(a.shape[0], 1, a.shape[1])
    args = (h, hs, p["rw_mu8"], p["rw_wr"], p["rw_wk"], p["rw_wv"], p["rw_w1c"], p["rw_w2c"], p["rw_w0"],
            p["rw_a1c"], p["rw_a2c"], p["rw_a0"], p["rw_g1p"], p["rw_g2p"], row(p["rw_k_k"]), row(p["rw_k_a"]),
            p["head_sel"], p["head_sel_t"])
    tok = pl.BlockSpec((TM, d), lambda t: (t, 0))
    in_specs = [tok, tok] + [_layer_spec(a.shape, j) for a in args[2:16]] + [_full_spec(args[16].shape),
                                                                             _full_spec(args[17].shape)]
    dir_spec = pl.BlockSpec((2, TM, d), lambda t: (0, t, 0))
    tok_shape = jax.ShapeDtypeStruct((t_tok, d), F32)
    dir_shape = jax.ShapeDtypeStruct((2, t_tok, d), F32)
    return pl.pallas_call(
        _rwkv_prep_kernel,
        grid=(t_tok // TM,),
        in_specs=in_specs,
        out_specs=(tok, tok, tok, tok, dir_spec, dir_spec, dir_spec),
        out_shape=(tok_shape, tok_shape, tok_shape, tok_shape, dir_shape, dir_shape, dir_shape),
        compiler_params=pltpu.CompilerParams(dimension_semantics=("parallel",), vmem_limit_bytes=VMEM_LIMIT),
        name="rwkv_prep",
    )(*args)


def _scan_prepare(r, v, kk, lw, km, bb, rev, out):
    c = SCAN_CHUNK
    nb = SCAN_BLOCK
    a = -kk
    row = lax.broadcasted_iota(jnp.int32, (nb, nb), 0)
    col = lax.broadcasted_iota(jnp.int32, (nb, nb), 1)
    shift_c = int(math.log2(c))
    same = (row >> shift_c) == (col >> shift_c)
    if rev:
        incl = same & (col >= row)
        strict = same & (col > row)
    else:
        incl = same & (col <= row)
        strict = same & (col < row)
    incl_b = jnp.where(incl, 1.0, 0.0).astype(BF16)
    same_b = jnp.where(same, 1.0, 0.0).astype(BF16)
    eye = jnp.where(row == col, 1.0, 0.0)
    head_of_lane = col >> int(math.log2(RWKV_HEAD))

    cum = _dot_sel(incl_b, lw)
    yield
    tot = _dot_sel(same_b, lw)
    yield
    at = a * jnp.exp(cum - lw)
    rt = r * jnp.exp(cum)
    e_inv = jnp.exp(-cum)
    bt = bb * e_inv
    kt = km * e_inv
    e_rem = jnp.exp(tot - cum)
    bh = (bb * e_rem).astype(BF16)
    kh = (km * e_rem).astype(BF16)
    e_tot = jnp.exp(tot)
    et = jnp.concatenate([e_tot[ci * c:ci * c + 1] for ci in range(nb // c)], axis=0)
    bk = jnp.concatenate([bt, kt], axis=0).astype(BF16)
    yield
    ah = jnp.zeros_like(r)
    vh = jnp.zeros_like(r)
    rh = jnp.zeros_like(r)
    yh = jnp.zeros_like(r)
    for hh in range(SCAN_LANES // RWKV_HEAD):
        hm = head_of_lane == hh
        at_h = jnp.where(hm, at, 0.0)
        rt_h = jnp.where(hm, rt, 0.0)
        v_h = jnp.where(hm, v, 0.0).astype(BF16)
        s = _dot_nt(jnp.concatenate([at_h, rt_h], axis=0), bk)
        a_ab = jnp.where(strict, s[:nb, :nb], 0.0)
        a_ak = jnp.where(strict, s[:nb, nb:], 0.0)
        m_rb = jnp.where(incl, s[nb:, :nb], 0.0).astype(BF16)
        m_rk = jnp.where(incl, s[nb:, nb:], 0.0)
        yield
        pw = a_ab
        tinv = eye + a_ab
        for _ in range(shift_c - 1):
            pw = _dotb(pw, pw)
            tinv = tinv + _dotb(tinv, pw)
            yield
        ah_h = _dotb(tinv, at_h)
        x1 = _dotb(a_ak, v_h)
        yield
        vh_h = _dotb(tinv, x1)
        ah = ah + ah_h
        vh = vh + vh_h
        rh = rh + rt_h + _dotb(m_rb, ah_h)
        yield
        yh = yh + _dotb(m_rk, v_h) + _dotb(m_rb, vh_h)
        yield
    out.extend([ah.astype(BF16), rh.astype(BF16), vh, yh, bh, kh, v.astype(BF16), et])


def _scan_prepare_stages():
    return 3 + (SCAN_LANES // RWKV_HEAD) * (int(math.log2(SCAN_CHUNK)) - 1 + 4)


def _scan_chain(g, ops, y_ref, lanes, rev, out):
    c = SCAN_CHUNK
    ah, rh, vh, yh, bh, kh, vb, et = ops
    n = g.shape[0]
    shift_h = int(math.log2(RWKV_HEAD))
    blockdiag = ((lax.broadcasted_iota(jnp.int32, (n, n), 0) >> shift_h)
                 == (lax.broadcasted_iota(jnp.int32, (n, n), 1) >> shift_h))
    n_chunks = SCAN_BLOCK // c
    order = range(n_chunks - 1, -1, -1) if rev else range(n_chunks)
    for ci in order:
        rs = slice(ci * c, (ci + 1) * c)
        p = _dot_nt(jnp.concatenate([ah[rs], rh[rs]], axis=0), g)
        u = p[:c] + vh[rs]
        y_ref[rs, lanes] = p[c:] + yh[rs]
        yield
        dg = _dot_tn(jnp.concatenate([u.astype(BF16), vb[rs]], axis=0), jnp.concatenate([bh[rs], kh[rs]], axis=0))
        g = jnp.where(blockdiag, g * et[ci:ci + 1, :] + dg, 0.0)
        yield
    out.append(g)


def _scan_chain_stages():
    return 2 * (SCAN_BLOCK // SCAN_CHUNK)


def _interleave(gen_a, n_a, gen_b, n_b):
    done_b = 0
    for i in range(n_a):
        next(gen_a, None)
        want_b = ((i + 1) * n_b) // n_a
        while done_b < want_b:
            next(gen_b, None)
            done_b += 1
    for g in (gen_a, gen_b):
        for _ in g:
            pass


def _scan_kernel(r_ref, v_ref, kk_ref, lw_ref, km_ref, bb_ref, y_ref, g_scr, *ops_scr, rev, groups):
    @pl.when(pl.program_id(2) == 0)
    def _():
        g_scr[...] = jnp.zeros_like(g_scr)
        for ref in ops_scr:
            ref[...] = jnp.zeros_like(ref)

    slot = pl.program_id(2) % 2
    for q in range(groups):
        lanes = slice(q * SCAN_LANES, (q + 1) * SCAN_LANES)
        prev = [ref.at[1 - slot, :, lanes] for ref in ops_scr]
        new, g_end = [], []
        prep = _scan_prepare(r_ref[:, lanes], v_ref[:, lanes], kk_ref[:, lanes], lw_ref[:, lanes], km_ref[:, lanes],
                             bb_ref[:, lanes], rev, new)
        chain = _scan_chain(g_scr[q], prev, y_ref, lanes, rev, g_end)
        _interleave(prep, _scan_prepare_stages(), chain, _scan_chain_stages())
        g_scr[q] = g_end[0]
        for ref, val in zip(ops_scr, new):
            ref[slot, :, lanes] = val


def _wkv_scan(r, v, kkn, lw, km, bb, z, bsz, n_lat, n_ctx):
    t_tok, d = r.shape
    lat_blocks = n_lat // SCAN_BLOCK
    ctx_blocks = n_ctx // SCAN_BLOCK
    steps = ctx_blocks + lat_blocks
    ctx_base = bsz * lat_blocks
    rev = z == 1
    width = SCAN_LANES * SCAN_GROUPS

    def blk(b, s):
        if rev:
            cb = ctx_base + b * ctx_blocks + (ctx_blocks - 1 - s)
            lb = b * lat_blocks + (lat_blocks - 1 - (s - ctx_blocks))
        else:
            cb = ctx_base + b * ctx_blocks + s
            lb = b * lat_blocks + (s - ctx_blocks)
        return jnp.where(s < ctx_blocks, cb, lb)

    in_blk = lambda b, s: blk(b, jnp.minimum(s, steps - 1))
    out_blk = lambda b, s: blk(b, jnp.maximum(s - 1, 0))
    tok = pl.BlockSpec((SCAN_BLOCK, width), lambda b, g, s: (in_blk(b, s), g))
    dirs = pl.BlockSpec((None, SCAN_BLOCK, width), lambda b, g, s: (z, in_blk(b, s), g))
    n_chunks = SCAN_BLOCK // SCAN_CHUNK
    ops_scratch = [pltpu.VMEM((2, SCAN_BLOCK, width), BF16), pltpu.VMEM((2, SCAN_BLOCK, width), BF16),
                   pltpu.VMEM((2, SCAN_BLOCK, width), F32), pltpu.VMEM((2, SCAN_BLOCK, width), F32),
                   pltpu.VMEM((2, SCAN_BLOCK, width), BF16), pltpu.VMEM((2, SCAN_BLOCK, width), BF16),
                   pltpu.VMEM((2, SCAN_BLOCK, width), BF16), pltpu.VMEM((2, n_chunks, width), F32)]
    return pl.pallas_call(
        functools.partial(_scan_kernel, rev=rev, groups=SCAN_GROUPS),
        grid=(bsz, d // width, steps + 1),
        in_specs=[tok, tok, tok, dirs, dirs, dirs],
        out_specs=pl.BlockSpec((SCAN_BLOCK, width), lambda b, g, s: (out_blk(b, s), g)),
        out_shape=jax.ShapeDtypeStruct((t_tok, d), F32),
        scratch_shapes=[pltpu.VMEM((SCAN_GROUPS, SCAN_LANES, SCAN_LANES), F32)] + ops_scratch,
        compiler_params=pltpu.CompilerParams(dimension_semantics=("parallel", "parallel", "arbitrary"),
                                             vmem_limit_bytes=VMEM_LIMIT),
        name="wkv_scan_rev" if rev else "wkv_scan_fwd",
    )(r, v, kkn, lw, km, bb)


def _rwkv_out_kernel(x_ref, mod_ref, y0_ref, y1_ref, r_ref, v_ref, km_ref, gate_ref, lng_ref, lnb_ref, rk_ref,
                     wo_ref, e_ref, et_ref, g2_ref, wrt_ref, brt_ref, xo_ref, h2_ref, route_ref):
    x = x_ref[...]
    mods = _mod_parts(mod_ref)
    y = y0_ref[...] + y1_ref[...]
    inv_n = 1.0 / RWKV_HEAD
    mu = _seg_expand(_seg_sum(y, e_ref) * inv_n, et_ref)
    yc = y - mu
    var = _seg_sum(yc * yc, e_ref) * inv_n
    yn = (yc * _seg_expand(lax.rsqrt(var + GN_EPS), et_ref)) * lng_ref[...] + lnb_ref[...]
    v = v_ref[...]
    bonus = _seg_sum((r_ref[...] * rk_ref[...]) * (km_ref[0] + km_ref[1]), e_ref)
    yn = yn + _seg_expand(bonus, et_ref) * v
    out = _dotb(yn * gate_ref[...], wo_ref[...])
    _residual_and_route(x, out, mods, g2_ref, wrt_ref, brt_ref, xo_ref, h2_ref, route_ref)


def _rwkv_out(x, mods3, mod_spec, y0, y1, r, v, km, gate, i, j, p):
    t_tok, d = x.shape
    row = lambda a: a.reshape(a.shape[0], 1, a.shape[1])
    tok = pl.BlockSpec((TM, d), lambda t: (t, 0))
    dir_spec = pl.BlockSpec((2, TM, d), lambda t: (0, t, 0))
    args = (x, mods3, y0, y1, r, v, km, gate, row(p["rw_ln_g"]), row(p["rw_ln_b"]), row(p["rw_r_k"].reshape(-1, d)),
            p["rw_wo"], p["head_sel"], p["head_sel_t"], row(p["norm2_g"]), p["wrt"], p["brt"])
    in_specs = [tok, mod_spec, tok, tok, tok, tok, dir_spec, tok,
                _layer_spec(args[8].shape, j), _layer_spec(args[9].shape, j), _layer_spec(args[10].shape, j),
                _layer_spec(args[11].shape, j), _full_spec(args[12].shape), _full_spec(args[13].shape),
                _layer_spec(args[14].shape, i), _layer_spec(args[15].shape, i), _layer_spec(args[16].shape, i)]
    return pl.pallas_call(
        _rwkv_out_kernel,
        grid=(t_tok // TM,),
        in_specs=in_specs,
        out_specs=_mixer_out_specs(d),
        out_shape=_mixer_out_shapes(t_tok, d),
        compiler_params=pltpu.CompilerParams(dimension_semantics=("parallel",), vmem_limit_bytes=VMEM_LIMIT),
        name="rwkv_out",
    )(*args)


def _grid_shift(h, bsz, n_lat, n_ctx):
    d = h.shape[-1]
    q = d // 4
    g = h[:bsz * n_lat].reshape(bsz, n_lat // GRID_W, GRID_W, d)
    left = jnp.pad(g[:, :, :-1, :q], ((0, 0), (0, 0), (1, 0), (0, 0)))
    right = jnp.pad(g[:, :, 1:, q:2 * q], ((0, 0), (0, 0), (0, 1), (0, 0)))
    up = jnp.pad(g[:, :-1, :, 2 * q:3 * q], ((0, 0), (1, 0), (0, 0), (0, 0)))
    down = jnp.pad(g[:, 1:, :, 3 * q:], ((0, 0), (0, 1), (0, 0), (0, 0)))
    lat = jnp.concatenate([left, right, up, down], axis=-1).reshape(bsz * n_lat, d)
    c = h[bsz * n_lat:].reshape(bsz, n_ctx, d)
    half = d // 2
    prev = jnp.pad(c[:, :-1, :half], ((0, 0), (1, 0), (0, 0)))
    nxt = jnp.pad(c[:, 1:, half:], ((0, 0), (0, 1), (0, 0)))
    ctx = jnp.concatenate([prev, nxt], axis=-1).reshape(bsz * n_ctx, d)
    return jnp.concatenate([lat, ctx], axis=0)


def _expert_kernel(be_ref, nu_ref, x_ref, wg_ref, wu_ref, wd_ref, o_ref, wg_s, wu_s, wd_s):
    b = pl.program_id(0)
    e = be_ref[b]
    prev = be_ref[jnp.maximum(b - 1, 0)]
    used = b < nu_ref[0]

    @pl.when(used & ((b == 0) | (e != prev)))
    def _():
        wg_s[...] = wg_ref[...].astype(BF16)
        wu_s[...] = wu_ref[...].astype(BF16)
        wd_s[...] = wd_ref[...].astype(BF16)

    @pl.when(used)
    def _():
        xb = x_ref[...]
        g = jnp.dot(xb, wg_s[...], preferred_element_type=F32)
        u = jnp.dot(xb, wu_s[...], preferred_element_type=F32)
        hmid = (g * jax.nn.sigmoid(g)) * u
        o_ref[...] = jnp.dot(hmid.astype(BF16), wd_s[...], preferred_element_type=F32)

    @pl.when(jnp.logical_not(used))
    def _():
        o_ref[...] = jnp.zeros_like(o_ref)


def _experts(buf, block_expert, n_used, w_gate, w_up, w_down, i):
    cap, d = buf.shape
    f = w_gate.shape[-1]
    n_blocks = cap // MOE_ROWS
    grid_spec = pltpu.PrefetchScalarGridSpec(
        num_scalar_prefetch=2,
        grid=(n_blocks,),
        in_specs=[pl.BlockSpec((MOE_ROWS, d), lambda b, be, nu: (b, 0)),
                  pl.BlockSpec((None, None, d, f), lambda b, be, nu: (i, be[b], 0, 0)),
                  pl.BlockSpec((None, None, d, f), lambda b, be, nu: (i, be[b], 0, 0)),
                  pl.BlockSpec((None, None, f, d), lambda b, be, nu: (i, be[b], 0, 0))],
        out_specs=pl.BlockSpec((MOE_ROWS, d), lambda b, be, nu: (b, 0)),
        scratch_shapes=[pltpu.VMEM((d, f), BF16), pltpu.VMEM((d, f), BF16), pltpu.VMEM((f, d), BF16)],
    )
    return pl.pallas_call(
        _expert_kernel,
        grid_spec=grid_spec,
        out_shape=jax.ShapeDtypeStruct((cap, d), F32),
        compiler_params=pltpu.CompilerParams(dimension_semantics=("arbitrary",), vmem_limit_bytes=VMEM_LIMIT),
        name="experts",
    )(block_expert, n_used, buf, w_gate, w_up, w_down)


def _combine_kernel(x_ref, mod_ref, route_ref, y0_ref, y1_ref, fg_ref, xo_ref, *, final):
    gt2 = _mod_parts(mod_ref)[5]
    route = route_ref[...]
    xn = x_ref[...] + gt2 * (route[:, 2:3] * y0_ref[...] + route[:, 3:4] * y1_ref[...])
    if final:
        xn = (xn * lax.rsqrt(jnp.mean(xn * xn, axis=-1, keepdims=True) + NORM_EPS)) * fg_ref[...]
    xo_ref[...] = xn


def _combine(x, mods3, mod_spec, route, y0, y1, final_g, final):
    t_tok, d = x.shape
    tok = pl.BlockSpec((TM, d), lambda t: (t, 0))
    return pl.pallas_call(
        functools.partial(_combine_kernel, final=final),
        grid=(t_tok // TM,),
        in_specs=[tok, mod_spec, pl.BlockSpec((TM, LANES), lambda t: (t, 0)), tok, tok, _full_spec((1, d))],
        out_specs=tok,
        out_shape=jax.ShapeDtypeStruct((t_tok, d), F32),
        compiler_params=pltpu.CompilerParams(dimension_semantics=("parallel",), vmem_limit_bytes=VMEM_LIMIT),
        name="moe_combine",
    )(x, mods3, route, y0, y1, final_g.reshape(1, d))


def _moe(x, h2, route, mods3, mod_spec, i, p, final):
    t_tok, d = x.shape
    n_assign = t_tok * TOP_K
    expert_id = route[:, :TOP_K].astype(jnp.int32).reshape(-1)
    onehot = (expert_id[:, None] == jnp.arange(N_EXPERTS, dtype=jnp.int32)[None, :]).astype(jnp.int32)
    csum = jnp.cumsum(onehot, axis=0)
    rank = jnp.sum(onehot * csum, axis=1) - 1
    counts = csum[-1]
    padded = ((counts + MOE_ROWS - 1) // MOE_ROWS) * MOE_ROWS
    pad_end = jnp.cumsum(padded)
    pad_start = pad_end - padded
    dest = pad_start[expert_id] + rank
    n_blocks = (n_assign + MOE_ROWS - 1) // MOE_ROWS + N_EXPERTS
    cap = n_blocks * MOE_ROWS
    slot_token = jnp.zeros((cap,), jnp.int32).at[dest].set(jnp.arange(n_assign, dtype=jnp.int32) // TOP_K)
    block_start = jnp.arange(n_blocks, dtype=jnp.int32) * MOE_ROWS
    block_expert = jnp.minimum(jnp.sum((pad_end[None, :] <= block_start[:, None]).astype(jnp.int32), axis=1),
                               N_EXPERTS - 1)
    n_used = (pad_end[-1:] // MOE_ROWS).astype(jnp.int32)
    buf = jnp.take(h2, slot_token, axis=0)
    out = _experts(buf, block_expert, n_used, p["moe_w_gate"], p["moe_w_up"], p["moe_w_down"], i)
    dest2 = dest.reshape(t_tok, TOP_K)
    y0 = jnp.take(out, dest2[:, 0], axis=0)
    y1 = jnp.take(out, dest2[:, 1], axis=0)
    return _combine(x, mods3, mod_spec, route, y0, y1, p["final_g"], final)


def kernel(x, c, ctx, c_ctx, ada_w, ada_b, norm1_g, norm2_g, final_g, ga_w_in, ga_b_in, ga_ln_g, ga_ln_b, ga_w_s,
           ga_b_s, ga_w_out, ga_b_out, rw_mu, rw_wr, rw_wk, rw_wv, rw_wo, rw_w0, rw_w1, rw_w2, rw_a0, rw_a1, rw_a2,
           rw_g1, rw_g2, rw_k_k, rw_k_a, rw_r_k, rw_ln_g, rw_ln_b, moe_w_grp, moe_b_grp, moe_w_exp, moe_b_exp,
           moe_w_gate, moe_w_up, moe_w_down):
    bsz, n_lat, d = x.shape
    n_ctx = ctx.shape[1]
    depth = ada_w.shape[0]
    n_mixers = 2
    assert n_lat % TM == 0 and n_ctx % TM == 0 and n_lat % SCAN_BLOCK == 0 and n_ctx % SCAN_BLOCK == 0
    assert bsz + 1 <= 8 and d % SCAN_LANES == 0
    t_lat = bsz * n_lat
    lat_tiles = t_lat // TM
    tiles_per_batch = n_lat // TM
    heads = d // RWKV_HEAD

    s_rows = jnp.concatenate([c, c_ctx[None, :], jnp.zeros((8 - bsz - 1, d), F32)], axis=0)
    mods = _mods(s_rows, ada_w, ada_b)
    mods3 = mods.reshape(depth * 8, 1, N_MOD * d)

    def mod_spec_for(i):
        def imap(t):
            return (i * 8 + jnp.where(t < lat_tiles, t // tiles_per_batch, bsz), 0, 0)
        return pl.BlockSpec((1, 1, N_MOD * d), imap)

    head_sel = (jnp.arange(d)[:, None] // RWKV_HEAD == jnp.arange(LANES)[None, :]).astype(BF16)
    zpad = lambda a, axis, n: jnp.pad(a, [(0, n - a.shape[k]) if k == axis else (0, 0) for k in range(a.ndim)])
    glora = ((rw_g1.shape[-1] + LANES - 1) // LANES) * LANES
    wrt = jnp.concatenate([moe_w_grp, moe_w_exp], axis=-1)
    brt = jnp.concatenate([moe_b_grp, moe_b_exp], axis=-1)
    p = dict(
        norm1_g=norm1_g, norm2_g=norm2_g, final_g=final_g,
        ga_w_in=ga_w_in.astype(BF16), ga_b_in=ga_b_in, ga_ln_g=ga_ln_g, ga_ln_b=ga_ln_b,
        ga_w_s=ga_w_s.astype(BF16), ga_b_s=ga_b_s[..., None], ga_w_out=ga_w_out.astype(BF16), ga_b_out=ga_b_out,
        rw_mu8=zpad(rw_mu, 1, 8),
        rw_wr=rw_wr.astype(BF16), rw_wk=rw_wk.astype(BF16), rw_wv=rw_wv.astype(BF16), rw_wo=rw_wo.astype(BF16),
        rw_w1c=jnp.concatenate([rw_w1[:, 0], rw_w1[:, 1]], axis=-1).astype(BF16),
        rw_w2c=jnp.concatenate([rw_w2[:, 0], rw_w2[:, 1]], axis=-2).astype(BF16),
        rw_a1c=jnp.concatenate([rw_a1[:, 0], rw_a1[:, 1]], axis=-1).astype(BF16),
        rw_a2c=jnp.concatenate([rw_a2[:, 0], rw_a2[:, 1]], axis=-2).astype(BF16),
        rw_w0=rw_w0, rw_a0=rw_a0,
        rw_g1p=zpad(rw_g1, 2, glora).astype(BF16), rw_g2p=zpad(rw_g2, 1, glora).astype(BF16),
        rw_k_k=rw_k_k, rw_k_a=rw_k_a, rw_r_k=rw_r_k, rw_ln_g=rw_ln_g, rw_ln_b=rw_ln_b,
        head_sel=head_sel, head_sel_t=head_sel.T,
        wrt=zpad(wrt, 2, LANES).astype(BF16), brt=zpad(brt, 1, LANES).reshape(depth, 1, LANES),
        moe_w_gate=moe_w_gate, moe_w_up=moe_w_up, moe_w_down=moe_w_down,
    )
    assert heads <= LANES and 2 * rw_w1.shape[-1] == LANES and 2 * rw_a1.shape[-1] == LANES

    xs = jnp.concatenate([x.reshape(t_lat, d), ctx.reshape(bsz * n_ctx, d)], axis=0)
    for i in range(depth):
        j = i // n_mixers
        mod_spec = mod_spec_for(i)
        if i % n_mixers == 0:
            xs, h2, route = _gmlp_layer(xs, mods3, mod_spec, i, j, p)
        else:
            h = _normmod(xs, mods3, mod_spec, i, p)
            hs = _grid_shift(h, bsz, n_lat, n_ctx)
            r, v, kkn, gate, lw, km, bb = _rwkv_prep(h, hs, j, p)
            y0 = _wkv_scan(r, v, kkn, lw, km, bb, 0, bsz, n_lat, n_ctx)
            y1 = _wkv_scan(r, v, kkn, lw, km, bb, 1, bsz, n_lat, n_ctx)
            xs, h2, route = _rwkv_out(xs, mods3, mod_spec, y0, y1, r, v, km, gate, i, j, p)
        xs = _moe(xs, h2, route, mods3, mod_spec, i, p, final=(i == depth - 1))
    return xs[:t_lat].reshape(bsz, n_lat, d)
```

```python
import functools
import math

import jax
import jax.numpy as jnp
from jax import lax
from jax.experimental import pallas as pl
from jax.experimental.pallas import tpu as pltpu

F32 = jnp.float32
BF16 = jnp.bfloat16

N_MOD = 6
NORM_EPS = 1e-6
GRID_W = 64
CHUNK = 128
GMLP_GROUP_CH = 128
LN_EPS = 1e-5
RWKV_HEAD = 64
GN_EPS = 64e-5
N_GROUPS = 4
EXPERTS_PER_GROUP = 8
N_EXPERTS = N_GROUPS * EXPERTS_PER_GROUP
TOP_K = 2

LANES = 128
TM = 256
MOE_ROWS = 256
SCAN_BLOCK = 256
SCAN_CHUNK = 32
SCAN_LANES = 256
SCAN_GROUPS = 1
VMEM_LIMIT = 56 * 1024 * 1024
NEG_BIG = -3.0e38
SQRT_HALF = 0.7071067811865476


def _dotb(a, b):
    return jnp.dot(a.astype(BF16), b.astype(BF16), preferred_element_type=F32)


def _dot_nt(a, b):
    return lax.dot_general(a.astype(BF16), b.astype(BF16), (((1,), (1,)), ((), ())), preferred_element_type=F32)


def _dot_tn(a, b):
    return lax.dot_general(a.astype(BF16), b.astype(BF16), (((0,), (0,)), ((), ())), preferred_element_type=F32)


def _split2(x):
    hi = x.astype(BF16)
    lo = (x - hi.astype(F32)).astype(BF16)
    return hi, lo


def _split3(x):
    hi = x.astype(BF16)
    r1 = x - hi.astype(F32)
    mid = r1.astype(BF16)
    lo = (r1 - mid.astype(F32)).astype(BF16)
    return hi, mid, lo


def _dot_sel(sel_bf16, x):
    hi, mid, lo = _split3(x)
    d = lambda p: jnp.dot(sel_bf16, p, preferred_element_type=F32)
    return d(hi) + d(mid) + d(lo)


def _dot_x_sel(x, sel_bf16):
    hi, mid, lo = _split3(x)
    d = lambda p: jnp.dot(p, sel_bf16, preferred_element_type=F32)
    return d(hi) + d(mid) + d(lo)


def _norm_mod(x, g, sc, sh):
    y = x * lax.rsqrt(jnp.mean(x * x, axis=-1, keepdims=True) + NORM_EPS)
    return (y * g) * (1.0 + sc) + sh


def _mod_parts(mod_ref):
    m = mod_ref[0]
    d = m.shape[-1] // N_MOD
    return [m[:, i * d:(i + 1) * d] for i in range(N_MOD)]


def _mods_kernel(s_ref, w_ref, b_ref, o_ref):
    s = s_ref[...]
    s = s * jax.nn.sigmoid(s)
    o_ref[0] = _dotb(s, w_ref[0]) + b_ref[0]


def _mods(s_rows, ada_w, ada_b):
    depth, d, nd = ada_w.shape
    tn = nd // 4
    return pl.pallas_call(
        _mods_kernel,
        grid=(depth, nd // tn),
        in_specs=[pl.BlockSpec(s_rows.shape, lambda i, j: (0, 0)),
                  pl.BlockSpec((1, d, tn), lambda i, j: (i, 0, j)),
                  pl.BlockSpec((1, 1, tn), lambda i, j: (i, 0, j))],
        out_specs=pl.BlockSpec((1, s_rows.shape[0], tn), lambda i, j: (i, 0, j)),
        out_shape=jax.ShapeDtypeStruct((depth, s_rows.shape[0], nd), F32),
        compiler_params=pltpu.CompilerParams(dimension_semantics=("parallel", "parallel"),
                                             vmem_limit_bytes=VMEM_LIMIT),
        name="mods",
    )(s_rows, ada_w, ada_b.reshape(depth, 1, nd))


def _route(logits):
    lane = lax.broadcasted_iota(jnp.int32, logits.shape, 1)
    lane_f = lane.astype(F32)
    big = jnp.float32(1e9)
    is_g = lane < N_GROUPS
    gl = jnp.where(is_g, logits, NEG_BIG)
    gm = jnp.max(gl, axis=1, keepdims=True)
    grp = jnp.min(jnp.where(is_g & (gl == gm), lane_f, big), axis=1, keepdims=True)
    p_grp = 1.0 / jnp.sum(jnp.where(is_g, jnp.exp(gl - gm), 0.0), axis=1, keepdims=True)
    lo = N_GROUPS + grp * EXPERTS_PER_GROUP
    in_grp = (lane_f >= lo) & (lane_f < lo + EXPERTS_PER_GROUP)
    el = jnp.where(in_grp, logits, NEG_BIG)
    m1 = jnp.max(el, axis=1, keepdims=True)
    i1 = jnp.min(jnp.where(in_grp & (el == m1), lane_f, big), axis=1, keepdims=True)
    rest = in_grp & (lane_f != i1)
    el2 = jnp.where(rest, logits, NEG_BIG)
    m2 = jnp.max(el2, axis=1, keepdims=True)
    i2 = jnp.min(jnp.where(rest & (el2 == m2), lane_f, big), axis=1, keepdims=True)
    e21 = jnp.exp(m2 - m1)
    s0 = 1.0 / (1.0 + e21)
    w0 = p_grp * s0
    w1 = p_grp * (e21 * s0)
    out = jnp.where(lane == 0, i1 - N_GROUPS,
                    jnp.where(lane == 1, i2 - N_GROUPS,
                              jnp.where(lane == 2, w0, jnp.where(lane == 3, w1, 0.0))))
    return out


def _residual_and_route(x, y, mods, g2_ref, wrt_ref, brt_ref, xo_ref, h2_ref, route_ref):
    _, _, gt1, sh2, sc2, _ = mods
    xn = x + gt1 * y
    xo_ref[...] = xn
    h2 = _norm_mod(xn, g2_ref[...], sc2, sh2)
    h2_ref[...] = h2
    logits = _dotb(h2, wrt_ref[...]) + brt_ref[...]
    route_ref[...] = _route(logits)


def _gmlp_kernel(x_ref, mod_ref, g1_ref, g2_ref, win_ref, bin_ref, lng_ref, lnb_ref, ws_ref, bs_ref,
                 wout_ref, bout_ref, wrt_ref, brt_ref, xo_ref, h2_ref, route_ref, gated_ref):
    x = x_ref[...]
    mods = _mod_parts(mod_ref)
    sh1, sc1 = mods[0], mods[1]
    h = _norm_mod(x, g1_ref[...], sc1, sh1)
    z = _dotb(h, win_ref[...]) + bin_ref[...]
    z = 0.5 * z * (1.0 + lax.erf(z * SQRT_HALF))
    width = z.shape[1] // 2
    u = z[:, :width]
    v = z[:, width:]
    mu = jnp.mean(v, axis=-1, keepdims=True)
    vc = v - mu
    var = jnp.mean(vc * vc, axis=-1, keepdims=True)
    vn = ((vc * lax.rsqrt(var + LN_EPS)) * lng_ref[...] + lnb_ref[...]).astype(BF16)
    n_chunks = x.shape[0] // CHUNK
    for g in range(width // GMLP_GROUP_CH):
        cs = slice(g * GMLP_GROUP_CH, (g + 1) * GMLP_GROUP_CH)
        rhs = jnp.concatenate([vn[c * CHUNK:(c + 1) * CHUNK, cs] for c in range(n_chunks)], axis=1)
        s = jnp.dot(ws_ref[g], rhs, preferred_element_type=F32) + bs_ref[g]
        for c in range(n_chunks):
            rsl = slice(c * CHUNK, (c + 1) * CHUNK)
            gated_ref[rsl, cs] = (u[rsl, cs] * s[:, c * GMLP_GROUP_CH:(c + 1) * GMLP_GROUP_CH]).astype(BF16)
    y = jnp.dot(gated_ref[...], wout_ref[...], preferred_element_type=F32) + bout_ref[...]
    _residual_and_route(x, y, mods, g2_ref, wrt_ref, brt_ref, xo_ref, h2_ref, route_ref)


def _tile_specs(t_tok, d, seg_of):
    x_spec = pl.BlockSpec((TM, d), lambda t: (t, 0))
    return x_spec


def _full_spec(shape):
    nd = len(shape)
    return pl.BlockSpec(shape, lambda *_: (0,) * nd)


def _layer_spec(shape, j):
    nd = len(shape)
    return pl.BlockSpec((None,) + tuple(shape[1:]), lambda *_: (j,) + (0,) * (nd - 1))


def _mixer_out_shapes(t_tok, d):
    return (jax.ShapeDtypeStruct((t_tok, d), F32), jax.ShapeDtypeStruct((t_tok, d), F32),
            jax.ShapeDtypeStruct((t_tok, LANES), F32))


def _mixer_out_specs(d):
    return (pl.BlockSpec((TM, d), lambda t: (t, 0)), pl.BlockSpec((TM, d), lambda t: (t, 0)),
            pl.BlockSpec((TM, LANES), lambda t: (t, 0)))


def _gmlp_layer(x, mods3, mod_spec, i, j, p):
    t_tok, d = x.shape
    width = p["ga_w_out"].shape[1]
    row = lambda a: a.reshape(a.shape[0], 1, a.shape[1])
    args = (x, mods3, row(p["norm1_g"]), row(p["norm2_g"]), p["ga_w_in"], row(p["ga_b_in"]), row(p["ga_ln_g"]),
            row(p["ga_ln_b"]), p["ga_w_s"], p["ga_b_s"], p["ga_w_out"], row(p["ga_b_out"]), p["wrt"], p["brt"])
    in_specs = [pl.BlockSpec((TM, d), lambda t: (t, 0)), mod_spec,
                _layer_spec(args[2].shape, i), _layer_spec(args[3].shape, i),
                _layer_spec(args[4].shape, j), _layer_spec(args[5].shape, j), _layer_spec(args[6].shape, j),
                _layer_spec(args[7].shape, j), _layer_spec(args[8].shape, j), _layer_spec(args[9].shape, j),
                _layer_spec(args[10].shape, j), _layer_spec(args[11].shape, j),
                _layer_spec(args[12].shape, i), _layer_spec(args[13].shape, i)]
    return pl.pallas_call(
        _gmlp_kernel,
        grid=(t_tok // TM,),
        in_specs=in_specs,
        out_specs=_mixer_out_specs(d),
        out_shape=_mixer_out_shapes(t_tok, d),
        scratch_shapes=[pltpu.VMEM((TM, width), BF16)],
        compiler_params=pltpu.CompilerParams(dimension_semantics=("parallel",), vmem_limit_bytes=VMEM_LIMIT),
        name="gmlp_layer",
    )(*args)


def _normmod_kernel(x_ref, mod_ref, g1_ref, h_ref):
    mods = _mod_parts(mod_ref)
    h_ref[...] = _norm_mod(x_ref[...], g1_ref[...], mods[1], mods[0])


def _normmod(x, mods3, mod_spec, i, p):
    t_tok, d = x.shape
    g1 = p["norm1_g"].reshape(-1, 1, d)
    return pl.pallas_call(
        _normmod_kernel,
        grid=(t_tok // TM,),
        in_specs=[pl.BlockSpec((TM, d), lambda t: (t, 0)), mod_spec, _layer_spec(g1.shape, i)],
        out_specs=pl.BlockSpec((TM, d), lambda t: (t, 0)),
        out_shape=jax.ShapeDtypeStruct((t_tok, d), F32),
        compiler_params=pltpu.CompilerParams(dimension_semantics=("parallel",), vmem_limit_bytes=VMEM_LIMIT),
        name="normmod",
    )(x, mods3, g1)


def _seg_sum(q, e_ref):
    return _dot_x_sel(q, e_ref[...])


def _seg_expand(s, et_ref):
    return _dot_x_sel(s, et_ref[...])


def _rwkv_prep_kernel(h_ref, hs_ref, mu_ref, wr_ref, wk_ref, wv_ref, w1_ref, w2_ref, w0_ref, a1_ref, a2_ref, a0_ref,
                      g1_ref, g2_ref, kk_ref, ka_ref, e_ref, et_ref,
                      r_o, v_o, kkn_o, gate_o, lw_o, km_o, bb_o):
    h = h_ref[...]
    xx = hs_ref[...] - h
    mu = mu_ref[...]
    xr, xw, xk, xv, xa, xg = [h + xx * mu[q:q + 1] for q in range(6)]
    r = _dotb(xr, wr_ref[...])
    k = _dotb(xk, wk_ref[...])
    v = _dotb(xv, wv_ref[...])
    r_o[...] = r
    v_o[...] = v
    kkr = k * kk_ref[...]
    rs = lax.rsqrt(_seg_sum(kkr * kkr, e_ref) + 1e-12)
    kkn = kkr * _seg_expand(rs, et_ref)
    kkn_o[...] = kkn
    gate_o[...] = _dotb(jax.nn.sigmoid(_dotb(xg, g1_ref[...])), g2_ref[...])
    tw = jnp.tanh(_dotb(xw, w1_ref[...]))
    ta = _dotb(xa, a1_ref[...])
    lane = lax.broadcasted_iota(jnp.int32, tw.shape, 1)
    half = tw.shape[1] // 2
    w0 = w0_ref[...]
    a0 = a0_ref[...]
    ka = ka_ref[...]
    for z in range(2):
        zm = (lane >= z * half) & (lane < (z + 1) * half)
        lwz = _dotb(jnp.where(zm, tw, 0.0), w2_ref[...])
        t = -(w0[z:z + 1] + lwz)
        sp = jnp.maximum(t, 0.0) + jnp.log1p(jnp.exp(-jnp.abs(t)))
        lw_o[z] = -jnp.exp(-sp - 0.5)
        la = _dotb(jnp.where(zm, ta, 0.0), a2_ref[...])
        a = jax.nn.sigmoid(a0[z:z + 1] + la)
        km_o[z] = k * (1.0 + (a - 1.0) * ka)
        bb_o[z] = kkn * a


def _rwkv_prep(h, hs, j, p):
    t_tok, d = h.shape
    row = lambda a: a.reshape(a.shape[0], 1, a.shape[1])
    args = (h, hs, p["rw_mu8"], p["rw_wr"], p["rw_wk"], p["rw_wv"], p["rw_w1c"], p["rw_w2c"], p["rw_w0"],
            p["rw_a1c"], p["rw_a2c"], p["rw_a0"], p["rw_g1p"], p["rw_g2p"], row(p["rw_k_k"]), row(p["rw_k_a"]),
            p["head_sel"], p["head_sel_t"])
    tok = pl.BlockSpec((TM, d), lambda t: (t, 0))
    in_specs = [tok, tok] + [_layer_spec(a.shape, j) for a in args[2:16]] + [_full_spec(args[16].shape),
                                                                             _full_spec(args[17].shape)]
    dir_spec = pl.BlockSpec((2, TM, d), lambda t: (0, t, 0))
    tok_shape = jax.ShapeDtypeStruct((t_tok, d), F32)
    dir_shape = jax.ShapeDtypeStruct((2, t_tok, d), F32)
    return pl.pallas_call(
        _rwkv_prep_kernel,
        grid=(t_tok // TM,),
        in_specs=in_specs,
        out_specs=(tok, tok, tok, tok, dir_spec, dir_spec, dir_spec),
        out_shape=(tok_shape, tok_shape, tok_shape, tok_shape, dir_shape, dir_shape, dir_shape),
        compiler_params=pltpu.CompilerParams(dimension_semantics=("parallel",), vmem_limit_bytes=VMEM_LIMIT),
        name="rwkv_prep",
    )(*args)


def _scan_prepare(r, v, kk, lw, km, bb, rev, out):
    c = SCAN_CHUNK
    nb = SCAN_BLOCK
    a = -kk
    row = lax.broadcasted_iota(jnp.int32, (nb, nb), 0)
    col = lax.broadcasted_iota(jnp.int32, (nb, nb), 1)
    shift_c = int(math.log2(c))
    same = (row >> shift_c) == (col >> shift_c)
    if rev:
        incl = same & (col >= row)
        strict = same & (col > row)
    else:
        incl = same & (col <= row)
        strict = same & (col < row)
    incl_b = jnp.where(incl, 1.0, 0.0).astype(BF16)
    eye = jnp.where(row == col, 1.0, 0.0)
    head_of_lane = col >> int(math.log2(RWKV_HEAD))

    lw_hi, lw_lo = _split2(lw)
    cum = (jnp.dot(incl_b, lw_hi, preferred_element_type=F32)
           + jnp.dot(incl_b, lw_lo, preferred_element_type=F32))
    yield
    last = [ci * c if rev else ci * c + c - 1 for ci in range(nb // c)]
    tot_rows = [cum[t:t + 1] for t in last]
    tot = jnp.concatenate([jnp.broadcast_to(t, (c, t.shape[1])) for t in tot_rows], axis=0)
    yield
    at = a * jnp.exp(cum - lw)
    rt = r * jnp.exp(cum)
    e_inv = jnp.exp(-cum)
    bt = bb * e_inv
    kt = km * e_inv
    e_rem = jnp.exp(tot - cum)
    bh = (bb * e_rem).astype(BF16)
    kh = (km * e_rem).astype(BF16)
    et = jnp.exp(jnp.concatenate(tot_rows, axis=0))
    bk = jnp.concatenate([bt, kt], axis=0).astype(BF16)
    yield
    n_heads = SCAN_LANES // RWKV_HEAD
    heads = range(n_heads)
    hms = [head_of_lane == hh for hh in heads]
    at_h = [jnp.where(hm, at, 0.0) for hm in hms]
    rt_h = [jnp.where(hm, rt, 0.0) for hm in hms]
    v_h = [jnp.where(hm, v, 0.0).astype(BF16) for hm in hms]
    v_roll = pltpu.roll(v, RWKV_HEAD, axis=1)
    v_s = [jnp.where(hms[(hh + 1) % n_heads], v_roll, 0.0).astype(BF16) for hh in heads]
    a_ab, a_ak, m_rb, m_rk = [], [], [], []
    for hh in heads:
        s = _dot_nt(jnp.concatenate([at_h[hh], rt_h[hh]], axis=0), bk)
        a_ab.append(jnp.where(strict, s[:nb, :nb], 0.0))
        a_ak.append(jnp.where(strict, s[:nb, nb:], 0.0).astype(BF16))
        m_rb.append(jnp.where(incl, s[nb:, :nb], 0.0).astype(BF16))
        m_rk.append(jnp.where(incl, s[nb:, nb:], 0.0).astype(BF16))
        yield
    pw = list(a_ab)
    tinv = [eye + m for m in a_ab]
    for _ in range(shift_c - 1):
        for hh in heads:
            pw[hh] = _dotb(pw[hh], pw[hh])
        yield
        for hh in heads:
            tinv[hh] = tinv[hh] + _dotb(tinv[hh], pw[hh])
        yield
    x1 = [jnp.dot(a_ak[hh], v_s[hh], preferred_element_type=F32) for hh in heads]
    yield
    tc = [_dotb(tinv[hh], at_h[hh] + x1[hh]) for hh in heads]
    yield
    mc = [_dotb(m_rb[hh], tc[hh]) for hh in heads]
    yield

    def pick(parts, shift):
        res = parts[(n_heads - 1 - shift) % n_heads]
        for lb in range(n_heads - 2, -1, -1):
            res = jnp.where(hms[lb], parts[(lb - shift) % n_heads], res)
        return res

    back = SCAN_LANES - RWKV_HEAD
    ah = pick(tc, 0)
    vh = pltpu.roll(pick(tc, 1), back, axis=1)
    rh = rt + pick(mc, 0)
    yh = (jnp.dot(jnp.concatenate(m_rk, axis=1), jnp.concatenate(v_h, axis=0), preferred_element_type=F32)
          + pltpu.roll(pick(mc, 1), back, axis=1))
    yield
    out.extend([ah.astype(BF16), rh.astype(BF16), vh, yh, bh, kh, v.astype(BF16), et])


def _scan_prepare_stages():
    return 7 + (SCAN_LANES // RWKV_HEAD) + 2 * (int(math.log2(SCAN_CHUNK)) - 1)


def _scan_chain(g, ops, y_ref, lanes, rev, out):
    c = SCAN_CHUNK
    ah, rh, vh, yh, bh, kh, vb, et = ops
    n = g.shape[0]
    shift_h = int(math.log2(RWKV_HEAD))
    blockdiag = ((lax.broadcasted_iota(jnp.int32, (n, n), 0) >> shift_h)
                 == (lax.broadcasted_iota(jnp.int32, (n, n), 1) >> shift_h))
    n_chunks = SCAN_BLOCK // c
    order = range(n_chunks - 1, -1, -1) if rev else range(n_chunks)
    for ci in order:
        rs = slice(ci * c, (ci + 1) * c)
        p = _dot_nt(jnp.concatenate([ah[rs], rh[rs]], axis=0), g)
        u = p[:c] + vh[rs]
        y_ref[rs, lanes] = p[c:] + yh[rs]
        yield
        dg = _dot_tn(jnp.concatenate([u.astype(BF16), vb[rs]], axis=0), jnp.concatenate([bh[rs], kh[rs]], axis=0))
        g = jnp.where(blockdiag, g * et[ci:ci + 1, :] + dg, 0.0)
        yield
    out.append(g)


def _scan_chain_stages():
    return 2 * (SCAN_BLOCK // SCAN_CHUNK)


def _interleave(gen_a, n_a, gen_b, n_b):
    done_b = 0
    for i in range(n_a):
        next(gen_a, None)
        want_b = ((i + 1) * n_b) // n_a
        while done_b < want_b:
            next(gen_b, None)
            done_b += 1
    for g in (gen_a, gen_b):
        for _ in g:
            pass


def _scan_kernel(r_ref, v_ref, kk_ref, lw_ref, km_ref, bb_ref, y_ref, g_scr, *ops_scr, rev, groups):
    @pl.when(pl.program_id(2) == 0)
    def _():
        g_scr[...] = jnp.zeros_like(g_scr)
        for ref in ops_scr:
            ref[...] = jnp.zeros_like(ref)

    slot = pl.program_id(2) % 2
    for q in range(groups):
        lanes = slice(q * SCAN_LANES, (q + 1) * SCAN_LANES)
        prev = [ref.at[1 - slot, :, lanes] for ref in ops_scr]
        new, g_end = [], []
        prep = _scan_prepare(r_ref[:, lanes], v_ref[:, lanes], kk_ref[:, lanes], lw_ref[:, lanes], km_ref[:, lanes],
                             bb_ref[:, lanes], rev, new)
        chain = _scan_chain(g_scr[q], prev, y_ref, lanes, rev, g_end)
        _interleave(prep, _scan_prepare_stages(), chain, _scan_chain_stages())
        g_scr[q] = g_end[0]
        for ref, val in zip(ops_scr, new):
            ref[slot, :, lanes] = val


def _wkv_scan(r, v, kkn, lw, km, bb, z, bsz, n_lat, n_ctx):
    t_tok, d = r.shape
    lat_blocks = n_lat // SCAN_BLOCK
    ctx_blocks = n_ctx // SCAN_BLOCK
    steps = ctx_blocks + lat_blocks
    ctx_base = bsz * lat_blocks
    rev = z == 1
    width = SCAN_LANES * SCAN_GROUPS

    def blk(b, s):
        if rev:
            cb = ctx_base + b * ctx_blocks + (ctx_blocks - 1 - s)
            lb = b * lat_blocks + (lat_blocks - 1 - (s - ctx_blocks))
        else:
            cb = ctx_base + b * ctx_blocks + s
            lb = b * lat_blocks + (s - ctx_blocks)
        return jnp.where(s < ctx_blocks, cb, lb)

    in_blk = lambda b, s: blk(b, jnp.minimum(s, steps - 1))
    out_blk = lambda b, s: blk(b, jnp.maximum(s - 1, 0))
    tok = pl.BlockSpec((SCAN_BLOCK, width), lambda b, g, s: (in_blk(b, s), g))
    dirs = pl.BlockSpec((None, SCAN_BLOCK, width), lambda b, g, s: (z, in_blk(b, s), g))
    n_chunks = SCAN_BLOCK // SCAN_CHUNK
    ops_scratch = [pltpu.VMEM((2, SCAN_BLOCK, width), BF16), pltpu.VMEM((2, SCAN_BLOCK, width), BF16),
                   pltpu.VMEM((2, SCAN_BLOCK, width), F32), pltpu.VMEM((2, SCAN_BLOCK, width), F32),
                   pltpu.VMEM((2, SCAN_BLOCK, width), BF16), pltpu.VMEM((2, SCAN_BLOCK, width), BF16),
                   pltpu.VMEM((2, SCAN_BLOCK, width), BF16), pltpu.VMEM((2, n_chunks, width), F32)]
    return pl.pallas_call(
        functools.partial(_scan_kernel, rev=rev, groups=SCAN_GROUPS),
        grid=(bsz, d // width, steps + 1),
        in_specs=[tok, tok, tok, dirs, dirs, dirs],
        out_specs=pl.BlockSpec((SCAN_BLOCK, width), lambda b, g, s: (out_blk(b, s), g)),
        out_shape=jax.ShapeDtypeStruct((t_tok, d), F32),
        scratch_shapes=[pltpu.VMEM((SCAN_GROUPS, SCAN_LANES, SCAN_LANES), F32)] + ops_scratch,
        compiler_params=pltpu.CompilerParams(dimension_semantics=("parallel", "parallel", "arbitrary"),
                                             vmem_limit_bytes=VMEM_LIMIT),
        name="wkv_scan_rev" if rev else "wkv_scan_fwd",
    )(r, v, kkn, lw, km, bb)


def _rwkv_out_kernel(x_ref, mod_ref, y0_ref, y1_ref, r_ref, v_ref, km_ref, gate_ref, lng_ref, lnb_ref, rk_ref,
                     wo_ref, e_ref, et_ref, g2_ref, wrt_ref, brt_ref, xo_ref, h2_ref, route_ref):
    x = x_ref[...]
    mods = _mod_parts(mod_ref)
    y = y0_ref[...] + y1_ref[...]
    inv_n = 1.0 / RWKV_HEAD
    mu = _seg_expand(_seg_sum(y, e_ref) * inv_n, et_ref)
    yc = y - mu
    var = _seg_sum(yc * yc, e_ref) * inv_n
    yn = (yc * _seg_expand(lax.rsqrt(var + GN_EPS), et_ref)) * lng_ref[...] + lnb_ref[...]
    v = v_ref[...]
    bonus = _seg_sum((r_ref[...] * rk_ref[...]) * (km_ref[0] + km_ref[1]), e_ref)
    yn = yn + _seg_expand(bonus, et_ref) * v
    out = _dotb(yn * gate_ref[...], wo_ref[...])
    _residual_and_route(x, out, mods, g2_ref, wrt_ref, brt_ref, xo_ref, h2_ref, route_ref)


def _rwkv_out(x, mods3, mod_spec, y0, y1, r, v, km, gate, i, j, p):
    t_tok, d = x.shape
    row = lambda a: a.reshape(a.shape[0], 1, a.shape[1])
    tok = pl.BlockSpec((TM, d), lambda t: (t, 0))
    dir_spec = pl.BlockSpec((2, TM, d), lambda t: (0, t, 0))
    args = (x, mods3, y0, y1, r, v, km, gate, row(p["rw_ln_g"]), row(p["rw_ln_b"]), row(p["rw_r_k"].reshape(-1, d)),
            p["rw_wo"], p["head_sel"], p["head_sel_t"], row(p["norm2_g"]), p["wrt"], p["brt"])
    in_specs = [tok, mod_spec, tok, tok, tok, tok, dir_spec, tok,
                _layer_spec(args[8].shape, j), _layer_spec(args[9].shape, j), _layer_spec(args[10].shape, j),
                _layer_spec(args[11].shape, j), _full_spec(args[12].shape), _full_spec(args[13].shape),
                _layer_spec(args[14].shape, i), _layer_spec(args[15].shape, i), _layer_spec(args[16].shape, i)]
    return pl.pallas_call(
        _rwkv_out_kernel,
        grid=(t_tok // TM,),
        in_specs=in_specs,
        out_specs=_mixer_out_specs(d),
        out_shape=_mixer_out_shapes(t_tok, d),
        compiler_params=pltpu.CompilerParams(dimension_semantics=("parallel",), vmem_limit_bytes=VMEM_LIMIT),
        name="rwkv_out",
    )(*args)


def _grid_shift(h, bsz, n_lat, n_ctx):
    d = h.shape[-1]
    q = d // 4
    g = h[:bsz * n_lat].reshape(bsz, n_lat // GRID_W, GRID_W, d)
    left = jnp.pad(g[:, :, :-1, :q], ((0, 0), (0, 0), (1, 0), (0, 0)))
    right = jnp.pad(g[:, :, 1:, q:2 * q], ((0, 0), (0, 0), (0, 1), (0, 0)))
    up = jnp.pad(g[:, :-1, :, 2 * q:3 * q], ((0, 0), (1, 0), (0, 0), (0, 0)))
    down = jnp.pad(g[:, 1:, :, 3 * q:], ((0, 0), (0, 1), (0, 0), (0, 0)))
    lat = jnp.concatenate([left, right, up, down], axis=-1).reshape(bsz * n_lat, d)
    c = h[bsz * n_lat:].reshape(bsz, n_ctx, d)
    half = d // 2
    prev = jnp.pad(c[:, :-1, :half], ((0, 0), (1, 0), (0, 0)))
    nxt = jnp.pad(c[:, 1:, half:], ((0, 0), (0, 1), (0, 0)))
    ctx = jnp.concatenate([prev, nxt], axis=-1).reshape(bsz * n_ctx, d)
    return jnp.concatenate([lat, ctx], axis=0)


def _expert_kernel(be_ref, nu_ref, x_ref, wg_ref, wu_ref, wd_ref, o_ref, wg_s, wu_s, wd_s):
    b = pl.program_id(0)
    e = be_ref[b]
    prev = be_ref[jnp.maximum(b - 1, 0)]
    used = b < nu_ref[0]

    @pl.when(used & ((b == 0) | (e != prev)))
    def _():
        wg_s[...] = wg_ref[...].astype(BF16)
        wu_s[...] = wu_ref[...].astype(BF16)
        wd_s[...] = wd_ref[...].astype(BF16)

    @pl.when(used)
    def _():
        xb = x_ref[...].astype(BF16)
        g = jnp.dot(xb, wg_s[...], preferred_element_type=F32)
        u = jnp.dot(xb, wu_s[...], preferred_element_type=F32)
        hmid = (g * jax.nn.sigmoid(g)) * u
        o_ref[...] = jnp.dot(hmid.astype(BF16), wd_s[...], preferred_element_type=F32)

    @pl.when(jnp.logical_not(used))
    def _():
        o_ref[...] = jnp.zeros_like(o_ref)


def _experts(buf, block_expert, n_used, w_gate, w_up, w_down, i):
    cap, d = buf.shape
    f = w_gate.shape[-1]
    n_blocks = cap // MOE_ROWS
    grid_spec = pltpu.PrefetchScalarGridSpec(
        num_scalar_prefetch=2,
        grid=(n_blocks,),
        in_specs=[pl.BlockSpec((MOE_ROWS, d), lambda b, be, nu: (b, 0)),
                  pl.BlockSpec((None, None, d, f), lambda b, be, nu: (i, be[b], 0, 0)),
                  pl.BlockSpec((None, None, d, f), lambda b, be, nu: (i, be[b], 0, 0)),
                  pl.BlockSpec((None, None, f, d), lambda b, be, nu: (i, be[b], 0, 0))],
        out_specs=pl.BlockSpec((MOE_ROWS, d), lambda b, be, nu: (b, 0)),
        scratch_shapes=[pltpu.VMEM((d, f), BF16), pltpu.VMEM((d, f), BF16), pltpu.VMEM((f, d), BF16)],
    )
    return pl.pallas_call(
        _expert_kernel,
        grid_spec=grid_spec,
        out_shape=jax.ShapeDtypeStruct((cap, d), F32),
        compiler_params=pltpu.CompilerParams(dimension_semantics=("arbitrary",), vmem_limit_bytes=VMEM_LIMIT),
        name="experts",
    )(block_expert, n_used, buf, w_gate, w_up, w_down)


def _combine_kernel(x_ref, mod_ref, route_ref, y0_ref, y1_ref, fg_ref, xo_ref, *, final):
    gt2 = _mod_parts(mod_ref)[5]
    route = route_ref[...]
    xn = x_ref[...] + gt2 * (route[:, 2:3] * y0_ref[...] + route[:, 3:4] * y1_ref[...])
    if final:
        xn = (xn * lax.rsqrt(jnp.mean(xn * xn, axis=-1, keepdims=True) + NORM_EPS)) * fg_ref[...]
    xo_ref[...] = xn


def _combine(x, mods3, mod_spec, route, y0, y1, final_g, final):
    t_tok, d = x.shape
    tok = pl.BlockSpec((TM, d), lambda t: (t, 0))
    return pl.pallas_call(
        functools.partial(_combine_kernel, final=final),
        grid=(t_tok // TM,),
        in_specs=[tok, mod_spec, pl.BlockSpec((TM, LANES), lambda t: (t, 0)), tok, tok, _full_spec((1, d))],
        out_specs=tok,
        out_shape=jax.ShapeDtypeStruct((t_tok, d), F32),
        compiler_params=pltpu.CompilerParams(dimension_semantics=("parallel",), vmem_limit_bytes=VMEM_LIMIT),
        name="moe_combine",
    )(x, mods3, route, y0, y1, final_g.reshape(1, d))


def _moe(x, h2, route, mods3, mod_spec, i, p, final):
    t_tok, d = x.shape
    n_assign = t_tok * TOP_K
    expert_id = route[:, :TOP_K].astype(jnp.int32).reshape(-1)
    onehot = (expert_id[:, None] == jnp.arange(N_EXPERTS, dtype=jnp.int32)[None, :]).astype(jnp.int32)
    csum = jnp.cumsum(onehot, axis=0)
    rank = jnp.sum(onehot * csum, axis=1) - 1
    counts = csum[-1]
    padded = ((counts + MOE_ROWS - 1) // MOE_ROWS) * MOE_ROWS
    pad_end = jnp.cumsum(padded)
    pad_start = pad_end - padded
    dest = pad_start[expert_id] + rank
    n_blocks = (n_assign + MOE_ROWS - 1) // MOE_ROWS + N_EXPERTS
    cap = n_blocks * MOE_ROWS
    slot_token = jnp.zeros((cap,), jnp.int32).at[dest].set(jnp.arange(n_assign, dtype=jnp.int32) // TOP_K)
    block_start = jnp.arange(n_blocks, dtype=jnp.int32) * MOE_ROWS
    block_expert = jnp.minimum(jnp.sum((pad_end[None, :] <= block_start[:, None]).astype(jnp.int32), axis=1),
                               N_EXPERTS - 1)
    n_used = (pad_end[-1:] // MOE_ROWS).astype(jnp.int32)
    buf = jnp.take(h2, slot_token, axis=0)
    out = _experts(buf, block_expert, n_used, p["moe_w_gate"], p["moe_w_up"], p["moe_w_down"], i)
    dest2 = dest.reshape(t_tok, TOP_K)
    y0 = jnp.take(out, dest2[:, 0], axis=0)
    y1 = jnp.take(out, dest2[:, 1], axis=0)
    return _combine(x, mods3, mod_spec, route, y0, y1, p["final_g"], final)


def kernel(x, c, ctx, c_ctx, ada_w, ada_b, norm1_g, norm2_g, final_g, ga_w_in, ga_b_in, ga_ln_g, ga_ln_b, ga_w_s,
           ga_b_s, ga_w_out, ga_b_out, rw_mu, rw_wr, rw_wk, rw_wv, rw_wo, rw_w0, rw_w1, rw_w2, rw_a0, rw_a1, rw_a2,
           rw_g1, rw_g2, rw_k_k, rw_k_a, rw_r_k, rw_ln_g, rw_ln_b, moe_w_grp, moe_b_grp, moe_w_exp, moe_b_exp,
           moe_w_gate, moe_w_up, moe_w_down):
    bsz, n_lat, d = x.shape
    n_ctx = ctx.shape[1]
    depth = ada_w.shape[0]
    n_mixers = 2
    assert n_lat % TM == 0 and n_ctx % TM == 0 and n_lat % SCAN_BLOCK == 0 and n_ctx % SCAN_BLOCK == 0
    assert bsz + 1 <= 8 and d % SCAN_LANES == 0
    t_lat = bsz * n_lat
    lat_tiles = t_lat // TM
    tiles_per_batch = n_lat // TM
    heads = d // RWKV_HEAD

    s_rows = jnp.concatenate([c, c_ctx[None, :], jnp.zeros((8 - bsz - 1, d), F32)], axis=0)
    mods = _mods(s_rows, ada_w, ada_b)
    mods3 = mods.reshape(depth * 8, 1, N_MOD * d)

    def mod_spec_for(i):
        def imap(t):
            return (i * 8 + jnp.where(t < lat_tiles, t // tiles_per_batch, bsz), 0, 0)
        return pl.BlockSpec((1, 1, N_MOD * d), imap)

    head_sel = (jnp.arange(d)[:, None] // RWKV_HEAD == jnp.arange(LANES)[None, :]).astype(BF16)
    zpad = lambda a, axis, n: jnp.pad(a, [(0, n - a.shape[k]) if k == axis else (0, 0) for k in range(a.ndim)])
    glora = ((rw_g1.shape[-1] + LANES - 1) // LANES) * LANES
    wrt = jnp.concatenate([moe_w_grp, moe_w_exp], axis=-1)
    brt = jnp.concatenate([moe_b_grp, moe_b_exp], axis=-1)
    p = dict(
        norm1_g=norm1_g, norm2_g=norm2_g, final_g=final_g,
        ga_w_in=ga_w_in.astype(BF16), ga_b_in=ga_b_in, ga_ln_g=ga_ln_g, ga_ln_b=ga_ln_b,
        ga_w_s=ga_w_s.astype(BF16), ga_b_s=ga_b_s[..., None], ga_w_out=ga_w_out.astype(BF16), ga_b_out=ga_b_out,
        rw_mu8=zpad(rw_mu, 1, 8),
        rw_wr=rw_wr.astype(BF16), rw_wk=rw_wk.astype(BF16), rw_wv=rw_wv.astype(BF16), rw_wo=rw_wo.astype(BF16),
        rw_w1c=jnp.concatenate([rw_w1[:, 0], rw_w1[:, 1]], axis=-1).astype(BF16),
        rw_w2c=jnp.concatenate([rw_w2[:, 0], rw_w2[:, 1]], axis=-2).astype(BF16),
        rw_a1c=jnp.concatenate([rw_a1[:, 0], rw_a1[:, 1]], axis=-1).astype(BF16),
        rw_a2c=jnp.concatenate([rw_a2[:, 0], rw_a2[:, 1]], axis=-2).astype(BF16),
        rw_w0=rw_w0, rw_a0=rw_a0,
        rw_g1p=zpad(rw_g1, 2, glora).astype(BF16), rw_g2p=zpad(rw_g2, 1, glora).astype(BF16),
        rw_k_k=rw_k_k, rw_k_a=rw_k_a, rw_r_k=rw_r_k, rw_ln_g=rw_ln_g, rw_ln_b=rw_ln_b,
        head_sel=head_sel, head_sel_t=head_sel.T,
        wrt=zpad(wrt, 2, LANES).astype(BF16), brt=zpad(brt, 1, LANES).reshape(depth, 1, LANES),
        moe_w_gate=moe_w_gate, moe_w_up=moe_w_up, moe_w_down=moe_w_down,
    )
    assert heads <= LANES and 2 * rw_w1.shape[-1] == LANES and 2 * rw_a1.shape[-1] == LANES

    xs = jnp.concatenate([x.reshape(t_lat, d), ctx.reshape(bsz * n_ctx, d)], axis=0)
    for i in range(depth):
        j = i // n_mixers
        mod_spec = mod_spec_for(i)
        if i % n_mixers == 0:
            xs, h2, route = _gmlp_layer(xs, mods3, mod_spec, i, j, p)
        else:
            h = _normmod(xs, mods3, mod_spec, i, p)
            hs = _grid_shift(h, bsz, n_lat, n_ctx)
            r, v, kkn, gate, lw, km, bb = _rwkv_prep(h, hs, j, p)
            y0 = _wkv_scan(r, v, kkn, lw, km, bb, 0, bsz, n_lat, n_ctx)
            y1 = _wkv_scan(r, v, kkn, lw, km, bb, 1, bsz, n_lat, n_ctx)
            xs, h2, route = _rwkv_out(xs, mods3, mod_spec, y0, y1, r, v, km, gate, i, j, p)
        xs = _moe(xs, h2, route, mods3, mod_spec, i, p, final=(i == depth - 1))
    return xs[:t_lat].reshape(bsz, n_lat, d)
```

```python
import functools
import math

import jax
import jax.numpy as jnp
from jax import lax
from jax.experimental import pallas as pl
from jax.experimental.pallas import tpu as pltpu

F32 = jnp.float32
BF16 = jnp.bfloat16

N_MOD = 6
NORM_EPS = 1e-6
GRID_W = 64
CHUNK = 128
GMLP_GROUP_CH = 128
LN_EPS = 1e-5
RWKV_HEAD = 64
GN_EPS = 64e-5
N_GROUPS = 4
EXPERTS_PER_GROUP = 8
N_EXPERTS = N_GROUPS * EXPERTS_PER_GROUP
TOP_K = 2

LANES = 128
TM = 256
MOE_ROWS = 256
SCAN_BLOCK = 256
SCAN_CHUNK = 32
SCAN_LANES = 256
SCAN_GROUPS = 1
VMEM_LIMIT = 56 * 1024 * 1024
NEG_BIG = -3.0e38
SQRT_HALF = 0.7071067811865476


def _dotb(a, b):
    return jnp.dot(a.astype(BF16), b.astype(BF16), preferred_element_type=F32)


def _dot_nt(a, b):
    return lax.dot_general(a.astype(BF16), b.astype(BF16), (((1,), (1,)), ((), ())), preferred_element_type=F32)


def _dot_tn(a, b):
    return lax.dot_general(a.astype(BF16), b.astype(BF16), (((0,), (0,)), ((), ())), preferred_element_type=F32)


def _split2(x):
    hi = x.astype(BF16)
    lo = (x - hi.astype(F32)).astype(BF16)
    return hi, lo


def _split3(x):
    hi = x.astype(BF16)
    r1 = x - hi.astype(F32)
    mid = r1.astype(BF16)
    lo = (r1 - mid.astype(F32)).astype(BF16)
    return hi, mid, lo


def _dot_x_sel(x, sel_bf16):
    hi, mid, lo = _split3(x)
    d = lambda p: jnp.dot(p, sel_bf16, preferred_element_type=F32)
    return d(hi) + d(mid) + d(lo)


def _norm_mod(x, g, sc, sh):
    y = x * lax.rsqrt(jnp.mean(x * x, axis=-1, keepdims=True) + NORM_EPS)
    return (y * g) * (1.0 + sc) + sh


def _mod_parts(mod_ref):
    m = mod_ref[0]
    d = m.shape[-1] // N_MOD
    return [m[:, i * d:(i + 1) * d] for i in range(N_MOD)]


def _mods_kernel(s_ref, w_ref, b_ref, o_ref):
    s = s_ref[...]
    s = s * jax.nn.sigmoid(s)
    o_ref[0] = _dotb(s, w_ref[0]) + b_ref[0]


def _mods(s_rows, ada_w, ada_b):
    depth, d, nd = ada_w.shape
    tn = nd // 4
    return pl.pallas_call(
        _mods_kernel,
        grid=(depth, nd // tn),
        in_specs=[pl.BlockSpec(s_rows.shape, lambda i, j: (0, 0)),
                  pl.BlockSpec((1, d, tn), lambda i, j: (i, 0, j)),
                  pl.BlockSpec((1, 1, tn), lambda i, j: (i, 0, j))],
        out_specs=pl.BlockSpec((1, s_rows.shape[0], tn), lambda i, j: (i, 0, j)),
        out_shape=jax.ShapeDtypeStruct((depth, s_rows.shape[0], nd), F32),
        compiler_params=pltpu.CompilerParams(dimension_semantics=("parallel", "parallel"),
                                             vmem_limit_bytes=VMEM_LIMIT),
        name="mods",
    )(s_rows, ada_w, ada_b.reshape(depth, 1, nd))


def _route(logits):
    lane = lax.broadcasted_iota(jnp.int32, logits.shape, 1)
    lane_f = lane.astype(F32)
    big = jnp.float32(1e9)
    is_g = lane < N_GROUPS
    gl = jnp.where(is_g, logits, NEG_BIG)
    gm = jnp.max(gl, axis=1, keepdims=True)
    grp = jnp.min(jnp.where(is_g & (gl == gm), lane_f, big), axis=1, keepdims=True)
    p_grp = 1.0 / jnp.sum(jnp.where(is_g, jnp.exp(gl - gm), 0.0), axis=1, keepdims=True)
    lo = N_GROUPS + grp * EXPERTS_PER_GROUP
    in_grp = (lane_f >= lo) & (lane_f < lo + EXPERTS_PER_GROUP)
    el = jnp.where(in_grp, logits, NEG_BIG)
    m1 = jnp.max(el, axis=1, keepdims=True)
    i1 = jnp.min(jnp.where(in_grp & (el == m1), lane_f, big), axis=1, keepdims=True)
    rest = in_grp & (lane_f != i1)
    el2 = jnp.where(rest, logits, NEG_BIG)
    m2 = jnp.max(el2, axis=1, keepdims=True)
    i2 = jnp.min(jnp.where(rest & (el2 == m2), lane_f, big), axis=1, keepdims=True)
    e21 = jnp.exp(m2 - m1)
    s0 = 1.0 / (1.0 + e21)
    w0 = p_grp * s0
    w1 = p_grp * (e21 * s0)
    out = jnp.where(lane == 0, i1 - N_GROUPS,
                    jnp.where(lane == 1, i2 - N_GROUPS,
                              jnp.where(lane == 2, w0, jnp.where(lane == 3, w1, 0.0))))
    return out


def _residual_and_route(x, y, mods, g2_ref, wrt_ref, brt_ref, xo_ref, h2_ref, route_ref):
    _, _, gt1, sh2, sc2, _ = mods
    xn = x + gt1 * y
    xo_ref[...] = xn
    h2 = _norm_mod(xn, g2_ref[...], sc2, sh2)
    h2_ref[...] = h2
    logits = _dotb(h2, wrt_ref[...]) + brt_ref[...]
    route_ref[...] = _route(logits)


def _gmlp_kernel(x_ref, mod_ref, g1_ref, g2_ref, win_ref, bin_ref, lng_ref, lnb_ref, ws_ref, bs_ref,
                 wout_ref, bout_ref, wrt_ref, brt_ref, xo_ref, h2_ref, route_ref, gated_ref):
    x = x_ref[...]
    mods = _mod_parts(mod_ref)
    sh1, sc1 = mods[0], mods[1]
    h = _norm_mod(x, g1_ref[...], sc1, sh1)
    z = _dotb(h, win_ref[...]) + bin_ref[...]
    z = 0.5 * z * (1.0 + lax.erf(z * SQRT_HALF))
    width = z.shape[1] // 2
    u = z[:, :width]
    v = z[:, width:]
    mu = jnp.mean(v, axis=-1, keepdims=True)
    vc = v - mu
    var = jnp.mean(vc * vc, axis=-1, keepdims=True)
    vn = ((vc * lax.rsqrt(var + LN_EPS)) * lng_ref[...] + lnb_ref[...]).astype(BF16)
    n_chunks = x.shape[0] // CHUNK
    for g in range(width // GMLP_GROUP_CH):
        cs = slice(g * GMLP_GROUP_CH, (g + 1) * GMLP_GROUP_CH)
        rhs = jnp.concatenate([vn[c * CHUNK:(c + 1) * CHUNK, cs] for c in range(n_chunks)], axis=1)
        s = jnp.dot(ws_ref[g], rhs, preferred_element_type=F32) + bs_ref[g]
        for c in range(n_chunks):
            rsl = slice(c * CHUNK, (c + 1) * CHUNK)
            gated_ref[rsl, cs] = (u[rsl, cs] * s[:, c * GMLP_GROUP_CH:(c + 1) * GMLP_GROUP_CH]).astype(BF16)
    y = jnp.dot(gated_ref[...], wout_ref[...], preferred_element_type=F32) + bout_ref[...]
    _residual_and_route(x, y, mods, g2_ref, wrt_ref, brt_ref, xo_ref, h2_ref, route_ref)


def _full_spec(shape):
    nd = len(shape)
    return pl.BlockSpec(shape, lambda *_: (0,) * nd)


def _layer_spec(shape, j):
    nd = len(shape)
    return pl.BlockSpec((None,) + tuple(shape[1:]), lambda *_: (j,) + (0,) * (nd - 1))


def _mixer_out_shapes(t_tok, d):
    return (jax.ShapeDtypeStruct((t_tok, d), F32), jax.ShapeDtypeStruct((t_tok, d), F32),
            jax.ShapeDtypeStruct((t_tok, LANES), F32))


def _mixer_out_specs(d):
    return (pl.BlockSpec((TM, d), lambda t: (t, 0)), pl.BlockSpec((TM, d), lambda t: (t, 0)),
            pl.BlockSpec((TM, LANES), lambda t: (t, 0)))


def _gmlp_layer(x, mods3, mod_spec, i, j, p):
    t_tok, d = x.shape
    width = p["ga_w_out"].shape[1]
    row = lambda a: a.reshape(a.shape[0], 1, a.shape[1])
    args = (x, mods3, row(p["norm1_g"]), row(p["norm2_g"]), p["ga_w_in"], row(p["ga_b_in"]), row(p["ga_ln_g"]),
            row(p["ga_ln_b"]), p["ga_w_s"], p["ga_b_s"], p["ga_w_out"], row(p["ga_b_out"]), p["wrt"], p["brt"])
    in_specs = [pl.BlockSpec((TM, d), lambda t: (t, 0)), mod_spec,
                _layer_spec(args[2].shape, i), _layer_spec(args[3].shape, i),
                _layer_spec(args[4].shape, j), _layer_spec(args[5].shape, j), _layer_spec(args[6].shape, j),
                _layer_spec(args[7].shape, j), _layer_spec(args[8].shape, j), _layer_spec(args[9].shape, j),
                _layer_spec(args[10].shape, j), _layer_spec(args[11].shape, j),
                _layer_spec(args[12].shape, i), _layer_spec(args[13].shape, i)]
    return pl.pallas_call(
        _gmlp_kernel,
        grid=(t_tok // TM,),
        in_specs=in_specs,
        out_specs=_mixer_out_specs(d),
        out_shape=_mixer_out_shapes(t_tok, d),
        scratch_shapes=[pltpu.VMEM((TM, width), BF16)],
        compiler_params=pltpu.CompilerParams(dimension_semantics=("parallel",), vmem_limit_bytes=VMEM_LIMIT),
        name="gmlp_layer",
    )(*args)


def _seg_sum(q, e_ref):
    return _dot_x_sel(q, e_ref[...])


def _seg_expand(s, et_ref):
    return _dot_x_sel(s, et_ref[...])


def _shifted(h, h_above, h_below, is_ctx, first_row_tile, last_row_tile):
    tm, d = h.shape
    q = d // 4
    row = lax.broadcasted_iota(jnp.int32, (tm, 1), 0)
    ctx_i = is_ctx.astype(jnp.int32)
    col = row & (jnp.where(is_ctx, tm, GRID_W) - 1)
    last_col = jnp.where(is_ctx, tm, GRID_W) - 1
    prev1 = jnp.where(col == 0, 0.0, pltpu.roll(h, 1, axis=0))
    next1 = jnp.where(col == last_col, 0.0, pltpu.roll(h, tm - 1, axis=0))
    up = jnp.concatenate([h_above, h[:tm - GRID_W]], axis=0)
    up = jnp.where(row < jnp.where(first_row_tile, GRID_W, 0), 0.0, up)
    down = jnp.concatenate([h[GRID_W:], h_below], axis=0)
    down = jnp.where(row >= jnp.where(last_row_tile, tm - GRID_W, tm), 0.0, down)
    lane_q = lax.broadcasted_iota(jnp.int32, (1, d), 1) >> int(math.log2(q))
    src = lane_q * (1 - ctx_i) + (lane_q >> 1) * ctx_i
    return jnp.where(src == 0, prev1, jnp.where(src == 1, next1, jnp.where(src == 2, up, down)))


def _rwkv_prep_kernel(x_ref, xa_ref, xb_ref, mod_ref, n1_ref, mu_ref, wr_ref, wk_ref, wv_ref, w1_ref, w2_ref, w0_ref,
                      a1_ref, a2_ref, a0_ref, g1_ref, g2_ref, kk_ref, ka_ref, e_ref, et_ref,
                      r_o, v_o, kkn_o, gate_o, lw_o, km_o, bb_o, *, lat_tiles, tiles_per_batch):
    t = pl.program_id(0)
    mods = _mod_parts(mod_ref)
    norm = lambda x: _norm_mod(x, n1_ref[...], mods[1], mods[0])
    h = norm(x_ref[...])
    tb = t % tiles_per_batch
    hs = _shifted(h, norm(xa_ref[...]), norm(xb_ref[...]), t >= lat_tiles, tb == 0, tb == tiles_per_batch - 1)
    xx = hs - h
    mu = mu_ref[...]
    xr, xw, xk, xv, xa, xg = [h + xx * mu[q:q + 1] for q in range(6)]
    r = _dotb(xr, wr_ref[...])
    k = _dotb(xk, wk_ref[...])
    v = _dotb(xv, wv_ref[...])
    r_o[...] = r
    v_o[...] = v
    kkr = k * kk_ref[...]
    rs = lax.rsqrt(_seg_sum(kkr * kkr, e_ref) + 1e-12)
    kkn = kkr * _seg_expand(rs, et_ref)
    kkn_o[...] = kkn
    gate_o[...] = _dotb(jax.nn.sigmoid(_dotb(xg, g1_ref[...])), g2_ref[...])
    tw = jnp.tanh(_dotb(xw, w1_ref[...]))
    ta = _dotb(xa, a1_ref[...])
    lane = lax.broadcasted_iota(jnp.int32, tw.shape, 1)
    half = tw.shape[1] // 2
    w0 = w0_ref[...]
    a0 = a0_ref[...]
    ka = ka_ref[...]
    for z in range(2):
        zm = (lane >= z * half) & (lane < (z + 1) * half)
        lwz = _dotb(jnp.where(zm, tw, 0.0), w2_ref[...])
        t = -(w0[z:z + 1] + lwz)
        sp = jnp.maximum(t, 0.0) + jnp.log1p(jnp.exp(-jnp.abs(t)))
        lw_o[z] = -jnp.exp(-sp - 0.5)
        la = _dotb(jnp.where(zm, ta, 0.0), a2_ref[...])
        a = jax.nn.sigmoid(a0[z:z + 1] + la)
        km_o[z] = k * (1.0 + (a - 1.0) * ka)
        bb_o[z] = kkn * a


def _rwkv_prep(x, mods3, mod_spec, i, j, p, lat_tiles, tiles_per_batch):
    t_tok, d = x.shape
    row = lambda a: a.reshape(a.shape[0], 1, a.shape[1])
    args = (x, x, x, mods3, row(p["norm1_g"]), p["rw_mu8"], p["rw_wr"], p["rw_wk"], p["rw_wv"], p["rw_w1c"],
            p["rw_w2c"], p["rw_w0"], p["rw_a1c"], p["rw_a2c"], p["rw_a0"], p["rw_g1p"], p["rw_g2p"],
            row(p["rw_k_k"]), row(p["rw_k_a"]), p["head_sel"], p["head_sel_t"])
    tok = pl.BlockSpec((TM, d), lambda t: (t, 0))
    rows_per_tile = TM // GRID_W
    last_row = t_tok // GRID_W - 1
    above = pl.BlockSpec((GRID_W, d), lambda t: (jnp.maximum(t * rows_per_tile - 1, 0), 0))
    below = pl.BlockSpec((GRID_W, d), lambda t: (jnp.minimum((t + 1) * rows_per_tile, last_row), 0))
    in_specs = ([tok, above, below, mod_spec, _layer_spec(args[4].shape, i)]
                + [_layer_spec(a.shape, j) for a in args[5:19]]
                + [_full_spec(args[19].shape), _full_spec(args[20].shape)])
    dir_spec = pl.BlockSpec((2, TM, d), lambda t: (0, t, 0))
    tok_shape = jax.ShapeDtypeStruct((t_tok, d), F32)
    dir_shape = jax.ShapeDtypeStruct((2, t_tok, d), F32)
    return pl.pallas_call(
        functools.partial(_rwkv_prep_kernel, lat_tiles=lat_tiles, tiles_per_batch=tiles_per_batch),
        grid=(t_tok // TM,),
        in_specs=in_specs,
        out_specs=(tok, tok, tok, tok, dir_spec, dir_spec, dir_spec),
        out_shape=(tok_shape, tok_shape, tok_shape, tok_shape, dir_shape, dir_shape, dir_shape),
        compiler_params=pltpu.CompilerParams(dimension_semantics=("parallel",), vmem_limit_bytes=VMEM_LIMIT),
        name="rwkv_prep",
    )(*args)


def _scan_prepare(r, v, kk, lw, km, bb, rev, out):
    c = SCAN_CHUNK
    nb = SCAN_BLOCK
    a = -kk
    row = lax.broadcasted_iota(jnp.int32, (nb, nb), 0)
    col = lax.broadcasted_iota(jnp.int32, (nb, nb), 1)
    shift_c = int(math.log2(c))
    same = (row >> shift_c) == (col >> shift_c)
    if rev:
        incl = same & (col >= row)
        strict = same & (col > row)
    else:
        incl = same & (col <= row)
        strict = same & (col < row)
    incl_b = jnp.where(incl, 1.0, 0.0).astype(BF16)
    eye = jnp.where(row == col, 1.0, 0.0)
    head_of_lane = col >> int(math.log2(RWKV_HEAD))

    lw_hi, lw_lo = _split2(lw)
    cum = (jnp.dot(incl_b, lw_hi, preferred_element_type=F32)
           + jnp.dot(incl_b, lw_lo, preferred_element_type=F32))
    yield
    last = [ci * c if rev else ci * c + c - 1 for ci in range(nb // c)]
    tot_rows = [cum[t:t + 1] for t in last]
    tot = jnp.concatenate([jnp.broadcast_to(t, (c, t.shape[1])) for t in tot_rows], axis=0)
    yield
    at = a * jnp.exp(cum - lw)
    rt = r * jnp.exp(cum)
    e_inv = jnp.exp(-cum)
    bt = bb * e_inv
    kt = km * e_inv
    e_rem = jnp.exp(tot - cum)
    bh = (bb * e_rem).astype(BF16)
    kh = (km * e_rem).astype(BF16)
    et = jnp.exp(jnp.concatenate(tot_rows, axis=0))
    bk = jnp.concatenate([bt, kt], axis=0).astype(BF16)
    yield
    n_heads = SCAN_LANES // RWKV_HEAD
    heads = range(n_heads)
    hms = [head_of_lane == hh for hh in heads]
    at_h = [jnp.where(hm, at, 0.0) for hm in hms]
    rt_h = [jnp.where(hm, rt, 0.0) for hm in hms]
    v_h = [jnp.where(hm, v, 0.0).astype(BF16) for hm in hms]
    v_roll = pltpu.roll(v, RWKV_HEAD, axis=1)
    v_s = [jnp.where(hms[(hh + 1) % n_heads], v_roll, 0.0).astype(BF16) for hh in heads]
    a_ab, a_ak, m_rb, m_rk = [], [], [], []
    for hh in heads:
        s = _dot_nt(jnp.concatenate([at_h[hh], rt_h[hh]], axis=0), bk)
        a_ab.append(jnp.where(strict, s[:nb, :nb], 0.0))
        a_ak.append(jnp.where(strict, s[:nb, nb:], 0.0).astype(BF16))
        m_rb.append(jnp.where(incl, s[nb:, :nb], 0.0).astype(BF16))
        m_rk.append(jnp.where(incl, s[nb:, nb:], 0.0).astype(BF16))
        yield
    pw = list(a_ab)
    tinv = [eye + m for m in a_ab]
    for _ in range(shift_c - 1):
        for hh in heads:
            pw[hh] = _dotb(pw[hh], pw[hh])
        yield
        for hh in heads:
            tinv[hh] = tinv[hh] + _dotb(tinv[hh], pw[hh])
        yield
    x1 = [jnp.dot(a_ak[hh], v_s[hh], preferred_element_type=F32) for hh in heads]
    yield
    tc = [_dotb(tinv[hh], at_h[hh] + x1[hh]) for hh in heads]
    yield
    mc = [_dotb(m_rb[hh], tc[hh]) for hh in heads]
    yield

    def pick(parts, shift):
        res = parts[(n_heads - 1 - shift) % n_heads]
        for lb in range(n_heads - 2, -1, -1):
            res = jnp.where(hms[lb], parts[(lb - shift) % n_heads], res)
        return res

    back = SCAN_LANES - RWKV_HEAD
    ah = pick(tc, 0)
    vh = pltpu.roll(pick(tc, 1), back, axis=1)
    rh = rt + pick(mc, 0)
    yh = (jnp.dot(jnp.concatenate(m_rk, axis=1), jnp.concatenate(v_h, axis=0), preferred_element_type=F32)
          + pltpu.roll(pick(mc, 1), back, axis=1))
    yield
    out.extend([ah.astype(BF16), rh.astype(BF16), vh, yh, bh, kh, v.astype(BF16), et])


def _scan_prepare_stages():
    return 7 + (SCAN_LANES // RWKV_HEAD) + 2 * (int(math.log2(SCAN_CHUNK)) - 1)


def _scan_chain(g, ops, y_ref, lanes, rev, out):
    c = SCAN_CHUNK
    ah, rh, vh, yh, bh, kh, vb, et = ops
    n = g.shape[0]
    shift_h = int(math.log2(RWKV_HEAD))
    blockdiag = ((lax.broadcasted_iota(jnp.int32, (n, n), 0) >> shift_h)
                 == (lax.broadcasted_iota(jnp.int32, (n, n), 1) >> shift_h))
    n_chunks = SCAN_BLOCK // c
    order = range(n_chunks - 1, -1, -1) if rev else range(n_chunks)
    for ci in order:
        rs = slice(ci * c, (ci + 1) * c)
        p = _dot_nt(jnp.concatenate([ah[rs], rh[rs]], axis=0), g)
        u = p[:c] + vh[rs]
        y_ref[rs, lanes] = p[c:] + yh[rs]
        yield
        dg = _dot_tn(jnp.concatenate([u.astype(BF16), vb[rs]], axis=0), jnp.concatenate([bh[rs], kh[rs]], axis=0))
        g = jnp.where(blockdiag, g * et[ci:ci + 1, :] + dg, 0.0)
        yield
    out.append(g)


def _scan_chain_stages():
    return 2 * (SCAN_BLOCK // SCAN_CHUNK)


def _interleave(gen_a, n_a, gen_b, n_b):
    done_b = 0
    for i in range(n_a):
        next(gen_a, None)
        want_b = ((i + 1) * n_b) // n_a
        while done_b < want_b:
            next(gen_b, None)
            done_b += 1
    for g in (gen_a, gen_b):
        for _ in g:
            pass


def _scan_kernel(r_ref, v_ref, kk_ref, lw_ref, km_ref, bb_ref, y_ref, g_scr, *ops_scr, rev, groups):
    @pl.when(pl.program_id(2) == 0)
    def _():
        g_scr[...] = jnp.zeros_like(g_scr)
        for ref in ops_scr:
            ref[...] = jnp.zeros_like(ref)

    slot = pl.program_id(2) % 2
    for q in range(groups):
        lanes = slice(q * SCAN_LANES, (q + 1) * SCAN_LANES)
        prev = [ref.at[1 - slot, :, lanes] for ref in ops_scr]
        new, g_end = [], []
        prep = _scan_prepare(r_ref[:, lanes], v_ref[:, lanes], kk_ref[:, lanes], lw_ref[:, lanes], km_ref[:, lanes],
                             bb_ref[:, lanes], rev, new)
        chain = _scan_chain(g_scr[q], prev, y_ref, lanes, rev, g_end)
        _interleave(prep, _scan_prepare_stages(), chain, _scan_chain_stages())
        g_scr[q] = g_end[0]
        for ref, val in zip(ops_scr, new):
            ref[slot, :, lanes] = val


def _wkv_scan(r, v, kkn, lw, km, bb, z, bsz, n_lat, n_ctx):
    t_tok, d = r.shape
    lat_blocks = n_lat // SCAN_BLOCK
    ctx_blocks = n_ctx // SCAN_BLOCK
    steps = ctx_blocks + lat_blocks
    ctx_base = bsz * lat_blocks
    rev = z == 1
    width = SCAN_LANES * SCAN_GROUPS

    def blk(b, s):
        if rev:
            cb = ctx_base + b * ctx_blocks + (ctx_blocks - 1 - s)
            lb = b * lat_blocks + (lat_blocks - 1 - (s - ctx_blocks))
        else:
            cb = ctx_base + b * ctx_blocks + s
            lb = b * lat_blocks + (s - ctx_blocks)
        return jnp.where(s < ctx_blocks, cb, lb)

    in_blk = lambda b, s: blk(b, jnp.minimum(s, steps - 1))
    out_blk = lambda b, s: blk(b, jnp.maximum(s - 1, 0))
    tok = pl.BlockSpec((SCAN_BLOCK, width), lambda b, g, s: (in_blk(b, s), g))
    dirs = pl.BlockSpec((None, SCAN_BLOCK, width), lambda b, g, s: (z, in_blk(b, s), g))
    n_chunks = SCAN_BLOCK // SCAN_CHUNK
    ops_scratch = [pltpu.VMEM((2, SCAN_BLOCK, width), BF16), pltpu.VMEM((2, SCAN_BLOCK, width), BF16),
                   pltpu.VMEM((2, SCAN_BLOCK, width), F32), pltpu.VMEM((2, SCAN_BLOCK, width), F32),
                   pltpu.VMEM((2, SCAN_BLOCK, width), BF16), pltpu.VMEM((2, SCAN_BLOCK, width), BF16),
                   pltpu.VMEM((2, SCAN_BLOCK, width), BF16), pltpu.VMEM((2, n_chunks, width), F32)]
    return pl.pallas_call(
        functools.partial(_scan_kernel, rev=rev, groups=SCAN_GROUPS),
        grid=(bsz, d // width, steps + 1),
        in_specs=[tok, tok, tok, dirs, dirs, dirs],
        out_specs=pl.BlockSpec((SCAN_BLOCK, width), lambda b, g, s: (out_blk(b, s), g)),
        out_shape=jax.ShapeDtypeStruct((t_tok, d), F32),
        scratch_shapes=[pltpu.VMEM((SCAN_GROUPS, SCAN_LANES, SCAN_LANES), F32)] + ops_scratch,
        compiler_params=pltpu.CompilerParams(dimension_semantics=("parallel", "parallel", "arbitrary"),
                                             vmem_limit_bytes=VMEM_LIMIT),
        name="wkv_scan_rev" if rev else "wkv_scan_fwd",
    )(r, v, kkn, lw, km, bb)


def _rwkv_out_kernel(x_ref, mod_ref, y0_ref, y1_ref, r_ref, v_ref, km_ref, gate_ref, lng_ref, lnb_ref, rk_ref,
                     wo_ref, e_ref, et_ref, g2_ref, wrt_ref, brt_ref, xo_ref, h2_ref, route_ref):
    x = x_ref[...]
    mods = _mod_parts(mod_ref)
    y = y0_ref[...] + y1_ref[...]
    inv_n = 1.0 / RWKV_HEAD
    mu = _seg_expand(_seg_sum(y, e_ref) * inv_n, et_ref)
    yc = y - mu
    var = _seg_sum(yc * yc, e_ref) * inv_n
    yn = (yc * _seg_expand(lax.rsqrt(var + GN_EPS), et_ref)) * lng_ref[...] + lnb_ref[...]
    v = v_ref[...]
    bonus = _seg_sum((r_ref[...] * rk_ref[...]) * (km_ref[0] + km_ref[1]), e_ref)
    yn = yn + _seg_expand(bonus, et_ref) * v
    out = _dotb(yn * gate_ref[...], wo_ref[...])
    _residual_and_route(x, out, mods, g2_ref, wrt_ref, brt_ref, xo_ref, h2_ref, route_ref)


def _rwkv_out(x, mods3, mod_spec, y0, y1, r, v, km, gate, i, j, p):
    t_tok, d = x.shape
    row = lambda a: a.reshape(a.shape[0], 1, a.shape[1])
    tok = pl.BlockSpec((TM, d), lambda t: (t, 0))
    dir_spec = pl.BlockSpec((2, TM, d), lambda t: (0, t, 0))
    args = (x, mods3, y0, y1, r, v, km, gate, row(p["rw_ln_g"]), row(p["rw_ln_b"]), row(p["rw_r_k"].reshape(-1, d)),
            p["rw_wo"], p["head_sel"], p["head_sel_t"], row(p["norm2_g"]), p["wrt"], p["brt"])
    in_specs = [tok, mod_spec, tok, tok, tok, tok, dir_spec, tok,
                _layer_spec(args[8].shape, j), _layer_spec(args[9].shape, j), _layer_spec(args[10].shape, j),
                _layer_spec(args[11].shape, j), _full_spec(args[12].shape), _full_spec(args[13].shape),
                _layer_spec(args[14].shape, i), _layer_spec(args[15].shape, i), _layer_spec(args[16].shape, i)]
    return pl.pallas_call(
        _rwkv_out_kernel,
        grid=(t_tok // TM,),
        in_specs=in_specs,
        out_specs=_mixer_out_specs(d),
        out_shape=_mixer_out_shapes(t_tok, d),
        compiler_params=pltpu.CompilerParams(dimension_semantics=("parallel",), vmem_limit_bytes=VMEM_LIMIT),
        name="rwkv_out",
    )(*args)


def _expert_kernel(be_ref, nu_ref, x_ref, wg_ref, wu_ref, wd_ref, o_ref, wg_s, wu_s, wd_s):
    b = pl.program_id(0)
    e = be_ref[b]
    prev = be_ref[jnp.maximum(b - 1, 0)]
    used = b < nu_ref[0]

    @pl.when(used & ((b == 0) | (e != prev)))
    def _():
        wg_s[...] = wg_ref[...].astype(BF16)
        wu_s[...] = wu_ref[...].astype(BF16)
        wd_s[...] = wd_ref[...].astype(BF16)

    @pl.when(used)
    def _():
        xb = x_ref[...].astype(BF16)
        g = jnp.dot(xb, wg_s[...], preferred_element_type=F32)
        u = jnp.dot(xb, wu_s[...], preferred_element_type=F32)
        hmid = (g * jax.nn.sigmoid(g)) * u
        o_ref[...] = jnp.dot(hmid.astype(BF16), wd_s[...], preferred_element_type=F32)

    @pl.when(jnp.logical_not(used))
    def _():
        o_ref[...] = jnp.zeros_like(o_ref)


def _experts(buf, block_expert, n_used, w_gate, w_up, w_down, i):
    cap, d = buf.shape
    f = w_gate.shape[-1]
    n_blocks = cap // MOE_ROWS
    grid_spec = pltpu.PrefetchScalarGridSpec(
        num_scalar_prefetch=2,
        grid=(n_blocks,),
        in_specs=[pl.BlockSpec((MOE_ROWS, d), lambda b, be, nu: (b, 0)),
                  pl.BlockSpec((None, None, d, f), lambda b, be, nu: (i, be[b], 0, 0)),
                  pl.BlockSpec((None, None, d, f), lambda b, be, nu: (i, be[b], 0, 0)),
                  pl.BlockSpec((None, None, f, d), lambda b, be, nu: (i, be[b], 0, 0))],
        out_specs=pl.BlockSpec((MOE_ROWS, d), lambda b, be, nu: (b, 0)),
        scratch_shapes=[pltpu.VMEM((d, f), BF16), pltpu.VMEM((d, f), BF16), pltpu.VMEM((f, d), BF16)],
    )
    return pl.pallas_call(
        _expert_kernel,
        grid_spec=grid_spec,
        out_shape=jax.ShapeDtypeStruct((cap, d), F32),
        compiler_params=pltpu.CompilerParams(dimension_semantics=("arbitrary",), vmem_limit_bytes=VMEM_LIMIT),
        name="experts",
    )(block_expert, n_used, buf, w_gate, w_up, w_down)


def _combine_kernel(x_ref, mod_ref, route_ref, y0_ref, y1_ref, fg_ref, xo_ref, *, final):
    gt2 = _mod_parts(mod_ref)[5]
    route = route_ref[...]
    xn = x_ref[...] + gt2 * (route[:, 2:3] * y0_ref[...] + route[:, 3:4] * y1_ref[...])
    if final:
        xn = (xn * lax.rsqrt(jnp.mean(xn * xn, axis=-1, keepdims=True) + NORM_EPS)) * fg_ref[...]
    xo_ref[...] = xn


def _combine(x, mods3, mod_spec, route, y0, y1, final_g, final):
    t_tok, d = x.shape
    tok = pl.BlockSpec((TM, d), lambda t: (t, 0))
    return pl.pallas_call(
        functools.partial(_combine_kernel, final=final),
        grid=(t_tok // TM,),
        in_specs=[tok, mod_spec, pl.BlockSpec((TM, LANES), lambda t: (t, 0)), tok, tok, _full_spec((1, d))],
        out_specs=tok,
        out_shape=jax.ShapeDtypeStruct((t_tok, d), F32),
        compiler_params=pltpu.CompilerParams(dimension_semantics=("parallel",), vmem_limit_bytes=VMEM_LIMIT),
        name="moe_combine",
    )(x, mods3, route, y0, y1, final_g.reshape(1, d))


def _rank_kernel(route_ref, rank_ref, counts_ref, base_scr):
    @pl.when(pl.program_id(0) == 0)
    def _():
        base_scr[...] = jnp.zeros_like(base_scr)

    route = route_ref[...]
    tm = route.shape[0]
    lane = lax.broadcasted_iota(jnp.int32, route.shape, 1)
    lane_f = lane.astype(F32)
    earlier = (lax.broadcasted_iota(jnp.int32, (tm, tm), 1)
               < lax.broadcasted_iota(jnp.int32, (tm, tm), 0))
    earlier_b = jnp.where(earlier, 1.0, 0.0).astype(BF16)
    base = base_scr[...]
    ranks = []
    for k in range(TOP_K):
        onehot = jnp.where(lane_f == route[:, k:k + 1], 1.0, 0.0)
        before = jnp.dot(earlier_b, onehot.astype(BF16), preferred_element_type=F32)
        ranks.append(jnp.sum(onehot * (base + before), axis=1, keepdims=True))
        base = base + jnp.sum(onehot, axis=0, keepdims=True)
    base_scr[...] = base
    counts_ref[...] = base
    rank_ref[...] = jnp.where(lane == 0, ranks[0], jnp.where(lane == 1, ranks[1], 0.0))


def _ranks(route):
    t_tok = route.shape[0]
    return pl.pallas_call(
        _rank_kernel,
        grid=(t_tok // TM,),
        in_specs=[pl.BlockSpec((TM, LANES), lambda t: (t, 0))],
        out_specs=(pl.BlockSpec((TM, LANES), lambda t: (t, 0)), pl.BlockSpec((1, LANES), lambda t: (0, 0))),
        out_shape=(jax.ShapeDtypeStruct((t_tok, LANES), F32), jax.ShapeDtypeStruct((1, LANES), F32)),
        scratch_shapes=[pltpu.VMEM((1, LANES), F32)],
        compiler_params=pltpu.CompilerParams(dimension_semantics=("arbitrary",), vmem_limit_bytes=VMEM_LIMIT),
        name="moe_ranks",
    )(route)


def _moe(x, h2, route, mods3, mod_spec, i, p, final):
    t_tok, d = x.shape
    n_assign = t_tok * TOP_K
    expert_id = route[:, :TOP_K].astype(jnp.int32).reshape(-1)
    rank_f, counts_f = _ranks(route)
    rank = rank_f[:, :TOP_K].astype(jnp.int32).reshape(-1)
    counts = counts_f[0, :N_EXPERTS].astype(jnp.int32)
    padded = ((counts + MOE_ROWS - 1) // MOE_ROWS) * MOE_ROWS
    pad_end = jnp.cumsum(padded)
    pad_start = pad_end - padded
    expert_onehot = expert_id[:, None] == jnp.arange(N_EXPERTS, dtype=jnp.int32)[None, :]
    dest = jnp.sum(jnp.where(expert_onehot, pad_start[None, :], 0), axis=1) + rank
    n_blocks = (n_assign + MOE_ROWS - 1) // MOE_ROWS + N_EXPERTS
    cap = n_blocks * MOE_ROWS
    slot_token = (jnp.arange(cap, dtype=jnp.int32) % t_tok).at[dest].set(
        jnp.arange(n_assign, dtype=jnp.int32) // TOP_K)
    block_start = jnp.arange(n_blocks, dtype=jnp.int32) * MOE_ROWS
    block_expert = jnp.minimum(jnp.sum((pad_end[None, :] <= block_start[:, None]).astype(jnp.int32), axis=1),
                               N_EXPERTS - 1)
    n_used = (pad_end[-1:] // MOE_ROWS).astype(jnp.int32)
    buf = jnp.take(h2, slot_token, axis=0)
    out = _experts(buf, block_expert, n_used, p["moe_w_gate"], p["moe_w_up"], p["moe_w_down"], i)
    dest2 = dest.reshape(t_tok, TOP_K)
    y0 = jnp.take(out, dest2[:, 0], axis=0)
    y1 = jnp.take(out, dest2[:, 1], axis=0)
    return _combine(x, mods3, mod_spec, route, y0, y1, p["final_g"], final)


def kernel(x, c, ctx, c_ctx, ada_w, ada_b, norm1_g, norm2_g, final_g, ga_w_in, ga_b_in, ga_ln_g, ga_ln_b, ga_w_s,
           ga_b_s, ga_w_out, ga_b_out, rw_mu, rw_wr, rw_wk, rw_wv, rw_wo, rw_w0, rw_w1, rw_w2, rw_a0, rw_a1, rw_a2,
           rw_g1, rw_g2, rw_k_k, rw_k_a, rw_r_k, rw_ln_g, rw_ln_b, moe_w_grp, moe_b_grp, moe_w_exp, moe_b_exp,
           moe_w_gate, moe_w_up, moe_w_down):
    bsz, n_lat, d = x.shape
    n_ctx = ctx.shape[1]
    depth = ada_w.shape[0]
    n_mixers = 2
    assert n_lat % TM == 0 and n_ctx == TM and TM % GRID_W == 0
    assert n_lat % SCAN_BLOCK == 0 and n_ctx % SCAN_BLOCK == 0
    assert bsz + 1 <= 8 and d % SCAN_LANES == 0
    t_lat = bsz * n_lat
    lat_tiles = t_lat // TM
    tiles_per_batch = n_lat // TM
    heads = d // RWKV_HEAD

    s_rows = jnp.concatenate([c, c_ctx[None, :], jnp.zeros((8 - bsz - 1, d), F32)], axis=0)
    mods = _mods(s_rows, ada_w, ada_b)
    mods3 = mods.reshape(depth * 8, 1, N_MOD * d)

    def mod_spec_for(i):
        def imap(t):
            return (i * 8 + jnp.where(t < lat_tiles, t // tiles_per_batch, bsz), 0, 0)
        return pl.BlockSpec((1, 1, N_MOD * d), imap)

    head_sel = (jnp.arange(d)[:, None] // RWKV_HEAD == jnp.arange(LANES)[None, :]).astype(BF16)
    zpad = lambda a, axis, n: jnp.pad(a, [(0, n - a.shape[k]) if k == axis else (0, 0) for k in range(a.ndim)])
    glora = ((rw_g1.shape[-1] + LANES - 1) // LANES) * LANES
    wrt = jnp.concatenate([moe_w_grp, moe_w_exp], axis=-1)
    brt = jnp.concatenate([moe_b_grp, moe_b_exp], axis=-1)
    p = dict(
        norm1_g=norm1_g, norm2_g=norm2_g, final_g=final_g,
        ga_w_in=ga_w_in.astype(BF16), ga_b_in=ga_b_in, ga_ln_g=ga_ln_g, ga_ln_b=ga_ln_b,
        ga_w_s=ga_w_s.astype(BF16), ga_b_s=ga_b_s[..., None], ga_w_out=ga_w_out.astype(BF16), ga_b_out=ga_b_out,
        rw_mu8=zpad(rw_mu, 1, 8),
        rw_wr=rw_wr.astype(BF16), rw_wk=rw_wk.astype(BF16), rw_wv=rw_wv.astype(BF16), rw_wo=rw_wo.astype(BF16),
        rw_w1c=jnp.concatenate([rw_w1[:, 0], rw_w1[:, 1]], axis=-1).astype(BF16),
        rw_w2c=jnp.concatenate([rw_w2[:, 0], rw_w2[:, 1]], axis=-2).astype(BF16),
        rw_a1c=jnp.concatenate([rw_a1[:, 0], rw_a1[:, 1]], axis=-1).astype(BF16),
        rw_a2c=jnp.concatenate([rw_a2[:, 0], rw_a2[:, 1]], axis=-2).astype(BF16),
        rw_w0=rw_w0, rw_a0=rw_a0,
        rw_g1p=zpad(rw_g1, 2, glora).astype(BF16), rw_g2p=zpad(rw_g2, 1, glora).astype(BF16),
        rw_k_k=rw_k_k, rw_k_a=rw_k_a, rw_r_k=rw_r_k, rw_ln_g=rw_ln_g, rw_ln_b=rw_ln_b,
        head_sel=head_sel, head_sel_t=head_sel.T,
        wrt=zpad(wrt, 2, LANES).astype(BF16), brt=zpad(brt, 1, LANES).reshape(depth, 1, LANES),
        moe_w_gate=moe_w_gate, moe_w_up=moe_w_up, moe_w_down=moe_w_down,
    )
    assert heads <= LANES and 2 * rw_w1.shape[-1] == LANES and 2 * rw_a1.shape[-1] == LANES

    xs = jnp.concatenate([x.reshape(t_lat, d), ctx.reshape(bsz * n_ctx, d)], axis=0)
    for i in range(depth):
        j = i // n_mixers
        mod_spec = mod_spec_for(i)
        if i % n_mixers == 0:
            xs, h2, route = _gmlp_layer(xs, mods3, mod_spec, i, j, p)
        else:
            r, v, kkn, gate, lw, km, bb = _rwkv_prep(xs, mods3, mod_spec, i, j, p, lat_tiles, tiles_per_batch)
            y0 = _wkv_scan(r, v, kkn, lw, km, bb, 0, bsz, n_lat, n_ctx)
            y1 = _wkv_scan(r, v, kkn, lw, km, bb, 1, bsz, n_lat, n_ctx)
            xs, h2, route = _rwkv_out(xs, mods3, mod_spec, y0, y1, r, v, km, gate, i, j, p)
        xs = _moe(xs, h2, route, mods3, mod_spec, i, p, final=(i == depth - 1))
    return xs[:t_lat].reshape(bsz, n_lat, d)
```

```python
import functools
import math

import jax
import jax.numpy as jnp
from jax import lax
from jax.experimental import pallas as pl
from jax.experimental.pallas import tpu as pltpu

F32 = jnp.float32
BF16 = jnp.bfloat16

N_MOD = 6
NORM_EPS = 1e-6
GRID_W = 64
CHUNK = 128
GMLP_GROUP_CH = 128
LN_EPS = 1e-5
RWKV_HEAD = 64
GN_EPS = 64e-5
N_GROUPS = 4
EXPERTS_PER_GROUP = 8
N_EXPERTS = N_GROUPS * EXPERTS_PER_GROUP
TOP_K = 2

LANES = 128
TM = 256
MOE_ROWS = 256
SCAN_BLOCK = 256
SCAN_CHUNK = 32
SCAN_LANES = 256
SCAN_GROUPS = 1
VMEM_LIMIT = 56 * 1024 * 1024
NEG_BIG = -3.0e38
SQRT_HALF = 0.7071067811865476


def _dotb(a, b):
    return jnp.dot(a.astype(BF16), b.astype(BF16), preferred_element_type=F32)


def _dot_nt(a, b):
    return lax.dot_general(a.astype(BF16), b.astype(BF16), (((1,), (1,)), ((), ())), preferred_element_type=F32)


def _dot_tn(a, b):
    return lax.dot_general(a.astype(BF16), b.astype(BF16), (((0,), (0,)), ((), ())), preferred_element_type=F32)


def _split2(x):
    hi = x.astype(BF16)
    lo = (x - hi.astype(F32)).astype(BF16)
    return hi, lo


def _split3(x):
    hi = x.astype(BF16)
    r1 = x - hi.astype(F32)
    mid = r1.astype(BF16)
    lo = (r1 - mid.astype(F32)).astype(BF16)
    return hi, mid, lo


def _dot_x_sel(x, sel_bf16):
    hi, mid, lo = _split3(x)
    d = lambda p: jnp.dot(p, sel_bf16, preferred_element_type=F32)
    return d(hi) + d(mid) + d(lo)


def _norm_mod(x, g, sc, sh):
    y = x * lax.rsqrt(jnp.mean(x * x, axis=-1, keepdims=True) + NORM_EPS)
    return (y * g) * (1.0 + sc) + sh


def _mod_parts(mod_ref):
    m = mod_ref[0]
    d = m.shape[-1] // N_MOD
    return [m[:, i * d:(i + 1) * d] for i in range(N_MOD)]


def _mods_kernel(s_ref, w_ref, b_ref, o_ref):
    s = s_ref[...]
    s = s * jax.nn.sigmoid(s)
    o_ref[0] = _dotb(s, w_ref[0]) + b_ref[0]


def _mods(s_rows, ada_w, ada_b):
    depth, d, nd = ada_w.shape
    tn = nd // 4
    return pl.pallas_call(
        _mods_kernel,
        grid=(depth, nd // tn),
        in_specs=[pl.BlockSpec(s_rows.shape, lambda i, j: (0, 0)),
                  pl.BlockSpec((1, d, tn), lambda i, j: (i, 0, j)),
                  pl.BlockSpec((1, 1, tn), lambda i, j: (i, 0, j))],
        out_specs=pl.BlockSpec((1, s_rows.shape[0], tn), lambda i, j: (i, 0, j)),
        out_shape=jax.ShapeDtypeStruct((depth, s_rows.shape[0], nd), F32),
        compiler_params=pltpu.CompilerParams(dimension_semantics=("parallel", "parallel"),
                                             vmem_limit_bytes=VMEM_LIMIT),
        name="mods",
    )(s_rows, ada_w, ada_b.reshape(depth, 1, nd))


def _route(logits):
    lane = lax.broadcasted_iota(jnp.int32, logits.shape, 1)
    lane_f = lane.astype(F32)
    big = jnp.float32(1e9)
    is_g = lane < N_GROUPS
    gl = jnp.where(is_g, logits, NEG_BIG)
    gm = jnp.max(gl, axis=1, keepdims=True)
    grp = jnp.min(jnp.where(is_g & (gl == gm), lane_f, big), axis=1, keepdims=True)
    p_grp = 1.0 / jnp.sum(jnp.where(is_g, jnp.exp(gl - gm), 0.0), axis=1, keepdims=True)
    lo = N_GROUPS + grp * EXPERTS_PER_GROUP
    in_grp = (lane_f >= lo) & (lane_f < lo + EXPERTS_PER_GROUP)
    el = jnp.where(in_grp, logits, NEG_BIG)
    m1 = jnp.max(el, axis=1, keepdims=True)
    i1 = jnp.min(jnp.where(in_grp & (el == m1), lane_f, big), axis=1, keepdims=True)
    rest = in_grp & (lane_f != i1)
    el2 = jnp.where(rest, logits, NEG_BIG)
    m2 = jnp.max(el2, axis=1, keepdims=True)
    i2 = jnp.min(jnp.where(rest & (el2 == m2), lane_f, big), axis=1, keepdims=True)
    e21 = jnp.exp(m2 - m1)
    s0 = 1.0 / (1.0 + e21)
    w0 = p_grp * s0
    w1 = p_grp * (e21 * s0)
    out = jnp.where(lane == 0, i1 - N_GROUPS,
                    jnp.where(lane == 1, i2 - N_GROUPS,
                              jnp.where(lane == 2, w0, jnp.where(lane == 3, w1, 0.0))))
    return out


def _residual_and_route(x, y, mods, g2_ref, wrt_ref, brt_ref, xo_ref, h2_ref, route_ref):
    _, _, gt1, sh2, sc2, _ = mods
    xn = x + gt1 * y
    xo_ref[...] = xn
    h2 = _norm_mod(xn, g2_ref[...], sc2, sh2)
    h2_ref[...] = h2
    logits = _dotb(h2, wrt_ref[...]) + brt_ref[...]
    route_ref[...] = _route(logits)


def _gmlp_kernel(x_ref, mod_ref, g1_ref, g2_ref, win_ref, bin_ref, lng_ref, lnb_ref, ws_ref, bs_ref,
                 wout_ref, bout_ref, wrt_ref, brt_ref, xo_ref, h2_ref, route_ref, gated_ref):
    x = x_ref[...]
    mods = _mod_parts(mod_ref)
    sh1, sc1 = mods[0], mods[1]
    h = _norm_mod(x, g1_ref[...], sc1, sh1)
    z = _dotb(h, win_ref[...]) + bin_ref[...]
    z = 0.5 * z * (1.0 + lax.erf(z * SQRT_HALF))
    width = z.shape[1] // 2
    u = z[:, :width]
    v = z[:, width:]
    mu = jnp.mean(v, axis=-1, keepdims=True)
    vc = v - mu
    var = jnp.mean(vc * vc, axis=-1, keepdims=True)
    vn = ((vc * lax.rsqrt(var + LN_EPS)) * lng_ref[...] + lnb_ref[...]).astype(BF16)
    n_chunks = x.shape[0] // CHUNK
    for g in range(width // GMLP_GROUP_CH):
        cs = slice(g * GMLP_GROUP_CH, (g + 1) * GMLP_GROUP_CH)
        rhs = jnp.concatenate([vn[c * CHUNK:(c + 1) * CHUNK, cs] for c in range(n_chunks)], axis=1)
        s = jnp.dot(ws_ref[g], rhs, preferred_element_type=F32) + bs_ref[g]
        for c in range(n_chunks):
            rsl = slice(c * CHUNK, (c + 1) * CHUNK)
            gated_ref[rsl, cs] = (u[rsl, cs] * s[:, c * GMLP_GROUP_CH:(c + 1) * GMLP_GROUP_CH]).astype(BF16)
    y = jnp.dot(gated_ref[...], wout_ref[...], preferred_element_type=F32) + bout_ref[...]
    _residual_and_route(x, y, mods, g2_ref, wrt_ref, brt_ref, xo_ref, h2_ref, route_ref)


def _full_spec(shape):
    nd = len(shape)
    return pl.BlockSpec(shape, lambda *_: (0,) * nd)


def _layer_spec(shape, j):
    nd = len(shape)
    return pl.BlockSpec((None,) + tuple(shape[1:]), lambda *_: (j,) + (0,) * (nd - 1))


def _mixer_out_shapes(t_tok, d):
    return (jax.ShapeDtypeStruct((t_tok, d), F32), jax.ShapeDtypeStruct((t_tok, d), F32),
            jax.ShapeDtypeStruct((t_tok, LANES), F32))


def _mixer_out_specs(d):
    return (pl.BlockSpec((TM, d), lambda t: (t, 0)), pl.BlockSpec((TM, d), lambda t: (t, 0)),
            pl.BlockSpec((TM, LANES), lambda t: (t, 0)))


def _gmlp_layer(x, mods3, mod_spec, i, j, p):
    t_tok, d = x.shape
    width = p["ga_w_out"].shape[1]
    row = lambda a: a.reshape(a.shape[0], 1, a.shape[1])
    args = (x, mods3, row(p["norm1_g"]), row(p["norm2_g"]), p["ga_w_in"], row(p["ga_b_in"]), row(p["ga_ln_g"]),
            row(p["ga_ln_b"]), p["ga_w_s"], p["ga_b_s"], p["ga_w_out"], row(p["ga_b_out"]), p["wrt"], p["brt"])
    in_specs = [pl.BlockSpec((TM, d), lambda t: (t, 0)), mod_spec,
                _layer_spec(args[2].shape, i), _layer_spec(args[3].shape, i),
                _layer_spec(args[4].shape, j), _layer_spec(args[5].shape, j), _layer_spec(args[6].shape, j),
                _layer_spec(args[7].shape, j), _layer_spec(args[8].shape, j), _layer_spec(args[9].shape, j),
                _layer_spec(args[10].shape, j), _layer_spec(args[11].shape, j),
                _layer_spec(args[12].shape, i), _layer_spec(args[13].shape, i)]
    return pl.pallas_call(
        _gmlp_kernel,
        grid=(t_tok // TM,),
        in_specs=in_specs,
        out_specs=_mixer_out_specs(d),
        out_shape=_mixer_out_shapes(t_tok, d),
        scratch_shapes=[pltpu.VMEM((TM, width), BF16)],
        compiler_params=pltpu.CompilerParams(dimension_semantics=("parallel",), vmem_limit_bytes=VMEM_LIMIT),
        name="gmlp_layer",
    )(*args)


def _seg_sum(q, e_ref):
    return _dot_x_sel(q, e_ref[...])


def _seg_expand(s, et_ref):
    return _dot_x_sel(s, et_ref[...])


def _shifted(h, h_above, h_below, is_ctx, first_row_tile, last_row_tile):
    tm, d = h.shape
    q = d // 4
    row = lax.broadcasted_iota(jnp.int32, (tm, 1), 0)
    ctx_i = is_ctx.astype(jnp.int32)
    col = row & (jnp.where(is_ctx, tm, GRID_W) - 1)
    last_col = jnp.where(is_ctx, tm, GRID_W) - 1
    prev1 = jnp.where(col == 0, 0.0, pltpu.roll(h, 1, axis=0))
    next1 = jnp.where(col == last_col, 0.0, pltpu.roll(h, tm - 1, axis=0))
    up = jnp.concatenate([h_above, h[:tm - GRID_W]], axis=0)
    up = jnp.where(row < jnp.where(first_row_tile, GRID_W, 0), 0.0, up)
    down = jnp.concatenate([h[GRID_W:], h_below], axis=0)
    down = jnp.where(row >= jnp.where(last_row_tile, tm - GRID_W, tm), 0.0, down)
    lane_q = lax.broadcasted_iota(jnp.int32, (1, d), 1) >> int(math.log2(q))
    src = lane_q * (1 - ctx_i) + (lane_q >> 1) * ctx_i
    return jnp.where(src == 0, prev1, jnp.where(src == 1, next1, jnp.where(src == 2, up, down)))


def _rwkv_prep_kernel(x_ref, xa_ref, xb_ref, mod_ref, n1_ref, mu_ref, wr_ref, wk_ref, wv_ref, w1_ref, w2_ref, w0_ref,
                      a1_ref, a2_ref, a0_ref, g1_ref, g2_ref, kk_ref, ka_ref, e_ref, et_ref,
                      r_o, v_o, kkn_o, gate_o, lw_o, km_o, bb_o, *, lat_tiles, tiles_per_batch):
    t = pl.program_id(0)
    mods = _mod_parts(mod_ref)
    norm = lambda x: _norm_mod(x, n1_ref[...], mods[1], mods[0])
    h = norm(x_ref[...])
    tb = t % tiles_per_batch
    hs = _shifted(h, norm(xa_ref[...]), norm(xb_ref[...]), t >= lat_tiles, tb == 0, tb == tiles_per_batch - 1)
    xx = hs - h
    mu = mu_ref[...]
    xr, xw, xk, xv, xa, xg = [h + xx * mu[q:q + 1] for q in range(6)]
    r = _dotb(xr, wr_ref[...])
    k = _dotb(xk, wk_ref[...])
    v = _dotb(xv, wv_ref[...])
    r_o[...] = r
    v_o[...] = v
    kkr = k * kk_ref[...]
    rs = lax.rsqrt(_seg_sum(kkr * kkr, e_ref) + 1e-12)
    kkn = kkr * _seg_expand(rs, et_ref)
    kkn_o[...] = kkn
    gate_o[...] = _dotb(jax.nn.sigmoid(_dotb(xg, g1_ref[...])), g2_ref[...])
    tw = jnp.tanh(_dotb(xw, w1_ref[...]))
    ta = _dotb(xa, a1_ref[...])
    lane = lax.broadcasted_iota(jnp.int32, tw.shape, 1)
    half = tw.shape[1] // 2
    w0 = w0_ref[...]
    a0 = a0_ref[...]
    ka = ka_ref[...]
    for z in range(2):
        zm = (lane >= z * half) & (lane < (z + 1) * half)
        lwz = _dotb(jnp.where(zm, tw, 0.0), w2_ref[...])
        t = -(w0[z:z + 1] + lwz)
        sp = jnp.maximum(t, 0.0) + jnp.log1p(jnp.exp(-jnp.abs(t)))
        lw_o[z] = -jnp.exp(-sp - 0.5)
        la = _dotb(jnp.where(zm, ta, 0.0), a2_ref[...])
        a = jax.nn.sigmoid(a0[z:z + 1] + la)
        km_o[z] = k * (1.0 + (a - 1.0) * ka)
        bb_o[z] = kkn * a


def _rwkv_prep(x, mods3, mod_spec, i, j, p, lat_tiles, tiles_per_batch):
    t_tok, d = x.shape
    row = lambda a: a.reshape(a.shape[0], 1, a.shape[1])
    args = (x, x, x, mods3, row(p["norm1_g"]), p["rw_mu8"], p["rw_wr"], p["rw_wk"], p["rw_wv"], p["rw_w1c"],
            p["rw_w2c"], p["rw_w0"], p["rw_a1c"], p["rw_a2c"], p["rw_a0"], p["rw_g1p"], p["rw_g2p"],
            row(p["rw_k_k"]), row(p["rw_k_a"]), p["head_sel"], p["head_sel_t"])
    tok = pl.BlockSpec((TM, d), lambda t: (t, 0))
    rows_per_tile = TM // GRID_W
    last_row = t_tok // GRID_W - 1
    above = pl.BlockSpec((GRID_W, d), lambda t: (jnp.maximum(t * rows_per_tile - 1, 0), 0))
    below = pl.BlockSpec((GRID_W, d), lambda t: (jnp.minimum((t + 1) * rows_per_tile, last_row), 0))
    in_specs = ([tok, above, below, mod_spec, _layer_spec(args[4].shape, i)]
                + [_layer_spec(a.shape, j) for a in args[5:19]]
                + [_full_spec(args[19].shape), _full_spec(args[20].shape)])
    dir_spec = pl.BlockSpec((2, TM, d), lambda t: (0, t, 0))
    tok_shape = jax.ShapeDtypeStruct((t_tok, d), F32)
    dir_shape = jax.ShapeDtypeStruct((2, t_tok, d), F32)
    return pl.pallas_call(
        functools.partial(_rwkv_prep_kernel, lat_tiles=lat_tiles, tiles_per_batch=tiles_per_batch),
        grid=(t_tok // TM,),
        in_specs=in_specs,
        out_specs=(tok, tok, tok, tok, dir_spec, dir_spec, dir_spec),
        out_shape=(tok_shape, tok_shape, tok_shape, tok_shape, dir_shape, dir_shape, dir_shape),
        compiler_params=pltpu.CompilerParams(dimension_semantics=("parallel",), vmem_limit_bytes=VMEM_LIMIT),
        name="rwkv_prep",
    )(*args)


def _scan_prepare(r, v, kk, lw, km, bb, rev, out):
    c = SCAN_CHUNK
    nb = SCAN_BLOCK
    a = -kk
    row = lax.broadcasted_iota(jnp.int32, (nb, nb), 0)
    col = lax.broadcasted_iota(jnp.int32, (nb, nb), 1)
    shift_c = int(math.log2(c))
    same = (row >> shift_c) == (col >> shift_c)
    if rev:
        incl = same & (col >= row)
        strict = same & (col > row)
    else:
        incl = same & (col <= row)
        strict = same & (col < row)
    incl_b = jnp.where(incl, 1.0, 0.0).astype(BF16)
    eye = jnp.where(row == col, 1.0, 0.0)
    head_of_lane = col >> int(math.log2(RWKV_HEAD))

    lw_hi, lw_lo = _split2(lw)
    cum = (jnp.dot(incl_b, lw_hi, preferred_element_type=F32)
           + jnp.dot(incl_b, lw_lo, preferred_element_type=F32))
    yield
    last = [ci * c if rev else ci * c + c - 1 for ci in range(nb // c)]
    tot_rows = [cum[t:t + 1] for t in last]
    tot = jnp.concatenate([jnp.broadcast_to(t, (c, t.shape[1])) for t in tot_rows], axis=0)
    yield
    at = a * jnp.exp(cum - lw)
    rt = r * jnp.exp(cum)
    e_inv = jnp.exp(-cum)
    bt = bb * e_inv
    kt = km * e_inv
    e_rem = jnp.exp(tot - cum)
    bh = (bb * e_rem).astype(BF16)
    kh = (km * e_rem).astype(BF16)
    et = jnp.exp(jnp.concatenate(tot_rows, axis=0))
    bk = jnp.concatenate([bt, kt], axis=0).astype(BF16)
    yield
    n_heads = SCAN_LANES // RWKV_HEAD
    heads = range(n_heads)
    hms = [head_of_lane == hh for hh in heads]
    at_h = [jnp.where(hm, at, 0.0) for hm in hms]
    rt_h = [jnp.where(hm, rt, 0.0) for hm in hms]
    v_h = [jnp.where(hm, v, 0.0).astype(BF16) for hm in hms]
    v_roll = pltpu.roll(v, RWKV_HEAD, axis=1)
    v_s = [jnp.where(hms[(hh + 1) % n_heads], v_roll, 0.0).astype(BF16) for hh in heads]
    a_ab, a_ak, m_rb, m_rk = [], [], [], []
    for hh in heads:
        s = _dot_nt(jnp.concatenate([at_h[hh], rt_h[hh]], axis=0), bk)
        a_ab.append(jnp.where(strict, s[:nb, :nb], 0.0))
        a_ak.append(jnp.where(strict, s[:nb, nb:], 0.0).astype(BF16))
        m_rb.append(jnp.where(incl, s[nb:, :nb], 0.0).astype(BF16))
        m_rk.append(jnp.where(incl, s[nb:, nb:], 0.0).astype(BF16))
        yield
    pw = list(a_ab)
    tinv = [eye + m for m in a_ab]
    for _ in range(shift_c - 1):
        for hh in heads:
            pw[hh] = _dotb(pw[hh], pw[hh])
        yield
        for hh in heads:
            tinv[hh] = tinv[hh] + _dotb(tinv[hh], pw[hh])
        yield
    x1 = [jnp.dot(a_ak[hh], v_s[hh], preferred_element_type=F32) for hh in heads]
    yield
    tc = [_dotb(tinv[hh], at_h[hh] + x1[hh]) for hh in heads]
    yield
    mc = [_dotb(m_rb[hh], tc[hh]) for hh in heads]
    yield

    def pick(parts, shift):
        res = parts[(n_heads - 1 - shift) % n_heads]
        for lb in range(n_heads - 2, -1, -1):
            res = jnp.where(hms[lb], parts[(lb - shift) % n_heads], res)
        return res

    back = SCAN_LANES - RWKV_HEAD
    ah = pick(tc, 0)
    vh = pltpu.roll(pick(tc, 1), back, axis=1)
    rh = rt + pick(mc, 0)
    yh = (jnp.dot(jnp.concatenate(m_rk, axis=1), jnp.concatenate(v_h, axis=0), preferred_element_type=F32)
          + pltpu.roll(pick(mc, 1), back, axis=1))
    yield
    out.extend([ah.astype(BF16), rh.astype(BF16), vh, yh, bh, kh, v.astype(BF16), et])


def _scan_prepare_stages():
    return 7 + (SCAN_LANES // RWKV_HEAD) + 2 * (int(math.log2(SCAN_CHUNK)) - 1)


def _scan_chain(g, ops, y_ref, lanes, rev, out):
    c = SCAN_CHUNK
    ah, rh, vh, yh, bh, kh, vb, et = ops
    n = g.shape[0]
    shift_h = int(math.log2(RWKV_HEAD))
    blockdiag = ((lax.broadcasted_iota(jnp.int32, (n, n), 0) >> shift_h)
                 == (lax.broadcasted_iota(jnp.int32, (n, n), 1) >> shift_h))
    n_chunks = SCAN_BLOCK // c
    order = range(n_chunks - 1, -1, -1) if rev else range(n_chunks)
    for ci in order:
        rs = slice(ci * c, (ci + 1) * c)
        p = _dot_nt(jnp.concatenate([ah[rs], rh[rs]], axis=0), g)
        u = p[:c] + vh[rs]
        y_ref[rs, lanes] = p[c:] + yh[rs]
        yield
        dg = _dot_tn(jnp.concatenate([u.astype(BF16), vb[rs]], axis=0), jnp.concatenate([bh[rs], kh[rs]], axis=0))
        g = jnp.where(blockdiag, g * et[ci:ci + 1, :] + dg, 0.0)
        yield
    out.append(g)


def _scan_chain_stages():
    return 2 * (SCAN_BLOCK // SCAN_CHUNK)


def _interleave(gen_a, n_a, gen_b, n_b):
    done_b = 0
    for i in range(n_a):
        next(gen_a, None)
        want_b = ((i + 1) * n_b) // n_a
        while done_b < want_b:
            next(gen_b, None)
            done_b += 1
    for g in (gen_a, gen_b):
        for _ in g:
            pass


def _scan_kernel(r_ref, v_ref, kk_ref, lw_ref, km_ref, bb_ref, y_ref, g_scr, *ops_scr, rev, groups):
    @pl.when(pl.program_id(2) == 0)
    def _():
        g_scr[...] = jnp.zeros_like(g_scr)
        for ref in ops_scr:
            ref[...] = jnp.zeros_like(ref)

    slot = pl.program_id(2) % 2
    for q in range(groups):
        lanes = slice(q * SCAN_LANES, (q + 1) * SCAN_LANES)
        prev = [ref.at[1 - slot, :, lanes] for ref in ops_scr]
        new, g_end = [], []
        prep = _scan_prepare(r_ref[:, lanes], v_ref[:, lanes], kk_ref[:, lanes], lw_ref[:, lanes], km_ref[:, lanes],
                             bb_ref[:, lanes], rev, new)
        chain = _scan_chain(g_scr[q], prev, y_ref, lanes, rev, g_end)
        _interleave(prep, _scan_prepare_stages(), chain, _scan_chain_stages())
        g_scr[q] = g_end[0]
        for ref, val in zip(ops_scr, new):
            ref[slot, :, lanes] = val


def _wkv_scan(r, v, kkn, lw, km, bb, z, bsz, n_lat, n_ctx):
    t_tok, d = r.shape
    lat_blocks = n_lat // SCAN_BLOCK
    ctx_blocks = n_ctx // SCAN_BLOCK
    steps = ctx_blocks + lat_blocks
    ctx_base = bsz * lat_blocks
    rev = z == 1
    width = SCAN_LANES * SCAN_GROUPS

    def blk(b, s):
        if rev:
            cb = ctx_base + b * ctx_blocks + (ctx_blocks - 1 - s)
            lb = b * lat_blocks + (lat_blocks - 1 - (s - ctx_blocks))
        else:
            cb = ctx_base + b * ctx_blocks + s
            lb = b * lat_blocks + (s - ctx_blocks)
        return jnp.where(s < ctx_blocks, cb, lb)

    in_blk = lambda b, s: blk(b, jnp.minimum(s, steps - 1))
    out_blk = lambda b, s: blk(b, jnp.maximum(s - 1, 0))
    tok = pl.BlockSpec((SCAN_BLOCK, width), lambda b, g, s: (in_blk(b, s), g))
    dirs = pl.BlockSpec((None, SCAN_BLOCK, width), lambda b, g, s: (z, in_blk(b, s), g))
    n_chunks = SCAN_BLOCK // SCAN_CHUNK
    ops_scratch = [pltpu.VMEM((2, SCAN_BLOCK, width), BF16), pltpu.VMEM((2, SCAN_BLOCK, width), BF16),
                   pltpu.VMEM((2, SCAN_BLOCK, width), F32), pltpu.VMEM((2, SCAN_BLOCK, width), F32),
                   pltpu.VMEM((2, SCAN_BLOCK, width), BF16), pltpu.VMEM((2, SCAN_BLOCK, width), BF16),
                   pltpu.VMEM((2, SCAN_BLOCK, width), BF16), pltpu.VMEM((2, n_chunks, width), F32)]
    return pl.pallas_call(
        functools.partial(_scan_kernel, rev=rev, groups=SCAN_GROUPS),
        grid=(bsz, d // width, steps + 1),
        in_specs=[tok, tok, tok, dirs, dirs, dirs],
        out_specs=pl.BlockSpec((SCAN_BLOCK, width), lambda b, g, s: (out_blk(b, s), g)),
        out_shape=jax.ShapeDtypeStruct((t_tok, d), F32),
        scratch_shapes=[pltpu.VMEM((SCAN_GROUPS, SCAN_LANES, SCAN_LANES), F32)] + ops_scratch,
        compiler_params=pltpu.CompilerParams(dimension_semantics=("parallel", "parallel", "arbitrary"),
                                             vmem_limit_bytes=VMEM_LIMIT),
        name="wkv_scan_rev" if rev else "wkv_scan_fwd",
    )(r, v, kkn, lw, km, bb)


def _rwkv_out_kernel(x_ref, mod_ref, y0_ref, y1_ref, r_ref, v_ref, km_ref, gate_ref, lng_ref, lnb_ref, rk_ref,
                     wo_ref, e_ref, et_ref, g2_ref, wrt_ref, brt_ref, xo_ref, h2_ref, route_ref):
    x = x_ref[...]
    mods = _mod_parts(mod_ref)
    y = y0_ref[...] + y1_ref[...]
    inv_n = 1.0 / RWKV_HEAD
    mu = _seg_expand(_seg_sum(y, e_ref) * inv_n, et_ref)
    yc = y - mu
    var = _seg_sum(yc * yc, e_ref) * inv_n
    yn = (yc * _seg_expand(lax.rsqrt(var + GN_EPS), et_ref)) * lng_ref[...] + lnb_ref[...]
    v = v_ref[...]
    bonus = _seg_sum((r_ref[...] * rk_ref[...]) * (km_ref[0] + km_ref[1]), e_ref)
    yn = yn + _seg_expand(bonus, et_ref) * v
    out = _dotb(yn * gate_ref[...], wo_ref[...])
    _residual_and_route(x, out, mods, g2_ref, wrt_ref, brt_ref, xo_ref, h2_ref, route_ref)


def _rwkv_out(x, mods3, mod_spec, y0, y1, r, v, km, gate, i, j, p):
    t_tok, d = x.shape
    row = lambda a: a.reshape(a.shape[0], 1, a.shape[1])
    tok = pl.BlockSpec((TM, d), lambda t: (t, 0))
    dir_spec = pl.BlockSpec((2, TM, d), lambda t: (0, t, 0))
    args = (x, mods3, y0, y1, r, v, km, gate, row(p["rw_ln_g"]), row(p["rw_ln_b"]), row(p["rw_r_k"].reshape(-1, d)),
            p["rw_wo"], p["head_sel"], p["head_sel_t"], row(p["norm2_g"]), p["wrt"], p["brt"])
    in_specs = [tok, mod_spec, tok, tok, tok, tok, dir_spec, tok,
                _layer_spec(args[8].shape, j), _layer_spec(args[9].shape, j), _layer_spec(args[10].shape, j),
                _layer_spec(args[11].shape, j), _full_spec(args[12].shape), _full_spec(args[13].shape),
                _layer_spec(args[14].shape, i), _layer_spec(args[15].shape, i), _layer_spec(args[16].shape, i)]
    return pl.pallas_call(
        _rwkv_out_kernel,
        grid=(t_tok // TM,),
        in_specs=in_specs,
        out_specs=_mixer_out_specs(d),
        out_shape=_mixer_out_shapes(t_tok, d),
        compiler_params=pltpu.CompilerParams(dimension_semantics=("parallel",), vmem_limit_bytes=VMEM_LIMIT),
        name="rwkv_out",
    )(*args)


def _expert_kernel(be_ref, nu_ref, x_ref, wg_ref, wu_ref, wd_ref, o_ref, wg_s, wu_s, wd_s):
    b = pl.program_id(0)
    e = be_ref[b]
    prev = be_ref[jnp.maximum(b - 1, 0)]
    used = b < nu_ref[0]

    @pl.when(used & ((b == 0) | (e != prev)))
    def _():
        wg_s[...] = wg_ref[...].astype(BF16)
        wu_s[...] = wu_ref[...].astype(BF16)
        wd_s[...] = wd_ref[...].astype(BF16)

    @pl.when(used)
    def _():
        xb = x_ref[...].astype(BF16)
        g = jnp.dot(xb, wg_s[...], preferred_element_type=F32)
        u = jnp.dot(xb, wu_s[...], preferred_element_type=F32)
        hmid = (g * jax.nn.sigmoid(g)) * u
        o_ref[...] = jnp.dot(hmid.astype(BF16), wd_s[...], preferred_element_type=F32)

    @pl.when(jnp.logical_not(used))
    def _():
        o_ref[...] = jnp.zeros_like(o_ref)


def _experts(buf, block_expert, n_used, w_gate, w_up, w_down, i):
    cap, d = buf.shape
    f = w_gate.shape[-1]
    n_blocks = cap // MOE_ROWS
    grid_spec = pltpu.PrefetchScalarGridSpec(
        num_scalar_prefetch=2,
        grid=(n_blocks,),
        in_specs=[pl.BlockSpec((MOE_ROWS, d), lambda b, be, nu: (b, 0)),
                  pl.BlockSpec((None, None, d, f), lambda b, be, nu: (i, be[b], 0, 0)),
                  pl.BlockSpec((None, None, d, f), lambda b, be, nu: (i, be[b], 0, 0)),
                  pl.BlockSpec((None, None, f, d), lambda b, be, nu: (i, be[b], 0, 0))],
        out_specs=pl.BlockSpec((MOE_ROWS, d), lambda b, be, nu: (b, 0)),
        scratch_shapes=[pltpu.VMEM((d, f), BF16), pltpu.VMEM((d, f), BF16), pltpu.VMEM((f, d), BF16)],
    )
    return pl.pallas_call(
        _expert_kernel,
        grid_spec=grid_spec,
        out_shape=jax.ShapeDtypeStruct((cap, d), F32),
        compiler_params=pltpu.CompilerParams(dimension_semantics=("arbitrary",), vmem_limit_bytes=VMEM_LIMIT),
        name="experts",
    )(block_expert, n_used, buf, w_gate, w_up, w_down)


def _combine_kernel(x_ref, mod_ref, route_ref, y0_ref, y1_ref, fg_ref, xo_ref, *, final):
    gt2 = _mod_parts(mod_ref)[5]
    route = route_ref[...]
    xn = x_ref[...] + gt2 * (route[:, 2:3] * y0_ref[...] + route[:, 3:4] * y1_ref[...])
    if final:
        xn = (xn * lax.rsqrt(jnp.mean(xn * xn, axis=-1, keepdims=True) + NORM_EPS)) * fg_ref[...]
    xo_ref[...] = xn


def _combine(x, mods3, mod_spec, route, y0, y1, final_g, final, t_out):
    d = x.shape[1]
    tok = pl.BlockSpec((TM, d), lambda t: (t, 0))
    return pl.pallas_call(
        functools.partial(_combine_kernel, final=final),
        grid=(t_out // TM,),
        in_specs=[tok, mod_spec, pl.BlockSpec((TM, LANES), lambda t: (t, 0)), tok, tok, _full_spec((1, d))],
        out_specs=tok,
        out_shape=jax.ShapeDtypeStruct((t_out, d), F32),
        compiler_params=pltpu.CompilerParams(dimension_semantics=("parallel",), vmem_limit_bytes=VMEM_LIMIT),
        name="moe_combine",
    )(x, mods3, route, y0, y1, final_g.reshape(1, d))


def _dest_kernel(route_ref, d_ref, counts_ref, base_scr, start_scr):
    phase = pl.program_id(0)
    t = pl.program_id(1)
    route = route_ref[...]
    tm = route.shape[0]
    lane = lax.broadcasted_iota(jnp.int32, route.shape, 1)
    lane_f = lane.astype(F32)
    onehots = [jnp.where(lane_f == route[:, k:k + 1], 1.0, 0.0) for k in range(TOP_K)]

    @pl.when((phase == 0) & (t == 0))
    def _():
        base_scr[...] = jnp.zeros_like(base_scr)

    @pl.when((phase == 1) & (t == 0))
    def _():
        counts = base_scr[...]
        counts_ref[...] = counts
        padded = jnp.floor((counts + (MOE_ROWS - 1)) * (1.0 / MOE_ROWS)) * MOE_ROWS
        before = (lax.broadcasted_iota(jnp.int32, (LANES, LANES), 0)
                  < lax.broadcasted_iota(jnp.int32, (LANES, LANES), 1))
        start_scr[...] = _dot_x_sel(padded, jnp.where(before, 1.0, 0.0).astype(BF16))
        base_scr[...] = jnp.zeros_like(base_scr)

    @pl.when(phase == 0)
    def _():
        base_scr[...] = base_scr[...] + sum(jnp.sum(oh, axis=0, keepdims=True) for oh in onehots)

    @pl.when(phase == 1)
    def _():
        earlier = (lax.broadcasted_iota(jnp.int32, (tm, tm), 1)
                   < lax.broadcasted_iota(jnp.int32, (tm, tm), 0))
        earlier_b = jnp.where(earlier, 1.0, 0.0).astype(BF16)
        diag = (lax.broadcasted_iota(jnp.int32, (LANES, LANES), 0)
                == lax.broadcasted_iota(jnp.int32, (LANES, LANES), 1))
        base = base_scr[...] + start_scr[...]
        for k, oh in enumerate(onehots):
            before = jnp.dot(earlier_b, oh.astype(BF16), preferred_element_type=F32)
            dest = jnp.sum(oh * (base + before), axis=1, keepdims=True)
            rows = [jnp.sum(jnp.where(diag, jnp.broadcast_to(dest[i * LANES:(i + 1) * LANES], (LANES, LANES)), 0.0),
                            axis=0, keepdims=True) for i in range(tm // LANES)]
            d_ref[k, 0] = jnp.concatenate(rows, axis=0).astype(jnp.int32)
            base = base + jnp.sum(oh, axis=0, keepdims=True)
        base_scr[...] = base - start_scr[...]


def _dests(route):
    t_tok = route.shape[0]
    tiles = t_tok // TM
    return pl.pallas_call(
        _dest_kernel,
        grid=(2, tiles),
        in_specs=[pl.BlockSpec((TM, LANES), lambda ph, t: (t, 0))],
        out_specs=(pl.BlockSpec((TOP_K, 1, TM // LANES, LANES), lambda ph, t: (0, t * ph, 0, 0)),
                   pl.BlockSpec((1, LANES), lambda ph, t: (0, 0))),
        out_shape=(jax.ShapeDtypeStruct((TOP_K, tiles, TM // LANES, LANES), jnp.int32),
                   jax.ShapeDtypeStruct((1, LANES), F32)),
        scratch_shapes=[pltpu.VMEM((1, LANES), F32), pltpu.VMEM((1, LANES), F32)],
        compiler_params=pltpu.CompilerParams(dimension_semantics=("arbitrary", "arbitrary"),
                                             vmem_limit_bytes=VMEM_LIMIT),
        name="moe_dests",
    )(route)


def _moe(x, h2, route, mods3, mod_spec, i, p, final, t_out):
    t_tok, d = x.shape
    n_assign = t_tok * TOP_K
    dest4, counts_f = _dests(route)
    dest = dest4.reshape(TOP_K, t_tok)
    counts = counts_f[0, :N_EXPERTS].astype(jnp.int32)
    pad_end = jnp.cumsum(((counts + MOE_ROWS - 1) // MOE_ROWS) * MOE_ROWS)
    n_blocks = (n_assign + MOE_ROWS - 1) // MOE_ROWS + N_EXPERTS
    cap = n_blocks * MOE_ROWS
    token = jnp.arange(t_tok, dtype=jnp.int32)
    slot_token = (jnp.arange(cap, dtype=jnp.int32) % t_tok).at[dest.reshape(-1)].set(jnp.tile(token, TOP_K))
    block_start = jnp.arange(n_blocks, dtype=jnp.int32) * MOE_ROWS
    block_expert = jnp.minimum(jnp.sum((pad_end[None, :] <= block_start[:, None]).astype(jnp.int32), axis=1),
                               N_EXPERTS - 1)
    n_used = (pad_end[-1:] // MOE_ROWS).astype(jnp.int32)
    buf = jnp.take(h2, slot_token, axis=0)
    out = _experts(buf, block_expert, n_used, p["moe_w_gate"], p["moe_w_up"], p["moe_w_down"], i)
    y0 = jnp.take(out, dest[0], axis=0)
    y1 = jnp.take(out, dest[1], axis=0)
    return _combine(x, mods3, mod_spec, route, y0, y1, p["final_g"], final, t_out)


def kernel(x, c, ctx, c_ctx, ada_w, ada_b, norm1_g, norm2_g, final_g, ga_w_in, ga_b_in, ga_ln_g, ga_ln_b, ga_w_s,
           ga_b_s, ga_w_out, ga_b_out, rw_mu, rw_wr, rw_wk, rw_wv, rw_wo, rw_w0, rw_w1, rw_w2, rw_a0, rw_a1, rw_a2,
           rw_g1, rw_g2, rw_k_k, rw_k_a, rw_r_k, rw_ln_g, rw_ln_b, moe_w_grp, moe_b_grp, moe_w_exp, moe_b_exp,
           moe_w_gate, moe_w_up, moe_w_down):
    bsz, n_lat, d = x.shape
    n_ctx = ctx.shape[1]
    depth = ada_w.shape[0]
    n_mixers = 2
    assert n_lat % TM == 0 and n_ctx == TM and TM % GRID_W == 0
    assert n_lat % SCAN_BLOCK == 0 and n_ctx % SCAN_BLOCK == 0
    assert bsz + 1 <= 8 and d % SCAN_LANES == 0
    t_lat = bsz * n_lat
    lat_tiles = t_lat // TM
    tiles_per_batch = n_lat // TM
    heads = d // RWKV_HEAD

    s_rows = jnp.concatenate([c, c_ctx[None, :], jnp.zeros((8 - bsz - 1, d), F32)], axis=0)
    mods = _mods(s_rows, ada_w, ada_b)
    mods3 = mods.reshape(depth * 8, 1, N_MOD * d)

    def mod_spec_for(i):
        def imap(t):
            return (i * 8 + jnp.where(t < lat_tiles, t // tiles_per_batch, bsz), 0, 0)
        return pl.BlockSpec((1, 1, N_MOD * d), imap)

    head_sel = (jnp.arange(d)[:, None] // RWKV_HEAD == jnp.arange(LANES)[None, :]).astype(BF16)
    zpad = lambda a, axis, n: jnp.pad(a, [(0, n - a.shape[k]) if k == axis else (0, 0) for k in range(a.ndim)])
    glora = ((rw_g1.shape[-1] + LANES - 1) // LANES) * LANES
    wrt = jnp.concatenate([moe_w_grp, moe_w_exp], axis=-1)
    brt = jnp.concatenate([moe_b_grp, moe_b_exp], axis=-1)
    p = dict(
        norm1_g=norm1_g, norm2_g=norm2_g, final_g=final_g,
        ga_w_in=ga_w_in.astype(BF16), ga_b_in=ga_b_in, ga_ln_g=ga_ln_g, ga_ln_b=ga_ln_b,
        ga_w_s=ga_w_s.astype(BF16), ga_b_s=ga_b_s[..., None], ga_w_out=ga_w_out.astype(BF16), ga_b_out=ga_b_out,
        rw_mu8=zpad(rw_mu, 1, 8),
        rw_wr=rw_wr.astype(BF16), rw_wk=rw_wk.astype(BF16), rw_wv=rw_wv.astype(BF16), rw_wo=rw_wo.astype(BF16),
        rw_w1c=jnp.concatenate([rw_w1[:, 0], rw_w1[:, 1]], axis=-1).astype(BF16),
        rw_w2c=jnp.concatenate([rw_w2[:, 0], rw_w2[:, 1]], axis=-2).astype(BF16),
        rw_a1c=jnp.concatenate([rw_a1[:, 0], rw_a1[:, 1]], axis=-1).astype(BF16),
        rw_a2c=jnp.concatenate([rw_a2[:, 0], rw_a2[:, 1]], axis=-2).astype(BF16),
        rw_w0=rw_w0, rw_a0=rw_a0,
        rw_g1p=zpad(rw_g1, 2, glora).astype(BF16), rw_g2p=zpad(rw_g2, 1, glora).astype(BF16),
        rw_k_k=rw_k_k, rw_k_a=rw_k_a, rw_r_k=rw_r_k, rw_ln_g=rw_ln_g, rw_ln_b=rw_ln_b,
        head_sel=head_sel, head_sel_t=head_sel.T,
        wrt=zpad(wrt, 2, LANES).astype(BF16), brt=zpad(brt, 1, LANES).reshape(depth, 1, LANES),
        moe_w_gate=moe_w_gate, moe_w_up=moe_w_up, moe_w_down=moe_w_down,
    )
    assert heads <= LANES and 2 * rw_w1.shape[-1] == LANES and 2 * rw_a1.shape[-1] == LANES

    xs = jnp.concatenate([x.reshape(t_lat, d), ctx.reshape(bsz * n_ctx, d)], axis=0)
    for i in range(depth):
        j = i // n_mixers
        mod_spec = mod_spec_for(i)
        if i % n_mixers == 0:
            xs, h2, route = _gmlp_layer(xs, mods3, mod_spec, i, j, p)
        else:
            r, v, kkn, gate, lw, km, bb = _rwkv_prep(xs, mods3, mod_spec, i, j, p, lat_tiles, tiles_per_batch)
            y0 = _wkv_scan(r, v, kkn, lw, km, bb, 0, bsz, n_lat, n_ctx)
            y1 = _wkv_scan(r, v, kkn, lw, km, bb, 1, bsz, n_lat, n_ctx)
            xs, h2, route = _rwkv_out(xs, mods3, mod_spec, y0, y1, r, v, km, gate, i, j, p)
        last = i == depth - 1
        xs = _moe(xs, h2, route, mods3, mod_spec, i, p, final=last, t_out=t_lat if last else xs.shape[0])
    return xs.reshape(bsz, n_lat, d)
```

```python
import functools
import math

import jax
import jax.numpy as jnp
from jax import lax
from jax.experimental import pallas as pl
from jax.experimental.pallas import tpu as pltpu

F32 = jnp.float32
BF16 = jnp.bfloat16

N_MOD = 6
NORM_EPS = 1e-6
GRID_W = 64
CHUNK = 128
GMLP_GROUP_CH = 128
LN_EPS = 1e-5
RWKV_HEAD = 64
GN_EPS = 64e-5
N_GROUPS = 4
EXPERTS_PER_GROUP = 8
N_EXPERTS = N_GROUPS * EXPERTS_PER_GROUP
TOP_K = 2

LANES = 128
TM = 256
MOE_ROWS = 256
SCAN_BLOCK = 256
SCAN_CHUNK = 32
SCAN_LANES = 256
SCAN_GROUPS = 1
VMEM_LIMIT = 56 * 1024 * 1024
NEG_BIG = -3.0e38
SQRT_HALF = 0.7071067811865476


def _dotb(a, b):
    return jnp.dot(a.astype(BF16), b.astype(BF16), preferred_element_type=F32)


def _dot_nt(a, b):
    return lax.dot_general(a.astype(BF16), b.astype(BF16), (((1,), (1,)), ((), ())), preferred_element_type=F32)


def _dot_tn(a, b):
    return lax.dot_general(a.astype(BF16), b.astype(BF16), (((0,), (0,)), ((), ())), preferred_element_type=F32)


def _split2(x):
    hi = x.astype(BF16)
    lo = (x - hi.astype(F32)).astype(BF16)
    return hi, lo


def _split3(x):
    hi = x.astype(BF16)
    r1 = x - hi.astype(F32)
    mid = r1.astype(BF16)
    lo = (r1 - mid.astype(F32)).astype(BF16)
    return hi, mid, lo


def _dot_x_sel(x, sel_bf16):
    hi, mid, lo = _split3(x)
    d = lambda p: jnp.dot(p, sel_bf16, preferred_element_type=F32)
    return d(hi) + d(mid) + d(lo)


def _norm_mod(x, g, sc, sh):
    y = x * lax.rsqrt(jnp.mean(x * x, axis=-1, keepdims=True) + NORM_EPS)
    return (y * g) * (1.0 + sc) + sh


def _mod_parts(mod_ref):
    m = mod_ref[0]
    d = m.shape[-1] // N_MOD
    return [m[:, i * d:(i + 1) * d] for i in range(N_MOD)]


def _mods_kernel(s_ref, w_ref, b_ref, o_ref):
    s = s_ref[...]
    s = s * jax.nn.sigmoid(s)
    o_ref[0] = _dotb(s, w_ref[0]) + b_ref[0]


def _mods(s_rows, ada_w, ada_b):
    depth, d, nd = ada_w.shape
    tn = nd // 4
    return pl.pallas_call(
        _mods_kernel,
        grid=(depth, nd // tn),
        in_specs=[pl.BlockSpec(s_rows.shape, lambda i, j: (0, 0)),
                  pl.BlockSpec((1, d, tn), lambda i, j: (i, 0, j)),
                  pl.BlockSpec((1, 1, tn), lambda i, j: (i, 0, j))],
        out_specs=pl.BlockSpec((1, s_rows.shape[0], tn), lambda i, j: (i, 0, j)),
        out_shape=jax.ShapeDtypeStruct((depth, s_rows.shape[0], nd), F32),
        compiler_params=pltpu.CompilerParams(dimension_semantics=("parallel", "parallel"),
                                             vmem_limit_bytes=VMEM_LIMIT),
        name="mods",
    )(s_rows, ada_w, ada_b.reshape(depth, 1, nd))


def _route(logits):
    lane = lax.broadcasted_iota(jnp.int32, logits.shape, 1)
    lane_f = lane.astype(F32)
    big = jnp.float32(1e9)
    is_g = lane < N_GROUPS
    gl = jnp.where(is_g, logits, NEG_BIG)
    gm = jnp.max(gl, axis=1, keepdims=True)
    grp = jnp.min(jnp.where(is_g & (gl == gm), lane_f, big), axis=1, keepdims=True)
    p_grp = 1.0 / jnp.sum(jnp.where(is_g, jnp.exp(gl - gm), 0.0), axis=1, keepdims=True)
    lo = N_GROUPS + grp * EXPERTS_PER_GROUP
    in_grp = (lane_f >= lo) & (lane_f < lo + EXPERTS_PER_GROUP)
    el = jnp.where(in_grp, logits, NEG_BIG)
    m1 = jnp.max(el, axis=1, keepdims=True)
    i1 = jnp.min(jnp.where(in_grp & (el == m1), lane_f, big), axis=1, keepdims=True)
    rest = in_grp & (lane_f != i1)
    el2 = jnp.where(rest, logits, NEG_BIG)
    m2 = jnp.max(el2, axis=1, keepdims=True)
    i2 = jnp.min(jnp.where(rest & (el2 == m2), lane_f, big), axis=1, keepdims=True)
    e21 = jnp.exp(m2 - m1)
    s0 = 1.0 / (1.0 + e21)
    w0 = p_grp * s0
    w1 = p_grp * (e21 * s0)
    out = jnp.where(lane == 0, i1 - N_GROUPS,
                    jnp.where(lane == 1, i2 - N_GROUPS,
                              jnp.where(lane == 2, w0, jnp.where(lane == 3, w1, 0.0))))
    return out


def _residual_and_route(x, y, mods, g2_ref, wrt_ref, brt_ref, xo_ref, h2_ref, route_ref):
    _, _, gt1, sh2, sc2, _ = mods
    xn = x + gt1 * y
    xo_ref[...] = xn
    h2 = _norm_mod(xn, g2_ref[...], sc2, sh2)
    h2_ref[...] = h2
    logits = _dotb(h2, wrt_ref[...]) + brt_ref[...]
    route_ref[...] = _route(logits)


def _gmlp_kernel(x_ref, mod_ref, g1_ref, g2_ref, win_ref, bin_ref, lng_ref, lnb_ref, ws_ref, bs_ref,
                 wout_ref, bout_ref, wrt_ref, brt_ref, xo_ref, h2_ref, route_ref, gated_ref):
    x = x_ref[...]
    mods = _mod_parts(mod_ref)
    sh1, sc1 = mods[0], mods[1]
    h = _norm_mod(x, g1_ref[...], sc1, sh1)
    z = _dotb(h, win_ref[...]) + bin_ref[...]
    z = 0.5 * z * (1.0 + lax.erf(z * SQRT_HALF))
    width = z.shape[1] // 2
    u = z[:, :width]
    v = z[:, width:]
    mu = jnp.mean(v, axis=-1, keepdims=True)
    vc = v - mu
    var = jnp.mean(vc * vc, axis=-1, keepdims=True)
    vn = ((vc * lax.rsqrt(var + LN_EPS)) * lng_ref[...] + lnb_ref[...]).astype(BF16)
    n_chunks = x.shape[0] // CHUNK
    for g in range(width // GMLP_GROUP_CH):
        cs = slice(g * GMLP_GROUP_CH, (g + 1) * GMLP_GROUP_CH)
        rhs = jnp.concatenate([vn[c * CHUNK:(c + 1) * CHUNK, cs] for c in range(n_chunks)], axis=1)
        s = jnp.dot(ws_ref[g], rhs, preferred_element_type=F32) + bs_ref[g]
        for c in range(n_chunks):
            rsl = slice(c * CHUNK, (c + 1) * CHUNK)
            gated_ref[rsl, cs] = (u[rsl, cs] * s[:, c * GMLP_GROUP_CH:(c + 1) * GMLP_GROUP_CH]).astype(BF16)
    y = jnp.dot(gated_ref[...], wout_ref[...], preferred_element_type=F32) + bout_ref[...]
    _residual_and_route(x, y, mods, g2_ref, wrt_ref, brt_ref, xo_ref, h2_ref, route_ref)


def _full_spec(shape):
    nd = len(shape)
    return pl.BlockSpec(shape, lambda *_: (0,) * nd)


def _layer_spec(shape, j):
    nd = len(shape)
    return pl.BlockSpec((None,) + tuple(shape[1:]), lambda *_: (j,) + (0,) * (nd - 1))


def _mixer_out_shapes(t_tok, d):
    return (jax.ShapeDtypeStruct((t_tok, d), F32), jax.ShapeDtypeStruct((t_tok, d), F32),
            jax.ShapeDtypeStruct((t_tok, LANES), F32))


def _mixer_out_specs(d):
    return (pl.BlockSpec((TM, d), lambda t: (t, 0)), pl.BlockSpec((TM, d), lambda t: (t, 0)),
            pl.BlockSpec((TM, LANES), lambda t: (t, 0)))


def _gmlp_layer(x, mods3, mod_spec, i, j, p):
    t_tok, d = x.shape
    width = p["ga_w_out"].shape[1]
    row = lambda a: a.reshape(a.shape[0], 1, a.shape[1])
    args = (x, mods3, row(p["norm1_g"]), row(p["norm2_g"]), p["ga_w_in"], row(p["ga_b_in"]), row(p["ga_ln_g"]),
            row(p["ga_ln_b"]), p["ga_w_s"], p["ga_b_s"], p["ga_w_out"], row(p["ga_b_out"]), p["wrt"], p["brt"])
    in_specs = [pl.BlockSpec((TM, d), lambda t: (t, 0)), mod_spec,
                _layer_spec(args[2].shape, i), _layer_spec(args[3].shape, i),
                _layer_spec(args[4].shape, j), _layer_spec(args[5].shape, j), _layer_spec(args[6].shape, j),
                _layer_spec(args[7].shape, j), _layer_spec(args[8].shape, j), _layer_spec(args[9].shape, j),
                _layer_spec(args[10].shape, j), _layer_spec(args[11].shape, j),
                _layer_spec(args[12].shape, i), _layer_spec(args[13].shape, i)]
    return pl.pallas_call(
        _gmlp_kernel,
        grid=(t_tok // TM,),
        in_specs=in_specs,
        out_specs=_mixer_out_specs(d),
        out_shape=_mixer_out_shapes(t_tok, d),
        scratch_shapes=[pltpu.VMEM((TM, width), BF16)],
        compiler_params=pltpu.CompilerParams(dimension_semantics=("parallel",), vmem_limit_bytes=VMEM_LIMIT),
        name="gmlp_layer",
    )(*args)


def _seg_sum(q, e_ref):
    return _dot_x_sel(q, e_ref[...])


def _seg_expand(s, et_ref):
    return _dot_x_sel(s, et_ref[...])


def _shifted(h, h_above, h_below, is_ctx, first_row_tile, last_row_tile):
    tm, d = h.shape
    q = d // 4
    row = lax.broadcasted_iota(jnp.int32, (tm, 1), 0)
    ctx_i = is_ctx.astype(jnp.int32)
    col = row & (jnp.where(is_ctx, tm, GRID_W) - 1)
    last_col = jnp.where(is_ctx, tm, GRID_W) - 1
    prev1 = jnp.where(col == 0, 0.0, pltpu.roll(h, 1, axis=0))
    next1 = jnp.where(col == last_col, 0.0, pltpu.roll(h, tm - 1, axis=0))
    up = jnp.concatenate([h_above, h[:tm - GRID_W]], axis=0)
    up = jnp.where(row < jnp.where(first_row_tile, GRID_W, 0), 0.0, up)
    down = jnp.concatenate([h[GRID_W:], h_below], axis=0)
    down = jnp.where(row >= jnp.where(last_row_tile, tm - GRID_W, tm), 0.0, down)
    lane_q = lax.broadcasted_iota(jnp.int32, (1, d), 1) >> int(math.log2(q))
    src = lane_q * (1 - ctx_i) + (lane_q >> 1) * ctx_i
    return jnp.where(src == 0, prev1, jnp.where(src == 1, next1, jnp.where(src == 2, up, down)))


def _rwkv_prep_kernel(x_ref, xa_ref, xb_ref, mod_ref, n1_ref, mu_ref, wr_ref, wk_ref, wv_ref, w1_ref, w2_ref, w0_ref,
                      a1_ref, a2_ref, a0_ref, g1_ref, g2_ref, kk_ref, ka_ref, e_ref, et_ref,
                      r_o, v_o, kkn_o, gate_o, lw_o, km_o, bb_o, *, lat_tiles, tiles_per_batch):
    t = pl.program_id(0)
    mods = _mod_parts(mod_ref)
    norm = lambda x: _norm_mod(x, n1_ref[...], mods[1], mods[0])
    h = norm(x_ref[...])
    tb = t % tiles_per_batch
    hs = _shifted(h, norm(xa_ref[...]), norm(xb_ref[...]), t >= lat_tiles, tb == 0, tb == tiles_per_batch - 1)
    xx = hs - h
    mu = mu_ref[...]
    xr, xw, xk, xv, xa, xg = [h + xx * mu[q:q + 1] for q in range(6)]
    r = _dotb(xr, wr_ref[...])
    k = _dotb(xk, wk_ref[...])
    v = _dotb(xv, wv_ref[...])
    r_o[...] = r
    v_o[...] = v
    kkr = k * kk_ref[...]
    rs = lax.rsqrt(_seg_sum(kkr * kkr, e_ref) + 1e-12)
    kkn = kkr * _seg_expand(rs, et_ref)
    kkn_o[...] = kkn
    gate_o[...] = _dotb(jax.nn.sigmoid(_dotb(xg, g1_ref[...])), g2_ref[...])
    tw = jnp.tanh(_dotb(xw, w1_ref[...]))
    ta = _dotb(xa, a1_ref[...])
    lane = lax.broadcasted_iota(jnp.int32, tw.shape, 1)
    half = tw.shape[1] // 2
    w0 = w0_ref[...]
    a0 = a0_ref[...]
    ka = ka_ref[...]
    for z in range(2):
        zm = (lane >= z * half) & (lane < (z + 1) * half)
        lwz = _dotb(jnp.where(zm, tw, 0.0), w2_ref[...])
        t = -(w0[z:z + 1] + lwz)
        sp = jnp.maximum(t, 0.0) + jnp.log1p(jnp.exp(-jnp.abs(t)))
        lw_o[z] = -jnp.exp(-sp - 0.5)
        la = _dotb(jnp.where(zm, ta, 0.0), a2_ref[...])
        a = jax.nn.sigmoid(a0[z:z + 1] + la)
        km_o[z] = k * (1.0 + (a - 1.0) * ka)
        bb_o[z] = kkn * a


def _rwkv_prep(x, mods3, mod_spec, i, j, p, lat_tiles, tiles_per_batch):
    t_tok, d = x.shape
    row = lambda a: a.reshape(a.shape[0], 1, a.shape[1])
    args = (x, x, x, mods3, row(p["norm1_g"]), p["rw_mu8"], p["rw_wr"], p["rw_wk"], p["rw_wv"], p["rw_w1c"],
            p["rw_w2c"], p["rw_w0"], p["rw_a1c"], p["rw_a2c"], p["rw_a0"], p["rw_g1p"], p["rw_g2p"],
            row(p["rw_k_k"]), row(p["rw_k_a"]), p["head_sel"], p["head_sel_t"])
    tok = pl.BlockSpec((TM, d), lambda t: (t, 0))
    rows_per_tile = TM // GRID_W
    last_row = t_tok // GRID_W - 1
    above = pl.BlockSpec((GRID_W, d), lambda t: (jnp.maximum(t * rows_per_tile - 1, 0), 0))
    below = pl.BlockSpec((GRID_W, d), lambda t: (jnp.minimum((t + 1) * rows_per_tile, last_row), 0))
    in_specs = ([tok, above, below, mod_spec, _layer_spec(args[4].shape, i)]
                + [_layer_spec(a.shape, j) for a in args[5:19]]
                + [_full_spec(args[19].shape), _full_spec(args[20].shape)])
    dir_spec = pl.BlockSpec((2, TM, d), lambda t: (0, t, 0))
    tok_shape = jax.ShapeDtypeStruct((t_tok, d), F32)
    dir_shape = jax.ShapeDtypeStruct((2, t_tok, d), F32)
    return pl.pallas_call(
        functools.partial(_rwkv_prep_kernel, lat_tiles=lat_tiles, tiles_per_batch=tiles_per_batch),
        grid=(t_tok // TM,),
        in_specs=in_specs,
        out_specs=(tok, tok, tok, tok, dir_spec, dir_spec, dir_spec),
        out_shape=(tok_shape, tok_shape, tok_shape, tok_shape, dir_shape, dir_shape, dir_shape),
        compiler_params=pltpu.CompilerParams(dimension_semantics=("parallel",), vmem_limit_bytes=VMEM_LIMIT),
        name="rwkv_prep",
    )(*args)


def _scan_prepare(r, v, kk, lw, km, bb, rev, out):
    c = SCAN_CHUNK
    nb = SCAN_BLOCK
    a = -kk
    row = lax.broadcasted_iota(jnp.int32, (nb, nb), 0)
    col = lax.broadcasted_iota(jnp.int32, (nb, nb), 1)
    shift_c = int(math.log2(c))
    same = (row >> shift_c) == (col >> shift_c)
    if rev:
        incl = same & (col >= row)
        strict = same & (col > row)
    else:
        incl = same & (col <= row)
        strict = same & (col < row)
    incl_b = jnp.where(incl, 1.0, 0.0).astype(BF16)
    eye = jnp.where(row == col, 1.0, 0.0)
    head_of_lane = col >> int(math.log2(RWKV_HEAD))

    lw_hi, lw_lo = _split2(lw)
    cum = (jnp.dot(incl_b, lw_hi, preferred_element_type=F32)
           + jnp.dot(incl_b, lw_lo, preferred_element_type=F32))
    yield
    last = [ci * c if rev else ci * c + c - 1 for ci in range(nb // c)]
    tot_rows = [cum[t:t + 1] for t in last]
    tot = jnp.concatenate([jnp.broadcast_to(t, (c, t.shape[1])) for t in tot_rows], axis=0)
    yield
    at = a * jnp.exp(cum - lw)
    rt = r * jnp.exp(cum)
    e_inv = jnp.exp(-cum)
    bt = bb * e_inv
    kt = km * e_inv
    e_rem = jnp.exp(tot - cum)
    bh = (bb * e_rem).astype(BF16)
    kh = (km * e_rem).astype(BF16)
    et = jnp.exp(jnp.concatenate(tot_rows, axis=0))
    bk = jnp.concatenate([bt, kt], axis=0).astype(BF16)
    yield
    n_heads = SCAN_LANES // RWKV_HEAD
    heads = range(n_heads)
    hms = [head_of_lane == hh for hh in heads]
    at_h = [jnp.where(hm, at, 0.0) for hm in hms]
    rt_h = [jnp.where(hm, rt, 0.0) for hm in hms]
    v_h = [jnp.where(hm, v, 0.0).astype(BF16) for hm in hms]
    v_roll = pltpu.roll(v, RWKV_HEAD, axis=1)
    v_s = [jnp.where(hms[(hh + 1) % n_heads], v_roll, 0.0).astype(BF16) for hh in heads]
    a_ab, a_ak, m_rb, m_rk = [], [], [], []
    for hh in heads:
        s = _dot_nt(jnp.concatenate([at_h[hh], rt_h[hh]], axis=0), bk)
        a_ab.append(jnp.where(strict, s[:nb, :nb], 0.0))
        a_ak.append(jnp.where(strict, s[:nb, nb:], 0.0).astype(BF16))
        m_rb.append(jnp.where(incl, s[nb:, :nb], 0.0).astype(BF16))
        m_rk.append(jnp.where(incl, s[nb:, nb:], 0.0).astype(BF16))
        yield
    pw = list(a_ab)
    tinv = [eye + m for m in a_ab]
    for _ in range(shift_c - 1):
        for hh in heads:
            pw[hh] = _dotb(pw[hh], pw[hh])
        yield
        for hh in heads:
            tinv[hh] = tinv[hh] + _dotb(tinv[hh], pw[hh])
        yield
    x1 = [jnp.dot(a_ak[hh], v_s[hh], preferred_element_type=F32) for hh in heads]
    yield
    tc = [_dotb(tinv[hh], at_h[hh] + x1[hh]) for hh in heads]
    yield
    mc = [_dotb(m_rb[hh], tc[hh]) for hh in heads]
    yield

    def pick(parts, shift):
        res = parts[(n_heads - 1 - shift) % n_heads]
        for lb in range(n_heads - 2, -1, -1):
            res = jnp.where(hms[lb], parts[(lb - shift) % n_heads], res)
        return res

    back = SCAN_LANES - RWKV_HEAD
    ah = pick(tc, 0)
    vh = pltpu.roll(pick(tc, 1), back, axis=1)
    rh = rt + pick(mc, 0)
    yh = (jnp.dot(jnp.concatenate(m_rk, axis=1), jnp.concatenate(v_h, axis=0), preferred_element_type=F32)
          + pltpu.roll(pick(mc, 1), back, axis=1))
    yield
    out.extend([ah.astype(BF16), rh.astype(BF16), vh, yh, bh, kh, v.astype(BF16), et])


def _scan_prepare_stages():
    return 7 + (SCAN_LANES // RWKV_HEAD) + 2 * (int(math.log2(SCAN_CHUNK)) - 1)


def _scan_chain(g, ops, y_ref, lanes, rev, out):
    c = SCAN_CHUNK
    ah, rh, vh, yh, bh, kh, vb, et = ops
    n = g.shape[0]
    shift_h = int(math.log2(RWKV_HEAD))
    blockdiag = ((lax.broadcasted_iota(jnp.int32, (n, n), 0) >> shift_h)
                 == (lax.broadcasted_iota(jnp.int32, (n, n), 1) >> shift_h))
    n_chunks = SCAN_BLOCK // c
    order = range(n_chunks - 1, -1, -1) if rev else range(n_chunks)
    for ci in order:
        rs = slice(ci * c, (ci + 1) * c)
        p = _dot_nt(jnp.concatenate([ah[rs], rh[rs]], axis=0), g)
        u = p[:c] + vh[rs]
        y_ref[rs, lanes] = p[c:] + yh[rs]
        yield
        dg = _dot_tn(jnp.concatenate([u.astype(BF16), vb[rs]], axis=0), jnp.concatenate([bh[rs], kh[rs]], axis=0))
        g = jnp.where(blockdiag, g * et[ci:ci + 1, :] + dg, 0.0)
        yield
    out.append(g)


def _scan_chain_stages():
    return 2 * (SCAN_BLOCK // SCAN_CHUNK)


def _interleave(gen_a, n_a, gen_b, n_b):
    done_b = 0
    for i in range(n_a):
        next(gen_a, None)
        want_b = ((i + 1) * n_b) // n_a
        while done_b < want_b:
            next(gen_b, None)
            done_b += 1
    for g in (gen_a, gen_b):
        for _ in g:
            pass


def _scan_kernel(r_ref, v_ref, kk_ref, lw_ref, km_ref, bb_ref, y_ref, g_scr, *ops_scr, rev, groups):
    @pl.when(pl.program_id(2) == 0)
    def _():
        g_scr[...] = jnp.zeros_like(g_scr)
        for ref in ops_scr:
            ref[...] = jnp.zeros_like(ref)

    slot = pl.program_id(2) % 2
    for q in range(groups):
        lanes = slice(q * SCAN_LANES, (q + 1) * SCAN_LANES)
        prev = [ref.at[1 - slot, :, lanes] for ref in ops_scr]
        new, g_end = [], []
        prep = _scan_prepare(r_ref[:, lanes], v_ref[:, lanes], kk_ref[:, lanes], lw_ref[:, lanes], km_ref[:, lanes],
                             bb_ref[:, lanes], rev, new)
        chain = _scan_chain(g_scr[q], prev, y_ref, lanes, rev, g_end)
        _interleave(prep, _scan_prepare_stages(), chain, _scan_chain_stages())
        g_scr[q] = g_end[0]
        for ref, val in zip(ops_scr, new):
            ref[slot, :, lanes] = val


def _wkv_scan(r, v, kkn, lw, km, bb, z, bsz, n_lat, n_ctx):
    t_tok, d = r.shape
    lat_blocks = n_lat // SCAN_BLOCK
    ctx_blocks = n_ctx // SCAN_BLOCK
    steps = ctx_blocks + lat_blocks
    ctx_base = bsz * lat_blocks
    rev = z == 1
    width = SCAN_LANES * SCAN_GROUPS

    def blk(b, s):
        if rev:
            cb = ctx_base + b * ctx_blocks + (ctx_blocks - 1 - s)
            lb = b * lat_blocks + (lat_blocks - 1 - (s - ctx_blocks))
        else:
            cb = ctx_base + b * ctx_blocks + s
            lb = b * lat_blocks + (s - ctx_blocks)
        return jnp.where(s < ctx_blocks, cb, lb)

    in_blk = lambda b, s: blk(b, jnp.minimum(s, steps - 1))
    out_blk = lambda b, s: blk(b, jnp.maximum(s - 1, 0))
    tok = pl.BlockSpec((SCAN_BLOCK, width), lambda b, g, s: (in_blk(b, s), g))
    dirs = pl.BlockSpec((None, SCAN_BLOCK, width), lambda b, g, s: (z, in_blk(b, s), g))
    n_chunks = SCAN_BLOCK // SCAN_CHUNK
    ops_scratch = [pltpu.VMEM((2, SCAN_BLOCK, width), BF16), pltpu.VMEM((2, SCAN_BLOCK, width), BF16),
                   pltpu.VMEM((2, SCAN_BLOCK, width), F32), pltpu.VMEM((2, SCAN_BLOCK, width), F32),
                   pltpu.VMEM((2, SCAN_BLOCK, width), BF16), pltpu.VMEM((2, SCAN_BLOCK, width), BF16),
                   pltpu.VMEM((2, SCAN_BLOCK, width), BF16), pltpu.VMEM((2, n_chunks, width), F32)]
    return pl.pallas_call(
        functools.partial(_scan_kernel, rev=rev, groups=SCAN_GROUPS),
        grid=(bsz, d // width, steps + 1),
        in_specs=[tok, tok, tok, dirs, dirs, dirs],
        out_specs=pl.BlockSpec((SCAN_BLOCK, width), lambda b, g, s: (out_blk(b, s), g)),
        out_shape=jax.ShapeDtypeStruct((t_tok, d), F32),
        scratch_shapes=[pltpu.VMEM((SCAN_GROUPS, SCAN_LANES, SCAN_LANES), F32)] + ops_scratch,
        compiler_params=pltpu.CompilerParams(dimension_semantics=("parallel", "parallel", "arbitrary"),
                                             vmem_limit_bytes=VMEM_LIMIT),
        name="wkv_scan_rev" if rev else "wkv_scan_fwd",
    )(r, v, kkn, lw, km, bb)


def _rwkv_out_kernel(x_ref, mod_ref, y0_ref, y1_ref, r_ref, v_ref, km_ref, gate_ref, lng_ref, lnb_ref, rk_ref,
                     wo_ref, e_ref, et_ref, g2_ref, wrt_ref, brt_ref, xo_ref, h2_ref, route_ref):
    x = x_ref[...]
    mods = _mod_parts(mod_ref)
    y = y0_ref[...] + y1_ref[...]
    inv_n = 1.0 / RWKV_HEAD
    mu = _seg_expand(_seg_sum(y, e_ref) * inv_n, et_ref)
    yc = y - mu
    var = _seg_sum(yc * yc, e_ref) * inv_n
    yn = (yc * _seg_expand(lax.rsqrt(var + GN_EPS), et_ref)) * lng_ref[...] + lnb_ref[...]
    v = v_ref[...]
    bonus = _seg_sum((r_ref[...] * rk_ref[...]) * (km_ref[0] + km_ref[1]), e_ref)
    yn = yn + _seg_expand(bonus, et_ref) * v
    out = _dotb(yn * gate_ref[...], wo_ref[...])
    _residual_and_route(x, out, mods, g2_ref, wrt_ref, brt_ref, xo_ref, h2_ref, route_ref)


def _rwkv_out(x, mods3, mod_spec, y0, y1, r, v, km, gate, i, j, p):
    t_tok, d = x.shape
    row = lambda a: a.reshape(a.shape[0], 1, a.shape[1])
    tok = pl.BlockSpec((TM, d), lambda t: (t, 0))
    dir_spec = pl.BlockSpec((2, TM, d), lambda t: (0, t, 0))
    args = (x, mods3, y0, y1, r, v, km, gate, row(p["rw_ln_g"]), row(p["rw_ln_b"]), row(p["rw_r_k"].reshape(-1, d)),
            p["rw_wo"], p["head_sel"], p["head_sel_t"], row(p["norm2_g"]), p["wrt"], p["brt"])
    in_specs = [tok, mod_spec, tok, tok, tok, tok, dir_spec, tok,
                _layer_spec(args[8].shape, j), _layer_spec(args[9].shape, j), _layer_spec(args[10].shape, j),
                _layer_spec(args[11].shape, j), _full_spec(args[12].shape), _full_spec(args[13].shape),
                _layer_spec(args[14].shape, i), _layer_spec(args[15].shape, i), _layer_spec(args[16].shape, i)]
    return pl.pallas_call(
        _rwkv_out_kernel,
        grid=(t_tok // TM,),
        in_specs=in_specs,
        out_specs=_mixer_out_specs(d),
        out_shape=_mixer_out_shapes(t_tok, d),
        compiler_params=pltpu.CompilerParams(dimension_semantics=("parallel",), vmem_limit_bytes=VMEM_LIMIT),
        name="rwkv_out",
    )(*args)


def _expert_kernel(be_ref, nu_ref, x_ref, wg_ref, wu_ref, wd_ref, o_ref, wg_s, wu_s, wd_s):
    b = pl.program_id(0)
    e = be_ref[b]
    prev = be_ref[jnp.maximum(b - 1, 0)]
    used = b < nu_ref[0]

    @pl.when(used & ((b == 0) | (e != prev)))
    def _():
        wg_s[...] = wg_ref[...].astype(BF16)
        wu_s[...] = wu_ref[...].astype(BF16)
        wd_s[...] = wd_ref[...].astype(BF16)

    @pl.when(used)
    def _():
        xb = x_ref[...].astype(BF16)
        g = jnp.dot(xb, wg_s[...], preferred_element_type=F32)
        u = jnp.dot(xb, wu_s[...], preferred_element_type=F32)
        hmid = (g * jax.nn.sigmoid(g)) * u
        o_ref[...] = jnp.dot(hmid.astype(BF16), wd_s[...], preferred_element_type=F32)

    @pl.when(jnp.logical_not(used))
    def _():
        o_ref[...] = jnp.zeros_like(o_ref)


def _experts(buf, block_expert, n_used, w_gate, w_up, w_down, i):
    cap, d = buf.shape
    f = w_gate.shape[-1]
    n_blocks = cap // MOE_ROWS
    grid_spec = pltpu.PrefetchScalarGridSpec(
        num_scalar_prefetch=2,
        grid=(n_blocks,),
        in_specs=[pl.BlockSpec((MOE_ROWS, d), lambda b, be, nu: (b, 0)),
                  pl.BlockSpec((None, None, d, f), lambda b, be, nu: (i, be[b], 0, 0)),
                  pl.BlockSpec((None, None, d, f), lambda b, be, nu: (i, be[b], 0, 0)),
                  pl.BlockSpec((None, None, f, d), lambda b, be, nu: (i, be[b], 0, 0))],
        out_specs=pl.BlockSpec((MOE_ROWS, d), lambda b, be, nu: (b, 0)),
        scratch_shapes=[pltpu.VMEM((d, f), BF16), pltpu.VMEM((d, f), BF16), pltpu.VMEM((f, d), BF16)],
    )
    return pl.pallas_call(
        _expert_kernel,
        grid_spec=grid_spec,
        out_shape=jax.ShapeDtypeStruct((cap, d), F32),
        compiler_params=pltpu.CompilerParams(dimension_semantics=("arbitrary",), vmem_limit_bytes=VMEM_LIMIT),
        name="experts",
    )(block_expert, n_used, buf, w_gate, w_up, w_down)


def _combine_kernel(x_ref, mod_ref, route_ref, y0_ref, y1_ref, fg_ref, xo_ref, *, final):
    gt2 = _mod_parts(mod_ref)[5]
    route = route_ref[...]
    xn = x_ref[...] + gt2 * (route[:, 2:3] * y0_ref[...] + route[:, 3:4] * y1_ref[...])
    if final:
        xn = (xn * lax.rsqrt(jnp.mean(xn * xn, axis=-1, keepdims=True) + NORM_EPS)) * fg_ref[...]
    xo_ref[...] = xn


def _combine(x, mods3, mod_spec, route, y0, y1, final_g, final, t_out):
    d = x.shape[1]
    tok = pl.BlockSpec((TM, d), lambda t: (t, 0))
    return pl.pallas_call(
        functools.partial(_combine_kernel, final=final),
        grid=(t_out // TM,),
        in_specs=[tok, mod_spec, pl.BlockSpec((TM, LANES), lambda t: (t, 0)), tok, tok, _full_spec((1, d))],
        out_specs=tok,
        out_shape=jax.ShapeDtypeStruct((t_out, d), F32),
        compiler_params=pltpu.CompilerParams(dimension_semantics=("parallel",), vmem_limit_bytes=VMEM_LIMIT),
        name="moe_combine",
    )(x, mods3, route, y0, y1, final_g.reshape(1, d))


def _dest_kernel(route_ref, d_ref, counts_ref, base_scr, start_scr):
    phase = pl.program_id(0)
    t = pl.program_id(1)
    route = route_ref[...]
    tm = route.shape[0]
    lane = lax.broadcasted_iota(jnp.int32, route.shape, 1)
    lane_f = lane.astype(F32)
    onehots = [jnp.where(lane_f == route[:, k:k + 1], 1.0, 0.0) for k in range(TOP_K)]

    @pl.when((phase == 0) & (t == 0))
    def _():
        base_scr[...] = jnp.zeros_like(base_scr)

    @pl.when((phase == 1) & (t == 0))
    def _():
        counts = base_scr[...]
        counts_ref[...] = counts
        padded = jnp.floor((counts + (MOE_ROWS - 1)) * (1.0 / MOE_ROWS)) * MOE_ROWS
        before = (lax.broadcasted_iota(jnp.int32, (LANES, LANES), 0)
                  < lax.broadcasted_iota(jnp.int32, (LANES, LANES), 1))
        start_scr[...] = _dot_x_sel(padded, jnp.where(before, 1.0, 0.0).astype(BF16))
        base_scr[...] = jnp.zeros_like(base_scr)

    @pl.when(phase == 0)
    def _():
        base_scr[...] = base_scr[...] + sum(jnp.sum(oh, axis=0, keepdims=True) for oh in onehots)

    @pl.when(phase == 1)
    def _():
        earlier = (lax.broadcasted_iota(jnp.int32, (tm, tm), 1)
                   < lax.broadcasted_iota(jnp.int32, (tm, tm), 0))
        earlier_b = jnp.where(earlier, 1.0, 0.0).astype(BF16)
        diag = (lax.broadcasted_iota(jnp.int32, (LANES, LANES), 0)
                == lax.broadcasted_iota(jnp.int32, (LANES, LANES), 1))
        base = base_scr[...] + start_scr[...]
        for k, oh in enumerate(onehots):
            before = jnp.dot(earlier_b, oh.astype(BF16), preferred_element_type=F32)
            dest = jnp.sum(oh * (base + before), axis=1, keepdims=True)
            rows = [jnp.sum(jnp.where(diag, jnp.broadcast_to(dest[i * LANES:(i + 1) * LANES], (LANES, LANES)), 0.0),
                            axis=0, keepdims=True) for i in range(tm // LANES)]
            d_ref[k, 0] = jnp.concatenate(rows, axis=0).astype(jnp.int32)
            base = base + jnp.sum(oh, axis=0, keepdims=True)
        base_scr[...] = base - start_scr[...]


def _dests(route):
    t_tok = route.shape[0]
    tiles = t_tok // TM
    return pl.pallas_call(
        _dest_kernel,
        grid=(2, tiles),
        in_specs=[pl.BlockSpec((TM, LANES), lambda ph, t: (t, 0))],
        out_specs=(pl.BlockSpec((TOP_K, 1, TM // LANES, LANES), lambda ph, t: (0, t * ph, 0, 0)),
                   pl.BlockSpec((1, LANES), lambda ph, t: (0, 0))),
        out_shape=(jax.ShapeDtypeStruct((TOP_K, tiles, TM // LANES, LANES), jnp.int32),
                   jax.ShapeDtypeStruct((1, LANES), F32)),
        scratch_shapes=[pltpu.VMEM((1, LANES), F32), pltpu.VMEM((1, LANES), F32)],
        compiler_params=pltpu.CompilerParams(dimension_semantics=("arbitrary", "arbitrary"),
                                             vmem_limit_bytes=VMEM_LIMIT),
        name="moe_dests",
    )(route)


def _moe(x, h2, route, mods3, mod_spec, i, p, final, t_out):
    t_tok, d = x.shape
    n_assign = t_tok * TOP_K
    dest4, counts_f = _dests(route)
    dest = dest4.reshape(TOP_K, t_tok)
    counts = counts_f[0, :N_EXPERTS].astype(jnp.int32)
    pad_end = jnp.cumsum(((counts + MOE_ROWS - 1) // MOE_ROWS) * MOE_ROWS)
    n_blocks = (n_assign + MOE_ROWS - 1) // MOE_ROWS + N_EXPERTS
    cap = n_blocks * MOE_ROWS
    token = jnp.arange(t_tok, dtype=jnp.int32)
    slot_token = (jnp.arange(cap, dtype=jnp.int32) % t_tok).at[dest.reshape(-1)].set(
        jnp.tile(token, TOP_K), mode="promise_in_bounds", unique_indices=True)
    block_start = jnp.arange(n_blocks, dtype=jnp.int32) * MOE_ROWS
    block_expert = jnp.minimum(jnp.sum((pad_end[None, :] <= block_start[:, None]).astype(jnp.int32), axis=1),
                               N_EXPERTS - 1)
    n_used = (pad_end[-1:] // MOE_ROWS).astype(jnp.int32)
    buf = h2.at[slot_token].get(mode="promise_in_bounds")
    out = _experts(buf, block_expert, n_used, p["moe_w_gate"], p["moe_w_up"], p["moe_w_down"], i)
    y0 = out.at[dest[0]].get(mode="promise_in_bounds", unique_indices=True)
    y1 = out.at[dest[1]].get(mode="promise_in_bounds", unique_indices=True)
    return _combine(x, mods3, mod_spec, route, y0, y1, p["final_g"], final, t_out)


def kernel(x, c, ctx, c_ctx, ada_w, ada_b, norm1_g, norm2_g, final_g, ga_w_in, ga_b_in, ga_ln_g, ga_ln_b, ga_w_s,
           ga_b_s, ga_w_out, ga_b_out, rw_mu, rw_wr, rw_wk, rw_wv, rw_wo, rw_w0, rw_w1, rw_w2, rw_a0, rw_a1, rw_a2,
           rw_g1, rw_g2, rw_k_k, rw_k_a, rw_r_k, rw_ln_g, rw_ln_b, moe_w_grp, moe_b_grp, moe_w_exp, moe_b_exp,
           moe_w_gate, moe_w_up, moe_w_down):
    bsz, n_lat, d = x.shape
    n_ctx = ctx.shape[1]
    depth = ada_w.shape[0]
    n_mixers = 2
    assert n_lat % TM == 0 and n_ctx == TM and TM % GRID_W == 0
    assert n_lat % SCAN_BLOCK == 0 and n_ctx % SCAN_BLOCK == 0
    assert bsz + 1 <= 8 and d % SCAN_LANES == 0
    t_lat = bsz * n_lat
    lat_tiles = t_lat // TM
    tiles_per_batch = n_lat // TM
    heads = d // RWKV_HEAD

    s_rows = jnp.concatenate([c, c_ctx[None, :], jnp.zeros((8 - bsz - 1, d), F32)], axis=0)
    mods = _mods(s_rows, ada_w, ada_b)
    mods3 = mods.reshape(depth * 8, 1, N_MOD * d)

    def mod_spec_for(i):
        def imap(t):
            return (i * 8 + jnp.where(t < lat_tiles, t // tiles_per_batch, bsz), 0, 0)
        return pl.BlockSpec((1, 1, N_MOD * d), imap)

    head_sel = (jnp.arange(d)[:, None] // RWKV_HEAD == jnp.arange(LANES)[None, :]).astype(BF16)
    zpad = lambda a, axis, n: jnp.pad(a, [(0, n - a.shape[k]) if k == axis else (0, 0) for k in range(a.ndim)])
    glora = ((rw_g1.shape[-1] + LANES - 1) // LANES) * LANES
    wrt = jnp.concatenate([moe_w_grp, moe_w_exp], axis=-1)
    brt = jnp.concatenate([moe_b_grp, moe_b_exp], axis=-1)
    p = dict(
        norm1_g=norm1_g, norm2_g=norm2_g, final_g=final_g,
        ga_w_in=ga_w_in.astype(BF16), ga_b_in=ga_b_in, ga_ln_g=ga_ln_g, ga_ln_b=ga_ln_b,
        ga_w_s=ga_w_s.astype(BF16), ga_b_s=ga_b_s[..., None], ga_w_out=ga_w_out.astype(BF16), ga_b_out=ga_b_out,
        rw_mu8=zpad(rw_mu, 1, 8),
        rw_wr=rw_wr.astype(BF16), rw_wk=rw_wk.astype(BF16), rw_wv=rw_wv.astype(BF16), rw_wo=rw_wo.astype(BF16),
        rw_w1c=jnp.concatenate([rw_w1[:, 0], rw_w1[:, 1]], axis=-1).astype(BF16),
        rw_w2c=jnp.concatenate([rw_w2[:, 0], rw_w2[:, 1]], axis=-2).astype(BF16),
        rw_a1c=jnp.concatenate([rw_a1[:, 0], rw_a1[:, 1]], axis=-1).astype(BF16),
        rw_a2c=jnp.concatenate([rw_a2[:, 0], rw_a2[:, 1]], axis=-2).astype(BF16),
        rw_w0=rw_w0, rw_a0=rw_a0,
        rw_g1p=zpad(rw_g1, 2, glora).astype(BF16), rw_g2p=zpad(rw_g2, 1, glora).astype(BF16),
        rw_k_k=rw_k_k, rw_k_a=rw_k_a, rw_r_k=rw_r_k, rw_ln_g=rw_ln_g, rw_ln_b=rw_ln_b,
        head_sel=head_sel, head_sel_t=head_sel.T,
        wrt=zpad(wrt, 2, LANES).astype(BF16), brt=zpad(brt, 1, LANES).reshape(depth, 1, LANES),
        moe_w_gate=moe_w_gate, moe_w_up=moe_w_up, moe_w_down=moe_w_down,
    )
    assert heads <= LANES and 2 * rw_w1.shape[-1] == LANES and 2 * rw_a1.shape[-1] == LANES

    xs = jnp.concatenate([x.reshape(t_lat, d), ctx.reshape(bsz * n_ctx, d)], axis=0)
    for i in range(depth):
        j = i // n_mixers
        mod_spec = mod_spec_for(i)
        if i % n_mixers == 0:
            xs, h2, route = _gmlp_layer(xs, mods3, mod_spec, i, j, p)
        else:
            r, v, kkn, gate, lw, km, bb = _rwkv_prep(xs, mods3, mod_spec, i, j, p, lat_tiles, tiles_per_batch)
            y0 = _wkv_scan(r, v, kkn, lw, km, bb, 0, bsz, n_lat, n_ctx)
            y1 = _wkv_scan(r, v, kkn, lw, km, bb, 1, bsz, n_lat, n_ctx)
            xs, h2, route = _rwkv_out(xs, mods3, mod_spec, y0, y1, r, v, km, gate, i, j, p)
        last = i == depth - 1
        xs = _moe(xs, h2, route, mods3, mod_spec, i, p, final=last, t_out=t_lat if last else xs.shape[0])
    return xs.reshape(bsz, n_lat, d)
```

```python
import functools
import math

import jax
import jax.numpy as jnp
from jax import lax
from jax.experimental import pallas as pl
from jax.experimental.pallas import tpu as pltpu

F32 = jnp.float32
BF16 = jnp.bfloat16

N_MOD = 6
NORM_EPS = 1e-6
GRID_W = 64
CHUNK = 128
GMLP_GROUP_CH = 128
LN_EPS = 1e-5
RWKV_HEAD = 64
GN_EPS = 64e-5
N_GROUPS = 4
EXPERTS_PER_GROUP = 8
N_EXPERTS = N_GROUPS * EXPERTS_PER_GROUP
TOP_K = 2

LANES = 128
TM = 256
MOE_ROWS = 512
SCAN_BLOCK = 256
SCAN_CHUNK = 32
SCAN_LANES = 256
SCAN_GROUPS = 2
VMEM_LIMIT = 56 * 1024 * 1024
NEG_BIG = -3.0e38
SQRT_HALF = 0.7071067811865476


def _dotb(a, b):
    return jnp.dot(a.astype(BF16), b.astype(BF16), preferred_element_type=F32)


def _dot_nt(a, b):
    return lax.dot_general(a.astype(BF16), b.astype(BF16), (((1,), (1,)), ((), ())), preferred_element_type=F32)


def _dot_tn(a, b):
    return lax.dot_general(a.astype(BF16), b.astype(BF16), (((0,), (0,)), ((), ())), preferred_element_type=F32)


def _split2(x):
    hi = x.astype(BF16)
    lo = (x - hi.astype(F32)).astype(BF16)
    return hi, lo


def _split3(x):
    hi = x.astype(BF16)
    r1 = x - hi.astype(F32)
    mid = r1.astype(BF16)
    lo = (r1 - mid.astype(F32)).astype(BF16)
    return hi, mid, lo


def _dot_x_sel(x, sel_bf16, terms=3):
    d = lambda p: jnp.dot(p, sel_bf16, preferred_element_type=F32)
    return sum(d(p) for p in (_split3(x) if terms == 3 else _split2(x)))


def _norm_mod(x, g, sc, sh):
    y = x * lax.rsqrt(jnp.mean(x * x, axis=-1, keepdims=True) + NORM_EPS)
    return (y * g) * (1.0 + sc) + sh


def _mod_parts(mod_ref):
    m = mod_ref[0]
    d = m.shape[-1] // N_MOD
    return [m[:, i * d:(i + 1) * d] for i in range(N_MOD)]


def _mods_kernel(s_ref, w_ref, b_ref, o_ref):
    s = s_ref[...]
    s = s * jax.nn.sigmoid(s)
    o_ref[0] = _dotb(s, w_ref[0]) + b_ref[0]


def _mods(s_rows, ada_w, ada_b):
    depth, d, nd = ada_w.shape
    tn = nd // 4
    return pl.pallas_call(
        _mods_kernel,
        grid=(depth, nd // tn),
        in_specs=[pl.BlockSpec(s_rows.shape, lambda i, j: (0, 0)),
                  pl.BlockSpec((1, d, tn), lambda i, j: (i, 0, j)),
                  pl.BlockSpec((1, 1, tn), lambda i, j: (i, 0, j))],
        out_specs=pl.BlockSpec((1, s_rows.shape[0], tn), lambda i, j: (i, 0, j)),
        out_shape=jax.ShapeDtypeStruct((depth, s_rows.shape[0], nd), F32),
        compiler_params=pltpu.CompilerParams(dimension_semantics=("parallel", "parallel"),
                                             vmem_limit_bytes=VMEM_LIMIT),
        name="mods",
    )(s_rows, ada_w, ada_b.reshape(depth, 1, nd))


def _route(logits):
    lane = lax.broadcasted_iota(jnp.int32, logits.shape, 1)
    lane_f = lane.astype(F32)
    big = jnp.float32(1e9)
    is_g = lane < N_GROUPS
    gl = jnp.where(is_g, logits, NEG_BIG)
    gm = jnp.max(gl, axis=1, keepdims=True)
    grp = jnp.min(jnp.where(is_g & (gl == gm), lane_f, big), axis=1, keepdims=True)
    p_grp = 1.0 / jnp.sum(jnp.where(is_g, jnp.exp(gl - gm), 0.0), axis=1, keepdims=True)
    lo = N_GROUPS + grp * EXPERTS_PER_GROUP
    in_grp = (lane_f >= lo) & (lane_f < lo + EXPERTS_PER_GROUP)
    el = jnp.where(in_grp, logits, NEG_BIG)
    m1 = jnp.max(el, axis=1, keepdims=True)
    i1 = jnp.min(jnp.where(in_grp & (el == m1), lane_f, big), axis=1, keepdims=True)
    rest = in_grp & (lane_f != i1)
    el2 = jnp.where(rest, logits, NEG_BIG)
    m2 = jnp.max(el2, axis=1, keepdims=True)
    i2 = jnp.min(jnp.where(rest & (el2 == m2), lane_f, big), axis=1, keepdims=True)
    e21 = jnp.exp(m2 - m1)
    s0 = 1.0 / (1.0 + e21)
    w0 = p_grp * s0
    w1 = p_grp * (e21 * s0)
    out = jnp.where(lane == 0, i1 - N_GROUPS,
                    jnp.where(lane == 1, i2 - N_GROUPS,
                              jnp.where(lane == 2, w0, jnp.where(lane == 3, w1, 0.0))))
    return out


def _residual_and_route(x, y, mods, g2_ref, wrt_ref, brt_ref, xo_ref, h2_ref, route_ref):
    _, _, gt1, sh2, sc2, _ = mods
    xn = x + gt1 * y
    xo_ref[...] = xn
    h2 = _norm_mod(xn, g2_ref[...], sc2, sh2)
    h2_ref[...] = h2
    logits = _dotb(h2, wrt_ref[...]) + brt_ref[...]
    route_ref[...] = _route(logits)


def _gmlp_kernel(x_ref, mod_ref, g1_ref, g2_ref, win_ref, bin_ref, lng_ref, lnb_ref, ws_ref, bs_ref,
                 wout_ref, bout_ref, wrt_ref, brt_ref, xo_ref, h2_ref, route_ref, gated_ref):
    x = x_ref[...]
    mods = _mod_parts(mod_ref)
    sh1, sc1 = mods[0], mods[1]
    h = _norm_mod(x, g1_ref[...], sc1, sh1)
    z = _dotb(h, win_ref[...]) + bin_ref[...]
    z = 0.5 * z * (1.0 + lax.erf(z * SQRT_HALF))
    width = z.shape[1] // 2
    u = z[:, :width]
    v = z[:, width:]
    mu = jnp.mean(v, axis=-1, keepdims=True)
    vc = v - mu
    var = jnp.mean(vc * vc, axis=-1, keepdims=True)
    vn = ((vc * lax.rsqrt(var + LN_EPS)) * lng_ref[...] + lnb_ref[...]).astype(BF16)
    n_chunks = x.shape[0] // CHUNK
    for g in range(width // GMLP_GROUP_CH):
        cs = slice(g * GMLP_GROUP_CH, (g + 1) * GMLP_GROUP_CH)
        rhs = jnp.concatenate([vn[c * CHUNK:(c + 1) * CHUNK, cs] for c in range(n_chunks)], axis=1)
        s = jnp.dot(ws_ref[g], rhs, preferred_element_type=F32) + bs_ref[g]
        for c in range(n_chunks):
            rsl = slice(c * CHUNK, (c + 1) * CHUNK)
            gated_ref[rsl, cs] = (u[rsl, cs] * s[:, c * GMLP_GROUP_CH:(c + 1) * GMLP_GROUP_CH]).astype(BF16)
    y = jnp.dot(gated_ref[...], wout_ref[...], preferred_element_type=F32) + bout_ref[...]
    _residual_and_route(x, y, mods, g2_ref, wrt_ref, brt_ref, xo_ref, h2_ref, route_ref)


def _full_spec(shape):
    nd = len(shape)
    return pl.BlockSpec(shape, lambda *_: (0,) * nd)


def _layer_spec(shape, j):
    nd = len(shape)
    return pl.BlockSpec((None,) + tuple(shape[1:]), lambda *_: (j,) + (0,) * (nd - 1))


def _mixer_out_shapes(t_tok, d):
    return (jax.ShapeDtypeStruct((t_tok, d), F32), jax.ShapeDtypeStruct((t_tok, d), F32),
            jax.ShapeDtypeStruct((t_tok, LANES), F32))


def _mixer_out_specs(d):
    return (pl.BlockSpec((TM, d), lambda t: (t, 0)), pl.BlockSpec((TM, d), lambda t: (t, 0)),
            pl.BlockSpec((TM, LANES), lambda t: (t, 0)))


def _gmlp_layer(x, mods3, mod_spec, i, j, p):
    t_tok, d = x.shape
    width = p["ga_w_out"].shape[1]
    row = lambda a: a.reshape(a.shape[0], 1, a.shape[1])
    args = (x, mods3, row(p["norm1_g"]), row(p["norm2_g"]), p["ga_w_in"], row(p["ga_b_in"]), row(p["ga_ln_g"]),
            row(p["ga_ln_b"]), p["ga_w_s"], p["ga_b_s"], p["ga_w_out"], row(p["ga_b_out"]), p["wrt"], p["brt"])
    in_specs = [pl.BlockSpec((TM, d), lambda t: (t, 0)), mod_spec,
                _layer_spec(args[2].shape, i), _layer_spec(args[3].shape, i),
                _layer_spec(args[4].shape, j), _layer_spec(args[5].shape, j), _layer_spec(args[6].shape, j),
                _layer_spec(args[7].shape, j), _layer_spec(args[8].shape, j), _layer_spec(args[9].shape, j),
                _layer_spec(args[10].shape, j), _layer_spec(args[11].shape, j),
                _layer_spec(args[12].shape, i), _layer_spec(args[13].shape, i)]
    return pl.pallas_call(
        _gmlp_kernel,
        grid=(t_tok // TM,),
        in_specs=in_specs,
        out_specs=_mixer_out_specs(d),
        out_shape=_mixer_out_shapes(t_tok, d),
        scratch_shapes=[pltpu.VMEM((TM, width), BF16)],
        compiler_params=pltpu.CompilerParams(dimension_semantics=("parallel",), vmem_limit_bytes=VMEM_LIMIT),
        name="gmlp_layer",
    )(*args)


def _seg_sum(q, e_ref):
    return _dot_x_sel(q, e_ref[...], terms=2)


def _seg_expand(s, et_ref):
    return _dot_x_sel(s, et_ref[...], terms=2)


def _shifted(h, h_above, h_below, is_ctx, first_row_tile, last_row_tile):
    tm, d = h.shape
    q = d // 4
    row = lax.broadcasted_iota(jnp.int32, (tm, 1), 0)
    ctx_i = is_ctx.astype(jnp.int32)
    col = row & (jnp.where(is_ctx, tm, GRID_W) - 1)
    last_col = jnp.where(is_ctx, tm, GRID_W) - 1
    prev1 = jnp.where(col == 0, 0.0, pltpu.roll(h, 1, axis=0))
    next1 = jnp.where(col == last_col, 0.0, pltpu.roll(h, tm - 1, axis=0))
    up = jnp.concatenate([h_above, h[:tm - GRID_W]], axis=0)
    up = jnp.where(row < jnp.where(first_row_tile, GRID_W, 0), 0.0, up)
    down = jnp.concatenate([h[GRID_W:], h_below], axis=0)
    down = jnp.where(row >= jnp.where(last_row_tile, tm - GRID_W, tm), 0.0, down)
    lane_q = lax.broadcasted_iota(jnp.int32, (1, d), 1) >> int(math.log2(q))
    src = lane_q * (1 - ctx_i) + (lane_q >> 1) * ctx_i
    return jnp.where(src == 0, prev1, jnp.where(src == 1, next1, jnp.where(src == 2, up, down)))


def _rwkv_prep_kernel(x_ref, xa_ref, xb_ref, mod_ref, n1_ref, mu_ref, wr_ref, wk_ref, wv_ref, w1_ref, w2_ref, w0_ref,
                      a1_ref, a2_ref, a0_ref, g1_ref, g2_ref, kk_ref, ka_ref, e_ref, et_ref,
                      r_o, v_o, kkn_o, gate_o, lw_o, km_o, bb_o, *, lat_tiles, tiles_per_batch):
    t = pl.program_id(0)
    mods = _mod_parts(mod_ref)
    norm = lambda x: _norm_mod(x, n1_ref[...], mods[1], mods[0])
    h = norm(x_ref[...])
    tb = t % tiles_per_batch
    hs = _shifted(h, norm(xa_ref[...]), norm(xb_ref[...]), t >= lat_tiles, tb == 0, tb == tiles_per_batch - 1)
    xx = hs - h
    mu = mu_ref[...]
    xr, xw, xk, xv, xa, xg = [h + xx * mu[q:q + 1] for q in range(6)]
    r = _dotb(xr, wr_ref[...])
    k = _dotb(xk, wk_ref[...])
    v = _dotb(xv, wv_ref[...])
    r_o[...] = r
    v_o[...] = v
    kkr = k * kk_ref[...]
    rs = lax.rsqrt(_seg_sum(kkr * kkr, e_ref) + 1e-12)
    kkn = kkr * _seg_expand(rs, et_ref)
    kkn_o[...] = kkn
    gate_o[...] = _dotb(jax.nn.sigmoid(_dotb(xg, g1_ref[...])), g2_ref[...])
    tw = jnp.tanh(_dotb(xw, w1_ref[...]))
    ta = _dotb(xa, a1_ref[...])
    lane = lax.broadcasted_iota(jnp.int32, tw.shape, 1)
    half = tw.shape[1] // 2
    w0 = w0_ref[...]
    a0 = a0_ref[...]
    ka = ka_ref[...]
    for z in range(2):
        zm = (lane >= z * half) & (lane < (z + 1) * half)
        lwz = _dotb(jnp.where(zm, tw, 0.0), w2_ref[...])
        t = -(w0[z:z + 1] + lwz)
        sp = jnp.maximum(t, 0.0) + jnp.log1p(jnp.exp(-jnp.abs(t)))
        lw_o[z] = -jnp.exp(-sp - 0.5)
        la = _dotb(jnp.where(zm, ta, 0.0), a2_ref[...])
        a = jax.nn.sigmoid(a0[z:z + 1] + la)
        km_o[z] = k * (1.0 + (a - 1.0) * ka)
        bb_o[z] = kkn * a


def _rwkv_prep(x, mods3, mod_spec, i, j, p, lat_tiles, tiles_per_batch):
    t_tok, d = x.shape
    row = lambda a: a.reshape(a.shape[0], 1, a.shape[1])
    args = (x, x, x, mods3, row(p["norm1_g"]), p["rw_mu8"], p["rw_wr"], p["rw_wk"], p["rw_wv"], p["rw_w1c"],
            p["rw_w2c"], p["rw_w0"], p["rw_a1c"], p["rw_a2c"], p["rw_a0"], p["rw_g1p"], p["rw_g2p"],
            row(p["rw_k_k"]), row(p["rw_k_a"]), p["head_sel"], p["head_sel_t"])
    tok = pl.BlockSpec((TM, d), lambda t: (t, 0))
    rows_per_tile = TM // GRID_W
    last_row = t_tok // GRID_W - 1
    above = pl.BlockSpec((GRID_W, d), lambda t: (jnp.maximum(t * rows_per_tile - 1, 0), 0))
    below = pl.BlockSpec((GRID_W, d), lambda t: (jnp.minimum((t + 1) * rows_per_tile, last_row), 0))
    in_specs = ([tok, above, below, mod_spec, _layer_spec(args[4].shape, i)]
                + [_layer_spec(a.shape, j) for a in args[5:19]]
                + [_full_spec(args[19].shape), _full_spec(args[20].shape)])
    dir_spec = pl.BlockSpec((2, TM, d), lambda t: (0, t, 0))
    tok_shape = jax.ShapeDtypeStruct((t_tok, d), F32)
    dir_shape = jax.ShapeDtypeStruct((2, t_tok, d), F32)
    return pl.pallas_call(
        functools.partial(_rwkv_prep_kernel, lat_tiles=lat_tiles, tiles_per_batch=tiles_per_batch),
        grid=(t_tok // TM,),
        in_specs=in_specs,
        out_specs=(tok, tok, tok, tok, dir_spec, dir_spec, dir_spec),
        out_shape=(tok_shape, tok_shape, tok_shape, tok_shape, dir_shape, dir_shape, dir_shape),
        compiler_params=pltpu.CompilerParams(dimension_semantics=("parallel",), vmem_limit_bytes=VMEM_LIMIT),
        name="rwkv_prep",
    )(*args)


def _scan_prepare(r, v, kk, lw, km, bb, rev, out):
    c = SCAN_CHUNK
    nb = SCAN_BLOCK
    a = -kk
    row = lax.broadcasted_iota(jnp.int32, (nb, nb), 0)
    col = lax.broadcasted_iota(jnp.int32, (nb, nb), 1)
    shift_c = int(math.log2(c))
    same = (row >> shift_c) == (col >> shift_c)
    if rev:
        incl = same & (col >= row)
        strict = same & (col > row)
    else:
        incl = same & (col <= row)
        strict = same & (col < row)
    incl_b = jnp.where(incl, 1.0, 0.0).astype(BF16)
    eye = jnp.where(row == col, 1.0, 0.0)
    head_of_lane = col >> int(math.log2(RWKV_HEAD))

    lw_hi, lw_lo = _split2(lw)
    cum = (jnp.dot(incl_b, lw_hi, preferred_element_type=F32)
           + jnp.dot(incl_b, lw_lo, preferred_element_type=F32))
    yield
    last = [ci * c if rev else ci * c + c - 1 for ci in range(nb // c)]
    tot_rows = [cum[t:t + 1] for t in last]
    tot = jnp.concatenate([jnp.broadcast_to(t, (c, t.shape[1])) for t in tot_rows], axis=0)
    yield
    at = a * jnp.exp(cum - lw)
    rt = r * jnp.exp(cum)
    e_inv = jnp.exp(-cum)
    bt = bb * e_inv
    kt = km * e_inv
    e_rem = jnp.exp(tot - cum)
    bh = (bb * e_rem).astype(BF16)
    kh = (km * e_rem).astype(BF16)
    et = jnp.exp(jnp.concatenate(tot_rows, axis=0))
    bk = jnp.concatenate([bt, kt], axis=0).astype(BF16)
    yield
    n_heads = SCAN_LANES // RWKV_HEAD
    heads = range(n_heads)
    hms = [head_of_lane == hh for hh in heads]
    at_h = [jnp.where(hm, at, 0.0) for hm in hms]
    rt_h = [jnp.where(hm, rt, 0.0) for hm in hms]
    v_h = [jnp.where(hm, v, 0.0).astype(BF16) for hm in hms]
    v_roll = pltpu.roll(v, RWKV_HEAD, axis=1)
    v_s = [jnp.where(hms[(hh + 1) % n_heads], v_roll, 0.0).astype(BF16) for hh in heads]
    a_ab, a_ak, m_rb, m_rk = [], [], [], []
    for hh in heads:
        s = _dot_nt(jnp.concatenate([at_h[hh], rt_h[hh]], axis=0), bk)
        a_ab.append(jnp.where(strict, s[:nb, :nb], 0.0))
        a_ak.append(jnp.where(strict, s[:nb, nb:], 0.0).astype(BF16))
        m_rb.append(jnp.where(incl, s[nb:, :nb], 0.0).astype(BF16))
        m_rk.append(jnp.where(incl, s[nb:, nb:], 0.0).astype(BF16))
        yield
    pw = list(a_ab)
    tinv = [eye + m for m in a_ab]
    for _ in range(shift_c - 1):
        for hh in heads:
            pw[hh] = _dotb(pw[hh], pw[hh])
        yield
        for hh in heads:
            tinv[hh] = tinv[hh] + _dotb(tinv[hh], pw[hh])
        yield
    x1 = [jnp.dot(a_ak[hh], v_s[hh], preferred_element_type=F32) for hh in heads]
    yield
    tc = [_dotb(tinv[hh], at_h[hh] + x1[hh]) for hh in heads]
    yield
    mc = [_dotb(m_rb[hh], tc[hh]) for hh in heads]
    yield

    def pick(parts, shift):
        res = parts[(n_heads - 1 - shift) % n_heads]
        for lb in range(n_heads - 2, -1, -1):
            res = jnp.where(hms[lb], parts[(lb - shift) % n_heads], res)
        return res

    back = SCAN_LANES - RWKV_HEAD
    ah = pick(tc, 0)
    vh = pltpu.roll(pick(tc, 1), back, axis=1)
    rh = rt + pick(mc, 0)
    yh = (jnp.dot(jnp.concatenate(m_rk, axis=1), jnp.concatenate(v_h, axis=0), preferred_element_type=F32)
          + pltpu.roll(pick(mc, 1), back, axis=1))
    yield
    out.extend([ah.astype(BF16), rh.astype(BF16), vh, yh, bh, kh, v.astype(BF16), et])


def _scan_prepare_stages():
    return 7 + (SCAN_LANES // RWKV_HEAD) + 2 * (int(math.log2(SCAN_CHUNK)) - 1)


def _scan_chain(g, ops, y_ref, lanes, rev, out):
    c = SCAN_CHUNK
    ah, rh, vh, yh, bh, kh, vb, et = ops
    n = g.shape[0]
    shift_h = int(math.log2(RWKV_HEAD))
    blockdiag = ((lax.broadcasted_iota(jnp.int32, (n, n), 0) >> shift_h)
                 == (lax.broadcasted_iota(jnp.int32, (n, n), 1) >> shift_h))
    n_chunks = SCAN_BLOCK // c
    order = range(n_chunks - 1, -1, -1) if rev else range(n_chunks)
    for ci in order:
        rs = slice(ci * c, (ci + 1) * c)
        p = _dot_nt(jnp.concatenate([ah[rs], rh[rs]], axis=0), g)
        u = p[:c] + vh[rs]
        y_ref[rs, lanes] = p[c:] + yh[rs]
        yield
        dg = _dot_tn(jnp.concatenate([u.astype(BF16), vb[rs]], axis=0), jnp.concatenate([bh[rs], kh[rs]], axis=0))
        g = jnp.where(blockdiag, g * et[ci:ci + 1, :] + dg, 0.0)
        yield
    out.append(g)


def _scan_chain_stages():
    return 2 * (SCAN_BLOCK // SCAN_CHUNK)


def _interleave(gen_a, n_a, gen_b, n_b):
    done_b = 0
    for i in range(n_a):
        next(gen_a, None)
        want_b = ((i + 1) * n_b) // n_a
        while done_b < want_b:
            next(gen_b, None)
            done_b += 1
        yield
    for g in (gen_a, gen_b):
        for _ in g:
            pass


def _scan_kernel(r_ref, v_ref, kk_ref, lw_ref, km_ref, bb_ref, y_ref, g_scr, *ops_scr, rev, groups):
    @pl.when(pl.program_id(2) == 0)
    def _():
        g_scr[...] = jnp.zeros_like(g_scr)
        for ref in ops_scr:
            ref[...] = jnp.zeros_like(ref)

    slot = pl.program_id(2) % 2
    work, results = [], []
    for q in range(groups):
        lanes = slice(q * SCAN_LANES, (q + 1) * SCAN_LANES)
        prev = [ref.at[1 - slot, :, lanes] for ref in ops_scr]
        new, g_end = [], []
        prep = _scan_prepare(r_ref[:, lanes], v_ref[:, lanes], kk_ref[:, lanes], lw_ref[:, lanes], km_ref[:, lanes],
                             bb_ref[:, lanes], rev, new)
        chain = _scan_chain(g_scr[q], prev, y_ref, lanes, rev, g_end)
        work.append(_interleave(prep, _scan_prepare_stages(), chain, _scan_chain_stages()))
        results.append((lanes, new, g_end))
    while work:
        work = [w for w in work if next(w, StopIteration) is not StopIteration]
    for q, (lanes, new, g_end) in enumerate(results):
        g_scr[q] = g_end[0]
        for ref, val in zip(ops_scr, new):
            ref[slot, :, lanes] = val


def _wkv_scan(r, v, kkn, lw, km, bb, z, bsz, n_lat, n_ctx):
    t_tok, d = r.shape
    lat_blocks = n_lat // SCAN_BLOCK
    ctx_blocks = n_ctx // SCAN_BLOCK
    steps = ctx_blocks + lat_blocks
    ctx_base = bsz * lat_blocks
    rev = z == 1
    width = SCAN_LANES * SCAN_GROUPS

    def blk(b, s):
        if rev:
            cb = ctx_base + b * ctx_blocks + (ctx_blocks - 1 - s)
            lb = b * lat_blocks + (lat_blocks - 1 - (s - ctx_blocks))
        else:
            cb = ctx_base + b * ctx_blocks + s
            lb = b * lat_blocks + (s - ctx_blocks)
        return jnp.where(s < ctx_blocks, cb, lb)

    in_blk = lambda b, s: blk(b, jnp.minimum(s, steps - 1))
    out_blk = lambda b, s: blk(b, jnp.maximum(s - 1, 0))
    tok = pl.BlockSpec((SCAN_BLOCK, width), lambda b, g, s: (in_blk(b, s), g))
    dirs = pl.BlockSpec((None, SCAN_BLOCK, width), lambda b, g, s: (z, in_blk(b, s), g))
    n_chunks = SCAN_BLOCK // SCAN_CHUNK
    ops_scratch = [pltpu.VMEM((2, SCAN_BLOCK, width), BF16), pltpu.VMEM((2, SCAN_BLOCK, width), BF16),
                   pltpu.VMEM((2, SCAN_BLOCK, width), F32), pltpu.VMEM((2, SCAN_BLOCK, width), F32),
                   pltpu.VMEM((2, SCAN_BLOCK, width), BF16), pltpu.VMEM((2, SCAN_BLOCK, width), BF16),
                   pltpu.VMEM((2, SCAN_BLOCK, width), BF16), pltpu.VMEM((2, n_chunks, width), F32)]
    return pl.pallas_call(
        functools.partial(_scan_kernel, rev=rev, groups=SCAN_GROUPS),
        grid=(bsz, d // width, steps + 1),
        in_specs=[tok, tok, tok, dirs, dirs, dirs],
        out_specs=pl.BlockSpec((SCAN_BLOCK, width), lambda b, g, s: (out_blk(b, s), g)),
        out_shape=jax.ShapeDtypeStruct((t_tok, d), F32),
        scratch_shapes=[pltpu.VMEM((SCAN_GROUPS, SCAN_LANES, SCAN_LANES), F32)] + ops_scratch,
        compiler_params=pltpu.CompilerParams(dimension_semantics=("parallel", "parallel", "arbitrary"),
                                             vmem_limit_bytes=VMEM_LIMIT),
        name="wkv_scan_rev" if rev else "wkv_scan_fwd",
    )(r, v, kkn, lw, km, bb)


def _rwkv_out_kernel(x_ref, mod_ref, y0_ref, y1_ref, r_ref, v_ref, km_ref, gate_ref, lng_ref, lnb_ref, rk_ref,
                     wo_ref, e_ref, et_ref, g2_ref, wrt_ref, brt_ref, xo_ref, h2_ref, route_ref):
    x = x_ref[...]
    mods = _mod_parts(mod_ref)
    y = y0_ref[...] + y1_ref[...]
    inv_n = 1.0 / RWKV_HEAD
    mu = _seg_expand(_seg_sum(y, e_ref) * inv_n, et_ref)
    yc = y - mu
    var = _seg_sum(yc * yc, e_ref) * inv_n
    yn = (yc * _seg_expand(lax.rsqrt(var + GN_EPS), et_ref)) * lng_ref[...] + lnb_ref[...]
    v = v_ref[...]
    bonus = _seg_sum((r_ref[...] * rk_ref[...]) * (km_ref[0] + km_ref[1]), e_ref)
    yn = yn + _seg_expand(bonus, et_ref) * v
    out = _dotb(yn * gate_ref[...], wo_ref[...])
    _residual_and_route(x, out, mods, g2_ref, wrt_ref, brt_ref, xo_ref, h2_ref, route_ref)


def _rwkv_out(x, mods3, mod_spec, y0, y1, r, v, km, gate, i, j, p):
    t_tok, d = x.shape
    row = lambda a: a.reshape(a.shape[0], 1, a.shape[1])
    tok = pl.BlockSpec((TM, d), lambda t: (t, 0))
    dir_spec = pl.BlockSpec((2, TM, d), lambda t: (0, t, 0))
    args = (x, mods3, y0, y1, r, v, km, gate, row(p["rw_ln_g"]), row(p["rw_ln_b"]), row(p["rw_r_k"].reshape(-1, d)),
            p["rw_wo"], p["head_sel"], p["head_sel_t"], row(p["norm2_g"]), p["wrt"], p["brt"])
    in_specs = [tok, mod_spec, tok, tok, tok, tok, dir_spec, tok,
                _layer_spec(args[8].shape, j), _layer_spec(args[9].shape, j), _layer_spec(args[10].shape, j),
                _layer_spec(args[11].shape, j), _full_spec(args[12].shape), _full_spec(args[13].shape),
                _layer_spec(args[14].shape, i), _layer_spec(args[15].shape, i), _layer_spec(args[16].shape, i)]
    return pl.pallas_call(
        _rwkv_out_kernel,
        grid=(t_tok // TM,),
        in_specs=in_specs,
        out_specs=_mixer_out_specs(d),
        out_shape=_mixer_out_shapes(t_tok, d),
        compiler_params=pltpu.CompilerParams(dimension_semantics=("parallel",), vmem_limit_bytes=VMEM_LIMIT),
        name="rwkv_out",
    )(*args)


def _expert_kernel(be_ref, nu_ref, x_ref, wg_ref, wu_ref, wd_ref, o_ref, wg_s, wu_s, wd_s):
    b = pl.program_id(0)
    e = be_ref[b]
    prev = be_ref[jnp.maximum(b - 1, 0)]
    used = b < nu_ref[0]

    @pl.when(used & ((b == 0) | (e != prev)))
    def _():
        wg_s[...] = wg_ref[...].astype(BF16)
        wu_s[...] = wu_ref[...].astype(BF16)
        wd_s[...] = wd_ref[...].astype(BF16)

    @pl.when(used)
    def _():
        xb = x_ref[...].astype(BF16)
        g = jnp.dot(xb, wg_s[...], preferred_element_type=F32)
        u = jnp.dot(xb, wu_s[...], preferred_element_type=F32)
        hmid = (g * jax.nn.sigmoid(g)) * u
        o_ref[...] = jnp.dot(hmid.astype(BF16), wd_s[...], preferred_element_type=F32)

    @pl.when(jnp.logical_not(used))
    def _():
        o_ref[...] = jnp.zeros_like(o_ref)


def _experts(buf, block_expert, n_used, w_gate, w_up, w_down, i):
    cap, d = buf.shape
    f = w_gate.shape[-1]
    n_blocks = cap // MOE_ROWS
    grid_spec = pltpu.PrefetchScalarGridSpec(
        num_scalar_prefetch=2,
        grid=(n_blocks,),
        in_specs=[pl.BlockSpec((MOE_ROWS, d), lambda b, be, nu: (b, 0)),
                  pl.BlockSpec((None, None, d, f), lambda b, be, nu: (i, be[b], 0, 0)),
                  pl.BlockSpec((None, None, d, f), lambda b, be, nu: (i, be[b], 0, 0)),
                  pl.BlockSpec((None, None, f, d), lambda b, be, nu: (i, be[b], 0, 0))],
        out_specs=pl.BlockSpec((MOE_ROWS, d), lambda b, be, nu: (b, 0)),
        scratch_shapes=[pltpu.VMEM((d, f), BF16), pltpu.VMEM((d, f), BF16), pltpu.VMEM((f, d), BF16)],
    )
    return pl.pallas_call(
        _expert_kernel,
        grid_spec=grid_spec,
        out_shape=jax.ShapeDtypeStruct((cap, d), F32),
        compiler_params=pltpu.CompilerParams(dimension_semantics=("arbitrary",), vmem_limit_bytes=VMEM_LIMIT),
        name="experts",
    )(block_expert, n_used, buf, w_gate, w_up, w_down)


def _combine_kernel(x_ref, mod_ref, route_ref, y0_ref, y1_ref, fg_ref, xo_ref, *, final):
    gt2 = _mod_parts(mod_ref)[5]
    route = route_ref[...]
    xn = x_ref[...] + gt2 * (route[:, 2:3] * y0_ref[...] + route[:, 3:4] * y1_ref[...])
    if final:
        xn = (xn * lax.rsqrt(jnp.mean(xn * xn, axis=-1, keepdims=True) + NORM_EPS)) * fg_ref[...]
    xo_ref[...] = xn


def _combine(x, mods3, mod_spec, route, y0, y1, final_g, final, t_out):
    d = x.shape[1]
    tok = pl.BlockSpec((TM, d), lambda t: (t, 0))
    return pl.pallas_call(
        functools.partial(_combine_kernel, final=final),
        grid=(t_out // TM,),
        in_specs=[tok, mod_spec, pl.BlockSpec((TM, LANES), lambda t: (t, 0)), tok, tok, _full_spec((1, d))],
        out_specs=tok,
        out_shape=jax.ShapeDtypeStruct((t_out, d), F32),
        compiler_params=pltpu.CompilerParams(dimension_semantics=("parallel",), vmem_limit_bytes=VMEM_LIMIT),
        name="moe_combine",
    )(x, mods3, route, y0, y1, final_g.reshape(1, d))


def _dest_kernel(route_ref, d_ref, counts_ref, base_scr, start_scr):
    phase = pl.program_id(0)
    t = pl.program_id(1)
    route = route_ref[...]
    tm = route.shape[0]
    lane = lax.broadcasted_iota(jnp.int32, route.shape, 1)
    lane_f = lane.astype(F32)
    onehots = [jnp.where(lane_f == route[:, k:k + 1], 1.0, 0.0) for k in range(TOP_K)]

    @pl.when((phase == 0) & (t == 0))
    def _():
        base_scr[...] = jnp.zeros_like(base_scr)

    @pl.when((phase == 1) & (t == 0))
    def _():
        counts = base_scr[...]
        counts_ref[...] = counts
        padded = jnp.floor((counts + (MOE_ROWS - 1)) * (1.0 / MOE_ROWS)) * MOE_ROWS
        before = (lax.broadcasted_iota(jnp.int32, (LANES, LANES), 0)
                  < lax.broadcasted_iota(jnp.int32, (LANES, LANES), 1))
        start_scr[...] = _dot_x_sel(padded, jnp.where(before, 1.0, 0.0).astype(BF16))
        base_scr[...] = jnp.zeros_like(base_scr)

    @pl.when(phase == 0)
    def _():
        base_scr[...] = base_scr[...] + sum(jnp.sum(oh, axis=0, keepdims=True) for oh in onehots)

    @pl.when(phase == 1)
    def _():
        earlier = (lax.broadcasted_iota(jnp.int32, (tm, tm), 1)
                   < lax.broadcasted_iota(jnp.int32, (tm, tm), 0))
        earlier_b = jnp.where(earlier, 1.0, 0.0).astype(BF16)
        diag = (lax.broadcasted_iota(jnp.int32, (LANES, LANES), 0)
                == lax.broadcasted_iota(jnp.int32, (LANES, LANES), 1))
        base = base_scr[...] + start_scr[...]
        for k, oh in enumerate(onehots):
            before = jnp.dot(earlier_b, oh.astype(BF16), preferred_element_type=F32)
            dest = jnp.sum(oh * (base + before), axis=1, keepdims=True)
            rows = [jnp.sum(jnp.where(diag, jnp.broadcast_to(dest[i * LANES:(i + 1) * LANES], (LANES, LANES)), 0.0),
                            axis=0, keepdims=True) for i in range(tm // LANES)]
            d_ref[k, 0] = jnp.concatenate(rows, axis=0).astype(jnp.int32)
            base = base + jnp.sum(oh, axis=0, keepdims=True)
        base_scr[...] = base - start_scr[...]


def _dests(route):
    t_tok = route.shape[0]
    tiles = t_tok // TM
    return pl.pallas_call(
        _dest_kernel,
        grid=(2, tiles),
        in_specs=[pl.BlockSpec((TM, LANES), lambda ph, t: (t, 0))],
        out_specs=(pl.BlockSpec((TOP_K, 1, TM // LANES, LANES), lambda ph, t: (0, t * ph, 0, 0)),
                   pl.BlockSpec((1, LANES), lambda ph, t: (0, 0))),
        out_shape=(jax.ShapeDtypeStruct((TOP_K, tiles, TM // LANES, LANES), jnp.int32),
                   jax.ShapeDtypeStruct((1, LANES), F32)),
        scratch_shapes=[pltpu.VMEM((1, LANES), F32), pltpu.VMEM((1, LANES), F32)],
        compiler_params=pltpu.CompilerParams(dimension_semantics=("arbitrary", "arbitrary"),
                                             vmem_limit_bytes=VMEM_LIMIT),
        name="moe_dests",
    )(route)


def _moe(x, h2, route, mods3, mod_spec, i, p, final, t_out):
    t_tok, d = x.shape
    n_assign = t_tok * TOP_K
    dest4, counts_f = _dests(route)
    dest = dest4.reshape(TOP_K, t_tok)
    counts = counts_f[0, :N_EXPERTS].astype(jnp.int32)
    pad_end = jnp.cumsum(((counts + MOE_ROWS - 1) // MOE_ROWS) * MOE_ROWS)
    n_blocks = (n_assign + MOE_ROWS - 1) // MOE_ROWS + N_EXPERTS
    cap = n_blocks * MOE_ROWS
    token = jnp.arange(t_tok, dtype=jnp.int32)
    slot_token = (jnp.arange(cap, dtype=jnp.int32) % t_tok).at[dest.reshape(-1)].set(
        jnp.tile(token, TOP_K), mode="promise_in_bounds", unique_indices=True)
    block_start = jnp.arange(n_blocks, dtype=jnp.int32) * MOE_ROWS
    block_expert = jnp.minimum(jnp.sum((pad_end[None, :] <= block_start[:, None]).astype(jnp.int32), axis=1),
                               N_EXPERTS - 1)
    n_used = (pad_end[-1:] // MOE_ROWS).astype(jnp.int32)
    buf = h2.at[slot_token].get(mode="promise_in_bounds")
    out = _experts(buf, block_expert, n_used, p["moe_w_gate"], p["moe_w_up"], p["moe_w_down"], i)
    y0 = out.at[dest[0]].get(mode="promise_in_bounds", unique_indices=True)
    y1 = out.at[dest[1]].get(mode="promise_in_bounds", unique_indices=True)
    return _combine(x, mods3, mod_spec, route, y0, y1, p["final_g"], final, t_out)


def kernel(x, c, ctx, c_ctx, ada_w, ada_b, norm1_g, norm2_g, final_g, ga_w_in, ga_b_in, ga_ln_g, ga_ln_b, ga_w_s,
           ga_b_s, ga_w_out, ga_b_out, rw_mu, rw_wr, rw_wk, rw_wv, rw_wo, rw_w0, rw_w1, rw_w2, rw_a0, rw_a1, rw_a2,
           rw_g1, rw_g2, rw_k_k, rw_k_a, rw_r_k, rw_ln_g, rw_ln_b, moe_w_grp, moe_b_grp, moe_w_exp, moe_b_exp,
           moe_w_gate, moe_w_up, moe_w_down):
    bsz, n_lat, d = x.shape
    n_ctx = ctx.shape[1]
    depth = ada_w.shape[0]
    n_mixers = 2
    assert n_lat % TM == 0 and n_ctx == TM and TM % GRID_W == 0
    assert n_lat % SCAN_BLOCK == 0 and n_ctx % SCAN_BLOCK == 0
    assert bsz + 1 <= 8 and d % (SCAN_LANES * SCAN_GROUPS) == 0
    assert MOE_ROWS & (MOE_ROWS - 1) == 0
    t_lat = bsz * n_lat
    lat_tiles = t_lat // TM
    tiles_per_batch = n_lat // TM
    heads = d // RWKV_HEAD

    s_rows = jnp.concatenate([c, c_ctx[None, :], jnp.zeros((8 - bsz - 1, d), F32)], axis=0)
    mods = _mods(s_rows, ada_w, ada_b)
    mods3 = mods.reshape(depth * 8, 1, N_MOD * d)

    def mod_spec_for(i):
        def imap(t):
            return (i * 8 + jnp.where(t < lat_tiles, t // tiles_per_batch, bsz), 0, 0)
        return pl.BlockSpec((1, 1, N_MOD * d), imap)

    head_sel = (jnp.arange(d)[:, None] // RWKV_HEAD == jnp.arange(LANES)[None, :]).astype(BF16)
    zpad = lambda a, axis, n: jnp.pad(a, [(0, n - a.shape[k]) if k == axis else (0, 0) for k in range(a.ndim)])
    glora = ((rw_g1.shape[-1] + LANES - 1) // LANES) * LANES
    wrt = jnp.concatenate([moe_w_grp, moe_w_exp], axis=-1)
    brt = jnp.concatenate([moe_b_grp, moe_b_exp], axis=-1)
    p = dict(
        norm1_g=norm1_g, norm2_g=norm2_g, final_g=final_g,
        ga_w_in=ga_w_in.astype(BF16), ga_b_in=ga_b_in, ga_ln_g=ga_ln_g, ga_ln_b=ga_ln_b,
        ga_w_s=ga_w_s.astype(BF16), ga_b_s=ga_b_s[..., None], ga_w_out=ga_w_out.astype(BF16), ga_b_out=ga_b_out,
        rw_mu8=zpad(rw_mu, 1, 8),
        rw_wr=rw_wr.astype(BF16), rw_wk=rw_wk.astype(BF16), rw_wv=rw_wv.astype(BF16), rw_wo=rw_wo.astype(BF16),
        rw_w1c=jnp.concatenate([rw_w1[:, 0], rw_w1[:, 1]], axis=-1).astype(BF16),
        rw_w2c=jnp.concatenate([rw_w2[:, 0], rw_w2[:, 1]], axis=-2).astype(BF16),
        rw_a1c=jnp.concatenate([rw_a1[:, 0], rw_a1[:, 1]], axis=-1).astype(BF16),
        rw_a2c=jnp.concatenate([rw_a2[:, 0], rw_a2[:, 1]], axis=-2).astype(BF16),
        rw_w0=rw_w0, rw_a0=rw_a0,
        rw_g1p=zpad(rw_g1, 2, glora).astype(BF16), rw_g2p=zpad(rw_g2, 1, glora).astype(BF16),
        rw_k_k=rw_k_k, rw_k_a=rw_k_a, rw_r_k=rw_r_k, rw_ln_g=rw_ln_g, rw_ln_b=rw_ln_b,
        head_sel=head_sel, head_sel_t=head_sel.T,
        wrt=zpad(wrt, 2, LANES).astype(BF16), brt=zpad(brt, 1, LANES).reshape(depth, 1, LANES),
        moe_w_gate=moe_w_gate, moe_w_up=moe_w_up, moe_w_down=moe_w_down,
    )
    assert heads <= LANES and 2 * rw_w1.shape[-1] == LANES and 2 * rw_a1.shape[-1] == LANES

    xs = jnp.concatenate([x.reshape(t_lat, d), ctx.reshape(bsz * n_ctx, d)], axis=0)
    for i in range(depth):
        j = i // n_mixers
        mod_spec = mod_spec_for(i)
        if i % n_mixers == 0:
            xs, h2, route = _gmlp_layer(xs, mods3, mod_spec, i, j, p)
        else:
            r, v, kkn, gate, lw, km, bb = _rwkv_prep(xs, mods3, mod_spec, i, j, p, lat_tiles, tiles_per_batch)
            y0 = _wkv_scan(r, v, kkn, lw, km, bb, 0, bsz, n_lat, n_ctx)
            y1 = _wkv_scan(r, v, kkn, lw, km, bb, 1, bsz, n_lat, n_ctx)
            xs, h2, route = _rwkv_out(xs, mods3, mod_spec, y0, y1, r, v, km, gate, i, j, p)
        last = i == depth - 1
        xs = _moe(xs, h2, route, mods3, mod_spec, i, p, final=last, t_out=t_lat if last else xs.shape[0])
    return xs.reshape(bsz, n_lat, d)
```

```python
import functools
import math

import jax
import jax.numpy as jnp
from jax import lax
from jax.experimental import pallas as pl
from jax.experimental.pallas import tpu as pltpu

F32 = jnp.float32
BF16 = jnp.bfloat16

N_MOD = 6
NORM_EPS = 1e-6
GRID_W = 64
CHUNK = 128
GMLP_GROUP_CH = 128
LN_EPS = 1e-5
RWKV_HEAD = 64
GN_EPS = 64e-5
N_GROUPS = 4
EXPERTS_PER_GROUP = 8
N_EXPERTS = N_GROUPS * EXPERTS_PER_GROUP
TOP_K = 2

LANES = 128
TM = 256
MOE_ROWS = 512
SCAN_BLOCK = 256
SCAN_CHUNK = 32
SCAN_LANES = 256
SCAN_GROUPS = 2
VMEM_LIMIT = 56 * 1024 * 1024
NEG_BIG = -3.0e38
SQRT_HALF = 0.7071067811865476
EXP_MINUS_HALF = 0.6065306597126334


def _dotb(a, b):
    return jnp.dot(a.astype(BF16), b.astype(BF16), preferred_element_type=F32)


def _dot_nt(a, b):
    return lax.dot_general(a.astype(BF16), b.astype(BF16), (((1,), (1,)), ((), ())), preferred_element_type=F32)


def _dot_tn(a, b):
    return lax.dot_general(a.astype(BF16), b.astype(BF16), (((0,), (0,)), ((), ())), preferred_element_type=F32)


def _split2(x):
    hi = x.astype(BF16)
    lo = (x - hi.astype(F32)).astype(BF16)
    return hi, lo


def _split3(x):
    hi = x.astype(BF16)
    r1 = x - hi.astype(F32)
    mid = r1.astype(BF16)
    lo = (r1 - mid.astype(F32)).astype(BF16)
    return hi, mid, lo


def _dot_x_sel(x, sel_bf16, terms=3):
    d = lambda p: jnp.dot(p, sel_bf16, preferred_element_type=F32)
    return sum(d(p) for p in (_split3(x) if terms == 3 else _split2(x)))


def _norm_mod(x, g, sc, sh):
    y = x * lax.rsqrt(jnp.mean(x * x, axis=-1, keepdims=True) + NORM_EPS)
    return (y * g) * (1.0 + sc) + sh


def _mod_parts(mod_ref):
    m = mod_ref[0]
    d = m.shape[-1] // N_MOD
    return [m[:, i * d:(i + 1) * d] for i in range(N_MOD)]


def _mods_kernel(s_ref, w_ref, b_ref, o_ref):
    s = s_ref[...]
    s = s * jax.nn.sigmoid(s)
    o_ref[0] = _dotb(s, w_ref[0]) + b_ref[0]


def _mods(s_rows, ada_w, ada_b):
    depth, d, nd = ada_w.shape
    tn = nd // 4
    return pl.pallas_call(
        _mods_kernel,
        grid=(depth, nd // tn),
        in_specs=[pl.BlockSpec(s_rows.shape, lambda i, j: (0, 0)),
                  pl.BlockSpec((1, d, tn), lambda i, j: (i, 0, j)),
                  pl.BlockSpec((1, 1, tn), lambda i, j: (i, 0, j))],
        out_specs=pl.BlockSpec((1, s_rows.shape[0], tn), lambda i, j: (i, 0, j)),
        out_shape=jax.ShapeDtypeStruct((depth, s_rows.shape[0], nd), F32),
        compiler_params=pltpu.CompilerParams(dimension_semantics=("parallel", "parallel"),
                                             vmem_limit_bytes=VMEM_LIMIT),
        name="mods",
    )(s_rows, ada_w, ada_b.reshape(depth, 1, nd))


def _route(logits):
    lane = lax.broadcasted_iota(jnp.int32, logits.shape, 1)
    lane_f = lane.astype(F32)
    big = jnp.float32(1e9)
    is_g = lane < N_GROUPS
    gl = jnp.where(is_g, logits, NEG_BIG)
    gm = jnp.max(gl, axis=1, keepdims=True)
    grp = jnp.min(jnp.where(is_g & (gl == gm), lane_f, big), axis=1, keepdims=True)
    p_grp = 1.0 / jnp.sum(jnp.where(is_g, jnp.exp(gl - gm), 0.0), axis=1, keepdims=True)
    lo = N_GROUPS + grp * EXPERTS_PER_GROUP
    in_grp = (lane_f >= lo) & (lane_f < lo + EXPERTS_PER_GROUP)
    el = jnp.where(in_grp, logits, NEG_BIG)
    m1 = jnp.max(el, axis=1, keepdims=True)
    i1 = jnp.min(jnp.where(in_grp & (el == m1), lane_f, big), axis=1, keepdims=True)
    rest = in_grp & (lane_f != i1)
    el2 = jnp.where(rest, logits, NEG_BIG)
    m2 = jnp.max(el2, axis=1, keepdims=True)
    i2 = jnp.min(jnp.where(rest & (el2 == m2), lane_f, big), axis=1, keepdims=True)
    e21 = jnp.exp(m2 - m1)
    s0 = 1.0 / (1.0 + e21)
    w0 = p_grp * s0
    w1 = p_grp * (e21 * s0)
    out = jnp.where(lane == 0, i1 - N_GROUPS,
                    jnp.where(lane == 1, i2 - N_GROUPS,
                              jnp.where(lane == 2, w0, jnp.where(lane == 3, w1, 0.0))))
    return out


def _residual_and_route(x, y, mods, g2_ref, wrt_ref, brt_ref, xo_ref, h2_ref, route_ref):
    _, _, gt1, sh2, sc2, _ = mods
    xn = x + gt1 * y
    xo_ref[...] = xn
    h2 = _norm_mod(xn, g2_ref[...], sc2, sh2)
    h2_ref[...] = h2
    logits = _dotb(h2, wrt_ref[...]) + brt_ref[...]
    route_ref[...] = _route(logits)


def _gmlp_kernel(x_ref, mod_ref, g1_ref, g2_ref, win_ref, bin_ref, lng_ref, lnb_ref, ws_ref, bs_ref,
                 wout_ref, bout_ref, wrt_ref, brt_ref, xo_ref, h2_ref, route_ref, gated_ref):
    x = x_ref[...]
    mods = _mod_parts(mod_ref)
    sh1, sc1 = mods[0], mods[1]
    h = _norm_mod(x, g1_ref[...], sc1, sh1)
    z = _dotb(h, win_ref[...]) + bin_ref[...]
    z = 0.5 * z * (1.0 + lax.erf(z * SQRT_HALF))
    width = z.shape[1] // 2
    u = z[:, :width]
    v = z[:, width:]
    mu = jnp.mean(v, axis=-1, keepdims=True)
    vc = v - mu
    var = jnp.mean(vc * vc, axis=-1, keepdims=True)
    vn = ((vc * lax.rsqrt(var + LN_EPS)) * lng_ref[...] + lnb_ref[...]).astype(BF16)
    n_chunks = x.shape[0] // CHUNK
    for g in range(width // GMLP_GROUP_CH):
        cs = slice(g * GMLP_GROUP_CH, (g + 1) * GMLP_GROUP_CH)
        rhs = jnp.concatenate([vn[c * CHUNK:(c + 1) * CHUNK, cs] for c in range(n_chunks)], axis=1)
        s = jnp.dot(ws_ref[g], rhs, preferred_element_type=F32) + bs_ref[g]
        for c in range(n_chunks):
            rsl = slice(c * CHUNK, (c + 1) * CHUNK)
            gated_ref[rsl, cs] = (u[rsl, cs] * s[:, c * GMLP_GROUP_CH:(c + 1) * GMLP_GROUP_CH]).astype(BF16)
    y = jnp.dot(gated_ref[...], wout_ref[...], preferred_element_type=F32) + bout_ref[...]
    _residual_and_route(x, y, mods, g2_ref, wrt_ref, brt_ref, xo_ref, h2_ref, route_ref)


def _full_spec(shape):
    nd = len(shape)
    return pl.BlockSpec(shape, lambda *_: (0,) * nd)


def _layer_spec(shape, j):
    nd = len(shape)
    return pl.BlockSpec((None,) + tuple(shape[1:]), lambda *_: (j,) + (0,) * (nd - 1))


def _mixer_out_shapes(t_tok, d):
    return (jax.ShapeDtypeStruct((t_tok, d), F32), jax.ShapeDtypeStruct((t_tok, d), F32),
            jax.ShapeDtypeStruct((t_tok, LANES), F32))


def _mixer_out_specs(d):
    return (pl.BlockSpec((TM, d), lambda t: (t, 0)), pl.BlockSpec((TM, d), lambda t: (t, 0)),
            pl.BlockSpec((TM, LANES), lambda t: (t, 0)))


def _gmlp_layer(x, mods3, mod_spec, i, j, p):
    t_tok, d = x.shape
    width = p["ga_w_out"].shape[1]
    row = lambda a: a.reshape(a.shape[0], 1, a.shape[1])
    args = (x, mods3, row(p["norm1_g"]), row(p["norm2_g"]), p["ga_w_in"], row(p["ga_b_in"]), row(p["ga_ln_g"]),
            row(p["ga_ln_b"]), p["ga_w_s"], p["ga_b_s"], p["ga_w_out"], row(p["ga_b_out"]), p["wrt"], p["brt"])
    in_specs = [pl.BlockSpec((TM, d), lambda t: (t, 0)), mod_spec,
                _layer_spec(args[2].shape, i), _layer_spec(args[3].shape, i),
                _layer_spec(args[4].shape, j), _layer_spec(args[5].shape, j), _layer_spec(args[6].shape, j),
                _layer_spec(args[7].shape, j), _layer_spec(args[8].shape, j), _layer_spec(args[9].shape, j),
                _layer_spec(args[10].shape, j), _layer_spec(args[11].shape, j),
                _layer_spec(args[12].shape, i), _layer_spec(args[13].shape, i)]
    return pl.pallas_call(
        _gmlp_kernel,
        grid=(t_tok // TM,),
        in_specs=in_specs,
        out_specs=_mixer_out_specs(d),
        out_shape=_mixer_out_shapes(t_tok, d),
        scratch_shapes=[pltpu.VMEM((TM, width), BF16)],
        compiler_params=pltpu.CompilerParams(dimension_semantics=("parallel",), vmem_limit_bytes=VMEM_LIMIT),
        name="gmlp_layer",
    )(*args)


def _seg_sum(q, e_ref):
    return _dot_x_sel(q, e_ref[...], terms=2)


def _seg_expand(s, et_ref):
    return _dot_x_sel(s, et_ref[...], terms=2)


def _shifted(h, h_above, h_below, is_ctx, first_row_tile, last_row_tile):
    tm, d = h.shape
    q = d // 4
    row = lax.broadcasted_iota(jnp.int32, (tm, 1), 0)
    ctx_i = is_ctx.astype(jnp.int32)
    col = row & (jnp.where(is_ctx, tm, GRID_W) - 1)
    last_col = jnp.where(is_ctx, tm, GRID_W) - 1
    prev1 = jnp.where(col == 0, 0.0, pltpu.roll(h, 1, axis=0))
    next1 = jnp.where(col == last_col, 0.0, pltpu.roll(h, tm - 1, axis=0))
    up = jnp.concatenate([h_above, h[:tm - GRID_W]], axis=0)
    up = jnp.where(row < jnp.where(first_row_tile, GRID_W, 0), 0.0, up)
    down = jnp.concatenate([h[GRID_W:], h_below], axis=0)
    down = jnp.where(row >= jnp.where(last_row_tile, tm - GRID_W, tm), 0.0, down)
    lane_q = lax.broadcasted_iota(jnp.int32, (1, d), 1) >> int(math.log2(q))
    src = lane_q * (1 - ctx_i) + (lane_q >> 1) * ctx_i
    return jnp.where(src == 0, prev1, jnp.where(src == 1, next1, jnp.where(src == 2, up, down)))


def _rwkv_prep_kernel(x_ref, xa_ref, xb_ref, mod_ref, n1_ref, mu_ref, wr_ref, wk_ref, wv_ref, w1_ref, w2_ref, w0_ref,
                      a1_ref, a2_ref, a0_ref, g1_ref, g2_ref, kk_ref, ka_ref, e_ref, et_ref,
                      r_o, v_o, kkn_o, gate_o, lw_o, km_o, bb_o, *, lat_tiles, tiles_per_batch):
    t = pl.program_id(0)
    mods = _mod_parts(mod_ref)
    norm = lambda x: _norm_mod(x, n1_ref[...], mods[1], mods[0])
    h = norm(x_ref[...])
    tb = t % tiles_per_batch
    hs = _shifted(h, norm(xa_ref[...]), norm(xb_ref[...]), t >= lat_tiles, tb == 0, tb == tiles_per_batch - 1)
    xx = hs - h
    mu = mu_ref[...]
    xr, xw, xk, xv, xa, xg = [h + xx * mu[q:q + 1] for q in range(6)]
    r = _dotb(xr, wr_ref[...])
    k = _dotb(xk, wk_ref[...])
    v = _dotb(xv, wv_ref[...])
    r_o[...] = r
    v_o[...] = v
    kkr = k * kk_ref[...]
    rs = lax.rsqrt(_seg_sum(kkr * kkr, e_ref) + 1e-12)
    kkn = kkr * _seg_expand(rs, et_ref)
    kkn_o[...] = kkn
    gate_o[...] = _dotb(jax.nn.sigmoid(_dotb(xg, g1_ref[...])), g2_ref[...])
    tw = jnp.tanh(_dotb(xw, w1_ref[...]))
    ta = _dotb(xa, a1_ref[...])
    lane = lax.broadcasted_iota(jnp.int32, tw.shape, 1)
    half = tw.shape[1] // 2
    w0 = w0_ref[...]
    a0 = a0_ref[...]
    ka = ka_ref[...]
    for z in range(2):
        zm = (lane >= z * half) & (lane < (z + 1) * half)
        lwz = _dotb(jnp.where(zm, tw, 0.0), w2_ref[...])
        lw_o[z] = -EXP_MINUS_HALF * jax.nn.sigmoid(w0[z:z + 1] + lwz)
        la = _dotb(jnp.where(zm, ta, 0.0), a2_ref[...])
        a = jax.nn.sigmoid(a0[z:z + 1] + la)
        km_o[z] = k * (1.0 + (a - 1.0) * ka)
        bb_o[z] = kkn * a


def _rwkv_prep(x, mods3, mod_spec, i, j, p, lat_tiles, tiles_per_batch):
    t_tok, d = x.shape
    row = lambda a: a.reshape(a.shape[0], 1, a.shape[1])
    args = (x, x, x, mods3, row(p["norm1_g"]), p["rw_mu8"], p["rw_wr"], p["rw_wk"], p["rw_wv"], p["rw_w1c"],
            p["rw_w2c"], p["rw_w0"], p["rw_a1c"], p["rw_a2c"], p["rw_a0"], p["rw_g1p"], p["rw_g2p"],
            row(p["rw_k_k"]), row(p["rw_k_a"]), p["head_sel"], p["head_sel_t"])
    tok = pl.BlockSpec((TM, d), lambda t: (t, 0))
    rows_per_tile = TM // GRID_W
    last_row = t_tok // GRID_W - 1
    above = pl.BlockSpec((GRID_W, d), lambda t: (jnp.maximum(t * rows_per_tile - 1, 0), 0))
    below = pl.BlockSpec((GRID_W, d), lambda t: (jnp.minimum((t + 1) * rows_per_tile, last_row), 0))
    in_specs = ([tok, above, below, mod_spec, _layer_spec(args[4].shape, i)]
                + [_layer_spec(a.shape, j) for a in args[5:19]]
                + [_full_spec(args[19].shape), _full_spec(args[20].shape)])
    dir_spec = pl.BlockSpec((2, TM, d), lambda t: (0, t, 0))
    tok_shape = jax.ShapeDtypeStruct((t_tok, d), F32)
    dir_shape = jax.ShapeDtypeStruct((2, t_tok, d), F32)
    return pl.pallas_call(
        functools.partial(_rwkv_prep_kernel, lat_tiles=lat_tiles, tiles_per_batch=tiles_per_batch),
        grid=(t_tok // TM,),
        in_specs=in_specs,
        out_specs=(tok, tok, tok, tok, dir_spec, dir_spec, dir_spec),
        out_shape=(tok_shape, tok_shape, tok_shape, tok_shape, dir_shape, dir_shape, dir_shape),
        compiler_params=pltpu.CompilerParams(dimension_semantics=("parallel",), vmem_limit_bytes=VMEM_LIMIT),
        name="rwkv_prep",
    )(*args)


def _scan_prepare(r, v, kk, lw, km, bb, rev, out):
    c = SCAN_CHUNK
    nb = SCAN_BLOCK
    a = -kk
    row = lax.broadcasted_iota(jnp.int32, (nb, nb), 0)
    col = lax.broadcasted_iota(jnp.int32, (nb, nb), 1)
    shift_c = int(math.log2(c))
    same = (row >> shift_c) == (col >> shift_c)
    if rev:
        incl = same & (col >= row)
        strict = same & (col > row)
    else:
        incl = same & (col <= row)
        strict = same & (col < row)
    incl_b = jnp.where(incl, 1.0, 0.0).astype(BF16)
    eye = jnp.where(row == col, 1.0, 0.0)
    head_of_lane = col >> int(math.log2(RWKV_HEAD))

    lw_hi, lw_lo = _split2(lw)
    cum = (jnp.dot(incl_b, lw_hi, preferred_element_type=F32)
           + jnp.dot(incl_b, lw_lo, preferred_element_type=F32))
    yield
    last = [ci * c if rev else ci * c + c - 1 for ci in range(nb // c)]
    tot_rows = [cum[t:t + 1] for t in last]
    tot = jnp.concatenate([jnp.broadcast_to(t, (c, t.shape[1])) for t in tot_rows], axis=0)
    yield
    at = a * jnp.exp(cum - lw)
    rt = r * jnp.exp(cum)
    e_inv = jnp.exp(-cum)
    bt = bb * e_inv
    kt = km * e_inv
    e_rem = jnp.exp(tot - cum)
    bh = (bb * e_rem).astype(BF16)
    kh = (km * e_rem).astype(BF16)
    et = jnp.exp(jnp.concatenate(tot_rows, axis=0))
    bk = jnp.concatenate([bt, kt], axis=0).astype(BF16)
    yield
    n_heads = SCAN_LANES // RWKV_HEAD
    heads = range(n_heads)
    hms = [head_of_lane == hh for hh in heads]
    at_h = [jnp.where(hm, at, 0.0) for hm in hms]
    rt_h = [jnp.where(hm, rt, 0.0) for hm in hms]
    v_h = [jnp.where(hm, v, 0.0).astype(BF16) for hm in hms]
    v_roll = pltpu.roll(v, RWKV_HEAD, axis=1)
    v_s = [jnp.where(hms[(hh + 1) % n_heads], v_roll, 0.0).astype(BF16) for hh in heads]
    a_ab, a_ak, m_rb, m_rk = [], [], [], []
    for hh in heads:
        s = _dot_nt(jnp.concatenate([at_h[hh], rt_h[hh]], axis=0), bk)
        a_ab.append(jnp.where(strict, s[:nb, :nb], 0.0))
        a_ak.append(jnp.where(strict, s[:nb, nb:], 0.0).astype(BF16))
        m_rb.append(jnp.where(incl, s[nb:, :nb], 0.0).astype(BF16))
        m_rk.append(jnp.where(incl, s[nb:, nb:], 0.0).astype(BF16))
        yield
    pw = list(a_ab)
    tinv = [eye + m for m in a_ab]
    for _ in range(shift_c - 1):
        for hh in heads:
            pw[hh] = _dotb(pw[hh], pw[hh])
        yield
        for hh in heads:
            tinv[hh] = tinv[hh] + _dotb(tinv[hh], pw[hh])
        yield
    x1 = [jnp.dot(a_ak[hh], v_s[hh], preferred_element_type=F32) for hh in heads]
    yield
    tc = [_dotb(tinv[hh], at_h[hh] + x1[hh]) for hh in heads]
    yield
    mc = [_dotb(m_rb[hh], tc[hh]) for hh in heads]
    yield

    def pick(parts, shift):
        res = parts[(n_heads - 1 - shift) % n_heads]
        for lb in range(n_heads - 2, -1, -1):
            res = jnp.where(hms[lb], parts[(lb - shift) % n_heads], res)
        return res

    back = SCAN_LANES - RWKV_HEAD
    ah = pick(tc, 0)
    vh = pltpu.roll(pick(tc, 1), back, axis=1)
    rh = rt + pick(mc, 0)
    yh = (jnp.dot(jnp.concatenate(m_rk, axis=1), jnp.concatenate(v_h, axis=0), preferred_element_type=F32)
          + pltpu.roll(pick(mc, 1), back, axis=1))
    yield
    out.extend([ah.astype(BF16), rh.astype(BF16), vh, yh, bh, kh, v.astype(BF16), et])


def _scan_prepare_stages():
    return 7 + (SCAN_LANES // RWKV_HEAD) + 2 * (int(math.log2(SCAN_CHUNK)) - 1)


def _scan_chain(g, ops, y_ref, lanes, rev, out):
    c = SCAN_CHUNK
    ah, rh, vh, yh, bh, kh, vb, et = ops
    n = g.shape[0]
    shift_h = int(math.log2(RWKV_HEAD))
    blockdiag = ((lax.broadcasted_iota(jnp.int32, (n, n), 0) >> shift_h)
                 == (lax.broadcasted_iota(jnp.int32, (n, n), 1) >> shift_h))
    n_chunks = SCAN_BLOCK // c
    order = range(n_chunks - 1, -1, -1) if rev else range(n_chunks)
    for ci in order:
        rs = slice(ci * c, (ci + 1) * c)
        p = _dot_nt(jnp.concatenate([ah[rs], rh[rs]], axis=0), g)
        u = p[:c] + vh[rs]
        y_ref[rs, lanes] = p[c:] + yh[rs]
        yield
        dg = _dot_tn(jnp.concatenate([u.astype(BF16), vb[rs]], axis=0), jnp.concatenate([bh[rs], kh[rs]], axis=0))
        g = jnp.where(blockdiag, g * et[ci:ci + 1, :] + dg, 0.0)
        yield
    out.append(g)


def _scan_chain_stages():
    return 2 * (SCAN_BLOCK // SCAN_CHUNK)


def _interleave(gen_a, n_a, gen_b, n_b):
    done_b = 0
    for i in range(n_a):
        next(gen_a, None)
        want_b = ((i + 1) * n_b) // n_a
        while done_b < want_b:
            next(gen_b, None)
            done_b += 1
        yield
    for g in (gen_a, gen_b):
        for _ in g:
            pass


def _scan_kernel(r_ref, v_ref, kk_ref, lw_ref, km_ref, bb_ref, y_ref, g_scr, *ops_scr, rev, groups):
    @pl.when(pl.program_id(2) == 0)
    def _():
        g_scr[...] = jnp.zeros_like(g_scr)
        for ref in ops_scr:
            ref[...] = jnp.zeros_like(ref)

    slot = pl.program_id(2) % 2
    work, results = [], []
    for q in range(groups):
        lanes = slice(q * SCAN_LANES, (q + 1) * SCAN_LANES)
        prev = [ref.at[1 - slot, :, lanes] for ref in ops_scr]
        new, g_end = [], []
        prep = _scan_prepare(r_ref[:, lanes], v_ref[:, lanes], kk_ref[:, lanes], lw_ref[:, lanes], km_ref[:, lanes],
                             bb_ref[:, lanes], rev, new)
        chain = _scan_chain(g_scr[q], prev, y_ref, lanes, rev, g_end)
        work.append(_interleave(prep, _scan_prepare_stages(), chain, _scan_chain_stages()))
        results.append((lanes, new, g_end))
    while work:
        work = [w for w in work if next(w, StopIteration) is not StopIteration]
    for q, (lanes, new, g_end) in enumerate(results):
        g_scr[q] = g_end[0]
        for ref, val in zip(ops_scr, new):
            ref[slot, :, lanes] = val


def _wkv_scan(r, v, kkn, lw, km, bb, z, bsz, n_lat, n_ctx):
    t_tok, d = r.shape
    lat_blocks = n_lat // SCAN_BLOCK
    ctx_blocks = n_ctx // SCAN_BLOCK
    steps = ctx_blocks + lat_blocks
    ctx_base = bsz * lat_blocks
    rev = z == 1
    width = SCAN_LANES * SCAN_GROUPS

    def blk(b, s):
        if rev:
            cb = ctx_base + b * ctx_blocks + (ctx_blocks - 1 - s)
            lb = b * lat_blocks + (lat_blocks - 1 - (s - ctx_blocks))
        else:
            cb = ctx_base + b * ctx_blocks + s
            lb = b * lat_blocks + (s - ctx_blocks)
        return jnp.where(s < ctx_blocks, cb, lb)

    in_blk = lambda b, s: blk(b, jnp.minimum(s, steps - 1))
    out_blk = lambda b, s: blk(b, jnp.maximum(s - 1, 0))
    tok = pl.BlockSpec((SCAN_BLOCK, width), lambda b, g, s: (in_blk(b, s), g))
    dirs = pl.BlockSpec((None, SCAN_BLOCK, width), lambda b, g, s: (z, in_blk(b, s), g))
    n_chunks = SCAN_BLOCK // SCAN_CHUNK
    ops_scratch = [pltpu.VMEM((2, SCAN_BLOCK, width), BF16), pltpu.VMEM((2, SCAN_BLOCK, width), BF16),
                   pltpu.VMEM((2, SCAN_BLOCK, width), F32), pltpu.VMEM((2, SCAN_BLOCK, width), F32),
                   pltpu.VMEM((2, SCAN_BLOCK, width), BF16), pltpu.VMEM((2, SCAN_BLOCK, width), BF16),
                   pltpu.VMEM((2, SCAN_BLOCK, width), BF16), pltpu.VMEM((2, n_chunks, width), F32)]
    return pl.pallas_call(
        functools.partial(_scan_kernel, rev=rev, groups=SCAN_GROUPS),
        grid=(bsz, d // width, steps + 1),
        in_specs=[tok, tok, tok, dirs, dirs, dirs],
        out_specs=pl.BlockSpec((SCAN_BLOCK, width), lambda b, g, s: (out_blk(b, s), g)),
        out_shape=jax.ShapeDtypeStruct((t_tok, d), F32),
        scratch_shapes=[pltpu.VMEM((SCAN_GROUPS, SCAN_LANES, SCAN_LANES), F32)] + ops_scratch,
        compiler_params=pltpu.CompilerParams(dimension_semantics=("parallel", "parallel", "arbitrary"),
                                             vmem_limit_bytes=VMEM_LIMIT),
        name="wkv_scan_rev" if rev else "wkv_scan_fwd",
    )(r, v, kkn, lw, km, bb)


def _rwkv_out_kernel(x_ref, mod_ref, y0_ref, y1_ref, r_ref, v_ref, km_ref, gate_ref, lng_ref, lnb_ref, rk_ref,
                     wo_ref, e_ref, et_ref, g2_ref, wrt_ref, brt_ref, xo_ref, h2_ref, route_ref):
    x = x_ref[...]
    mods = _mod_parts(mod_ref)
    y = y0_ref[...] + y1_ref[...]
    inv_n = 1.0 / RWKV_HEAD
    mu = _seg_expand(_seg_sum(y, e_ref) * inv_n, et_ref)
    yc = y - mu
    var = _seg_sum(yc * yc, e_ref) * inv_n
    yn = (yc * _seg_expand(lax.rsqrt(var + GN_EPS), et_ref)) * lng_ref[...] + lnb_ref[...]
    v = v_ref[...]
    bonus = _seg_sum((r_ref[...] * rk_ref[...]) * (km_ref[0] + km_ref[1]), e_ref)
    yn = yn + _seg_expand(bonus, et_ref) * v
    out = _dotb(yn * gate_ref[...], wo_ref[...])
    _residual_and_route(x, out, mods, g2_ref, wrt_ref, brt_ref, xo_ref, h2_ref, route_ref)


def _rwkv_out(x, mods3, mod_spec, y0, y1, r, v, km, gate, i, j, p):
    t_tok, d = x.shape
    row = lambda a: a.reshape(a.shape[0], 1, a.shape[1])
    tok = pl.BlockSpec((TM, d), lambda t: (t, 0))
    dir_spec = pl.BlockSpec((2, TM, d), lambda t: (0, t, 0))
    args = (x, mods3, y0, y1, r, v, km, gate, row(p["rw_ln_g"]), row(p["rw_ln_b"]), row(p["rw_r_k"].reshape(-1, d)),
            p["rw_wo"], p["head_sel"], p["head_sel_t"], row(p["norm2_g"]), p["wrt"], p["brt"])
    in_specs = [tok, mod_spec, tok, tok, tok, tok, dir_spec, tok,
                _layer_spec(args[8].shape, j), _layer_spec(args[9].shape, j), _layer_spec(args[10].shape, j),
                _layer_spec(args[11].shape, j), _full_spec(args[12].shape), _full_spec(args[13].shape),
                _layer_spec(args[14].shape, i), _layer_spec(args[15].shape, i), _layer_spec(args[16].shape, i)]
    return pl.pallas_call(
        _rwkv_out_kernel,
        grid=(t_tok // TM,),
        in_specs=in_specs,
        out_specs=_mixer_out_specs(d),
        out_shape=_mixer_out_shapes(t_tok, d),
        compiler_params=pltpu.CompilerParams(dimension_semantics=("parallel",), vmem_limit_bytes=VMEM_LIMIT),
        name="rwkv_out",
    )(*args)


def _expert_kernel(be_ref, nu_ref, x_ref, wg_ref, wu_ref, wd_ref, o_ref, wg_s, wu_s, wd_s):
    b = pl.program_id(0)
    e = be_ref[b]
    prev = be_ref[jnp.maximum(b - 1, 0)]
    used = b < nu_ref[0]

    @pl.when(used & ((b == 0) | (e != prev)))
    def _():
        wg_s[...] = wg_ref[...].astype(BF16)
        wu_s[...] = wu_ref[...].astype(BF16)
        wd_s[...] = wd_ref[...].astype(BF16)

    @pl.when(used)
    def _():
        xb = x_ref[...].astype(BF16)
        g = jnp.dot(xb, wg_s[...], preferred_element_type=F32)
        u = jnp.dot(xb, wu_s[...], preferred_element_type=F32)
        hmid = (g * jax.nn.sigmoid(g)) * u
        o_ref[...] = jnp.dot(hmid.astype(BF16), wd_s[...], preferred_element_type=F32)

    @pl.when(jnp.logical_not(used))
    def _():
        o_ref[...] = jnp.zeros_like(o_ref)


def _experts(buf, block_expert, n_used, w_gate, w_up, w_down, i):
    cap, d = buf.shape
    f = w_gate.shape[-1]
    n_blocks = cap // MOE_ROWS
    grid_spec = pltpu.PrefetchScalarGridSpec(
        num_scalar_prefetch=2,
        grid=(n_blocks,),
        in_specs=[pl.BlockSpec((MOE_ROWS, d), lambda b, be, nu: (b, 0)),
                  pl.BlockSpec((None, None, d, f), lambda b, be, nu: (i, be[b], 0, 0)),
                  pl.BlockSpec((None, None, d, f), lambda b, be, nu: (i, be[b], 0, 0)),
                  pl.BlockSpec((None, None, f, d), lambda b, be, nu: (i, be[b], 0, 0))],
        out_specs=pl.BlockSpec((MOE_ROWS, d), lambda b, be, nu: (b, 0)),
        scratch_shapes=[pltpu.VMEM((d, f), BF16), pltpu.VMEM((d, f), BF16), pltpu.VMEM((f, d), BF16)],
    )
    return pl.pallas_call(
        _expert_kernel,
        grid_spec=grid_spec,
        out_shape=jax.ShapeDtypeStruct((cap, d), F32),
        compiler_params=pltpu.CompilerParams(dimension_semantics=("arbitrary",), vmem_limit_bytes=VMEM_LIMIT),
        name="experts",
    )(block_expert, n_used, buf, w_gate, w_up, w_down)


def _combine_kernel(x_ref, mod_ref, route_ref, y0_ref, y1_ref, fg_ref, xo_ref, *, final):
    gt2 = _mod_parts(mod_ref)[5]
    route = route_ref[...]
    xn = x_ref[...] + gt2 * (route[:, 2:3] * y0_ref[...] + route[:, 3:4] * y1_ref[...])
    if final:
        xn = (xn * lax.rsqrt(jnp.mean(xn * xn, axis=-1, keepdims=True) + NORM_EPS)) * fg_ref[...]
    xo_ref[...] = xn


def _combine(x, mods3, mod_spec, route, y0, y1, final_g, final, t_out):
    d = x.shape[1]
    tok = pl.BlockSpec((TM, d), lambda t: (t, 0))
    return pl.pallas_call(
        functools.partial(_combine_kernel, final=final),
        grid=(t_out // TM,),
        in_specs=[tok, mod_spec, pl.BlockSpec((TM, LANES), lambda t: (t, 0)), tok, tok, _full_spec((1, d))],
        out_specs=tok,
        out_shape=jax.ShapeDtypeStruct((t_out, d), F32),
        compiler_params=pltpu.CompilerParams(dimension_semantics=("parallel",), vmem_limit_bytes=VMEM_LIMIT),
        name="moe_combine",
    )(x, mods3, route, y0, y1, final_g.reshape(1, d))


def _dest_kernel(route_ref, d_ref, counts_ref, base_scr, start_scr):
    phase = pl.program_id(0)
    t = pl.program_id(1)
    route = route_ref[...]
    tm = route.shape[0]
    lane = lax.broadcasted_iota(jnp.int32, route.shape, 1)
    lane_f = lane.astype(F32)
    onehots = [jnp.where(lane_f == route[:, k:k + 1], 1.0, 0.0) for k in range(TOP_K)]

    @pl.when((phase == 0) & (t == 0))
    def _():
        base_scr[...] = jnp.zeros_like(base_scr)

    @pl.when((phase == 1) & (t == 0))
    def _():
        counts = base_scr[...]
        counts_ref[...] = counts
        padded = jnp.floor((counts + (MOE_ROWS - 1)) * (1.0 / MOE_ROWS)) * MOE_ROWS
        before = (lax.broadcasted_iota(jnp.int32, (LANES, LANES), 0)
                  < lax.broadcasted_iota(jnp.int32, (LANES, LANES), 1))
        start_scr[...] = _dot_x_sel(padded, jnp.where(before, 1.0, 0.0).astype(BF16))
        base_scr[...] = jnp.zeros_like(base_scr)

    @pl.when(phase == 0)
    def _():
        base_scr[...] = base_scr[...] + sum(jnp.sum(oh, axis=0, keepdims=True) for oh in onehots)

    @pl.when(phase == 1)
    def _():
        earlier = (lax.broadcasted_iota(jnp.int32, (tm, tm), 1)
                   < lax.broadcasted_iota(jnp.int32, (tm, tm), 0))
        earlier_b = jnp.where(earlier, 1.0, 0.0).astype(BF16)
        diag = (lax.broadcasted_iota(jnp.int32, (LANES, LANES), 0)
                == lax.broadcasted_iota(jnp.int32, (LANES, LANES), 1))
        base = base_scr[...] + start_scr[...]
        for k, oh in enumerate(onehots):
            before = jnp.dot(earlier_b, oh.astype(BF16), preferred_element_type=F32)
            dest = jnp.sum(oh * (base + before), axis=1, keepdims=True)
            rows = [jnp.sum(jnp.where(diag, jnp.broadcast_to(dest[i * LANES:(i + 1) * LANES], (LANES, LANES)), 0.0),
                            axis=0, keepdims=True) for i in range(tm // LANES)]
            d_ref[k, 0] = jnp.concatenate(rows, axis=0).astype(jnp.int32)
            base = base + jnp.sum(oh, axis=0, keepdims=True)
        base_scr[...] = base - start_scr[...]


def _dests(route):
    t_tok = route.shape[0]
    tiles = t_tok // TM
    return pl.pallas_call(
        _dest_kernel,
        grid=(2, tiles),
        in_specs=[pl.BlockSpec((TM, LANES), lambda ph, t: (t, 0))],
        out_specs=(pl.BlockSpec((TOP_K, 1, TM // LANES, LANES), lambda ph, t: (0, t * ph, 0, 0)),
                   pl.BlockSpec((1, LANES), lambda ph, t: (0, 0))),
        out_shape=(jax.ShapeDtypeStruct((TOP_K, tiles, TM // LANES, LANES), jnp.int32),
                   jax.ShapeDtypeStruct((1, LANES), F32)),
        scratch_shapes=[pltpu.VMEM((1, LANES), F32), pltpu.VMEM((1, LANES), F32)],
        compiler_params=pltpu.CompilerParams(dimension_semantics=("arbitrary", "arbitrary"),
                                             vmem_limit_bytes=VMEM_LIMIT),
        name="moe_dests",
    )(route)


def _slot_token_kernel(dest_ref, out_ref, *, t_tok, fill_mask):
    def fill(s, carry):
        out_ref[s] = s & fill_mask
        return carry

    lax.fori_loop(0, out_ref.shape[0], fill, 0, unroll=8)

    def place(a, carry):
        out_ref[dest_ref[a]] = a - jnp.where(a >= t_tok, t_tok, 0)
        return carry

    lax.fori_loop(0, dest_ref.shape[0], place, 0, unroll=8)


def _slot_tokens(dest_flat, t_tok, cap):
    assert dest_flat.shape[0] == TOP_K * t_tok and TOP_K == 2 and cap % 8 == 0 and dest_flat.shape[0] % 8 == 0
    fill_mask = (1 << (t_tok.bit_length() - 1)) - 1
    return pl.pallas_call(
        functools.partial(_slot_token_kernel, t_tok=t_tok, fill_mask=fill_mask),
        grid_spec=pltpu.PrefetchScalarGridSpec(num_scalar_prefetch=1, grid=(1,), in_specs=[],
                                               out_specs=pl.BlockSpec(memory_space=pltpu.SMEM)),
        out_shape=jax.ShapeDtypeStruct((cap,), jnp.int32),
        name="moe_slot_tokens",
    )(dest_flat)


def _moe(x, h2, route, mods3, mod_spec, i, p, final, t_out):
    t_tok, d = x.shape
    n_assign = t_tok * TOP_K
    dest4, counts_f = _dests(route)
    dest = dest4.reshape(TOP_K, t_tok)
    counts = counts_f[0, :N_EXPERTS].astype(jnp.int32)
    pad_end = jnp.cumsum(((counts + MOE_ROWS - 1) // MOE_ROWS) * MOE_ROWS)
    n_blocks = (n_assign + MOE_ROWS - 1) // MOE_ROWS + N_EXPERTS
    slot_token = _slot_tokens(dest.reshape(-1), t_tok, n_blocks * MOE_ROWS)
    block_start = jnp.arange(n_blocks, dtype=jnp.int32) * MOE_ROWS
    block_expert = jnp.minimum(jnp.sum((pad_end[None, :] <= block_start[:, None]).astype(jnp.int32), axis=1),
                               N_EXPERTS - 1)
    n_used = (pad_end[-1:] // MOE_ROWS).astype(jnp.int32)
    buf = h2.at[slot_token].get(mode="promise_in_bounds")
    out = _experts(buf, block_expert, n_used, p["moe_w_gate"], p["moe_w_up"], p["moe_w_down"], i)
    y0 = out.at[dest[0]].get(mode="promise_in_bounds", unique_indices=True)
    y1 = out.at[dest[1]].get(mode="promise_in_bounds", unique_indices=True)
    return _combine(x, mods3, mod_spec, route, y0, y1, p["final_g"], final, t_out)


def kernel(x, c, ctx, c_ctx, ada_w, ada_b, norm1_g, norm2_g, final_g, ga_w_in, ga_b_in, ga_ln_g, ga_ln_b, ga_w_s,
           ga_b_s, ga_w_out, ga_b_out, rw_mu, rw_wr, rw_wk, rw_wv, rw_wo, rw_w0, rw_w1, rw_w2, rw_a0, rw_a1, rw_a2,
           rw_g1, rw_g2, rw_k_k, rw_k_a, rw_r_k, rw_ln_g, rw_ln_b, moe_w_grp, moe_b_grp, moe_w_exp, moe_b_exp,
           moe_w_gate, moe_w_up, moe_w_down):
    bsz, n_lat, d = x.shape
    n_ctx = ctx.shape[1]
    depth = ada_w.shape[0]
    n_mixers = 2
    assert n_lat % TM == 0 and n_ctx == TM and TM % GRID_W == 0
    assert n_lat % SCAN_BLOCK == 0 and n_ctx % SCAN_BLOCK == 0
    assert bsz + 1 <= 8 and d % (SCAN_LANES * SCAN_GROUPS) == 0
    assert MOE_ROWS & (MOE_ROWS - 1) == 0
    t_lat = bsz * n_lat
    lat_tiles = t_lat // TM
    tiles_per_batch = n_lat // TM
    heads = d // RWKV_HEAD

    s_rows = jnp.concatenate([c, c_ctx[None, :], jnp.zeros((8 - bsz - 1, d), F32)], axis=0)
    mods = _mods(s_rows, ada_w, ada_b)
    mods3 = mods.reshape(depth * 8, 1, N_MOD * d)

    def mod_spec_for(i):
        def imap(t):
            return (i * 8 + jnp.where(t < lat_tiles, t // tiles_per_batch, bsz), 0, 0)
        return pl.BlockSpec((1, 1, N_MOD * d), imap)

    head_sel = (jnp.arange(d)[:, None] // RWKV_HEAD == jnp.arange(LANES)[None, :]).astype(BF16)
    zpad = lambda a, axis, n: jnp.pad(a, [(0, n - a.shape[k]) if k == axis else (0, 0) for k in range(a.ndim)])
    glora = ((rw_g1.shape[-1] + LANES - 1) // LANES) * LANES
    wrt = jnp.concatenate([moe_w_grp, moe_w_exp], axis=-1)
    brt = jnp.concatenate([moe_b_grp, moe_b_exp], axis=-1)
    p = dict(
        norm1_g=norm1_g, norm2_g=norm2_g, final_g=final_g,
        ga_w_in=ga_w_in.astype(BF16), ga_b_in=ga_b_in, ga_ln_g=ga_ln_g, ga_ln_b=ga_ln_b,
        ga_w_s=ga_w_s.astype(BF16), ga_b_s=ga_b_s[..., None], ga_w_out=ga_w_out.astype(BF16), ga_b_out=ga_b_out,
        rw_mu8=zpad(rw_mu, 1, 8),
        rw_wr=rw_wr.astype(BF16), rw_wk=rw_wk.astype(BF16), rw_wv=rw_wv.astype(BF16), rw_wo=rw_wo.astype(BF16),
        rw_w1c=jnp.concatenate([rw_w1[:, 0], rw_w1[:, 1]], axis=-1).astype(BF16),
        rw_w2c=jnp.concatenate([rw_w2[:, 0], rw_w2[:, 1]], axis=-2).astype(BF16),
        rw_a1c=jnp.concatenate([rw_a1[:, 0], rw_a1[:, 1]], axis=-1).astype(BF16),
        rw_a2c=jnp.concatenate([rw_a2[:, 0], rw_a2[:, 1]], axis=-2).astype(BF16),
        rw_w0=rw_w0, rw_a0=rw_a0,
        rw_g1p=zpad(rw_g1, 2, glora).astype(BF16), rw_g2p=zpad(rw_g2, 1, glora).astype(BF16),
        rw_k_k=rw_k_k, rw_k_a=rw_k_a, rw_r_k=rw_r_k, rw_ln_g=rw_ln_g, rw_ln_b=rw_ln_b,
        head_sel=head_sel, head_sel_t=head_sel.T,
        wrt=zpad(wrt, 2, LANES).astype(BF16), brt=zpad(brt, 1, LANES).reshape(depth, 1, LANES),
        moe_w_gate=moe_w_gate, moe_w_up=moe_w_up, moe_w_down=moe_w_down,
    )
    assert heads <= LANES and 2 * rw_w1.shape[-1] == LANES and 2 * rw_a1.shape[-1] == LANES

    xs = jnp.concatenate([x.reshape(t_lat, d), ctx.reshape(bsz * n_ctx, d)], axis=0)
    for i in range(depth):
        j = i // n_mixers
        mod_spec = mod_spec_for(i)
        if i % n_mixers == 0:
            xs, h2, route = _gmlp_layer(xs, mods3, mod_spec, i, j, p)
        else:
            r, v, kkn, gate, lw, km, bb = _rwkv_prep(xs, mods3, mod_spec, i, j, p, lat_tiles, tiles_per_batch)
            y0 = _wkv_scan(r, v, kkn, lw, km, bb, 0, bsz, n_lat, n_ctx)
            y1 = _wkv_scan(r, v, kkn, lw, km, bb, 1, bsz, n_lat, n_ctx)
            xs, h2, route = _rwkv_out(xs, mods3, mod_spec, y0, y1, r, v, km, gate, i, j, p)
        last = i == depth - 1
        xs = _moe(xs, h2, route, mods3, mod_spec, i, p, final=last, t_out=t_lat if last else xs.shape[0])
    return xs.reshape(bsz, n_lat, d)
```

```python
import functools
import math

import jax
import jax.numpy as jnp
from jax import lax
from jax.experimental import pallas as pl
from jax.experimental.pallas import tpu as pltpu

F32 = jnp.float32
BF16 = jnp.bfloat16

N_MOD = 6
NORM_EPS = 1e-6
GRID_W = 64
CHUNK = 128
GMLP_GROUP_CH = 128
LN_EPS = 1e-5
RWKV_HEAD = 64
GN_EPS = 64e-5
N_GROUPS = 4
EXPERTS_PER_GROUP = 8
N_EXPERTS = N_GROUPS * EXPERTS_PER_GROUP
TOP_K = 2

LANES = 128
TM = 256
GMLP_TM = 512
GMLP_SPLIT = 2
MOE_ROWS = 512
DEST_TM = 512
SCAN_BLOCK = 256
SCAN_CHUNK = 32
SCAN_LANES = 256
SCAN_GROUPS = 2
VMEM_LIMIT = 56 * 1024 * 1024
NEG_BIG = -3.0e38
SQRT_HALF = 0.7071067811865476
EXP_MINUS_HALF = 0.6065306597126334


def _dotb(a, b):
    return jnp.dot(a.astype(BF16), b.astype(BF16), preferred_element_type=F32)


def _dot_nt(a, b):
    return lax.dot_general(a.astype(BF16), b.astype(BF16), (((1,), (1,)), ((), ())), preferred_element_type=F32)


def _dot_tn(a, b):
    return lax.dot_general(a.astype(BF16), b.astype(BF16), (((0,), (0,)), ((), ())), preferred_element_type=F32)


def _split2(x):
    hi = x.astype(BF16)
    lo = (x - hi.astype(F32)).astype(BF16)
    return hi, lo


def _split3(x):
    hi = x.astype(BF16)
    r1 = x - hi.astype(F32)
    mid = r1.astype(BF16)
    lo = (r1 - mid.astype(F32)).astype(BF16)
    return hi, mid, lo


def _dot_x_sel(x, sel_bf16, terms=3):
    d = lambda p: jnp.dot(p, sel_bf16, preferred_element_type=F32)
    return sum(d(p) for p in (_split3(x) if terms == 3 else _split2(x)))


def _norm_mod(x, g, sc, sh):
    y = x * lax.rsqrt(jnp.mean(x * x, axis=-1, keepdims=True) + NORM_EPS)
    return (y * g) * (1.0 + sc) + sh


def _mod_parts(mod_ref):
    m = mod_ref[0]
    d = m.shape[-1] // N_MOD
    return [m[:, i * d:(i + 1) * d] for i in range(N_MOD)]


def _mods_kernel(s_ref, w_ref, b_ref, o_ref):
    s = s_ref[...]
    s = s * jax.nn.sigmoid(s)
    o_ref[0] = _dotb(s, w_ref[0]) + b_ref[0]


def _mods(s_rows, ada_w, ada_b):
    depth, d, nd = ada_w.shape
    tn = nd // 4
    return pl.pallas_call(
        _mods_kernel,
        grid=(depth, nd // tn),
        in_specs=[pl.BlockSpec(s_rows.shape, lambda i, j: (0, 0)),
                  pl.BlockSpec((1, d, tn), lambda i, j: (i, 0, j)),
                  pl.BlockSpec((1, 1, tn), lambda i, j: (i, 0, j))],
        out_specs=pl.BlockSpec((1, s_rows.shape[0], tn), lambda i, j: (i, 0, j)),
        out_shape=jax.ShapeDtypeStruct((depth, s_rows.shape[0], nd), F32),
        compiler_params=pltpu.CompilerParams(dimension_semantics=("parallel", "parallel"),
                                             vmem_limit_bytes=VMEM_LIMIT),
        name="mods",
    )(s_rows, ada_w, ada_b.reshape(depth, 1, nd))


def _route(logits):
    lane = lax.broadcasted_iota(jnp.int32, logits.shape, 1)
    lane_f = lane.astype(F32)
    big = jnp.float32(1e9)
    is_g = lane < N_GROUPS
    gl = jnp.where(is_g, logits, NEG_BIG)
    gm = jnp.max(gl, axis=1, keepdims=True)
    grp = jnp.min(jnp.where(is_g & (gl == gm), lane_f, big), axis=1, keepdims=True)
    p_grp = 1.0 / jnp.sum(jnp.where(is_g, jnp.exp(gl - gm), 0.0), axis=1, keepdims=True)
    lo = N_GROUPS + grp * EXPERTS_PER_GROUP
    in_grp = (lane_f >= lo) & (lane_f < lo + EXPERTS_PER_GROUP)
    el = jnp.where(in_grp, logits, NEG_BIG)
    m1 = jnp.max(el, axis=1, keepdims=True)
    i1 = jnp.min(jnp.where(in_grp & (el == m1), lane_f, big), axis=1, keepdims=True)
    rest = in_grp & (lane_f != i1)
    el2 = jnp.where(rest, logits, NEG_BIG)
    m2 = jnp.max(el2, axis=1, keepdims=True)
    i2 = jnp.min(jnp.where(rest & (el2 == m2), lane_f, big), axis=1, keepdims=True)
    e21 = jnp.exp(m2 - m1)
    s0 = 1.0 / (1.0 + e21)
    w0 = p_grp * s0
    w1 = p_grp * (e21 * s0)
    out = jnp.where(lane == 0, i1 - N_GROUPS,
                    jnp.where(lane == 1, i2 - N_GROUPS,
                              jnp.where(lane == 2, w0, jnp.where(lane == 3, w1, 0.0))))
    return out


def _residual_and_route(x, y, mods, g2_ref, wrt_ref, brt_ref, xo_ref, h2_ref, route_ref, rows=slice(None)):
    _, _, gt1, sh2, sc2, _ = mods
    xn = x + gt1 * y
    xo_ref[rows, :] = xn
    h2 = _norm_mod(xn, g2_ref[...], sc2, sh2)
    h2_ref[rows, :] = h2
    logits = _dotb(h2, wrt_ref[...]) + brt_ref[...]
    route_ref[rows, :] = _route(logits)


def _round_robin(gens):
    while gens:
        gens = [g for g in gens if next(g, StopIteration) is not StopIteration]


def _gmlp_rows(rows, x_ref, mod_ref, g1_ref, g2_ref, win_ref, bin_ref, lng_ref, lnb_ref, ws_ref, bs_ref,
               wout_ref, bout_ref, wrt_ref, brt_ref, xo_ref, h2_ref, route_ref, gated_ref):
    x = x_ref[rows, :]
    mods = _mod_parts(mod_ref)
    sh1, sc1 = mods[0], mods[1]
    h = _norm_mod(x, g1_ref[...], sc1, sh1).astype(BF16)
    yield
    width = win_ref.shape[1] // 2
    gelu = lambda z: 0.5 * z * (1.0 + lax.erf(z * SQRT_HALF))
    u = gelu(jnp.dot(h, win_ref[:, :width], preferred_element_type=F32) + bin_ref[:, :width])
    yield
    v = gelu(jnp.dot(h, win_ref[:, width:], preferred_element_type=F32) + bin_ref[:, width:])
    yield
    mu = jnp.mean(v, axis=-1, keepdims=True)
    vc = v - mu
    var = jnp.mean(vc * vc, axis=-1, keepdims=True)
    vn = ((vc * lax.rsqrt(var + LN_EPS)) * lng_ref[...] + lnb_ref[...]).astype(BF16)
    yield
    n_chunks = x.shape[0] // CHUNK
    for g in range(width // GMLP_GROUP_CH):
        cs = slice(g * GMLP_GROUP_CH, (g + 1) * GMLP_GROUP_CH)
        rhs = jnp.concatenate([vn[c * CHUNK:(c + 1) * CHUNK, cs] for c in range(n_chunks)], axis=1)
        s = jnp.dot(ws_ref[g], rhs, preferred_element_type=F32) + bs_ref[g]
        for c in range(n_chunks):
            rsl = slice(c * CHUNK, (c + 1) * CHUNK)
            dst = slice(rows.start + c * CHUNK, rows.start + (c + 1) * CHUNK)
            gated_ref[dst, cs] = (u[rsl, cs] * s[:, c * GMLP_GROUP_CH:(c + 1) * GMLP_GROUP_CH]).astype(BF16)
        if g % 4 == 3:
            yield
    y = jnp.dot(gated_ref[rows, :], wout_ref[...], preferred_element_type=F32) + bout_ref[...]
    yield
    _residual_and_route(x, y, mods, g2_ref, wrt_ref, brt_ref, xo_ref, h2_ref, route_ref, rows)
    yield


def _gmlp_kernel(*refs):
    group = refs[0].shape[0] // GMLP_SPLIT
    _round_robin([_gmlp_rows(slice(q * group, (q + 1) * group), *refs) for q in range(GMLP_SPLIT)])


def _full_spec(shape):
    nd = len(shape)
    return pl.BlockSpec(shape, lambda *_: (0,) * nd)


def _layer_spec(shape, j, single_buffer=False):
    nd = len(shape)
    kwargs = dict(pipeline_mode=pl.Buffered(1)) if single_buffer else {}
    return pl.BlockSpec((None,) + tuple(shape[1:]), lambda *_: (j,) + (0,) * (nd - 1), **kwargs)


def _mixer_out_shapes(t_tok, d):
    return (jax.ShapeDtypeStruct((t_tok, d), F32), jax.ShapeDtypeStruct((t_tok, d), F32),
            jax.ShapeDtypeStruct((t_tok, LANES), F32))


def _mixer_out_specs(d, tm=TM):
    return (pl.BlockSpec((tm, d), lambda t: (t, 0)), pl.BlockSpec((tm, d), lambda t: (t, 0)),
            pl.BlockSpec((tm, LANES), lambda t: (t, 0)))


def _gmlp_layer(x, mods3, mod_spec, i, j, p):
    t_tok, d = x.shape
    width = p["ga_w_out"].shape[1]
    row = lambda a: a.reshape(a.shape[0], 1, a.shape[1])
    args = (x, mods3, row(p["norm1_g"]), row(p["norm2_g"]), p["ga_w_in"], row(p["ga_b_in"]), row(p["ga_ln_g"]),
            row(p["ga_ln_b"]), p["ga_w_s"], p["ga_b_s"], p["ga_w_out"], row(p["ga_b_out"]), p["wrt"], p["brt"])
    in_specs = [pl.BlockSpec((GMLP_TM, d), lambda t: (t, 0)), mod_spec,
                _layer_spec(args[2].shape, i), _layer_spec(args[3].shape, i),
                _layer_spec(args[4].shape, j, True), _layer_spec(args[5].shape, j), _layer_spec(args[6].shape, j),
                _layer_spec(args[7].shape, j), _layer_spec(args[8].shape, j, True), _layer_spec(args[9].shape, j),
                _layer_spec(args[10].shape, j, True), _layer_spec(args[11].shape, j),
                _layer_spec(args[12].shape, i), _layer_spec(args[13].shape, i)]
    return pl.pallas_call(
        _gmlp_kernel,
        grid=(t_tok // GMLP_TM,),
        in_specs=in_specs,
        out_specs=_mixer_out_specs(d, GMLP_TM),
        out_shape=_mixer_out_shapes(t_tok, d),
        scratch_shapes=[pltpu.VMEM((GMLP_TM, width), BF16)],
        compiler_params=pltpu.CompilerParams(dimension_semantics=("parallel",), vmem_limit_bytes=VMEM_LIMIT),
        name="gmlp_layer",
    )(*args)


def _seg_sum(q, e_ref):
    return _dot_x_sel(q, e_ref[...], terms=2)


def _seg_expand(s, et_ref):
    return _dot_x_sel(s, et_ref[...], terms=2)


def _shifted(h, h_above, h_below, is_ctx, first_row_tile, last_row_tile):
    tm, d = h.shape
    q = d // 4
    row = lax.broadcasted_iota(jnp.int32, (tm, 1), 0)
    ctx_i = is_ctx.astype(jnp.int32)
    col = row & (jnp.where(is_ctx, tm, GRID_W) - 1)
    last_col = jnp.where(is_ctx, tm, GRID_W) - 1
    prev1 = jnp.where(col == 0, 0.0, pltpu.roll(h, 1, axis=0))
    next1 = jnp.where(col == last_col, 0.0, pltpu.roll(h, tm - 1, axis=0))
    up = jnp.concatenate([h_above, h[:tm - GRID_W]], axis=0)
    up = jnp.where(row < jnp.where(first_row_tile, GRID_W, 0), 0.0, up)
    down = jnp.concatenate([h[GRID_W:], h_below], axis=0)
    down = jnp.where(row >= jnp.where(last_row_tile, tm - GRID_W, tm), 0.0, down)
    lane_q = lax.broadcasted_iota(jnp.int32, (1, d), 1) >> int(math.log2(q))
    src = lane_q * (1 - ctx_i) + (lane_q >> 1) * ctx_i
    return jnp.where(src == 0, prev1, jnp.where(src == 1, next1, jnp.where(src == 2, up, down)))


def _rwkv_prep_kernel(x_ref, xa_ref, xb_ref, mod_ref, n1_ref, mu_ref, wr_ref, wk_ref, wv_ref, w1_ref, w2_ref, w0_ref,
                      a1_ref, a2_ref, a0_ref, g1_ref, g2_ref, kk_ref, ka_ref, e_ref, et_ref,
                      r_o, v_o, kkn_o, gate_o, lw_o, km_o, bb_o, *, lat_tiles, tiles_per_batch):
    t = pl.program_id(0)
    mods = _mod_parts(mod_ref)
    norm = lambda x: _norm_mod(x, n1_ref[...], mods[1], mods[0])
    h = norm(x_ref[...])
    tb = t % tiles_per_batch
    hs = _shifted(h, norm(xa_ref[...]), norm(xb_ref[...]), t >= lat_tiles, tb == 0, tb == tiles_per_batch - 1)
    xx = hs - h
    mu = mu_ref[...]
    xr, xw, xk, xv, xa, xg = [h + xx * mu[q:q + 1] for q in range(6)]
    r = _dotb(xr, wr_ref[...])
    k = _dotb(xk, wk_ref[...])
    v = _dotb(xv, wv_ref[...])
    r_o[...] = r
    v_o[...] = v
    kkr = k * kk_ref[...]
    rs = lax.rsqrt(_seg_sum(kkr * kkr, e_ref) + 1e-12)
    kkn = kkr * _seg_expand(rs, et_ref)
    kkn_o[...] = kkn
    gate_o[...] = _dotb(jax.nn.sigmoid(_dotb(xg, g1_ref[...])), g2_ref[...])
    tw = jnp.tanh(_dotb(xw, w1_ref[...]))
    ta = _dotb(xa, a1_ref[...])
    lane = lax.broadcasted_iota(jnp.int32, tw.shape, 1)
    half = tw.shape[1] // 2
    w0 = w0_ref[...]
    a0 = a0_ref[...]
    ka = ka_ref[...]
    for z in range(2):
        zm = (lane >= z * half) & (lane < (z + 1) * half)
        lwz = _dotb(jnp.where(zm, tw, 0.0), w2_ref[...])
        lw_o[z] = -EXP_MINUS_HALF * jax.nn.sigmoid(w0[z:z + 1] + lwz)
        la = _dotb(jnp.where(zm, ta, 0.0), a2_ref[...])
        a = jax.nn.sigmoid(a0[z:z + 1] + la)
        km_o[z] = k * (1.0 + (a - 1.0) * ka)
        bb_o[z] = kkn * a


def _rwkv_prep(x, mods3, mod_spec, i, j, p, lat_tiles, tiles_per_batch):
    t_tok, d = x.shape
    row = lambda a: a.reshape(a.shape[0], 1, a.shape[1])
    args = (x, x, x, mods3, row(p["norm1_g"]), p["rw_mu8"], p["rw_wr"], p["rw_wk"], p["rw_wv"], p["rw_w1c"],
            p["rw_w2c"], p["rw_w0"], p["rw_a1c"], p["rw_a2c"], p["rw_a0"], p["rw_g1p"], p["rw_g2p"],
            row(p["rw_k_k"]), row(p["rw_k_a"]), p["head_sel"], p["head_sel_t"])
    tok = pl.BlockSpec((TM, d), lambda t: (t, 0))
    rows_per_tile = TM // GRID_W
    last_row = t_tok // GRID_W - 1
    above = pl.BlockSpec((GRID_W, d), lambda t: (jnp.maximum(t * rows_per_tile - 1, 0), 0))
    below = pl.BlockSpec((GRID_W, d), lambda t: (jnp.minimum((t + 1) * rows_per_tile, last_row), 0))
    in_specs = ([tok, above, below, mod_spec, _layer_spec(args[4].shape, i)]
                + [_layer_spec(a.shape, j) for a in args[5:19]]
                + [_full_spec(args[19].shape), _full_spec(args[20].shape)])
    dir_spec = pl.BlockSpec((2, TM, d), lambda t: (0, t, 0))
    tok_shape = jax.ShapeDtypeStruct((t_tok, d), F32)
    dir_shape = jax.ShapeDtypeStruct((2, t_tok, d), F32)
    return pl.pallas_call(
        functools.partial(_rwkv_prep_kernel, lat_tiles=lat_tiles, tiles_per_batch=tiles_per_batch),
        grid=(t_tok // TM,),
        in_specs=in_specs,
        out_specs=(tok, tok, tok, tok, dir_spec, dir_spec, dir_spec),
        out_shape=(tok_shape, tok_shape, tok_shape, tok_shape, dir_shape, dir_shape, dir_shape),
        compiler_params=pltpu.CompilerParams(dimension_semantics=("parallel",), vmem_limit_bytes=VMEM_LIMIT),
        name="rwkv_prep",
    )(*args)


def _scan_prepare(r, v, kk, lw, km, bb, rev, out):
    c = SCAN_CHUNK
    nb = SCAN_BLOCK
    a = -kk
    row = lax.broadcasted_iota(jnp.int32, (nb, nb), 0)
    col = lax.broadcasted_iota(jnp.int32, (nb, nb), 1)
    shift_c = int(math.log2(c))
    same = (row >> shift_c) == (col >> shift_c)
    if rev:
        incl = same & (col >= row)
        strict = same & (col > row)
    else:
        incl = same & (col <= row)
        strict = same & (col < row)
    incl_b = jnp.where(incl, 1.0, 0.0).astype(BF16)
    eye = jnp.where(row == col, 1.0, 0.0)
    head_of_lane = col >> int(math.log2(RWKV_HEAD))

    lw_hi, lw_lo = _split2(lw)
    cum = (jnp.dot(incl_b, lw_hi, preferred_element_type=F32)
           + jnp.dot(incl_b, lw_lo, preferred_element_type=F32))
    yield
    last = [ci * c if rev else ci * c + c - 1 for ci in range(nb // c)]
    tot_rows = [cum[t:t + 1] for t in last]
    tot = jnp.concatenate([jnp.broadcast_to(t, (c, t.shape[1])) for t in tot_rows], axis=0)
    yield
    at = a * jnp.exp(cum - lw)
    rt = r * jnp.exp(cum)
    e_inv = jnp.exp(-cum)
    bt = bb * e_inv
    kt = km * e_inv
    e_rem = jnp.exp(tot - cum)
    bh = (bb * e_rem).astype(BF16)
    kh = (km * e_rem).astype(BF16)
    et = jnp.exp(jnp.concatenate(tot_rows, axis=0))
    bk = jnp.concatenate([bt, kt], axis=0).astype(BF16)
    yield
    n_heads = SCAN_LANES // RWKV_HEAD
    heads = range(n_heads)
    hms = [head_of_lane == hh for hh in heads]
    at_h = [jnp.where(hm, at, 0.0) for hm in hms]
    rt_h = [jnp.where(hm, rt, 0.0) for hm in hms]
    v_h = [jnp.where(hm, v, 0.0).astype(BF16) for hm in hms]
    v_roll = pltpu.roll(v, RWKV_HEAD, axis=1)
    v_s = [jnp.where(hms[(hh + 1) % n_heads], v_roll, 0.0).astype(BF16) for hh in heads]
    a_ab, a_ak, m_rb, m_rk = [], [], [], []
    for hh in heads:
        s = _dot_nt(jnp.concatenate([at_h[hh], rt_h[hh]], axis=0), bk)
        a_ab.append(jnp.where(strict, s[:nb, :nb], 0.0))
        a_ak.append(jnp.where(strict, s[:nb, nb:], 0.0).astype(BF16))
        m_rb.append(jnp.where(incl, s[nb:, :nb], 0.0).astype(BF16))
        m_rk.append(jnp.where(incl, s[nb:, nb:], 0.0).astype(BF16))
        yield
    pw = list(a_ab)
    tinv = [eye + m for m in a_ab]
    for _ in range(shift_c - 1):
        for hh in heads:
            pw[hh] = _dotb(pw[hh], pw[hh])
        yield
        for hh in heads:
            tinv[hh] = tinv[hh] + _dotb(tinv[hh], pw[hh])
        yield
    x1 = [jnp.dot(a_ak[hh], v_s[hh], preferred_element_type=F32) for hh in heads]
    yield
    tc = [_dotb(tinv[hh], at_h[hh] + x1[hh]) for hh in heads]
    yield
    mc = [_dotb(m_rb[hh], tc[hh]) for hh in heads]
    yield

    def pick(parts, shift):
        res = parts[(n_heads - 1 - shift) % n_heads]
        for lb in range(n_heads - 2, -1, -1):
            res = jnp.where(hms[lb], parts[(lb - shift) % n_heads], res)
        return res

    back = SCAN_LANES - RWKV_HEAD
    ah = pick(tc, 0)
    vh = pltpu.roll(pick(tc, 1), back, axis=1)
    rh = rt + pick(mc, 0)
    yh = (jnp.dot(jnp.concatenate(m_rk, axis=1), jnp.concatenate(v_h, axis=0), preferred_element_type=F32)
          + pltpu.roll(pick(mc, 1), back, axis=1))
    yield
    out.extend([ah.astype(BF16), rh.astype(BF16), vh, yh, bh, kh, v.astype(BF16), et])


def _scan_prepare_stages():
    return 7 + (SCAN_LANES // RWKV_HEAD) + 2 * (int(math.log2(SCAN_CHUNK)) - 1)


def _scan_chain(g, ops, y_ref, lanes, rev, out):
    c = SCAN_CHUNK
    ah, rh, vh, yh, bh, kh, vb, et = ops
    n = g.shape[0]
    shift_h = int(math.log2(RWKV_HEAD))
    blockdiag = ((lax.broadcasted_iota(jnp.int32, (n, n), 0) >> shift_h)
                 == (lax.broadcasted_iota(jnp.int32, (n, n), 1) >> shift_h))
    n_chunks = SCAN_BLOCK // c
    order = range(n_chunks - 1, -1, -1) if rev else range(n_chunks)
    for ci in order:
        rs = slice(ci * c, (ci + 1) * c)
        p = _dot_nt(jnp.concatenate([ah[rs], rh[rs]], axis=0), g)
        u = p[:c] + vh[rs]
        y_ref[rs, lanes] = p[c:] + yh[rs]
        yield
        dg = _dot_tn(jnp.concatenate([u.astype(BF16), vb[rs]], axis=0), jnp.concatenate([bh[rs], kh[rs]], axis=0))
        g = jnp.where(blockdiag, g * et[ci:ci + 1, :] + dg, 0.0)
        yield
    out.append(g)


def _scan_chain_stages():
    return 2 * (SCAN_BLOCK // SCAN_CHUNK)


def _interleave(gen_a, n_a, gen_b, n_b):
    done_b = 0
    for i in range(n_a):
        next(gen_a, None)
        want_b = ((i + 1) * n_b) // n_a
        while done_b < want_b:
            next(gen_b, None)
            done_b += 1
        yield
    for g in (gen_a, gen_b):
        for _ in g:
            pass


def _scan_kernel(r_ref, v_ref, kk_ref, lw_ref, km_ref, bb_ref, y_ref, g_scr, *ops_scr, rev, groups):
    @pl.when(pl.program_id(2) == 0)
    def _():
        g_scr[...] = jnp.zeros_like(g_scr)
        for ref in ops_scr:
            ref[...] = jnp.zeros_like(ref)

    slot = pl.program_id(2) % 2
    work, results = [], []
    for q in range(groups):
        lanes = slice(q * SCAN_LANES, (q + 1) * SCAN_LANES)
        prev = [ref.at[1 - slot, :, lanes] for ref in ops_scr]
        new, g_end = [], []
        prep = _scan_prepare(r_ref[:, lanes], v_ref[:, lanes], kk_ref[:, lanes], lw_ref[:, lanes], km_ref[:, lanes],
                             bb_ref[:, lanes], rev, new)
        chain = _scan_chain(g_scr[q], prev, y_ref, lanes, rev, g_end)
        work.append(_interleave(prep, _scan_prepare_stages(), chain, _scan_chain_stages()))
        results.append((lanes, new, g_end))
    while work:
        work = [w for w in work if next(w, StopIteration) is not StopIteration]
    for q, (lanes, new, g_end) in enumerate(results):
        g_scr[q] = g_end[0]
        for ref, val in zip(ops_scr, new):
            ref[slot, :, lanes] = val


def _wkv_scan(r, v, kkn, lw, km, bb, z, bsz, n_lat, n_ctx):
    t_tok, d = r.shape
    lat_blocks = n_lat // SCAN_BLOCK
    ctx_blocks = n_ctx // SCAN_BLOCK
    steps = ctx_blocks + lat_blocks
    ctx_base = bsz * lat_blocks
    rev = z == 1
    width = SCAN_LANES * SCAN_GROUPS

    def blk(b, s):
        if rev:
            cb = ctx_base + b * ctx_blocks + (ctx_blocks - 1 - s)
            lb = b * lat_blocks + (lat_blocks - 1 - (s - ctx_blocks))
        else:
            cb = ctx_base + b * ctx_blocks + s
            lb = b * lat_blocks + (s - ctx_blocks)
        return jnp.where(s < ctx_blocks, cb, lb)

    in_blk = lambda b, s: blk(b, jnp.minimum(s, steps - 1))
    out_blk = lambda b, s: blk(b, jnp.maximum(s - 1, 0))
    tok = pl.BlockSpec((SCAN_BLOCK, width), lambda b, g, s: (in_blk(b, s), g))
    dirs = pl.BlockSpec((None, SCAN_BLOCK, width), lambda b, g, s: (z, in_blk(b, s), g))
    n_chunks = SCAN_BLOCK // SCAN_CHUNK
    ops_scratch = [pltpu.VMEM((2, SCAN_BLOCK, width), BF16), pltpu.VMEM((2, SCAN_BLOCK, width), BF16),
                   pltpu.VMEM((2, SCAN_BLOCK, width), F32), pltpu.VMEM((2, SCAN_BLOCK, width), F32),
                   pltpu.VMEM((2, SCAN_BLOCK, width), BF16), pltpu.VMEM((2, SCAN_BLOCK, width), BF16),
                   pltpu.VMEM((2, SCAN_BLOCK, width), BF16), pltpu.VMEM((2, n_chunks, width), F32)]
    return pl.pallas_call(
        functools.partial(_scan_kernel, rev=rev, groups=SCAN_GROUPS),
        grid=(bsz, d // width, steps + 1),
        in_specs=[tok, tok, tok, dirs, dirs, dirs],
        out_specs=pl.BlockSpec((SCAN_BLOCK, width), lambda b, g, s: (out_blk(b, s), g)),
        out_shape=jax.ShapeDtypeStruct((t_tok, d), F32),
        scratch_shapes=[pltpu.VMEM((SCAN_GROUPS, SCAN_LANES, SCAN_LANES), F32)] + ops_scratch,
        compiler_params=pltpu.CompilerParams(dimension_semantics=("parallel", "parallel", "arbitrary"),
                                             vmem_limit_bytes=VMEM_LIMIT),
        name="wkv_scan_rev" if rev else "wkv_scan_fwd",
    )(r, v, kkn, lw, km, bb)


def _rwkv_out_kernel(x_ref, mod_ref, y0_ref, y1_ref, r_ref, v_ref, km_ref, gate_ref, lng_ref, lnb_ref, rk_ref,
                     wo_ref, e_ref, et_ref, g2_ref, wrt_ref, brt_ref, xo_ref, h2_ref, route_ref):
    x = x_ref[...]
    mods = _mod_parts(mod_ref)
    y = y0_ref[...] + y1_ref[...]
    inv_n = 1.0 / RWKV_HEAD
    mu = _seg_expand(_seg_sum(y, e_ref) * inv_n, et_ref)
    yc = y - mu
    var = _seg_sum(yc * yc, e_ref) * inv_n
    yn = (yc * _seg_expand(lax.rsqrt(var + GN_EPS), et_ref)) * lng_ref[...] + lnb_ref[...]
    v = v_ref[...]
    bonus = _seg_sum((r_ref[...] * rk_ref[...]) * (km_ref[0] + km_ref[1]), e_ref)
    yn = yn + _seg_expand(bonus, et_ref) * v
    out = _dotb(yn * gate_ref[...], wo_ref[...])
    _residual_and_route(x, out, mods, g2_ref, wrt_ref, brt_ref, xo_ref, h2_ref, route_ref)


def _rwkv_out(x, mods3, mod_spec, y0, y1, r, v, km, gate, i, j, p):
    t_tok, d = x.shape
    row = lambda a: a.reshape(a.shape[0], 1, a.shape[1])
    tok = pl.BlockSpec((TM, d), lambda t: (t, 0))
    dir_spec = pl.BlockSpec((2, TM, d), lambda t: (0, t, 0))
    args = (x, mods3, y0, y1, r, v, km, gate, row(p["rw_ln_g"]), row(p["rw_ln_b"]), row(p["rw_r_k"].reshape(-1, d)),
            p["rw_wo"], p["head_sel"], p["head_sel_t"], row(p["norm2_g"]), p["wrt"], p["brt"])
    in_specs = [tok, mod_spec, tok, tok, tok, tok, dir_spec, tok,
                _layer_spec(args[8].shape, j), _layer_spec(args[9].shape, j), _layer_spec(args[10].shape, j),
                _layer_spec(args[11].shape, j), _full_spec(args[12].shape), _full_spec(args[13].shape),
                _layer_spec(args[14].shape, i), _layer_spec(args[15].shape, i), _layer_spec(args[16].shape, i)]
    return pl.pallas_call(
        _rwkv_out_kernel,
        grid=(t_tok // TM,),
        in_specs=in_specs,
        out_specs=_mixer_out_specs(d),
        out_shape=_mixer_out_shapes(t_tok, d),
        compiler_params=pltpu.CompilerParams(dimension_semantics=("parallel",), vmem_limit_bytes=VMEM_LIMIT),
        name="rwkv_out",
    )(*args)


def _expert_kernel(be_ref, nu_ref, x_ref, wg_ref, wu_ref, wd_ref, o_ref, wg_s, wu_s, wd_s):
    b = pl.program_id(0)
    e = be_ref[b]
    prev = be_ref[jnp.maximum(b - 1, 0)]
    used = b < nu_ref[0]

    @pl.when(used & ((b == 0) | (e != prev)))
    def _():
        wg_s[...] = wg_ref[...].astype(BF16)
        wu_s[...] = wu_ref[...].astype(BF16)
        wd_s[...] = wd_ref[...].astype(BF16)

    @pl.when(used)
    def _():
        xb = x_ref[...].astype(BF16)
        g = jnp.dot(xb, wg_s[...], preferred_element_type=F32)
        u = jnp.dot(xb, wu_s[...], preferred_element_type=F32)
        hmid = (g * jax.nn.sigmoid(g)) * u
        o_ref[...] = jnp.dot(hmid.astype(BF16), wd_s[...], preferred_element_type=F32)

    @pl.when(jnp.logical_not(used))
    def _():
        o_ref[...] = jnp.zeros_like(o_ref)


def _experts(buf, block_expert, n_used, w_gate, w_up, w_down, i):
    cap, d = buf.shape
    f = w_gate.shape[-1]
    n_blocks = cap // MOE_ROWS
    grid_spec = pltpu.PrefetchScalarGridSpec(
        num_scalar_prefetch=2,
        grid=(n_blocks,),
        in_specs=[pl.BlockSpec((MOE_ROWS, d), lambda b, be, nu: (b, 0)),
                  pl.BlockSpec((None, None, d, f), lambda b, be, nu: (i, be[b], 0, 0)),
                  pl.BlockSpec((None, None, d, f), lambda b, be, nu: (i, be[b], 0, 0)),
                  pl.BlockSpec((None, None, f, d), lambda b, be, nu: (i, be[b], 0, 0))],
        out_specs=pl.BlockSpec((MOE_ROWS, d), lambda b, be, nu: (b, 0)),
        scratch_shapes=[pltpu.VMEM((d, f), BF16), pltpu.VMEM((d, f), BF16), pltpu.VMEM((f, d), BF16)],
    )
    return pl.pallas_call(
        _expert_kernel,
        grid_spec=grid_spec,
        out_shape=jax.ShapeDtypeStruct((cap, d), F32),
        compiler_params=pltpu.CompilerParams(dimension_semantics=("arbitrary",), vmem_limit_bytes=VMEM_LIMIT),
        name="experts",
    )(block_expert, n_used, buf, w_gate, w_up, w_down)


def _combine_kernel(x_ref, mod_ref, route_ref, y0_ref, y1_ref, fg_ref, xo_ref, *, final):
    gt2 = _mod_parts(mod_ref)[5]
    route = route_ref[...]
    xn = x_ref[...] + gt2 * (route[:, 2:3] * y0_ref[...] + route[:, 3:4] * y1_ref[...])
    if final:
        xn = (xn * lax.rsqrt(jnp.mean(xn * xn, axis=-1, keepdims=True) + NORM_EPS)) * fg_ref[...]
    xo_ref[...] = xn


def _combine(x, mods3, mod_spec, route, y0, y1, final_g, final, t_out):
    d = x.shape[1]
    tok = pl.BlockSpec((TM, d), lambda t: (t, 0))
    return pl.pallas_call(
        functools.partial(_combine_kernel, final=final),
        grid=(t_out // TM,),
        in_specs=[tok, mod_spec, pl.BlockSpec((TM, LANES), lambda t: (t, 0)), tok, tok, _full_spec((1, d))],
        out_specs=tok,
        out_shape=jax.ShapeDtypeStruct((t_out, d), F32),
        compiler_params=pltpu.CompilerParams(dimension_semantics=("parallel",), vmem_limit_bytes=VMEM_LIMIT),
        name="moe_combine",
    )(x, mods3, route, y0, y1, final_g.reshape(1, d))


def _dest_kernel(route_ref, d_ref, counts_ref, base_scr, start_scr):
    phase = pl.program_id(0)
    t = pl.program_id(1)
    route = route_ref[...]
    tm = route.shape[0]
    lane = lax.broadcasted_iota(jnp.int32, route.shape, 1)
    lane_f = lane.astype(F32)
    onehots = [jnp.where(lane_f == route[:, k:k + 1], 1.0, 0.0) for k in range(TOP_K)]

    @pl.when((phase == 0) & (t == 0))
    def _():
        base_scr[...] = jnp.zeros_like(base_scr)

    @pl.when((phase == 1) & (t == 0))
    def _():
        counts = base_scr[...]
        counts_ref[...] = counts
        padded = jnp.floor((counts + (MOE_ROWS - 1)) * (1.0 / MOE_ROWS)) * MOE_ROWS
        before = (lax.broadcasted_iota(jnp.int32, (LANES, LANES), 0)
                  < lax.broadcasted_iota(jnp.int32, (LANES, LANES), 1))
        start_scr[...] = _dot_x_sel(padded, jnp.where(before, 1.0, 0.0).astype(BF16))
        base_scr[...] = jnp.zeros_like(base_scr)

    @pl.when(phase == 0)
    def _():
        base_scr[...] = base_scr[...] + sum(jnp.sum(oh, axis=0, keepdims=True) for oh in onehots)

    @pl.when(phase == 1)
    def _():
        earlier = (lax.broadcasted_iota(jnp.int32, (tm, tm), 1)
                   < lax.broadcasted_iota(jnp.int32, (tm, tm), 0))
        earlier_b = jnp.where(earlier, 1.0, 0.0).astype(BF16)
        diag = (lax.broadcasted_iota(jnp.int32, (LANES, LANES), 0)
                == lax.broadcasted_iota(jnp.int32, (LANES, LANES), 1))
        base = base_scr[...] + start_scr[...]
        for k, oh in enumerate(onehots):
            before = jnp.dot(earlier_b, oh.astype(BF16), preferred_element_type=F32)
            dest = jnp.sum(oh * (base + before), axis=1, keepdims=True)
            rows = [jnp.sum(jnp.where(diag, jnp.broadcast_to(dest[i * LANES:(i + 1) * LANES], (LANES, LANES)), 0.0),
                            axis=0, keepdims=True) for i in range(tm // LANES)]
            d_ref[k, 0] = jnp.concatenate(rows, axis=0).astype(jnp.int32)
            base = base + jnp.sum(oh, axis=0, keepdims=True)
        base_scr[...] = base - start_scr[...]


def _dests(route):
    t_tok = route.shape[0]
    assert t_tok % DEST_TM == 0
    tiles = t_tok // DEST_TM
    return pl.pallas_call(
        _dest_kernel,
        grid=(2, tiles),
        in_specs=[pl.BlockSpec((DEST_TM, LANES), lambda ph, t: (t, 0))],
        out_specs=(pl.BlockSpec((TOP_K, 1, DEST_TM // LANES, LANES), lambda ph, t: (0, t * ph, 0, 0)),
                   pl.BlockSpec((1, LANES), lambda ph, t: (0, 0))),
        out_shape=(jax.ShapeDtypeStruct((TOP_K, tiles, DEST_TM // LANES, LANES), jnp.int32),
                   jax.ShapeDtypeStruct((1, LANES), F32)),
        scratch_shapes=[pltpu.VMEM((1, LANES), F32), pltpu.VMEM((1, LANES), F32)],
        compiler_params=pltpu.CompilerParams(dimension_semantics=("arbitrary", "arbitrary"),
                                             vmem_limit_bytes=VMEM_LIMIT),
        name="moe_dests",
    )(route)


def _moe(x, h2, route, mods3, mod_spec, i, p, final, t_out):
    t_tok, d = x.shape
    n_assign = t_tok * TOP_K
    dest4, counts_f = _dests(route)
    dest = dest4.reshape(TOP_K, t_tok)
    counts = counts_f[0, :N_EXPERTS].astype(jnp.int32)
    pad_end = jnp.cumsum(((counts + MOE_ROWS - 1) // MOE_ROWS) * MOE_ROWS)
    n_blocks = (n_assign + MOE_ROWS - 1) // MOE_ROWS + N_EXPERTS
    token = jnp.arange(t_tok, dtype=jnp.int32)
    slot_token = (jnp.arange(n_blocks * MOE_ROWS, dtype=jnp.int32) % t_tok).at[dest.reshape(-1)].set(
        jnp.tile(token, TOP_K), mode="promise_in_bounds", unique_indices=True)
    block_start = jnp.arange(n_blocks, dtype=jnp.int32) * MOE_ROWS
    block_expert = jnp.minimum(jnp.sum((pad_end[None, :] <= block_start[:, None]).astype(jnp.int32), axis=1),
                               N_EXPERTS - 1)
    n_used = (pad_end[-1:] // MOE_ROWS).astype(jnp.int32)
    buf = h2.at[slot_token].get(mode="promise_in_bounds")
    out = _experts(buf, block_expert, n_used, p["moe_w_gate"], p["moe_w_up"], p["moe_w_down"], i)
    y0 = out.at[dest[0]].get(mode="promise_in_bounds", unique_indices=True)
    y1 = out.at[dest[1]].get(mode="promise_in_bounds", unique_indices=True)
    return _combine(x, mods3, mod_spec, route, y0, y1, p["final_g"], final, t_out)


def kernel(x, c, ctx, c_ctx, ada_w, ada_b, norm1_g, norm2_g, final_g, ga_w_in, ga_b_in, ga_ln_g, ga_ln_b, ga_w_s,
           ga_b_s, ga_w_out, ga_b_out, rw_mu, rw_wr, rw_wk, rw_wv, rw_wo, rw_w0, rw_w1, rw_w2, rw_a0, rw_a1, rw_a2,
           rw_g1, rw_g2, rw_k_k, rw_k_a, rw_r_k, rw_ln_g, rw_ln_b, moe_w_grp, moe_b_grp, moe_w_exp, moe_b_exp,
           moe_w_gate, moe_w_up, moe_w_down):
    bsz, n_lat, d = x.shape
    n_ctx = ctx.shape[1]
    depth = ada_w.shape[0]
    n_mixers = 2
    assert n_lat % TM == 0 and n_ctx == TM and TM % GRID_W == 0
    assert n_lat % GMLP_TM == 0 and (bsz * n_ctx) % GMLP_TM == 0 and GMLP_TM % (GMLP_SPLIT * CHUNK) == 0
    assert n_lat % SCAN_BLOCK == 0 and n_ctx % SCAN_BLOCK == 0
    assert bsz + 1 <= 8 and d % (SCAN_LANES * SCAN_GROUPS) == 0
    assert MOE_ROWS & (MOE_ROWS - 1) == 0
    t_lat = bsz * n_lat
    lat_tiles = t_lat // TM
    tiles_per_batch = n_lat // TM
    heads = d // RWKV_HEAD

    s_rows = jnp.concatenate([c, c_ctx[None, :], jnp.zeros((8 - bsz - 1, d), F32)], axis=0)
    mods = _mods(s_rows, ada_w, ada_b)
    mods3 = mods.reshape(depth * 8, 1, N_MOD * d)

    def mod_spec_for(i, tm=TM):
        def imap(t):
            return (i * 8 + jnp.where(t < t_lat // tm, t // (n_lat // tm), bsz), 0, 0)
        return pl.BlockSpec((1, 1, N_MOD * d), imap)

    head_sel = (jnp.arange(d)[:, None] // RWKV_HEAD == jnp.arange(LANES)[None, :]).astype(BF16)
    zpad = lambda a, axis, n: jnp.pad(a, [(0, n - a.shape[k]) if k == axis else (0, 0) for k in range(a.ndim)])
    glora = ((rw_g1.shape[-1] + LANES - 1) // LANES) * LANES
    wrt = jnp.concatenate([moe_w_grp, moe_w_exp], axis=-1)
    brt = jnp.concatenate([moe_b_grp, moe_b_exp], axis=-1)
    p = dict(
        norm1_g=norm1_g, norm2_g=norm2_g, final_g=final_g,
        ga_w_in=ga_w_in.astype(BF16), ga_b_in=ga_b_in, ga_ln_g=ga_ln_g, ga_ln_b=ga_ln_b,
        ga_w_s=ga_w_s.astype(BF16), ga_b_s=ga_b_s[..., None], ga_w_out=ga_w_out.astype(BF16), ga_b_out=ga_b_out,
        rw_mu8=zpad(rw_mu, 1, 8),
        rw_wr=rw_wr.astype(BF16), rw_wk=rw_wk.astype(BF16), rw_wv=rw_wv.astype(BF16), rw_wo=rw_wo.astype(BF16),
        rw_w1c=jnp.concatenate([rw_w1[:, 0], rw_w1[:, 1]], axis=-1).astype(BF16),
        rw_w2c=jnp.concatenate([rw_w2[:, 0], rw_w2[:, 1]], axis=-2).astype(BF16),
        rw_a1c=jnp.concatenate([rw_a1[:, 0], rw_a1[:, 1]], axis=-1).astype(BF16),
        rw_a2c=jnp.concatenate([rw_a2[:, 0], rw_a2[:, 1]], axis=-2).astype(BF16),
        rw_w0=rw_w0, rw_a0=rw_a0,
        rw_g1p=zpad(rw_g1, 2, glora).astype(BF16), rw_g2p=zpad(rw_g2, 1, glora).astype(BF16),
        rw_k_k=rw_k_k, rw_k_a=rw_k_a, rw_r_k=rw_r_k, rw_ln_g=rw_ln_g, rw_ln_b=rw_ln_b,
        head_sel=head_sel, head_sel_t=head_sel.T,
        wrt=zpad(wrt, 2, LANES).astype(BF16), brt=zpad(brt, 1, LANES).reshape(depth, 1, LANES),
        moe_w_gate=moe_w_gate, moe_w_up=moe_w_up, moe_w_down=moe_w_down,
    )
    assert heads <= LANES and 2 * rw_w1.shape[-1] == LANES and 2 * rw_a1.shape[-1] == LANES

    xs = jnp.concatenate([x.reshape(t_lat, d), ctx.reshape(bsz * n_ctx, d)], axis=0)
    for i in range(depth):
        j = i // n_mixers
        mod_spec = mod_spec_for(i)
        if i % n_mixers == 0:
            xs, h2, route = _gmlp_layer(xs, mods3, mod_spec_for(i, GMLP_TM), i, j, p)
        else:
            r, v, kkn, gate, lw, km, bb = _rwkv_prep(xs, mods3, mod_spec, i, j, p, lat_tiles, tiles_per_batch)
            y0 = _wkv_scan(r, v, kkn, lw, km, bb, 0, bsz, n_lat, n_ctx)
            y1 = _wkv_scan(r, v, kkn, lw, km, bb, 1, bsz, n_lat, n_ctx)
            xs, h2, route = _rwkv_out(xs, mods3, mod_spec, y0, y1, r, v, km, gate, i, j, p)
        last = i == depth - 1
        xs = _moe(xs, h2, route, mods3, mod_spec, i, p, final=last, t_out=t_lat if last else xs.shape[0])
    return xs.reshape(bsz, n_lat, d)
```

```python
import functools
import math

import jax
import jax.numpy as jnp
from jax import lax
from jax.experimental import pallas as pl
from jax.experimental.pallas import tpu as pltpu

F32 = jnp.float32
BF16 = jnp.bfloat16

N_MOD = 6
NORM_EPS = 1e-6
GRID_W = 64
CHUNK = 128
GMLP_GROUP_CH = 128
LN_EPS = 1e-5
RWKV_HEAD = 64
GN_EPS = 64e-5
N_GROUPS = 4
EXPERTS_PER_GROUP = 8
N_EXPERTS = N_GROUPS * EXPERTS_PER_GROUP
TOP_K = 2

LANES = 128
TM = 256
GMLP_TM = 512
GMLP_SPLIT = 2
MOE_ROWS = 512
DEST_TM = 512
SCAN_BLOCK = 256
SCAN_CHUNK = 32
SCAN_LANES = 256
SCAN_GROUPS = 2
VMEM_LIMIT = 56 * 1024 * 1024
NEG_BIG = -3.0e38
SQRT_HALF = 0.7071067811865476
EXP_MINUS_HALF = 0.6065306597126334


def _dotb(a, b):
    return jnp.dot(a.astype(BF16), b.astype(BF16), preferred_element_type=F32)


def _dot_nt(a, b):
    return lax.dot_general(a.astype(BF16), b.astype(BF16), (((1,), (1,)), ((), ())), preferred_element_type=F32)


def _dot_tn(a, b):
    return lax.dot_general(a.astype(BF16), b.astype(BF16), (((0,), (0,)), ((), ())), preferred_element_type=F32)


def _split2(x):
    hi = x.astype(BF16)
    lo = (x - hi.astype(F32)).astype(BF16)
    return hi, lo


def _split3(x):
    hi = x.astype(BF16)
    r1 = x - hi.astype(F32)
    mid = r1.astype(BF16)
    lo = (r1 - mid.astype(F32)).astype(BF16)
    return hi, mid, lo


def _dot_x_sel(x, sel_bf16, terms=3):
    d = lambda p: jnp.dot(p, sel_bf16, preferred_element_type=F32)
    return sum(d(p) for p in (_split3(x) if terms == 3 else _split2(x)))


def _norm_mod(x, g, sc, sh):
    y = x * lax.rsqrt(jnp.mean(x * x, axis=-1, keepdims=True) + NORM_EPS)
    return (y * g) * (1.0 + sc) + sh


def _mod_parts(mod_ref):
    m = mod_ref[0]
    d = m.shape[-1] // N_MOD
    return [m[:, i * d:(i + 1) * d] for i in range(N_MOD)]


def _mods_kernel(s_ref, w_ref, b_ref, o_ref):
    s = s_ref[...]
    s = s * jax.nn.sigmoid(s)
    o_ref[0] = _dotb(s, w_ref[0]) + b_ref[0]


def _mods(s_rows, ada_w, ada_b):
    depth, d, nd = ada_w.shape
    tn = nd // 4
    return pl.pallas_call(
        _mods_kernel,
        grid=(depth, nd // tn),
        in_specs=[pl.BlockSpec(s_rows.shape, lambda i, j: (0, 0)),
                  pl.BlockSpec((1, d, tn), lambda i, j: (i, 0, j)),
                  pl.BlockSpec((1, 1, tn), lambda i, j: (i, 0, j))],
        out_specs=pl.BlockSpec((1, s_rows.shape[0], tn), lambda i, j: (i, 0, j)),
        out_shape=jax.ShapeDtypeStruct((depth, s_rows.shape[0], nd), F32),
        compiler_params=pltpu.CompilerParams(dimension_semantics=("parallel", "parallel"),
                                             vmem_limit_bytes=VMEM_LIMIT),
        name="mods",
    )(s_rows, ada_w, ada_b.reshape(depth, 1, nd))


def _route(logits):
    lane = lax.broadcasted_iota(jnp.int32, logits.shape, 1)
    lane_f = lane.astype(F32)
    big = jnp.float32(1e9)
    is_g = lane < N_GROUPS
    gl = jnp.where(is_g, logits, NEG_BIG)
    gm = jnp.max(gl, axis=1, keepdims=True)
    grp = jnp.min(jnp.where(is_g & (gl == gm), lane_f, big), axis=1, keepdims=True)
    p_grp = 1.0 / jnp.sum(jnp.where(is_g, jnp.exp(gl - gm), 0.0), axis=1, keepdims=True)
    lo = N_GROUPS + grp * EXPERTS_PER_GROUP
    in_grp = (lane_f >= lo) & (lane_f < lo + EXPERTS_PER_GROUP)
    el = jnp.where(in_grp, logits, NEG_BIG)
    m1 = jnp.max(el, axis=1, keepdims=True)
    i1 = jnp.min(jnp.where(in_grp & (el == m1), lane_f, big), axis=1, keepdims=True)
    rest = in_grp & (lane_f != i1)
    el2 = jnp.where(rest, logits, NEG_BIG)
    m2 = jnp.max(el2, axis=1, keepdims=True)
    i2 = jnp.min(jnp.where(rest & (el2 == m2), lane_f, big), axis=1, keepdims=True)
    e21 = jnp.exp(m2 - m1)
    s0 = 1.0 / (1.0 + e21)
    w0 = p_grp * s0
    w1 = p_grp * (e21 * s0)
    out = jnp.where(lane == 0, i1 - N_GROUPS,
                    jnp.where(lane == 1, i2 - N_GROUPS,
                              jnp.where(lane == 2, w0, jnp.where(lane == 3, w1, 0.0))))
    return out


def _residual_and_route(x, y, mods, g2_ref, wrt_ref, brt_ref, xo_ref, h2_ref, route_ref, rows=slice(None)):
    _, _, gt1, sh2, sc2, _ = mods
    xn = x + gt1 * y
    xo_ref[rows, :] = xn
    h2 = _norm_mod(xn, g2_ref[...], sc2, sh2)
    h2_ref[rows, :] = h2
    logits = _dotb(h2, wrt_ref[...]) + brt_ref[...]
    route_ref[rows, :] = _route(logits)


def _round_robin(gens):
    while gens:
        gens = [g for g in gens if next(g, StopIteration) is not StopIteration]


def _gmlp_rows(rows, x_ref, mod_ref, g1_ref, g2_ref, win_ref, bin_ref, lng_ref, lnb_ref, ws_ref, bs_ref,
               wout_ref, bout_ref, wrt_ref, brt_ref, xo_ref, h2_ref, route_ref, gated_ref):
    x = x_ref[rows, :]
    mods = _mod_parts(mod_ref)
    sh1, sc1 = mods[0], mods[1]
    h = _norm_mod(x, g1_ref[...], sc1, sh1).astype(BF16)
    yield
    width = win_ref.shape[1] // 2
    gelu = lambda z: 0.5 * z * (1.0 + lax.erf(z * SQRT_HALF))
    u = gelu(jnp.dot(h, win_ref[:, :width], preferred_element_type=F32) + bin_ref[:, :width])
    yield
    v = gelu(jnp.dot(h, win_ref[:, width:], preferred_element_type=F32) + bin_ref[:, width:])
    yield
    mu = jnp.mean(v, axis=-1, keepdims=True)
    vc = v - mu
    var = jnp.mean(vc * vc, axis=-1, keepdims=True)
    vn = ((vc * lax.rsqrt(var + LN_EPS)) * lng_ref[...] + lnb_ref[...]).astype(BF16)
    yield
    n_chunks = x.shape[0] // CHUNK
    for g in range(width // GMLP_GROUP_CH):
        cs = slice(g * GMLP_GROUP_CH, (g + 1) * GMLP_GROUP_CH)
        rhs = jnp.concatenate([vn[c * CHUNK:(c + 1) * CHUNK, cs] for c in range(n_chunks)], axis=1)
        s = jnp.dot(ws_ref[g], rhs, preferred_element_type=F32) + bs_ref[g]
        for c in range(n_chunks):
            rsl = slice(c * CHUNK, (c + 1) * CHUNK)
            dst = slice(rows.start + c * CHUNK, rows.start + (c + 1) * CHUNK)
            gated_ref[dst, cs] = (u[rsl, cs] * s[:, c * GMLP_GROUP_CH:(c + 1) * GMLP_GROUP_CH]).astype(BF16)
        if g % 4 == 3:
            yield
    y = jnp.dot(gated_ref[rows, :], wout_ref[...], preferred_element_type=F32) + bout_ref[...]
    yield
    _residual_and_route(x, y, mods, g2_ref, wrt_ref, brt_ref, xo_ref, h2_ref, route_ref, rows)
    yield


def _gmlp_kernel(*refs):
    group = refs[0].shape[0] // GMLP_SPLIT
    _round_robin([_gmlp_rows(slice(q * group, (q + 1) * group), *refs) for q in range(GMLP_SPLIT)])


def _full_spec(shape):
    nd = len(shape)
    return pl.BlockSpec(shape, lambda *_: (0,) * nd)


def _layer_spec(shape, j, single_buffer=False):
    nd = len(shape)
    kwargs = dict(pipeline_mode=pl.Buffered(1)) if single_buffer else {}
    return pl.BlockSpec((None,) + tuple(shape[1:]), lambda *_: (j,) + (0,) * (nd - 1), **kwargs)


def _mixer_out_shapes(t_tok, d):
    return (jax.ShapeDtypeStruct((t_tok, d), F32), jax.ShapeDtypeStruct((t_tok, d), F32),
            jax.ShapeDtypeStruct((t_tok, LANES), F32))


def _mixer_out_specs(d, tm=TM):
    return (pl.BlockSpec((tm, d), lambda t: (t, 0)), pl.BlockSpec((tm, d), lambda t: (t, 0)),
            pl.BlockSpec((tm, LANES), lambda t: (t, 0)))


def _gmlp_layer(x, mods3, mod_spec, i, j, p):
    t_tok, d = x.shape
    width = p["ga_w_out"].shape[1]
    row = lambda a: a.reshape(a.shape[0], 1, a.shape[1])
    args = (x, mods3, row(p["norm1_g"]), row(p["norm2_g"]), p["ga_w_in"], row(p["ga_b_in"]), row(p["ga_ln_g"]),
            row(p["ga_ln_b"]), p["ga_w_s"], p["ga_b_s"], p["ga_w_out"], row(p["ga_b_out"]), p["wrt"], p["brt"])
    in_specs = [pl.BlockSpec((GMLP_TM, d), lambda t: (t, 0)), mod_spec,
                _layer_spec(args[2].shape, i), _layer_spec(args[3].shape, i),
                _layer_spec(args[4].shape, j, True), _layer_spec(args[5].shape, j), _layer_spec(args[6].shape, j),
                _layer_spec(args[7].shape, j), _layer_spec(args[8].shape, j, True), _layer_spec(args[9].shape, j),
                _layer_spec(args[10].shape, j, True), _layer_spec(args[11].shape, j),
                _layer_spec(args[12].shape, i), _layer_spec(args[13].shape, i)]
    return pl.pallas_call(
        _gmlp_kernel,
        grid=(t_tok // GMLP_TM,),
        in_specs=in_specs,
        out_specs=_mixer_out_specs(d, GMLP_TM),
        out_shape=_mixer_out_shapes(t_tok, d),
        scratch_shapes=[pltpu.VMEM((GMLP_TM, width), BF16)],
        compiler_params=pltpu.CompilerParams(dimension_semantics=("parallel",), vmem_limit_bytes=VMEM_LIMIT),
        name="gmlp_layer",
    )(*args)


def _seg_sum(q, e_ref):
    return _dot_x_sel(q, e_ref[...], terms=2)


def _seg_expand(s, et_ref):
    return _dot_x_sel(s, et_ref[...], terms=2)


def _shifted(h, h_above, h_below, is_ctx, first_row_tile, last_row_tile):
    tm, d = h.shape
    q = d // 4
    row = lax.broadcasted_iota(jnp.int32, (tm, 1), 0)
    ctx_i = is_ctx.astype(jnp.int32)
    col = row & (jnp.where(is_ctx, tm, GRID_W) - 1)
    last_col = jnp.where(is_ctx, tm, GRID_W) - 1
    prev1 = jnp.where(col == 0, 0.0, pltpu.roll(h, 1, axis=0))
    next1 = jnp.where(col == last_col, 0.0, pltpu.roll(h, tm - 1, axis=0))
    up = jnp.concatenate([h_above, h[:tm - GRID_W]], axis=0)
    up = jnp.where(row < jnp.where(first_row_tile, GRID_W, 0), 0.0, up)
    down = jnp.concatenate([h[GRID_W:], h_below], axis=0)
    down = jnp.where(row >= jnp.where(last_row_tile, tm - GRID_W, tm), 0.0, down)
    lane_q = lax.broadcasted_iota(jnp.int32, (1, d), 1) >> int(math.log2(q))
    src = lane_q * (1 - ctx_i) + (lane_q >> 1) * ctx_i
    return jnp.where(src == 0, prev1, jnp.where(src == 1, next1, jnp.where(src == 2, up, down)))


def _rwkv_prep_kernel(x_ref, xa_ref, xb_ref, mod_ref, n1_ref, mu_ref, wr_ref, wk_ref, wv_ref, w1_ref, w2_ref, w0_ref,
                      a1_ref, a2_ref, a0_ref, g1_ref, g2_ref, kk_ref, ka_ref, e_ref, et_ref,
                      r_o, v_o, kkn_o, gate_o, lw_o, km_o, bb_o, *, lat_tiles, tiles_per_batch):
    t = pl.program_id(0)
    mods = _mod_parts(mod_ref)
    norm = lambda x: _norm_mod(x, n1_ref[...], mods[1], mods[0])
    h = norm(x_ref[...])
    tb = t % tiles_per_batch
    hs = _shifted(h, norm(xa_ref[...]), norm(xb_ref[...]), t >= lat_tiles, tb == 0, tb == tiles_per_batch - 1)
    xx = hs - h
    mu = mu_ref[...]
    xr, xw, xk, xv, xa, xg = [h + xx * mu[q:q + 1] for q in range(6)]
    r = _dotb(xr, wr_ref[...])
    k = _dotb(xk, wk_ref[...])
    v = _dotb(xv, wv_ref[...])
    r_o[...] = r
    v_o[...] = v
    kkr = k * kk_ref[...]
    rs = lax.rsqrt(_seg_sum(kkr * kkr, e_ref) + 1e-12)
    kkn = kkr * _seg_expand(rs, et_ref)
    kkn_o[...] = kkn
    gate_o[...] = _dotb(jax.nn.sigmoid(_dotb(xg, g1_ref[...])), g2_ref[...])
    tw = jnp.tanh(_dotb(xw, w1_ref[...]))
    ta = _dotb(xa, a1_ref[...])
    lane = lax.broadcasted_iota(jnp.int32, tw.shape, 1)
    half = tw.shape[1] // 2
    w0 = w0_ref[...]
    a0 = a0_ref[...]
    ka = ka_ref[...]
    for z in range(2):
        zm = (lane >= z * half) & (lane < (z + 1) * half)
        lwz = _dotb(jnp.where(zm, tw, 0.0), w2_ref[...])
        lw_o[z] = -EXP_MINUS_HALF * jax.nn.sigmoid(w0[z:z + 1] + lwz)
        la = _dotb(jnp.where(zm, ta, 0.0), a2_ref[...])
        a = jax.nn.sigmoid(a0[z:z + 1] + la)
        km_o[z] = k * (1.0 + (a - 1.0) * ka)
        bb_o[z] = kkn * a


def _rwkv_prep(x, mods3, mod_spec, i, j, p, lat_tiles, tiles_per_batch):
    t_tok, d = x.shape
    row = lambda a: a.reshape(a.shape[0], 1, a.shape[1])
    args = (x, x, x, mods3, row(p["norm1_g"]), p["rw_mu8"], p["rw_wr"], p["rw_wk"], p["rw_wv"], p["rw_w1c"],
            p["rw_w2c"], p["rw_w0"], p["rw_a1c"], p["rw_a2c"], p["rw_a0"], p["rw_g1p"], p["rw_g2p"],
            row(p["rw_k_k"]), row(p["rw_k_a"]), p["head_sel"], p["head_sel_t"])
    tok = pl.BlockSpec((TM, d), lambda t: (t, 0))
    rows_per_tile = TM // GRID_W
    last_row = t_tok // GRID_W - 1
    above = pl.BlockSpec((GRID_W, d), lambda t: (jnp.maximum(t * rows_per_tile - 1, 0), 0))
    below = pl.BlockSpec((GRID_W, d), lambda t: (jnp.minimum((t + 1) * rows_per_tile, last_row), 0))
    in_specs = ([tok, above, below, mod_spec, _layer_spec(args[4].shape, i)]
                + [_layer_spec(a.shape, j) for a in args[5:19]]
                + [_full_spec(args[19].shape), _full_spec(args[20].shape)])
    dir_spec = pl.BlockSpec((2, TM, d), lambda t: (0, t, 0))
    tok_shape = jax.ShapeDtypeStruct((t_tok, d), F32)
    dir_shape = jax.ShapeDtypeStruct((2, t_tok, d), F32)
    return pl.pallas_call(
        functools.partial(_rwkv_prep_kernel, lat_tiles=lat_tiles, tiles_per_batch=tiles_per_batch),
        grid=(t_tok // TM,),
        in_specs=in_specs,
        out_specs=(tok, tok, tok, tok, dir_spec, dir_spec, dir_spec),
        out_shape=(tok_shape, tok_shape, tok_shape, tok_shape, dir_shape, dir_shape, dir_shape),
        compiler_params=pltpu.CompilerParams(dimension_semantics=("parallel",), vmem_limit_bytes=VMEM_LIMIT),
        name="rwkv_prep",
    )(*args)


def _scan_prepare(r, v, kk, lw, km, bb, rev, out):
    c = SCAN_CHUNK
    nb = SCAN_BLOCK
    a = -kk
    row = lax.broadcasted_iota(jnp.int32, (nb, nb), 0)
    col = lax.broadcasted_iota(jnp.int32, (nb, nb), 1)
    shift_c = int(math.log2(c))
    same = (row >> shift_c) == (col >> shift_c)
    if rev:
        incl = same & (col >= row)
        strict = same & (col > row)
    else:
        incl = same & (col <= row)
        strict = same & (col < row)
    incl_b = jnp.where(incl, 1.0, 0.0).astype(BF16)
    eye = jnp.where(row == col, 1.0, 0.0)
    head_of_lane = col >> int(math.log2(RWKV_HEAD))

    lw_hi, lw_lo = _split2(lw)
    cum = (jnp.dot(incl_b, lw_hi, preferred_element_type=F32)
           + jnp.dot(incl_b, lw_lo, preferred_element_type=F32))
    yield
    last = [ci * c if rev else ci * c + c - 1 for ci in range(nb // c)]
    tot_rows = [cum[t:t + 1] for t in last]
    tot = jnp.concatenate([jnp.broadcast_to(t, (c, t.shape[1])) for t in tot_rows], axis=0)
    yield
    at = a * jnp.exp(cum - lw)
    rt = r * jnp.exp(cum)
    e_inv = jnp.exp(-cum)
    bt = bb * e_inv
    kt = km * e_inv
    e_rem = jnp.exp(tot - cum)
    bh = (bb * e_rem).astype(BF16)
    kh = (km * e_rem).astype(BF16)
    et = jnp.exp(jnp.concatenate(tot_rows, axis=0))
    bk = jnp.concatenate([bt, kt], axis=0).astype(BF16)
    yield
    n_heads = SCAN_LANES // RWKV_HEAD
    heads = range(n_heads)
    hms = [head_of_lane == hh for hh in heads]
    at_h = [jnp.where(hm, at, 0.0) for hm in hms]
    rt_h = [jnp.where(hm, rt, 0.0) for hm in hms]
    v_h = [jnp.where(hm, v, 0.0).astype(BF16) for hm in hms]
    v_roll = pltpu.roll(v, RWKV_HEAD, axis=1)
    v_s = [jnp.where(hms[(hh + 1) % n_heads], v_roll, 0.0).astype(BF16) for hh in heads]
    a_ab, a_ak, m_rb, m_rk = [], [], [], []
    for hh in heads:
        s = _dot_nt(jnp.concatenate([at_h[hh], rt_h[hh]], axis=0), bk)
        a_ab.append(jnp.where(strict, s[:nb, :nb], 0.0))
        a_ak.append(jnp.where(strict, s[:nb, nb:], 0.0).astype(BF16))
        m_rb.append(jnp.where(incl, s[nb:, :nb], 0.0).astype(BF16))
        m_rk.append(jnp.where(incl, s[nb:, nb:], 0.0).astype(BF16))
        yield
    pw = list(a_ab)
    tinv = [eye + m for m in a_ab]
    for _ in range(shift_c - 1):
        for hh in heads:
            pw[hh] = _dotb(pw[hh], pw[hh])
        yield
        for hh in heads:
            tinv[hh] = tinv[hh] + _dotb(tinv[hh], pw[hh])
        yield
    x1 = [jnp.dot(a_ak[hh], v_s[hh], preferred_element_type=F32) for hh in heads]
    yield
    tc = [_dotb(tinv[hh], at_h[hh] + x1[hh]) for hh in heads]
    yield
    mc = [_dotb(m_rb[hh], tc[hh]) for hh in heads]
    yield

    def pick(parts, shift):
        res = parts[(n_heads - 1 - shift) % n_heads]
        for lb in range(n_heads - 2, -1, -1):
            res = jnp.where(hms[lb], parts[(lb - shift) % n_heads], res)
        return res

    back = SCAN_LANES - RWKV_HEAD
    ah = pick(tc, 0)
    vh = pltpu.roll(pick(tc, 1), back, axis=1)
    rh = rt + pick(mc, 0)
    yh = (jnp.dot(jnp.concatenate(m_rk, axis=1), jnp.concatenate(v_h, axis=0), preferred_element_type=F32)
          + pltpu.roll(pick(mc, 1), back, axis=1))
    yield
    out.extend([ah.astype(BF16), rh.astype(BF16), vh, yh, bh, kh, v.astype(BF16), et])


def _scan_prepare_stages():
    return 7 + (SCAN_LANES // RWKV_HEAD) + 2 * (int(math.log2(SCAN_CHUNK)) - 1)


def _scan_chain(g, ops, y_ref, lanes, rev, out):
    c = SCAN_CHUNK
    ah, rh, vh, yh, bh, kh, vb, et = ops
    n = g.shape[0]
    shift_h = int(math.log2(RWKV_HEAD))
    blockdiag = ((lax.broadcasted_iota(jnp.int32, (n, n), 0) >> shift_h)
                 == (lax.broadcasted_iota(jnp.int32, (n, n), 1) >> shift_h))
    n_chunks = SCAN_BLOCK // c
    order = range(n_chunks - 1, -1, -1) if rev else range(n_chunks)
    for ci in order:
        rs = slice(ci * c, (ci + 1) * c)
        p = _dot_nt(jnp.concatenate([ah[rs], rh[rs]], axis=0), g)
        u = p[:c] + vh[rs]
        y_ref[rs, lanes] = p[c:] + yh[rs]
        yield
        dg = _dot_tn(jnp.concatenate([u.astype(BF16), vb[rs]], axis=0), jnp.concatenate([bh[rs], kh[rs]], axis=0))
        g = jnp.where(blockdiag, g * et[ci:ci + 1, :] + dg, 0.0)
        yield
    out.append(g)


def _scan_chain_stages():
    return 2 * (SCAN_BLOCK // SCAN_CHUNK)


def _interleave(gen_a, n_a, gen_b, n_b):
    done_b = 0
    for i in range(n_a):
        next(gen_a, None)
        want_b = ((i + 1) * n_b) // n_a
        while done_b < want_b:
            next(gen_b, None)
            done_b += 1
        yield
    for g in (gen_a, gen_b):
        for _ in g:
            pass


def _scan_kernel(r_ref, v_ref, kk_ref, lw_ref, km_ref, bb_ref, y_ref, g_scr, *ops_scr, rev, groups):
    @pl.when(pl.program_id(2) == 0)
    def _():
        g_scr[...] = jnp.zeros_like(g_scr)
        for ref in ops_scr:
            ref[...] = jnp.zeros_like(ref)

    slot = pl.program_id(2) % 2
    work, results = [], []
    for q in range(groups):
        lanes = slice(q * SCAN_LANES, (q + 1) * SCAN_LANES)
        prev = [ref.at[1 - slot, :, lanes] for ref in ops_scr]
        new, g_end = [], []
        prep = _scan_prepare(r_ref[:, lanes], v_ref[:, lanes], kk_ref[:, lanes], lw_ref[:, lanes], km_ref[:, lanes],
                             bb_ref[:, lanes], rev, new)
        chain = _scan_chain(g_scr[q], prev, y_ref, lanes, rev, g_end)
        work.append(_interleave(prep, _scan_prepare_stages(), chain, _scan_chain_stages()))
        results.append((lanes, new, g_end))
    while work:
        work = [w for w in work if next(w, StopIteration) is not StopIteration]
    for q, (lanes, new, g_end) in enumerate(results):
        g_scr[q] = g_end[0]
        for ref, val in zip(ops_scr, new):
            ref[slot, :, lanes] = val


def _wkv_scan(r, v, kkn, lw, km, bb, z, bsz, n_lat, n_ctx):
    t_tok, d = r.shape
    lat_blocks = n_lat // SCAN_BLOCK
    ctx_blocks = n_ctx // SCAN_BLOCK
    steps = ctx_blocks + lat_blocks
    ctx_base = bsz * lat_blocks
    rev = z == 1
    width = SCAN_LANES * SCAN_GROUPS

    def blk(b, s):
        if rev:
            cb = ctx_base + b * ctx_blocks + (ctx_blocks - 1 - s)
            lb = b * lat_blocks + (lat_blocks - 1 - (s - ctx_blocks))
        else:
            cb = ctx_base + b * ctx_blocks + s
            lb = b * lat_blocks + (s - ctx_blocks)
        return jnp.where(s < ctx_blocks, cb, lb)

    in_blk = lambda b, s: blk(b, jnp.minimum(s, steps - 1))
    out_blk = lambda b, s: blk(b, jnp.maximum(s - 1, 0))
    tok = pl.BlockSpec((SCAN_BLOCK, width), lambda b, g, s: (in_blk(b, s), g))
    dirs = pl.BlockSpec((None, SCAN_BLOCK, width), lambda b, g, s: (z, in_blk(b, s), g))
    n_chunks = SCAN_BLOCK // SCAN_CHUNK
    ops_scratch = [pltpu.VMEM((2, SCAN_BLOCK, width), BF16), pltpu.VMEM((2, SCAN_BLOCK, width), BF16),
                   pltpu.VMEM((2, SCAN_BLOCK, width), F32), pltpu.VMEM((2, SCAN_BLOCK, width), F32),
                   pltpu.VMEM((2, SCAN_BLOCK, width), BF16), pltpu.VMEM((2, SCAN_BLOCK, width), BF16),
                   pltpu.VMEM((2, SCAN_BLOCK, width), BF16), pltpu.VMEM((2, n_chunks, width), F32)]
    return pl.pallas_call(
        functools.partial(_scan_kernel, rev=rev, groups=SCAN_GROUPS),
        grid=(bsz, d // width, steps + 1),
        in_specs=[tok, tok, tok, dirs, dirs, dirs],
        out_specs=pl.BlockSpec((SCAN_BLOCK, width), lambda b, g, s: (out_blk(b, s), g)),
        out_shape=jax.ShapeDtypeStruct((t_tok, d), F32),
        scratch_shapes=[pltpu.VMEM((SCAN_GROUPS, SCAN_LANES, SCAN_LANES), F32)] + ops_scratch,
        compiler_params=pltpu.CompilerParams(dimension_semantics=("parallel", "parallel", "arbitrary"),
                                             vmem_limit_bytes=VMEM_LIMIT),
        name="wkv_scan_rev" if rev else "wkv_scan_fwd",
    )(r, v, kkn, lw, km, bb)


def _rwkv_out_kernel(x_ref, mod_ref, y0_ref, y1_ref, r_ref, v_ref, km_ref, gate_ref, lng_ref, lnb_ref, rk_ref,
                     wo_ref, e_ref, et_ref, g2_ref, wrt_ref, brt_ref, xo_ref, h2_ref, route_ref):
    x = x_ref[...]
    mods = _mod_parts(mod_ref)
    y = y0_ref[...] + y1_ref[...]
    inv_n = 1.0 / RWKV_HEAD
    mu = _seg_expand(_seg_sum(y, e_ref) * inv_n, et_ref)
    yc = y - mu
    var = _seg_sum(yc * yc, e_ref) * inv_n
    yn = (yc * _seg_expand(lax.rsqrt(var + GN_EPS), et_ref)) * lng_ref[...] + lnb_ref[...]
    v = v_ref[...]
    bonus = _seg_sum((r_ref[...] * rk_ref[...]) * (km_ref[0] + km_ref[1]), e_ref)
    yn = yn + _seg_expand(bonus, et_ref) * v
    out = _dotb(yn * gate_ref[...], wo_ref[...])
    _residual_and_route(x, out, mods, g2_ref, wrt_ref, brt_ref, xo_ref, h2_ref, route_ref)


def _rwkv_out(x, mods3, mod_spec, y0, y1, r, v, km, gate, i, j, p):
    t_tok, d = x.shape
    row = lambda a: a.reshape(a.shape[0], 1, a.shape[1])
    tok = pl.BlockSpec((TM, d), lambda t: (t, 0))
    dir_spec = pl.BlockSpec((2, TM, d), lambda t: (0, t, 0))
    args = (x, mods3, y0, y1, r, v, km, gate, row(p["rw_ln_g"]), row(p["rw_ln_b"]), row(p["rw_r_k"].reshape(-1, d)),
            p["rw_wo"], p["head_sel"], p["head_sel_t"], row(p["norm2_g"]), p["wrt"], p["brt"])
    in_specs = [tok, mod_spec, tok, tok, tok, tok, dir_spec, tok,
                _layer_spec(args[8].shape, j), _layer_spec(args[9].shape, j), _layer_spec(args[10].shape, j),
                _layer_spec(args[11].shape, j), _full_spec(args[12].shape), _full_spec(args[13].shape),
                _layer_spec(args[14].shape, i), _layer_spec(args[15].shape, i), _layer_spec(args[16].shape, i)]
    return pl.pallas_call(
        _rwkv_out_kernel,
        grid=(t_tok // TM,),
        in_specs=in_specs,
        out_specs=_mixer_out_specs(d),
        out_shape=_mixer_out_shapes(t_tok, d),
        compiler_params=pltpu.CompilerParams(dimension_semantics=("parallel",), vmem_limit_bytes=VMEM_LIMIT),
        name="rwkv_out",
    )(*args)


def _expert_kernel(be_ref, nu_ref, x_ref, wg_ref, wu_ref, wd_ref, o_ref, wg_s, wu_s, wd_s):
    b = pl.program_id(0)
    e = be_ref[b]
    prev = be_ref[jnp.maximum(b - 1, 0)]
    used = b < nu_ref[0]

    @pl.when(used & ((b == 0) | (e != prev)))
    def _():
        wg_s[...] = wg_ref[...].astype(BF16)
        wu_s[...] = wu_ref[...].astype(BF16)
        wd_s[...] = wd_ref[...].astype(BF16)

    @pl.when(used)
    def _():
        xb = x_ref[...].astype(BF16)
        g = jnp.dot(xb, wg_s[...], preferred_element_type=F32)
        u = jnp.dot(xb, wu_s[...], preferred_element_type=F32)
        hmid = (g * jax.nn.sigmoid(g)) * u
        o_ref[...] = jnp.dot(hmid.astype(BF16), wd_s[...], preferred_element_type=F32)

    @pl.when(jnp.logical_not(used))
    def _():
        o_ref[...] = jnp.zeros_like(o_ref)


def _experts(buf, block_expert, n_used, w_gate, w_up, w_down, i):
    cap, d = buf.shape
    f = w_gate.shape[-1]
    n_blocks = cap // MOE_ROWS
    grid_spec = pltpu.PrefetchScalarGridSpec(
        num_scalar_prefetch=2,
        grid=(n_blocks,),
        in_specs=[pl.BlockSpec((MOE_ROWS, d), lambda b, be, nu: (b, 0)),
                  pl.BlockSpec((None, None, d, f), lambda b, be, nu: (i, be[b], 0, 0)),
                  pl.BlockSpec((None, None, d, f), lambda b, be, nu: (i, be[b], 0, 0)),
                  pl.BlockSpec((None, None, f, d), lambda b, be, nu: (i, be[b], 0, 0))],
        out_specs=pl.BlockSpec((MOE_ROWS, d), lambda b, be, nu: (b, 0)),
        scratch_shapes=[pltpu.VMEM((d, f), BF16), pltpu.VMEM((d, f), BF16), pltpu.VMEM((f, d), BF16)],
    )
    return pl.pallas_call(
        _expert_kernel,
        grid_spec=grid_spec,
        out_shape=jax.ShapeDtypeStruct((cap, d), F32),
        compiler_params=pltpu.CompilerParams(dimension_semantics=("arbitrary",), vmem_limit_bytes=VMEM_LIMIT),
        name="experts",
    )(block_expert, n_used, buf, w_gate, w_up, w_down)


def _combine_kernel(x_ref, mod_ref, route_ref, y0_ref, y1_ref, fg_ref, xo_ref, *, final):
    gt2 = _mod_parts(mod_ref)[5]
    route = route_ref[...]
    xn = x_ref[...] + gt2 * (route[:, 2:3] * y0_ref[...] + route[:, 3:4] * y1_ref[...])
    if final:
        xn = (xn * lax.rsqrt(jnp.mean(xn * xn, axis=-1, keepdims=True) + NORM_EPS)) * fg_ref[...]
    xo_ref[...] = xn


def _combine(x, mods3, mod_spec, route, y0, y1, final_g, final, t_out):
    d = x.shape[1]
    tok = pl.BlockSpec((TM, d), lambda t: (t, 0))
    return pl.pallas_call(
        functools.partial(_combine_kernel, final=final),
        grid=(t_out // TM,),
        in_specs=[tok, mod_spec, pl.BlockSpec((TM, LANES), lambda t: (t, 0)), tok, tok, _full_spec((1, d))],
        out_specs=tok,
        out_shape=jax.ShapeDtypeStruct((t_out, d), F32),
        compiler_params=pltpu.CompilerParams(dimension_semantics=("parallel",), vmem_limit_bytes=VMEM_LIMIT),
        name="moe_combine",
    )(x, mods3, route, y0, y1, final_g.reshape(1, d))


def _dest_kernel(route_ref, d_ref, counts_ref, base_scr, start_scr):
    phase = pl.program_id(0)
    t = pl.program_id(1)
    route = route_ref[...]
    tm = route.shape[0]
    lane = lax.broadcasted_iota(jnp.int32, route.shape, 1)
    lane_f = lane.astype(F32)
    onehots = [jnp.where(lane_f == route[:, k:k + 1], 1.0, 0.0) for k in range(TOP_K)]

    @pl.when((phase == 0) & (t == 0))
    def _():
        base_scr[...] = jnp.zeros_like(base_scr)

    @pl.when((phase == 1) & (t == 0))
    def _():
        counts = base_scr[...]
        counts_ref[...] = counts
        padded = jnp.floor((counts + (MOE_ROWS - 1)) * (1.0 / MOE_ROWS)) * MOE_ROWS
        before = (lax.broadcasted_iota(jnp.int32, (LANES, LANES), 0)
                  < lax.broadcasted_iota(jnp.int32, (LANES, LANES), 1))
        start_scr[...] = _dot_x_sel(padded, jnp.where(before, 1.0, 0.0).astype(BF16))
        base_scr[...] = jnp.zeros_like(base_scr)

    @pl.when(phase == 0)
    def _():
        base_scr[...] = base_scr[...] + sum(jnp.sum(oh, axis=0, keepdims=True) for oh in onehots)

    @pl.when(phase == 1)
    def _():
        earlier = (lax.broadcasted_iota(jnp.int32, (tm, tm), 1)
                   < lax.broadcasted_iota(jnp.int32, (tm, tm), 0))
        earlier_b = jnp.where(earlier, 1.0, 0.0).astype(BF16)
        diag = (lax.broadcasted_iota(jnp.int32, (LANES, LANES), 0)
                == lax.broadcasted_iota(jnp.int32, (LANES, LANES), 1))
        base = base_scr[...] + start_scr[...]
        for k, oh in enumerate(onehots):
            before = jnp.dot(earlier_b, oh.astype(BF16), preferred_element_type=F32)
            dest = jnp.sum(oh * (base + before), axis=1, keepdims=True)
            rows = [jnp.sum(jnp.where(diag, jnp.broadcast_to(dest[i * LANES:(i + 1) * LANES], (LANES, LANES)), 0.0),
                            axis=0, keepdims=True) for i in range(tm // LANES)]
            d_ref[k, 0] = jnp.concatenate(rows, axis=0).astype(jnp.int32)
            base = base + jnp.sum(oh, axis=0, keepdims=True)
        base_scr[...] = base - start_scr[...]


def _dests(route):
    t_tok = route.shape[0]
    assert t_tok % DEST_TM == 0
    tiles = t_tok // DEST_TM
    return pl.pallas_call(
        _dest_kernel,
        grid=(2, tiles),
        in_specs=[pl.BlockSpec((DEST_TM, LANES), lambda ph, t: (t, 0))],
        out_specs=(pl.BlockSpec((TOP_K, 1, DEST_TM // LANES, LANES), lambda ph, t: (0, t * ph, 0, 0)),
                   pl.BlockSpec((1, LANES), lambda ph, t: (0, 0))),
        out_shape=(jax.ShapeDtypeStruct((TOP_K, tiles, DEST_TM // LANES, LANES), jnp.int32),
                   jax.ShapeDtypeStruct((1, LANES), F32)),
        scratch_shapes=[pltpu.VMEM((1, LANES), F32), pltpu.VMEM((1, LANES), F32)],
        compiler_params=pltpu.CompilerParams(dimension_semantics=("arbitrary", "arbitrary"),
                                             vmem_limit_bytes=VMEM_LIMIT),
        name="moe_dests",
    )(route)


def _moe(x, h2, route, mods3, mod_spec, i, p, final, t_out):
    t_tok, d = x.shape
    n_assign = t_tok * TOP_K
    dest4, counts_f = _dests(route)
    dest = dest4.reshape(TOP_K, t_tok)
    counts = counts_f[0, :N_EXPERTS].astype(jnp.int32)
    padded = ((counts + MOE_ROWS - 1) // MOE_ROWS) * MOE_ROWS
    pad_end = jnp.cumsum(padded)
    n_blocks = (n_assign + MOE_ROWS - 1) // MOE_ROWS + N_EXPERTS
    block_start = jnp.arange(n_blocks, dtype=jnp.int32) * MOE_ROWS
    block_expert = jnp.minimum(jnp.sum((pad_end[None, :] <= block_start[:, None]).astype(jnp.int32), axis=1),
                               N_EXPERTS - 1)
    n_used = (pad_end[-1:] // MOE_ROWS).astype(jnp.int32)
    token = jnp.arange(t_tok, dtype=jnp.int32)
    _, row_token = lax.sort_key_val(dest.reshape(-1), jnp.tile(token, TOP_K))
    row_token = jnp.concatenate([row_token, token[:MOE_ROWS]])
    first = (jnp.cumsum(counts) - counts)[block_expert] + block_start - (pad_end - padded)[block_expert]
    first = jnp.clip(first, 0, n_assign)
    slot_token = jax.vmap(lambda s: lax.dynamic_slice(row_token, (s,), (MOE_ROWS,)))(first).reshape(-1)
    buf = h2.at[slot_token].get(mode="promise_in_bounds")
    out = _experts(buf, block_expert, n_used, p["moe_w_gate"], p["moe_w_up"], p["moe_w_down"], i)
    y0 = out.at[dest[0]].get(mode="promise_in_bounds", unique_indices=True)
    y1 = out.at[dest[1]].get(mode="promise_in_bounds", unique_indices=True)
    return _combine(x, mods3, mod_spec, route, y0, y1, p["final_g"], final, t_out)


def kernel(x, c, ctx, c_ctx, ada_w, ada_b, norm1_g, norm2_g, final_g, ga_w_in, ga_b_in, ga_ln_g, ga_ln_b, ga_w_s,
           ga_b_s, ga_w_out, ga_b_out, rw_mu, rw_wr, rw_wk, rw_wv, rw_wo, rw_w0, rw_w1, rw_w2, rw_a0, rw_a1, rw_a2,
           rw_g1, rw_g2, rw_k_k, rw_k_a, rw_r_k, rw_ln_g, rw_ln_b, moe_w_grp, moe_b_grp, moe_w_exp, moe_b_exp,
           moe_w_gate, moe_w_up, moe_w_down):
    bsz, n_lat, d = x.shape
    n_ctx = ctx.shape[1]
    depth = ada_w.shape[0]
    n_mixers = 2
    assert n_lat % TM == 0 and n_ctx == TM and TM % GRID_W == 0
    assert n_lat % GMLP_TM == 0 and (bsz * n_ctx) % GMLP_TM == 0 and GMLP_TM % (GMLP_SPLIT * CHUNK) == 0
    assert n_lat % SCAN_BLOCK == 0 and n_ctx % SCAN_BLOCK == 0
    assert bsz + 1 <= 8 and d % (SCAN_LANES * SCAN_GROUPS) == 0
    assert MOE_ROWS & (MOE_ROWS - 1) == 0
    t_lat = bsz * n_lat
    lat_tiles = t_lat // TM
    tiles_per_batch = n_lat // TM
    heads = d // RWKV_HEAD

    s_rows = jnp.concatenate([c, c_ctx[None, :], jnp.zeros((8 - bsz - 1, d), F32)], axis=0)
    mods = _mods(s_rows, ada_w, ada_b)
    mods3 = mods.reshape(depth * 8, 1, N_MOD * d)

    def mod_spec_for(i, tm=TM):
        def imap(t):
            return (i * 8 + jnp.where(t < t_lat // tm, t // (n_lat // tm), bsz), 0, 0)
        return pl.BlockSpec((1, 1, N_MOD * d), imap)

    head_sel = (jnp.arange(d)[:, None] // RWKV_HEAD == jnp.arange(LANES)[None, :]).astype(BF16)
    zpad = lambda a, axis, n: jnp.pad(a, [(0, n - a.shape[k]) if k == axis else (0, 0) for k in range(a.ndim)])
    glora = ((rw_g1.shape[-1] + LANES - 1) // LANES) * LANES
    wrt = jnp.concatenate([moe_w_grp, moe_w_exp], axis=-1)
    brt = jnp.concatenate([moe_b_grp, moe_b_exp], axis=-1)
    p = dict(
        norm1_g=norm1_g, norm2_g=norm2_g, final_g=final_g,
        ga_w_in=ga_w_in.astype(BF16), ga_b_in=ga_b_in, ga_ln_g=ga_ln_g, ga_ln_b=ga_ln_b,
        ga_w_s=ga_w_s.astype(BF16), ga_b_s=ga_b_s[..., None], ga_w_out=ga_w_out.astype(BF16), ga_b_out=ga_b_out,
        rw_mu8=zpad(rw_mu, 1, 8),
        rw_wr=rw_wr.astype(BF16), rw_wk=rw_wk.astype(BF16), rw_wv=rw_wv.astype(BF16), rw_wo=rw_wo.astype(BF16),
        rw_w1c=jnp.concatenate([rw_w1[:, 0], rw_w1[:, 1]], axis=-1).astype(BF16),
        rw_w2c=jnp.concatenate([rw_w2[:, 0], rw_w2[:, 1]], axis=-2).astype(BF16),
        rw_a1c=jnp.concatenate([rw_a1[:, 0], rw_a1[:, 1]], axis=-1).astype(BF16),
        rw_a2c=jnp.concatenate([rw_a2[:, 0], rw_a2[:, 1]], axis=-2).astype(BF16),
        rw_w0=rw_w0, rw_a0=rw_a0,
        rw_g1p=zpad(rw_g1, 2, glora).astype(BF16), rw_g2p=zpad(rw_g2, 1, glora).astype(BF16),
        rw_k_k=rw_k_k, rw_k_a=rw_k_a, rw_r_k=rw_r_k, rw_ln_g=rw_ln_g, rw_ln_b=rw_ln_b,
        head_sel=head_sel, head_sel_t=head_sel.T,
        wrt=zpad(wrt, 2, LANES).astype(BF16), brt=zpad(brt, 1, LANES).reshape(depth, 1, LANES),
        moe_w_gate=moe_w_gate, moe_w_up=moe_w_up, moe_w_down=moe_w_down,
    )
    assert heads <= LANES and 2 * rw_w1.shape[-1] == LANES and 2 * rw_a1.shape[-1] == LANES

    xs = jnp.concatenate([x.reshape(t_lat, d), ctx.reshape(bsz * n_ctx, d)], axis=0)
    for i in range(depth):
        j = i // n_mixers
        mod_spec = mod_spec_for(i)
        if i % n_mixers == 0:
            xs, h2, route = _gmlp_layer(xs, mods3, mod_spec_for(i, GMLP_TM), i, j, p)
        else:
            r, v, kkn, gate, lw, km, bb = _rwkv_prep(xs, mods3, mod_spec, i, j, p, lat_tiles, tiles_per_batch)
            y0 = _wkv_scan(r, v, kkn, lw, km, bb, 0, bsz, n_lat, n_ctx)
            y1 = _wkv_scan(r, v, kkn, lw, km, bb, 1, bsz, n_lat, n_ctx)
            xs, h2, route = _rwkv_out(xs, mods3, mod_spec, y0, y1, r, v, km, gate, i, j, p)
        last = i == depth - 1
        xs = _moe(xs, h2, route, mods3, mod_spec, i, p, final=last, t_out=t_lat if last else xs.shape[0])
    return xs.reshape(bsz, n_lat, d)
```

```python
import functools
import math

import jax
import jax.numpy as jnp
from jax import lax
from jax.experimental import pallas as pl
from jax.experimental.pallas import tpu as pltpu

F32 = jnp.float32
BF16 = jnp.bfloat16

N_MOD = 6
NORM_EPS = 1e-6
GRID_W = 64
CHUNK = 128
GMLP_GROUP_CH = 128
LN_EPS = 1e-5
RWKV_HEAD = 64
GN_EPS = 64e-5
N_GROUPS = 4
EXPERTS_PER_GROUP = 8
N_EXPERTS = N_GROUPS * EXPERTS_PER_GROUP
TOP_K = 2

LANES = 128
TM = 256
GMLP_TM = 512
GMLP_SPLIT = 2
MOE_ROWS = 512
DEST_TM = 512
SCAN_BLOCK = 256
SCAN_CHUNK = 32
SCAN_LANES = 256
SCAN_GROUPS = 2
VMEM_LIMIT = 56 * 1024 * 1024
NEG_BIG = -3.0e38
SQRT_HALF = 0.7071067811865476
EXP_MINUS_HALF = 0.6065306597126334


def _dotb(a, b):
    return jnp.dot(a.astype(BF16), b.astype(BF16), preferred_element_type=F32)


def _dot_nt(a, b):
    return lax.dot_general(a.astype(BF16), b.astype(BF16), (((1,), (1,)), ((), ())), preferred_element_type=F32)


def _dot_tn(a, b):
    return lax.dot_general(a.astype(BF16), b.astype(BF16), (((0,), (0,)), ((), ())), preferred_element_type=F32)


def _split2(x):
    hi = x.astype(BF16)
    lo = (x - hi.astype(F32)).astype(BF16)
    return hi, lo


def _split3(x):
    hi = x.astype(BF16)
    r1 = x - hi.astype(F32)
    mid = r1.astype(BF16)
    lo = (r1 - mid.astype(F32)).astype(BF16)
    return hi, mid, lo


def _dot_x_sel(x, sel_bf16, terms=3):
    d = lambda p: jnp.dot(p, sel_bf16, preferred_element_type=F32)
    return sum(d(p) for p in (_split3(x) if terms == 3 else _split2(x)))


def _norm_mod(x, g, sc, sh):
    y = x * lax.rsqrt(jnp.mean(x * x, axis=-1, keepdims=True) + NORM_EPS)
    return (y * g) * (1.0 + sc) + sh


def _mod_parts(mod_ref):
    m = mod_ref[0]
    d = m.shape[-1] // N_MOD
    return [m[:, i * d:(i + 1) * d] for i in range(N_MOD)]


def _mods_kernel(s_ref, w_ref, b_ref, o_ref):
    s = s_ref[...]
    s = s * jax.nn.sigmoid(s)
    o_ref[0] = _dotb(s, w_ref[0]) + b_ref[0]


def _mods(s_rows, ada_w, ada_b):
    depth, d, nd = ada_w.shape
    tn = nd // 4
    return pl.pallas_call(
        _mods_kernel,
        grid=(depth, nd // tn),
        in_specs=[pl.BlockSpec(s_rows.shape, lambda i, j: (0, 0)),
                  pl.BlockSpec((1, d, tn), lambda i, j: (i, 0, j)),
                  pl.BlockSpec((1, 1, tn), lambda i, j: (i, 0, j))],
        out_specs=pl.BlockSpec((1, s_rows.shape[0], tn), lambda i, j: (i, 0, j)),
        out_shape=jax.ShapeDtypeStruct((depth, s_rows.shape[0], nd), F32),
        compiler_params=pltpu.CompilerParams(dimension_semantics=("parallel", "parallel"),
                                             vmem_limit_bytes=VMEM_LIMIT),
        name="mods",
    )(s_rows, ada_w, ada_b.reshape(depth, 1, nd))


def _route(logits):
    lane = lax.broadcasted_iota(jnp.int32, logits.shape, 1)
    lane_f = lane.astype(F32)
    big = jnp.float32(1e9)
    is_g = lane < N_GROUPS
    gl = jnp.where(is_g, logits, NEG_BIG)
    gm = jnp.max(gl, axis=1, keepdims=True)
    grp = jnp.min(jnp.where(is_g & (gl == gm), lane_f, big), axis=1, keepdims=True)
    p_grp = 1.0 / jnp.sum(jnp.where(is_g, jnp.exp(gl - gm), 0.0), axis=1, keepdims=True)
    lo = N_GROUPS + grp * EXPERTS_PER_GROUP
    in_grp = (lane_f >= lo) & (lane_f < lo + EXPERTS_PER_GROUP)
    el = jnp.where(in_grp, logits, NEG_BIG)
    m1 = jnp.max(el, axis=1, keepdims=True)
    i1 = jnp.min(jnp.where(in_grp & (el == m1), lane_f, big), axis=1, keepdims=True)
    rest = in_grp & (lane_f != i1)
    el2 = jnp.where(rest, logits, NEG_BIG)
    m2 = jnp.max(el2, axis=1, keepdims=True)
    i2 = jnp.min(jnp.where(rest & (el2 == m2), lane_f, big), axis=1, keepdims=True)
    e21 = jnp.exp(m2 - m1)
    s0 = 1.0 / (1.0 + e21)
    w0 = p_grp * s0
    w1 = p_grp * (e21 * s0)
    out = jnp.where(lane == 0, i1 - N_GROUPS,
                    jnp.where(lane == 1, i2 - N_GROUPS,
                              jnp.where(lane == 2, w0, jnp.where(lane == 3, w1, 0.0))))
    return out


def _residual_and_route(x, y, mods, g2_ref, wrt_ref, brt_ref, xo_ref, h2_ref, route_ref, rows=slice(None)):
    _, _, gt1, sh2, sc2, _ = mods
    xn = x + gt1 * y
    xo_ref[rows, :] = xn
    h2 = _norm_mod(xn, g2_ref[...], sc2, sh2).astype(BF16)
    h2_ref[rows, :] = h2
    logits = jnp.dot(h2, wrt_ref[...], preferred_element_type=F32) + brt_ref[...]
    route_ref[rows, :] = _route(logits)


def _round_robin(gens):
    while gens:
        gens = [g for g in gens if next(g, StopIteration) is not StopIteration]


def _gmlp_rows(rows, x_ref, mod_ref, g1_ref, g2_ref, win_ref, bin_ref, lng_ref, lnb_ref, ws_ref, bs_ref,
               wout_ref, bout_ref, wrt_ref, brt_ref, xo_ref, h2_ref, route_ref, gated_ref):
    x = x_ref[rows, :]
    mods = _mod_parts(mod_ref)
    sh1, sc1 = mods[0], mods[1]
    h = _norm_mod(x, g1_ref[...], sc1, sh1).astype(BF16)
    yield
    width = win_ref.shape[1] // 2
    gelu = lambda z: 0.5 * z * (1.0 + lax.erf(z * SQRT_HALF))
    u = gelu(jnp.dot(h, win_ref[:, :width], preferred_element_type=F32) + bin_ref[:, :width])
    yield
    v = gelu(jnp.dot(h, win_ref[:, width:], preferred_element_type=F32) + bin_ref[:, width:])
    yield
    mu = jnp.mean(v, axis=-1, keepdims=True)
    vc = v - mu
    var = jnp.mean(vc * vc, axis=-1, keepdims=True)
    vn = ((vc * lax.rsqrt(var + LN_EPS)) * lng_ref[...] + lnb_ref[...]).astype(BF16)
    yield
    n_chunks = x.shape[0] // CHUNK
    for g in range(width // GMLP_GROUP_CH):
        cs = slice(g * GMLP_GROUP_CH, (g + 1) * GMLP_GROUP_CH)
        rhs = jnp.concatenate([vn[c * CHUNK:(c + 1) * CHUNK, cs] for c in range(n_chunks)], axis=1)
        s = jnp.dot(ws_ref[g], rhs, preferred_element_type=F32) + bs_ref[g]
        for c in range(n_chunks):
            rsl = slice(c * CHUNK, (c + 1) * CHUNK)
            dst = slice(rows.start + c * CHUNK, rows.start + (c + 1) * CHUNK)
            gated_ref[dst, cs] = (u[rsl, cs] * s[:, c * GMLP_GROUP_CH:(c + 1) * GMLP_GROUP_CH]).astype(BF16)
        if g % 4 == 3:
            yield
    y = jnp.dot(gated_ref[rows, :], wout_ref[...], preferred_element_type=F32) + bout_ref[...]
    yield
    _residual_and_route(x, y, mods, g2_ref, wrt_ref, brt_ref, xo_ref, h2_ref, route_ref, rows)
    yield


def _gmlp_kernel(*refs):
    group = refs[0].shape[0] // GMLP_SPLIT
    _round_robin([_gmlp_rows(slice(q * group, (q + 1) * group), *refs) for q in range(GMLP_SPLIT)])


def _full_spec(shape):
    nd = len(shape)
    return pl.BlockSpec(shape, lambda *_: (0,) * nd)


def _layer_spec(shape, j, single_buffer=False):
    nd = len(shape)
    kwargs = dict(pipeline_mode=pl.Buffered(1)) if single_buffer else {}
    return pl.BlockSpec((None,) + tuple(shape[1:]), lambda *_: (j,) + (0,) * (nd - 1), **kwargs)


def _mixer_out_shapes(t_tok, d):
    return (jax.ShapeDtypeStruct((t_tok, d), F32), jax.ShapeDtypeStruct((t_tok, d), BF16),
            jax.ShapeDtypeStruct((t_tok, LANES), F32))


def _mixer_out_specs(d, tm=TM):
    return (pl.BlockSpec((tm, d), lambda t: (t, 0)), pl.BlockSpec((tm, d), lambda t: (t, 0)),
            pl.BlockSpec((tm, LANES), lambda t: (t, 0)))


def _gmlp_layer(x, mods3, mod_spec, i, j, p):
    t_tok, d = x.shape
    width = p["ga_w_out"].shape[1]
    row = lambda a: a.reshape(a.shape[0], 1, a.shape[1])
    args = (x, mods3, row(p["norm1_g"]), row(p["norm2_g"]), p["ga_w_in"], row(p["ga_b_in"]), row(p["ga_ln_g"]),
            row(p["ga_ln_b"]), p["ga_w_s"], p["ga_b_s"], p["ga_w_out"], row(p["ga_b_out"]), p["wrt"], p["brt"])
    in_specs = [pl.BlockSpec((GMLP_TM, d), lambda t: (t, 0)), mod_spec,
                _layer_spec(args[2].shape, i), _layer_spec(args[3].shape, i),
                _layer_spec(args[4].shape, j, True), _layer_spec(args[5].shape, j), _layer_spec(args[6].shape, j),
                _layer_spec(args[7].shape, j), _layer_spec(args[8].shape, j, True), _layer_spec(args[9].shape, j),
                _layer_spec(args[10].shape, j, True), _layer_spec(args[11].shape, j),
                _layer_spec(args[12].shape, i), _layer_spec(args[13].shape, i)]
    return pl.pallas_call(
        _gmlp_kernel,
        grid=(t_tok // GMLP_TM,),
        in_specs=in_specs,
        out_specs=_mixer_out_specs(d, GMLP_TM),
        out_shape=_mixer_out_shapes(t_tok, d),
        scratch_shapes=[pltpu.VMEM((GMLP_TM, width), BF16)],
        compiler_params=pltpu.CompilerParams(dimension_semantics=("parallel",), vmem_limit_bytes=VMEM_LIMIT),
        name="gmlp_layer",
    )(*args)


def _seg_sum(q, e_ref):
    return _dot_x_sel(q, e_ref[...], terms=2)


def _seg_expand(s, et_ref):
    return _dot_x_sel(s, et_ref[...], terms=2)


def _shifted(h, h_above, h_below, is_ctx, first_row_tile, last_row_tile):
    tm, d = h.shape
    q = d // 4
    row = lax.broadcasted_iota(jnp.int32, (tm, 1), 0)
    ctx_i = is_ctx.astype(jnp.int32)
    col = row & (jnp.where(is_ctx, tm, GRID_W) - 1)
    last_col = jnp.where(is_ctx, tm, GRID_W) - 1
    prev1 = jnp.where(col == 0, 0.0, pltpu.roll(h, 1, axis=0))
    next1 = jnp.where(col == last_col, 0.0, pltpu.roll(h, tm - 1, axis=0))
    up = jnp.concatenate([h_above, h[:tm - GRID_W]], axis=0)
    up = jnp.where(row < jnp.where(first_row_tile, GRID_W, 0), 0.0, up)
    down = jnp.concatenate([h[GRID_W:], h_below], axis=0)
    down = jnp.where(row >= jnp.where(last_row_tile, tm - GRID_W, tm), 0.0, down)
    lane_q = lax.broadcasted_iota(jnp.int32, (1, d), 1) >> int(math.log2(q))
    src = lane_q * (1 - ctx_i) + (lane_q >> 1) * ctx_i
    return jnp.where(src == 0, prev1, jnp.where(src == 1, next1, jnp.where(src == 2, up, down)))


def _rwkv_prep_kernel(x_ref, xa_ref, xb_ref, mod_ref, n1_ref, mu_ref, wr_ref, wk_ref, wv_ref, w1_ref, w2_ref, w0_ref,
                      a1_ref, a2_ref, a0_ref, g1_ref, g2_ref, kk_ref, ka_ref, e_ref, et_ref,
                      r_o, v_o, kkn_o, gate_o, lw_o, km_o, bb_o, *, lat_tiles, tiles_per_batch):
    t = pl.program_id(0)
    mods = _mod_parts(mod_ref)
    norm = lambda x: _norm_mod(x, n1_ref[...], mods[1], mods[0])
    h = norm(x_ref[...])
    tb = t % tiles_per_batch
    hs = _shifted(h, norm(xa_ref[...]), norm(xb_ref[...]), t >= lat_tiles, tb == 0, tb == tiles_per_batch - 1)
    xx = hs - h
    mu = mu_ref[...]
    xr, xw, xk, xv, xa, xg = [h + xx * mu[q:q + 1] for q in range(6)]
    r = _dotb(xr, wr_ref[...])
    k = _dotb(xk, wk_ref[...])
    v = _dotb(xv, wv_ref[...])
    r_o[...] = r
    v_o[...] = v
    kkr = k * kk_ref[...]
    rs = lax.rsqrt(_seg_sum(kkr * kkr, e_ref) + 1e-12)
    kkn = kkr * _seg_expand(rs, et_ref)
    kkn_o[...] = kkn
    gate_o[...] = _dotb(jax.nn.sigmoid(_dotb(xg, g1_ref[...])), g2_ref[...])
    tw = jnp.tanh(_dotb(xw, w1_ref[...]))
    ta = _dotb(xa, a1_ref[...])
    lane = lax.broadcasted_iota(jnp.int32, tw.shape, 1)
    half = tw.shape[1] // 2
    w0 = w0_ref[...]
    a0 = a0_ref[...]
    ka = ka_ref[...]
    for z in range(2):
        zm = (lane >= z * half) & (lane < (z + 1) * half)
        lwz = _dotb(jnp.where(zm, tw, 0.0), w2_ref[...])
        lw_o[z] = -EXP_MINUS_HALF * jax.nn.sigmoid(w0[z:z + 1] + lwz)
        la = _dotb(jnp.where(zm, ta, 0.0), a2_ref[...])
        a = jax.nn.sigmoid(a0[z:z + 1] + la)
        km_o[z] = k * (1.0 + (a - 1.0) * ka)
        bb_o[z] = kkn * a


def _rwkv_prep(x, mods3, mod_spec, i, j, p, lat_tiles, tiles_per_batch):
    t_tok, d = x.shape
    row = lambda a: a.reshape(a.shape[0], 1, a.shape[1])
    args = (x, x, x, mods3, row(p["norm1_g"]), p["rw_mu8"], p["rw_wr"], p["rw_wk"], p["rw_wv"], p["rw_w1c"],
            p["rw_w2c"], p["rw_w0"], p["rw_a1c"], p["rw_a2c"], p["rw_a0"], p["rw_g1p"], p["rw_g2p"],
            row(p["rw_k_k"]), row(p["rw_k_a"]), p["head_sel"], p["head_sel_t"])
    tok = pl.BlockSpec((TM, d), lambda t: (t, 0))
    rows_per_tile = TM // GRID_W
    last_row = t_tok // GRID_W - 1
    above = pl.BlockSpec((GRID_W, d), lambda t: (jnp.maximum(t * rows_per_tile - 1, 0), 0))
    below = pl.BlockSpec((GRID_W, d), lambda t: (jnp.minimum((t + 1) * rows_per_tile, last_row), 0))
    in_specs = ([tok, above, below, mod_spec, _layer_spec(args[4].shape, i)]
                + [_layer_spec(a.shape, j) for a in args[5:19]]
                + [_full_spec(args[19].shape), _full_spec(args[20].shape)])
    dir_spec = pl.BlockSpec((2, TM, d), lambda t: (0, t, 0))
    tok_shape = jax.ShapeDtypeStruct((t_tok, d), F32)
    dir_shape = jax.ShapeDtypeStruct((2, t_tok, d), F32)
    return pl.pallas_call(
        functools.partial(_rwkv_prep_kernel, lat_tiles=lat_tiles, tiles_per_batch=tiles_per_batch),
        grid=(t_tok // TM,),
        in_specs=in_specs,
        out_specs=(tok, tok, tok, tok, dir_spec, dir_spec, dir_spec),
        out_shape=(tok_shape, tok_shape, tok_shape, tok_shape, dir_shape, dir_shape, dir_shape),
        compiler_params=pltpu.CompilerParams(dimension_semantics=("parallel",), vmem_limit_bytes=VMEM_LIMIT),
        name="rwkv_prep",
    )(*args)


def _scan_prepare(r, v, kk, lw, km, bb, rev, out):
    c = SCAN_CHUNK
    nb = SCAN_BLOCK
    a = -kk
    row = lax.broadcasted_iota(jnp.int32, (nb, nb), 0)
    col = lax.broadcasted_iota(jnp.int32, (nb, nb), 1)
    shift_c = int(math.log2(c))
    same = (row >> shift_c) == (col >> shift_c)
    if rev:
        incl = same & (col >= row)
        strict = same & (col > row)
    else:
        incl = same & (col <= row)
        strict = same & (col < row)
    incl_b = jnp.where(incl, 1.0, 0.0).astype(BF16)
    eye = jnp.where(row == col, 1.0, 0.0)
    head_of_lane = col >> int(math.log2(RWKV_HEAD))

    lw_hi, lw_lo = _split2(lw)
    cum = (jnp.dot(incl_b, lw_hi, preferred_element_type=F32)
           + jnp.dot(incl_b, lw_lo, preferred_element_type=F32))
    yield
    last = [ci * c if rev else ci * c + c - 1 for ci in range(nb // c)]
    tot_rows = [cum[t:t + 1] for t in last]
    tot = jnp.concatenate([jnp.broadcast_to(t, (c, t.shape[1])) for t in tot_rows], axis=0)
    yield
    at = a * jnp.exp(cum - lw)
    rt = r * jnp.exp(cum)
    e_inv = jnp.exp(-cum)
    bt = bb * e_inv
    kt = km * e_inv
    e_rem = jnp.exp(tot - cum)
    bh = (bb * e_rem).astype(BF16)
    kh = (km * e_rem).astype(BF16)
    et = jnp.exp(jnp.concatenate(tot_rows, axis=0))
    bk = jnp.concatenate([bt, kt], axis=0).astype(BF16)
    yield
    n_heads = SCAN_LANES // RWKV_HEAD
    heads = range(n_heads)
    hms = [head_of_lane == hh for hh in heads]
    at_h = [jnp.where(hm, at, 0.0) for hm in hms]
    rt_h = [jnp.where(hm, rt, 0.0) for hm in hms]
    v_h = [jnp.where(hm, v, 0.0).astype(BF16) for hm in hms]
    v_roll = pltpu.roll(v, RWKV_HEAD, axis=1)
    v_s = [jnp.where(hms[(hh + 1) % n_heads], v_roll, 0.0).astype(BF16) for hh in heads]
    a_ab, a_ak, m_rb, m_rk = [], [], [], []
    for hh in heads:
        s = _dot_nt(jnp.concatenate([at_h[hh], rt_h[hh]], axis=0), bk)
        a_ab.append(jnp.where(strict, s[:nb, :nb], 0.0))
        a_ak.append(jnp.where(strict, s[:nb, nb:], 0.0).astype(BF16))
        m_rb.append(jnp.where(incl, s[nb:, :nb], 0.0).astype(BF16))
        m_rk.append(jnp.where(incl, s[nb:, nb:], 0.0).astype(BF16))
        yield
    pw = list(a_ab)
    tinv = [eye + m for m in a_ab]
    for _ in range(shift_c - 1):
        for hh in heads:
            pw[hh] = _dotb(pw[hh], pw[hh])
        yield
        for hh in heads:
            tinv[hh] = tinv[hh] + _dotb(tinv[hh], pw[hh])
        yield
    x1 = [jnp.dot(a_ak[hh], v_s[hh], preferred_element_type=F32) for hh in heads]
    yield
    tc = [_dotb(tinv[hh], at_h[hh] + x1[hh]) for hh in heads]
    yield
    mc = [_dotb(m_rb[hh], tc[hh]) for hh in heads]
    yield

    def pick(parts, shift):
        res = parts[(n_heads - 1 - shift) % n_heads]
        for lb in range(n_heads - 2, -1, -1):
            res = jnp.where(hms[lb], parts[(lb - shift) % n_heads], res)
        return res

    back = SCAN_LANES - RWKV_HEAD
    ah = pick(tc, 0)
    vh = pltpu.roll(pick(tc, 1), back, axis=1)
    rh = rt + pick(mc, 0)
    yh = (jnp.dot(jnp.concatenate(m_rk, axis=1), jnp.concatenate(v_h, axis=0), preferred_element_type=F32)
          + pltpu.roll(pick(mc, 1), back, axis=1))
    yield
    out.extend([ah.astype(BF16), rh.astype(BF16), vh, yh, bh, kh, v.astype(BF16), et])


def _scan_prepare_stages():
    return 7 + (SCAN_LANES // RWKV_HEAD) + 2 * (int(math.log2(SCAN_CHUNK)) - 1)


def _scan_chain(g, ops, y_ref, lanes, rev, out):
    c = SCAN_CHUNK
    ah, rh, vh, yh, bh, kh, vb, et = ops
    n = g.shape[0]
    shift_h = int(math.log2(RWKV_HEAD))
    blockdiag = ((lax.broadcasted_iota(jnp.int32, (n, n), 0) >> shift_h)
                 == (lax.broadcasted_iota(jnp.int32, (n, n), 1) >> shift_h))
    n_chunks = SCAN_BLOCK // c
    order = range(n_chunks - 1, -1, -1) if rev else range(n_chunks)
    for ci in order:
        rs = slice(ci * c, (ci + 1) * c)
        p = _dot_nt(jnp.concatenate([ah[rs], rh[rs]], axis=0), g)
        u = p[:c] + vh[rs]
        y_ref[rs, lanes] = p[c:] + yh[rs]
        yield
        dg = _dot_tn(jnp.concatenate([u.astype(BF16), vb[rs]], axis=0), jnp.concatenate([bh[rs], kh[rs]], axis=0))
        g = jnp.where(blockdiag, g * et[ci:ci + 1, :] + dg, 0.0)
        yield
    out.append(g)


def _scan_chain_stages():
    return 2 * (SCAN_BLOCK // SCAN_CHUNK)


def _interleave(gen_a, n_a, gen_b, n_b):
    done_b = 0
    for i in range(n_a):
        next(gen_a, None)
        want_b = ((i + 1) * n_b) // n_a
        while done_b < want_b:
            next(gen_b, None)
            done_b += 1
        yield
    for g in (gen_a, gen_b):
        for _ in g:
            pass


def _scan_kernel(r_ref, v_ref, kk_ref, lw_ref, km_ref, bb_ref, y_ref, g_scr, *ops_scr, rev, groups):
    @pl.when(pl.program_id(2) == 0)
    def _():
        g_scr[...] = jnp.zeros_like(g_scr)
        for ref in ops_scr:
            ref[...] = jnp.zeros_like(ref)

    slot = pl.program_id(2) % 2
    work, results = [], []
    for q in range(groups):
        lanes = slice(q * SCAN_LANES, (q + 1) * SCAN_LANES)
        prev = [ref.at[1 - slot, :, lanes] for ref in ops_scr]
        new, g_end = [], []
        prep = _scan_prepare(r_ref[:, lanes], v_ref[:, lanes], kk_ref[:, lanes], lw_ref[:, lanes], km_ref[:, lanes],
                             bb_ref[:, lanes], rev, new)
        chain = _scan_chain(g_scr[q], prev, y_ref, lanes, rev, g_end)
        work.append(_interleave(prep, _scan_prepare_stages(), chain, _scan_chain_stages()))
        results.append((lanes, new, g_end))
    while work:
        work = [w for w in work if next(w, StopIteration) is not StopIteration]
    for q, (lanes, new, g_end) in enumerate(results):
        g_scr[q] = g_end[0]
        for ref, val in zip(ops_scr, new):
            ref[slot, :, lanes] = val


def _wkv_scan(r, v, kkn, lw, km, bb, z, bsz, n_lat, n_ctx):
    t_tok, d = r.shape
    lat_blocks = n_lat // SCAN_BLOCK
    ctx_blocks = n_ctx // SCAN_BLOCK
    steps = ctx_blocks + lat_blocks
    ctx_base = bsz * lat_blocks
    rev = z == 1
    width = SCAN_LANES * SCAN_GROUPS

    def blk(b, s):
        if rev:
            cb = ctx_base + b * ctx_blocks + (ctx_blocks - 1 - s)
            lb = b * lat_blocks + (lat_blocks - 1 - (s - ctx_blocks))
        else:
            cb = ctx_base + b * ctx_blocks + s
            lb = b * lat_blocks + (s - ctx_blocks)
        return jnp.where(s < ctx_blocks, cb, lb)

    in_blk = lambda b, s: blk(b, jnp.minimum(s, steps - 1))
    out_blk = lambda b, s: blk(b, jnp.maximum(s - 1, 0))
    tok = pl.BlockSpec((SCAN_BLOCK, width), lambda b, g, s: (in_blk(b, s), g))
    dirs = pl.BlockSpec((None, SCAN_BLOCK, width), lambda b, g, s: (z, in_blk(b, s), g))
    n_chunks = SCAN_BLOCK // SCAN_CHUNK
    ops_scratch = [pltpu.VMEM((2, SCAN_BLOCK, width), BF16), pltpu.VMEM((2, SCAN_BLOCK, width), BF16),
                   pltpu.VMEM((2, SCAN_BLOCK, width), F32), pltpu.VMEM((2, SCAN_BLOCK, width), F32),
                   pltpu.VMEM((2, SCAN_BLOCK, width), BF16), pltpu.VMEM((2, SCAN_BLOCK, width), BF16),
                   pltpu.VMEM((2, SCAN_BLOCK, width), BF16), pltpu.VMEM((2, n_chunks, width), F32)]
    return pl.pallas_call(
        functools.partial(_scan_kernel, rev=rev, groups=SCAN_GROUPS),
        grid=(bsz, d // width, steps + 1),
        in_specs=[tok, tok, tok, dirs, dirs, dirs],
        out_specs=pl.BlockSpec((SCAN_BLOCK, width), lambda b, g, s: (out_blk(b, s), g)),
        out_shape=jax.ShapeDtypeStruct((t_tok, d), F32),
        scratch_shapes=[pltpu.VMEM((SCAN_GROUPS, SCAN_LANES, SCAN_LANES), F32)] + ops_scratch,
        compiler_params=pltpu.CompilerParams(dimension_semantics=("parallel", "parallel", "arbitrary"),
                                             vmem_limit_bytes=VMEM_LIMIT),
        name="wkv_scan_rev" if rev else "wkv_scan_fwd",
    )(r, v, kkn, lw, km, bb)


def _rwkv_out_kernel(x_ref, mod_ref, y0_ref, y1_ref, r_ref, v_ref, km_ref, gate_ref, lng_ref, lnb_ref, rk_ref,
                     wo_ref, e_ref, et_ref, g2_ref, wrt_ref, brt_ref, xo_ref, h2_ref, route_ref):
    x = x_ref[...]
    mods = _mod_parts(mod_ref)
    y = y0_ref[...] + y1_ref[...]
    inv_n = 1.0 / RWKV_HEAD
    mu = _seg_expand(_seg_sum(y, e_ref) * inv_n, et_ref)
    yc = y - mu
    var = _seg_sum(yc * yc, e_ref) * inv_n
    yn = (yc * _seg_expand(lax.rsqrt(var + GN_EPS), et_ref)) * lng_ref[...] + lnb_ref[...]
    v = v_ref[...]
    bonus = _seg_sum((r_ref[...] * rk_ref[...]) * (km_ref[0] + km_ref[1]), e_ref)
    yn = yn + _seg_expand(bonus, et_ref) * v
    out = _dotb(yn * gate_ref[...], wo_ref[...])
    _residual_and_route(x, out, mods, g2_ref, wrt_ref, brt_ref, xo_ref, h2_ref, route_ref)


def _rwkv_out(x, mods3, mod_spec, y0, y1, r, v, km, gate, i, j, p):
    t_tok, d = x.shape
    row = lambda a: a.reshape(a.shape[0], 1, a.shape[1])
    tok = pl.BlockSpec((TM, d), lambda t: (t, 0))
    dir_spec = pl.BlockSpec((2, TM, d), lambda t: (0, t, 0))
    args = (x, mods3, y0, y1, r, v, km, gate, row(p["rw_ln_g"]), row(p["rw_ln_b"]), row(p["rw_r_k"].reshape(-1, d)),
            p["rw_wo"], p["head_sel"], p["head_sel_t"], row(p["norm2_g"]), p["wrt"], p["brt"])
    in_specs = [tok, mod_spec, tok, tok, tok, tok, dir_spec, tok,
                _layer_spec(args[8].shape, j), _layer_spec(args[9].shape, j), _layer_spec(args[10].shape, j),
                _layer_spec(args[11].shape, j), _full_spec(args[12].shape), _full_spec(args[13].shape),
                _layer_spec(args[14].shape, i), _layer_spec(args[15].shape, i), _layer_spec(args[16].shape, i)]
    return pl.pallas_call(
        _rwkv_out_kernel,
        grid=(t_tok // TM,),
        in_specs=in_specs,
        out_specs=_mixer_out_specs(d),
        out_shape=_mixer_out_shapes(t_tok, d),
        compiler_params=pltpu.CompilerParams(dimension_semantics=("parallel",), vmem_limit_bytes=VMEM_LIMIT),
        name="rwkv_out",
    )(*args)


def _expert_kernel(be_ref, nu_ref, x_ref, wg_ref, wu_ref, wd_ref, o_ref, wg_s, wu_s, wd_s):
    b = pl.program_id(0)
    e = be_ref[b]
    prev = be_ref[jnp.maximum(b - 1, 0)]
    used = b < nu_ref[0]

    @pl.when(used & ((b == 0) | (e != prev)))
    def _():
        wg_s[...] = wg_ref[...].astype(BF16)
        wu_s[...] = wu_ref[...].astype(BF16)
        wd_s[...] = wd_ref[...].astype(BF16)

    @pl.when(used)
    def _():
        xb = x_ref[...]
        g = jnp.dot(xb, wg_s[...], preferred_element_type=F32)
        u = jnp.dot(xb, wu_s[...], preferred_element_type=F32)
        hmid = (g * jax.nn.sigmoid(g)) * u
        o_ref[...] = jnp.dot(hmid.astype(BF16), wd_s[...], preferred_element_type=F32)

    @pl.when(jnp.logical_not(used))
    def _():
        o_ref[...] = jnp.zeros_like(o_ref)


def _experts(buf, block_expert, n_used, w_gate, w_up, w_down, i):
    cap, d = buf.shape
    f = w_gate.shape[-1]
    n_blocks = cap // MOE_ROWS
    grid_spec = pltpu.PrefetchScalarGridSpec(
        num_scalar_prefetch=2,
        grid=(n_blocks,),
        in_specs=[pl.BlockSpec((MOE_ROWS, d), lambda b, be, nu: (b, 0)),
                  pl.BlockSpec((None, None, d, f), lambda b, be, nu: (i, be[b], 0, 0)),
                  pl.BlockSpec((None, None, d, f), lambda b, be, nu: (i, be[b], 0, 0)),
                  pl.BlockSpec((None, None, f, d), lambda b, be, nu: (i, be[b], 0, 0))],
        out_specs=pl.BlockSpec((MOE_ROWS, d), lambda b, be, nu: (b, 0)),
        scratch_shapes=[pltpu.VMEM((d, f), BF16), pltpu.VMEM((d, f), BF16), pltpu.VMEM((f, d), BF16)],
    )
    return pl.pallas_call(
        _expert_kernel,
        grid_spec=grid_spec,
        out_shape=jax.ShapeDtypeStruct((cap, d), F32),
        compiler_params=pltpu.CompilerParams(dimension_semantics=("arbitrary",), vmem_limit_bytes=VMEM_LIMIT),
        name="experts",
    )(block_expert, n_used, buf, w_gate, w_up, w_down)


def _combine_kernel(x_ref, mod_ref, route_ref, y0_ref, y1_ref, fg_ref, xo_ref, *, final):
    gt2 = _mod_parts(mod_ref)[5]
    route = route_ref[...]
    xn = x_ref[...] + gt2 * (route[:, 2:3] * y0_ref[...] + route[:, 3:4] * y1_ref[...])
    if final:
        xn = (xn * lax.rsqrt(jnp.mean(xn * xn, axis=-1, keepdims=True) + NORM_EPS)) * fg_ref[...]
    xo_ref[...] = xn


def _combine(x, mods3, mod_spec, route, y0, y1, final_g, final, t_out):
    d = x.shape[1]
    tok = pl.BlockSpec((TM, d), lambda t: (t, 0))
    return pl.pallas_call(
        functools.partial(_combine_kernel, final=final),
        grid=(t_out // TM,),
        in_specs=[tok, mod_spec, pl.BlockSpec((TM, LANES), lambda t: (t, 0)), tok, tok, _full_spec((1, d))],
        out_specs=tok,
        out_shape=jax.ShapeDtypeStruct((t_out, d), F32),
        compiler_params=pltpu.CompilerParams(dimension_semantics=("parallel",), vmem_limit_bytes=VMEM_LIMIT),
        name="moe_combine",
    )(x, mods3, route, y0, y1, final_g.reshape(1, d))


def _dest_kernel(route_ref, d_ref, counts_ref, base_scr, start_scr):
    phase = pl.program_id(0)
    t = pl.program_id(1)
    route = route_ref[...]
    tm = route.shape[0]
    lane = lax.broadcasted_iota(jnp.int32, route.shape, 1)
    lane_f = lane.astype(F32)
    onehots = [jnp.where(lane_f == route[:, k:k + 1], 1.0, 0.0) for k in range(TOP_K)]

    @pl.when((phase == 0) & (t == 0))
    def _():
        base_scr[...] = jnp.zeros_like(base_scr)

    @pl.when((phase == 1) & (t == 0))
    def _():
        counts = base_scr[...]
        counts_ref[...] = counts
        padded = jnp.floor((counts + (MOE_ROWS - 1)) * (1.0 / MOE_ROWS)) * MOE_ROWS
        before = (lax.broadcasted_iota(jnp.int32, (LANES, LANES), 0)
                  < lax.broadcasted_iota(jnp.int32, (LANES, LANES), 1))
        start_scr[...] = _dot_x_sel(padded, jnp.where(before, 1.0, 0.0).astype(BF16))
        base_scr[...] = jnp.zeros_like(base_scr)

    @pl.when(phase == 0)
    def _():
        base_scr[...] = base_scr[...] + sum(jnp.sum(oh, axis=0, keepdims=True) for oh in onehots)

    @pl.when(phase == 1)
    def _():
        earlier = (lax.broadcasted_iota(jnp.int32, (tm, tm), 1)
                   < lax.broadcasted_iota(jnp.int32, (tm, tm), 0))
        earlier_b = jnp.where(earlier, 1.0, 0.0).astype(BF16)
        diag = (lax.broadcasted_iota(jnp.int32, (LANES, LANES), 0)
                == lax.broadcasted_iota(jnp.int32, (LANES, LANES), 1))
        base = base_scr[...] + start_scr[...]
        for k, oh in enumerate(onehots):
            before = jnp.dot(earlier_b, oh.astype(BF16), preferred_element_type=F32)
            dest = jnp.sum(oh * (base + before), axis=1, keepdims=True)
            rows = [jnp.sum(jnp.where(diag, jnp.broadcast_to(dest[i * LANES:(i + 1) * LANES], (LANES, LANES)), 0.0),
                            axis=0, keepdims=True) for i in range(tm // LANES)]
            d_ref[k, 0] = jnp.concatenate(rows, axis=0).astype(jnp.int32)
            base = base + jnp.sum(oh, axis=0, keepdims=True)
        base_scr[...] = base - start_scr[...]


def _dests(route):
    t_tok = route.shape[0]
    assert t_tok % DEST_TM == 0
    tiles = t_tok // DEST_TM
    return pl.pallas_call(
        _dest_kernel,
        grid=(2, tiles),
        in_specs=[pl.BlockSpec((DEST_TM, LANES), lambda ph, t: (t, 0))],
        out_specs=(pl.BlockSpec((TOP_K, 1, DEST_TM // LANES, LANES), lambda ph, t: (0, t * ph, 0, 0)),
                   pl.BlockSpec((1, LANES), lambda ph, t: (0, 0))),
        out_shape=(jax.ShapeDtypeStruct((TOP_K, tiles, DEST_TM // LANES, LANES), jnp.int32),
                   jax.ShapeDtypeStruct((1, LANES), F32)),
        scratch_shapes=[pltpu.VMEM((1, LANES), F32), pltpu.VMEM((1, LANES), F32)],
        compiler_params=pltpu.CompilerParams(dimension_semantics=("arbitrary", "arbitrary"),
                                             vmem_limit_bytes=VMEM_LIMIT),
        name="moe_dests",
    )(route)


def _moe(x, h2, route, mods3, mod_spec, i, p, final, t_out):
    t_tok, d = x.shape
    n_assign = t_tok * TOP_K
    dest4, counts_f = _dests(route)
    dest = dest4.reshape(TOP_K, t_tok)
    counts = counts_f[0, :N_EXPERTS].astype(jnp.int32)
    pad_end = jnp.cumsum(((counts + MOE_ROWS - 1) // MOE_ROWS) * MOE_ROWS)
    n_blocks = (n_assign + MOE_ROWS - 1) // MOE_ROWS + N_EXPERTS
    token = jnp.arange(t_tok, dtype=jnp.int32)
    slot_token = (jnp.arange(n_blocks * MOE_ROWS, dtype=jnp.int32) % t_tok).at[dest.reshape(-1)].set(
        jnp.tile(token, TOP_K), mode="promise_in_bounds", unique_indices=True)
    block_start = jnp.arange(n_blocks, dtype=jnp.int32) * MOE_ROWS
    block_expert = jnp.minimum(jnp.sum((pad_end[None, :] <= block_start[:, None]).astype(jnp.int32), axis=1),
                               N_EXPERTS - 1)
    n_used = (pad_end[-1:] // MOE_ROWS).astype(jnp.int32)
    buf = h2.at[slot_token].get(mode="promise_in_bounds")
    out = _experts(buf, block_expert, n_used, p["moe_w_gate"], p["moe_w_up"], p["moe_w_down"], i)
    y0 = out.at[dest[0]].get(mode="promise_in_bounds", unique_indices=True)
    y1 = out.at[dest[1]].get(mode="promise_in_bounds", unique_indices=True)
    return _combine(x, mods3, mod_spec, route, y0, y1, p["final_g"], final, t_out)


def kernel(x, c, ctx, c_ctx, ada_w, ada_b, norm1_g, norm2_g, final_g, ga_w_in, ga_b_in, ga_ln_g, ga_ln_b, ga_w_s,
           ga_b_s, ga_w_out, ga_b_out, rw_mu, rw_wr, rw_wk, rw_wv, rw_wo, rw_w0, rw_w1, rw_w2, rw_a0, rw_a1, rw_a2,
           rw_g1, rw_g2, rw_k_k, rw_k_a, rw_r_k, rw_ln_g, rw_ln_b, moe_w_grp, moe_b_grp, moe_w_exp, moe_b_exp,
           moe_w_gate, moe_w_up, moe_w_down):
    bsz, n_lat, d = x.shape
    n_ctx = ctx.shape[1]
    depth = ada_w.shape[0]
    n_mixers = 2
    assert n_lat % TM == 0 and n_ctx == TM and TM % GRID_W == 0
    assert n_lat % GMLP_TM == 0 and (bsz * n_ctx) % GMLP_TM == 0 and GMLP_TM % (GMLP_SPLIT * CHUNK) == 0
    assert n_lat % SCAN_BLOCK == 0 and n_ctx % SCAN_BLOCK == 0
    assert bsz + 1 <= 8 and d % (SCAN_LANES * SCAN_GROUPS) == 0
    assert MOE_ROWS & (MOE_ROWS - 1) == 0
    t_lat = bsz * n_lat
    lat_tiles = t_lat // TM
    tiles_per_batch = n_lat // TM
    heads = d // RWKV_HEAD

    s_rows = jnp.concatenate([c, c_ctx[None, :], jnp.zeros((8 - bsz - 1, d), F32)], axis=0)
    mods = _mods(s_rows, ada_w, ada_b)
    mods3 = mods.reshape(depth * 8, 1, N_MOD * d)

    def mod_spec_for(i, tm=TM):
        def imap(t):
            return (i * 8 + jnp.where(t < t_lat // tm, t // (n_lat // tm), bsz), 0, 0)
        return pl.BlockSpec((1, 1, N_MOD * d), imap)

    head_sel = (jnp.arange(d)[:, None] // RWKV_HEAD == jnp.arange(LANES)[None, :]).astype(BF16)
    zpad = lambda a, axis, n: jnp.pad(a, [(0, n - a.shape[k]) if k == axis else (0, 0) for k in range(a.ndim)])
    glora = ((rw_g1.shape[-1] + LANES - 1) // LANES) * LANES
    wrt = jnp.concatenate([moe_w_grp, moe_w_exp], axis=-1)
    brt = jnp.concatenate([moe_b_grp, moe_b_exp], axis=-1)
    p = dict(
        norm1_g=norm1_g, norm2_g=norm2_g, final_g=final_g,
        ga_w_in=ga_w_in.astype(BF16), ga_b_in=ga_b_in, ga_ln_g=ga_ln_g, ga_ln_b=ga_ln_b,
        ga_w_s=ga_w_s.astype(BF16), ga_b_s=ga_b_s[..., None], ga_w_out=ga_w_out.astype(BF16), ga_b_out=ga_b_out,
        rw_mu8=zpad(rw_mu, 1, 8),
        rw_wr=rw_wr.astype(BF16), rw_wk=rw_wk.astype(BF16), rw_wv=rw_wv.astype(BF16), rw_wo=rw_wo.astype(BF16),
        rw_w1c=jnp.concatenate([rw_w1[:, 0], rw_w1[:, 1]], axis=-1).astype(BF16),
        rw_w2c=jnp.concatenate([rw_w2[:, 0], rw_w2[:, 1]], axis=-2).astype(BF16),
        rw_a1c=jnp.concatenate([rw_a1[:, 0], rw_a1[:, 1]], axis=-1).astype(BF16),
        rw_a2c=jnp.concatenate([rw_a2[:, 0], rw_a2[:, 1]], axis=-2).astype(BF16),
        rw_w0=rw_w0, rw_a0=rw_a0,
        rw_g1p=zpad(rw_g1, 2, glora).astype(BF16), rw_g2p=zpad(rw_g2, 1, glora).astype(BF16),
        rw_k_k=rw_k_k, rw_k_a=rw_k_a, rw_r_k=rw_r_k, rw_ln_g=rw_ln_g, rw_ln_b=rw_ln_b,
        head_sel=head_sel, head_sel_t=head_sel.T,
        wrt=zpad(wrt, 2, LANES).astype(BF16), brt=zpad(brt, 1, LANES).reshape(depth, 1, LANES),
        moe_w_gate=moe_w_gate, moe_w_up=moe_w_up, moe_w_down=moe_w_down,
    )
    assert heads <= LANES and 2 * rw_w1.shape[-1] == LANES and 2 * rw_a1.shape[-1] == LANES

    xs = jnp.concatenate([x.reshape(t_lat, d), ctx.reshape(bsz * n_ctx, d)], axis=0)
    for i in range(depth):
        j = i // n_mixers
        mod_spec = mod_spec_for(i)
        if i % n_mixers == 0:
            xs, h2, route = _gmlp_layer(xs, mods3, mod_spec_for(i, GMLP_TM), i, j, p)
        else:
            r, v, kkn, gate, lw, km, bb = _rwkv_prep(xs, mods3, mod_spec, i, j, p, lat_tiles, tiles_per_batch)
            y0 = _wkv_scan(r, v, kkn, lw, km, bb, 0, bsz, n_lat, n_ctx)
            y1 = _wkv_scan(r, v, kkn, lw, km, bb, 1, bsz, n_lat, n_ctx)
            xs, h2, route = _rwkv_out(xs, mods3, mod_spec, y0, y1, r, v, km, gate, i, j, p)
        last = i == depth - 1
        xs = _moe(xs, h2, route, mods3, mod_spec, i, p, final=last, t_out=t_lat if last else xs.shape[0])
    return xs.reshape(bsz, n_lat, d)
```

```python
import functools
import math

import jax
import jax.numpy as jnp
from jax import lax
from jax.experimental import pallas as pl
from jax.experimental.pallas import tpu as pltpu

F32 = jnp.float32
BF16 = jnp.bfloat16

N_MOD = 6
NORM_EPS = 1e-6
GRID_W = 64
CHUNK = 128
GMLP_GROUP_CH = 128
LN_EPS = 1e-5
RWKV_HEAD = 64
GN_EPS = 64e-5
N_GROUPS = 4
EXPERTS_PER_GROUP = 8
N_EXPERTS = N_GROUPS * EXPERTS_PER_GROUP
TOP_K = 2

LANES = 128
TM = 256
GMLP_TM = 512
GMLP_SPLIT = 2
MOE_ROWS = 512
DEST_TM = 512
SCAN_BLOCK = 256
SCAN_CHUNK = 32
SCAN_LANES = 256
SCAN_GROUPS = 2
VMEM_LIMIT = 56 * 1024 * 1024
NEG_BIG = -3.0e38
SQRT_HALF = 0.7071067811865476
EXP_MINUS_HALF = 0.6065306597126334


def _dotb(a, b):
    return jnp.dot(a.astype(BF16), b.astype(BF16), preferred_element_type=F32)


def _dot_nt(a, b):
    return lax.dot_general(a.astype(BF16), b.astype(BF16), (((1,), (1,)), ((), ())), preferred_element_type=F32)


def _dot_tn(a, b):
    return lax.dot_general(a.astype(BF16), b.astype(BF16), (((0,), (0,)), ((), ())), preferred_element_type=F32)


def _split2(x):
    hi = x.astype(BF16)
    lo = (x - hi.astype(F32)).astype(BF16)
    return hi, lo


def _split3(x):
    hi = x.astype(BF16)
    r1 = x - hi.astype(F32)
    mid = r1.astype(BF16)
    lo = (r1 - mid.astype(F32)).astype(BF16)
    return hi, mid, lo


def _dot_x_sel(x, sel_bf16, terms=3):
    d = lambda p: jnp.dot(p, sel_bf16, preferred_element_type=F32)
    return sum(d(p) for p in (_split3(x) if terms == 3 else _split2(x)))


def _norm_mod(x, g, sc, sh):
    y = x * lax.rsqrt(jnp.mean(x * x, axis=-1, keepdims=True) + NORM_EPS)
    return (y * g) * (1.0 + sc) + sh


def _mod_parts(mod_ref):
    m = mod_ref[0]
    d = m.shape[-1] // N_MOD
    return [m[:, i * d:(i + 1) * d] for i in range(N_MOD)]


def _mods_kernel(s_ref, w_ref, b_ref, o_ref):
    s = s_ref[...]
    s = s * jax.nn.sigmoid(s)
    o_ref[0] = _dotb(s, w_ref[0]) + b_ref[0]


def _mods(s_rows, ada_w, ada_b):
    depth, d, nd = ada_w.shape
    tn = nd // 4
    return pl.pallas_call(
        _mods_kernel,
        grid=(depth, nd // tn),
        in_specs=[pl.BlockSpec(s_rows.shape, lambda i, j: (0, 0)),
                  pl.BlockSpec((1, d, tn), lambda i, j: (i, 0, j)),
                  pl.BlockSpec((1, 1, tn), lambda i, j: (i, 0, j))],
        out_specs=pl.BlockSpec((1, s_rows.shape[0], tn), lambda i, j: (i, 0, j)),
        out_shape=jax.ShapeDtypeStruct((depth, s_rows.shape[0], nd), F32),
        compiler_params=pltpu.CompilerParams(dimension_semantics=("parallel", "parallel"),
                                             vmem_limit_bytes=VMEM_LIMIT),
        name="mods",
    )(s_rows, ada_w, ada_b.reshape(depth, 1, nd))


def _route(logits):
    lane = lax.broadcasted_iota(jnp.int32, logits.shape, 1)
    lane_f = lane.astype(F32)
    big = jnp.float32(1e9)
    is_g = lane < N_GROUPS
    gl = jnp.where(is_g, logits, NEG_BIG)
    gm = jnp.max(gl, axis=1, keepdims=True)
    grp = jnp.min(jnp.where(is_g & (gl == gm), lane_f, big), axis=1, keepdims=True)
    p_grp = 1.0 / jnp.sum(jnp.where(is_g, jnp.exp(gl - gm), 0.0), axis=1, keepdims=True)
    lo = N_GROUPS + grp * EXPERTS_PER_GROUP
    in_grp = (lane_f >= lo) & (lane_f < lo + EXPERTS_PER_GROUP)
    el = jnp.where(in_grp, logits, NEG_BIG)
    m1 = jnp.max(el, axis=1, keepdims=True)
    i1 = jnp.min(jnp.where(in_grp & (el == m1), lane_f, big), axis=1, keepdims=True)
    rest = in_grp & (lane_f != i1)
    el2 = jnp.where(rest, logits, NEG_BIG)
    m2 = jnp.max(el2, axis=1, keepdims=True)
    i2 = jnp.min(jnp.where(rest & (el2 == m2), lane_f, big), axis=1, keepdims=True)
    e21 = jnp.exp(m2 - m1)
    s0 = 1.0 / (1.0 + e21)
    w0 = p_grp * s0
    w1 = p_grp * (e21 * s0)
    out = jnp.where(lane == 0, i1 - N_GROUPS,
                    jnp.where(lane == 1, i2 - N_GROUPS,
                              jnp.where(lane == 2, w0, jnp.where(lane == 3, w1, 0.0))))
    return out


def _residual_and_route(x, y, mods, g2_ref, wrt_ref, brt_ref, xo_ref, h2_ref, route_ref, rows=slice(None)):
    _, _, gt1, sh2, sc2, _ = mods
    xn = x + gt1 * y
    xo_ref[rows, :] = xn
    h2 = _norm_mod(xn, g2_ref[...], sc2, sh2).astype(BF16)
    h2_ref[rows, :] = h2
    logits = jnp.dot(h2, wrt_ref[...], preferred_element_type=F32) + brt_ref[...]
    route_ref[rows, :] = _route(logits)


def _round_robin(gens):
    while gens:
        gens = [g for g in gens if next(g, StopIteration) is not StopIteration]


def _gmlp_rows(rows, x_ref, mod_ref, g1_ref, g2_ref, win_ref, bin_ref, lng_ref, lnb_ref, ws_ref, bs_ref,
               wout_ref, bout_ref, wrt_ref, brt_ref, xo_ref, h2_ref, route_ref, gated_ref):
    x = x_ref[rows, :]
    mods = _mod_parts(mod_ref)
    sh1, sc1 = mods[0], mods[1]
    h = _norm_mod(x, g1_ref[...], sc1, sh1).astype(BF16)
    yield
    width = win_ref.shape[1] // 2
    gelu = lambda z: 0.5 * z * (1.0 + lax.erf(z * SQRT_HALF))
    u = gelu(jnp.dot(h, win_ref[:, :width], preferred_element_type=F32) + bin_ref[:, :width])
    yield
    v = gelu(jnp.dot(h, win_ref[:, width:], preferred_element_type=F32) + bin_ref[:, width:])
    yield
    mu = jnp.mean(v, axis=-1, keepdims=True)
    vc = v - mu
    var = jnp.mean(vc * vc, axis=-1, keepdims=True)
    vn = ((vc * lax.rsqrt(var + LN_EPS)) * lng_ref[...] + lnb_ref[...]).astype(BF16)
    yield
    n_chunks = x.shape[0] // CHUNK
    for g in range(width // GMLP_GROUP_CH):
        cs = slice(g * GMLP_GROUP_CH, (g + 1) * GMLP_GROUP_CH)
        rhs = jnp.concatenate([vn[c * CHUNK:(c + 1) * CHUNK, cs] for c in range(n_chunks)], axis=1)
        s = jnp.dot(ws_ref[g], rhs, preferred_element_type=F32) + bs_ref[g]
        for c in range(n_chunks):
            rsl = slice(c * CHUNK, (c + 1) * CHUNK)
            dst = slice(rows.start + c * CHUNK, rows.start + (c + 1) * CHUNK)
            gated_ref[dst, cs] = (u[rsl, cs] * s[:, c * GMLP_GROUP_CH:(c + 1) * GMLP_GROUP_CH]).astype(BF16)
        if g % 4 == 3:
            yield
    y = jnp.dot(gated_ref[rows, :], wout_ref[...], preferred_element_type=F32) + bout_ref[...]
    yield
    _residual_and_route(x, y, mods, g2_ref, wrt_ref, brt_ref, xo_ref, h2_ref, route_ref, rows)
    yield


def _gmlp_kernel(*refs):
    group = refs[0].shape[0] // GMLP_SPLIT
    _round_robin([_gmlp_rows(slice(q * group, (q + 1) * group), *refs) for q in range(GMLP_SPLIT)])


def _full_spec(shape):
    nd = len(shape)
    return pl.BlockSpec(shape, lambda *_: (0,) * nd)


def _layer_spec(shape, j, single_buffer=False):
    nd = len(shape)
    kwargs = dict(pipeline_mode=pl.Buffered(1)) if single_buffer else {}
    return pl.BlockSpec((None,) + tuple(shape[1:]), lambda *_: (j,) + (0,) * (nd - 1), **kwargs)


def _mixer_out_shapes(t_tok, d):
    return (jax.ShapeDtypeStruct((t_tok, d), F32), jax.ShapeDtypeStruct((t_tok, d), BF16),
            jax.ShapeDtypeStruct((t_tok, LANES), F32))


def _mixer_out_specs(d, tm=TM):
    return (pl.BlockSpec((tm, d), lambda t: (t, 0)), pl.BlockSpec((tm, d), lambda t: (t, 0)),
            pl.BlockSpec((tm, LANES), lambda t: (t, 0)))


def _gmlp_layer(x, mods3, mod_spec, i, j, p):
    t_tok, d = x.shape
    width = p["ga_w_out"].shape[1]
    row = lambda a: a.reshape(a.shape[0], 1, a.shape[1])
    args = (x, mods3, row(p["norm1_g"]), row(p["norm2_g"]), p["ga_w_in"], row(p["ga_b_in"]), row(p["ga_ln_g"]),
            row(p["ga_ln_b"]), p["ga_w_s"], p["ga_b_s"], p["ga_w_out"], row(p["ga_b_out"]), p["wrt"], p["brt"])
    in_specs = [pl.BlockSpec((GMLP_TM, d), lambda t: (t, 0)), mod_spec,
                _layer_spec(args[2].shape, i), _layer_spec(args[3].shape, i),
                _layer_spec(args[4].shape, j, True), _layer_spec(args[5].shape, j), _layer_spec(args[6].shape, j),
                _layer_spec(args[7].shape, j), _layer_spec(args[8].shape, j, True), _layer_spec(args[9].shape, j),
                _layer_spec(args[10].shape, j, True), _layer_spec(args[11].shape, j),
                _layer_spec(args[12].shape, i), _layer_spec(args[13].shape, i)]
    return pl.pallas_call(
        _gmlp_kernel,
        grid=(t_tok // GMLP_TM,),
        in_specs=in_specs,
        out_specs=_mixer_out_specs(d, GMLP_TM),
        out_shape=_mixer_out_shapes(t_tok, d),
        scratch_shapes=[pltpu.VMEM((GMLP_TM, width), BF16)],
        compiler_params=pltpu.CompilerParams(dimension_semantics=("parallel",), vmem_limit_bytes=VMEM_LIMIT),
        name="gmlp_layer",
    )(*args)


def _seg_sum(q, e_ref):
    return _dot_x_sel(q, e_ref[...], terms=2)


def _seg_expand(s, et_ref):
    return _dot_x_sel(s, et_ref[...], terms=2)


def _shifted(h, h_above, h_below, is_ctx, first_row_tile, last_row_tile):
    tm, d = h.shape
    q = d // 4
    row = lax.broadcasted_iota(jnp.int32, (tm, 1), 0)
    ctx_i = is_ctx.astype(jnp.int32)
    col = row & (jnp.where(is_ctx, tm, GRID_W) - 1)
    last_col = jnp.where(is_ctx, tm, GRID_W) - 1
    prev1 = jnp.where(col == 0, 0.0, pltpu.roll(h, 1, axis=0))
    next1 = jnp.where(col == last_col, 0.0, pltpu.roll(h, tm - 1, axis=0))
    up = jnp.concatenate([h_above, h[:tm - GRID_W]], axis=0)
    up = jnp.where(row < jnp.where(first_row_tile, GRID_W, 0), 0.0, up)
    down = jnp.concatenate([h[GRID_W:], h_below], axis=0)
    down = jnp.where(row >= jnp.where(last_row_tile, tm - GRID_W, tm), 0.0, down)
    lane_q = lax.broadcasted_iota(jnp.int32, (1, d), 1) >> int(math.log2(q))
    src = lane_q * (1 - ctx_i) + (lane_q >> 1) * ctx_i
    return jnp.where(src == 0, prev1, jnp.where(src == 1, next1, jnp.where(src == 2, up, down)))


def _rwkv_prep_kernel(x_ref, xa_ref, xb_ref, mod_ref, n1_ref, mu_ref, wr_ref, wk_ref, wv_ref, w1_ref, w2_ref, w0_ref,
                      a1_ref, a2_ref, a0_ref, g1_ref, g2_ref, kk_ref, ka_ref, e_ref, et_ref,
                      r_o, v_o, kkn_o, gate_o, lw_o, km_o, bb_o, *, lat_tiles, tiles_per_batch):
    t = pl.program_id(0)
    mods = _mod_parts(mod_ref)
    norm = lambda x: _norm_mod(x, n1_ref[...], mods[1], mods[0])
    h = norm(x_ref[...])
    tb = t % tiles_per_batch
    hs = _shifted(h, norm(xa_ref[...]), norm(xb_ref[...]), t >= lat_tiles, tb == 0, tb == tiles_per_batch - 1)
    xx = hs - h
    mu = mu_ref[...]
    xr, xw, xk, xv, xa, xg = [h + xx * mu[q:q + 1] for q in range(6)]
    r = _dotb(xr, wr_ref[...])
    k = _dotb(xk, wk_ref[...])
    v = _dotb(xv, wv_ref[...])
    r_o[...] = r
    v_o[...] = v
    kkr = k * kk_ref[...]
    rs = lax.rsqrt(_seg_sum(kkr * kkr, e_ref) + 1e-12)
    kkn = kkr * _seg_expand(rs, et_ref)
    kkn_o[...] = kkn
    gate_o[...] = _dotb(jax.nn.sigmoid(_dotb(xg, g1_ref[...])), g2_ref[...])
    tw = jnp.tanh(_dotb(xw, w1_ref[...]))
    ta = _dotb(xa, a1_ref[...])
    lane = lax.broadcasted_iota(jnp.int32, tw.shape, 1)
    half = tw.shape[1] // 2
    w0 = w0_ref[...]
    a0 = a0_ref[...]
    ka = ka_ref[...]
    for z in range(2):
        zm = (lane >= z * half) & (lane < (z + 1) * half)
        lwz = _dotb(jnp.where(zm, tw, 0.0), w2_ref[...])
        lw_o[z] = -EXP_MINUS_HALF * jax.nn.sigmoid(w0[z:z + 1] + lwz)
        la = _dotb(jnp.where(zm, ta, 0.0), a2_ref[...])
        a = jax.nn.sigmoid(a0[z:z + 1] + la)
        km_o[z] = k * (1.0 + (a - 1.0) * ka)
        bb_o[z] = kkn * a


def _rwkv_prep(x, mods3, mod_spec, i, j, p, lat_tiles, tiles_per_batch):
    t_tok, d = x.shape
    row = lambda a: a.reshape(a.shape[0], 1, a.shape[1])
    args = (x, x, x, mods3, row(p["norm1_g"]), p["rw_mu8"], p["rw_wr"], p["rw_wk"], p["rw_wv"], p["rw_w1c"],
            p["rw_w2c"], p["rw_w0"], p["rw_a1c"], p["rw_a2c"], p["rw_a0"], p["rw_g1p"], p["rw_g2p"],
            row(p["rw_k_k"]), row(p["rw_k_a"]), p["head_sel"], p["head_sel_t"])
    tok = pl.BlockSpec((TM, d), lambda t: (t, 0))
    rows_per_tile = TM // GRID_W
    last_row = t_tok // GRID_W - 1
    above = pl.BlockSpec((GRID_W, d), lambda t: (jnp.maximum(t * rows_per_tile - 1, 0), 0))
    below = pl.BlockSpec((GRID_W, d), lambda t: (jnp.minimum((t + 1) * rows_per_tile, last_row), 0))
    in_specs = ([tok, above, below, mod_spec, _layer_spec(args[4].shape, i)]
                + [_layer_spec(a.shape, j) for a in args[5:19]]
                + [_full_spec(args[19].shape), _full_spec(args[20].shape)])
    dir_spec = pl.BlockSpec((2, TM, d), lambda t: (0, t, 0))
    tok_shape = jax.ShapeDtypeStruct((t_tok, d), F32)
    dir_shape = jax.ShapeDtypeStruct((2, t_tok, d), F32)
    return pl.pallas_call(
        functools.partial(_rwkv_prep_kernel, lat_tiles=lat_tiles, tiles_per_batch=tiles_per_batch),
        grid=(t_tok // TM,),
        in_specs=in_specs,
        out_specs=(tok, tok, tok, tok, dir_spec, dir_spec, dir_spec),
        out_shape=(tok_shape, tok_shape, tok_shape, tok_shape, dir_shape, dir_shape, dir_shape),
        compiler_params=pltpu.CompilerParams(dimension_semantics=("parallel",), vmem_limit_bytes=VMEM_LIMIT),
        name="rwkv_prep",
    )(*args)


def _scan_prepare(r, v, kk, lw, km, bb, rev, out):
    c = SCAN_CHUNK
    nb = SCAN_BLOCK
    a = -kk
    row = lax.broadcasted_iota(jnp.int32, (nb, nb), 0)
    col = lax.broadcasted_iota(jnp.int32, (nb, nb), 1)
    shift_c = int(math.log2(c))
    same = (row >> shift_c) == (col >> shift_c)
    if rev:
        incl = same & (col >= row)
        strict = same & (col > row)
    else:
        incl = same & (col <= row)
        strict = same & (col < row)
    incl_b = jnp.where(incl, 1.0, 0.0).astype(BF16)
    eye = jnp.where(row == col, 1.0, 0.0)
    head_of_lane = col >> int(math.log2(RWKV_HEAD))

    lw_hi, lw_lo = _split2(lw)
    cum = (jnp.dot(incl_b, lw_hi, preferred_element_type=F32)
           + jnp.dot(incl_b, lw_lo, preferred_element_type=F32))
    yield
    last = [ci * c if rev else ci * c + c - 1 for ci in range(nb // c)]
    tot_rows = [cum[t:t + 1] for t in last]
    tot = jnp.concatenate([jnp.broadcast_to(t, (c, t.shape[1])) for t in tot_rows], axis=0)
    yield
    at = a * jnp.exp(cum - lw)
    rt = r * jnp.exp(cum)
    e_inv = jnp.exp(-cum)
    bt = bb * e_inv
    kt = km * e_inv
    e_rem = jnp.exp(tot - cum)
    bh = (bb * e_rem).astype(BF16)
    kh = (km * e_rem).astype(BF16)
    et = jnp.exp(jnp.concatenate(tot_rows, axis=0))
    bk = jnp.concatenate([bt, kt], axis=0).astype(BF16)
    yield
    n_heads = SCAN_LANES // RWKV_HEAD
    heads = range(n_heads)
    hms = [head_of_lane == hh for hh in heads]
    at_h = [jnp.where(hm, at, 0.0) for hm in hms]
    rt_h = [jnp.where(hm, rt, 0.0) for hm in hms]
    v_h = [jnp.where(hm, v, 0.0).astype(BF16) for hm in hms]
    v_roll = pltpu.roll(v, RWKV_HEAD, axis=1)
    v_s = [jnp.where(hms[(hh + 1) % n_heads], v_roll, 0.0).astype(BF16) for hh in heads]
    a_ab, a_ak, m_rb, m_rk = [], [], [], []
    for hh in heads:
        s = _dot_nt(jnp.concatenate([at_h[hh], rt_h[hh]], axis=0), bk)
        a_ab.append(jnp.where(strict, s[:nb, :nb], 0.0))
        a_ak.append(jnp.where(strict, s[:nb, nb:], 0.0).astype(BF16))
        m_rb.append(jnp.where(incl, s[nb:, :nb], 0.0).astype(BF16))
        m_rk.append(jnp.where(incl, s[nb:, nb:], 0.0).astype(BF16))
        yield
    pw = list(a_ab)
    tinv = [eye + m for m in a_ab]
    for _ in range(shift_c - 1):
        for hh in heads:
            pw[hh] = _dotb(pw[hh], pw[hh])
        yield
        for hh in heads:
            tinv[hh] = tinv[hh] + _dotb(tinv[hh], pw[hh])
        yield
    x1 = [jnp.dot(a_ak[hh], v_s[hh], preferred_element_type=F32) for hh in heads]
    yield
    tc = [_dotb(tinv[hh], at_h[hh] + x1[hh]) for hh in heads]
    yield
    mc = [_dotb(m_rb[hh], tc[hh]) for hh in heads]
    yield

    def pick(parts, shift):
        res = parts[(n_heads - 1 - shift) % n_heads]
        for lb in range(n_heads - 2, -1, -1):
            res = jnp.where(hms[lb], parts[(lb - shift) % n_heads], res)
        return res

    back = SCAN_LANES - RWKV_HEAD
    ah = pick(tc, 0)
    vh = pltpu.roll(pick(tc, 1), back, axis=1)
    rh = rt + pick(mc, 0)
    yh = (jnp.dot(jnp.concatenate(m_rk, axis=1), jnp.concatenate(v_h, axis=0), preferred_element_type=F32)
          + pltpu.roll(pick(mc, 1), back, axis=1))
    yield
    out.extend([ah.astype(BF16), rh.astype(BF16), vh, yh, bh, kh, v.astype(BF16), et])


def _scan_prepare_stages():
    return 7 + (SCAN_LANES // RWKV_HEAD) + 2 * (int(math.log2(SCAN_CHUNK)) - 1)


def _scan_chain(g, ops, y_ref, lanes, rev, out):
    c = SCAN_CHUNK
    ah, rh, vh, yh, bh, kh, vb, et = ops
    n = g.shape[0]
    shift_h = int(math.log2(RWKV_HEAD))
    blockdiag = ((lax.broadcasted_iota(jnp.int32, (n, n), 0) >> shift_h)
                 == (lax.broadcasted_iota(jnp.int32, (n, n), 1) >> shift_h))
    n_chunks = SCAN_BLOCK // c
    order = range(n_chunks - 1, -1, -1) if rev else range(n_chunks)
    for ci in order:
        rs = slice(ci * c, (ci + 1) * c)
        p = _dot_nt(jnp.concatenate([ah[rs], rh[rs]], axis=0), g)
        u = p[:c] + vh[rs]
        y_ref[rs, lanes] = p[c:] + yh[rs]
        yield
        dg = _dot_tn(jnp.concatenate([u.astype(BF16), vb[rs]], axis=0), jnp.concatenate([bh[rs], kh[rs]], axis=0))
        g = jnp.where(blockdiag, g * et[ci:ci + 1, :] + dg, 0.0)
        yield
    out.append(g)


def _scan_chain_stages():
    return 2 * (SCAN_BLOCK // SCAN_CHUNK)


def _interleave(gen_a, n_a, gen_b, n_b):
    done_b = 0
    for i in range(n_a):
        next(gen_a, None)
        want_b = ((i + 1) * n_b) // n_a
        while done_b < want_b:
            next(gen_b, None)
            done_b += 1
        yield
    for g in (gen_a, gen_b):
        for _ in g:
            pass


def _scan_kernel(r_ref, v_ref, kk_ref, lw_ref, km_ref, bb_ref, y_ref, g_scr, *ops_scr, rev, groups):
    @pl.when(pl.program_id(2) == 0)
    def _():
        g_scr[...] = jnp.zeros_like(g_scr)
        for ref in ops_scr:
            ref[...] = jnp.zeros_like(ref)

    slot = pl.program_id(2) % 2
    work, results = [], []
    for q in range(groups):
        lanes = slice(q * SCAN_LANES, (q + 1) * SCAN_LANES)
        prev = [ref.at[1 - slot, :, lanes] for ref in ops_scr]
        new, g_end = [], []
        prep = _scan_prepare(r_ref[:, lanes], v_ref[:, lanes], kk_ref[:, lanes], lw_ref[:, lanes], km_ref[:, lanes],
                             bb_ref[:, lanes], rev, new)
        chain = _scan_chain(g_scr[q], prev, y_ref, lanes, rev, g_end)
        work.append(_interleave(prep, _scan_prepare_stages(), chain, _scan_chain_stages()))
        results.append((lanes, new, g_end))
    while work:
        work = [w for w in work if next(w, StopIteration) is not StopIteration]
    for q, (lanes, new, g_end) in enumerate(results):
        g_scr[q] = g_end[0]
        for ref, val in zip(ops_scr, new):
            ref[slot, :, lanes] = val


def _wkv_scan(r, v, kkn, lw, km, bb, z, bsz, n_lat, n_ctx):
    t_tok, d = r.shape
    lat_blocks = n_lat // SCAN_BLOCK
    ctx_blocks = n_ctx // SCAN_BLOCK
    steps = ctx_blocks + lat_blocks
    ctx_base = bsz * lat_blocks
    rev = z == 1
    width = SCAN_LANES * SCAN_GROUPS

    def blk(b, s):
        if rev:
            cb = ctx_base + b * ctx_blocks + (ctx_blocks - 1 - s)
            lb = b * lat_blocks + (lat_blocks - 1 - (s - ctx_blocks))
        else:
            cb = ctx_base + b * ctx_blocks + s
            lb = b * lat_blocks + (s - ctx_blocks)
        return jnp.where(s < ctx_blocks, cb, lb)

    in_blk = lambda b, s: blk(b, jnp.minimum(s, steps - 1))
    out_blk = lambda b, s: blk(b, jnp.maximum(s - 1, 0))
    tok = pl.BlockSpec((SCAN_BLOCK, width), lambda b, g, s: (in_blk(b, s), g))
    dirs = pl.BlockSpec((None, SCAN_BLOCK, width), lambda b, g, s: (z, in_blk(b, s), g))
    n_chunks = SCAN_BLOCK // SCAN_CHUNK
    ops_scratch = [pltpu.VMEM((2, SCAN_BLOCK, width), BF16), pltpu.VMEM((2, SCAN_BLOCK, width), BF16),
                   pltpu.VMEM((2, SCAN_BLOCK, width), F32), pltpu.VMEM((2, SCAN_BLOCK, width), F32),
                   pltpu.VMEM((2, SCAN_BLOCK, width), BF16), pltpu.VMEM((2, SCAN_BLOCK, width), BF16),
                   pltpu.VMEM((2, SCAN_BLOCK, width), BF16), pltpu.VMEM((2, n_chunks, width), F32)]
    return pl.pallas_call(
        functools.partial(_scan_kernel, rev=rev, groups=SCAN_GROUPS),
        grid=(bsz, d // width, steps + 1),
        in_specs=[tok, tok, tok, dirs, dirs, dirs],
        out_specs=pl.BlockSpec((SCAN_BLOCK, width), lambda b, g, s: (out_blk(b, s), g)),
        out_shape=jax.ShapeDtypeStruct((t_tok, d), F32),
        scratch_shapes=[pltpu.VMEM((SCAN_GROUPS, SCAN_LANES, SCAN_LANES), F32)] + ops_scratch,
        compiler_params=pltpu.CompilerParams(dimension_semantics=("parallel", "parallel", "arbitrary"),
                                             vmem_limit_bytes=VMEM_LIMIT),
        name="wkv_scan_rev" if rev else "wkv_scan_fwd",
    )(r, v, kkn, lw, km, bb)


def _rwkv_out_kernel(x_ref, mod_ref, y0_ref, y1_ref, r_ref, v_ref, km_ref, gate_ref, lng_ref, lnb_ref, rk_ref,
                     wo_ref, e_ref, et_ref, g2_ref, wrt_ref, brt_ref, xo_ref, h2_ref, route_ref):
    x = x_ref[...]
    mods = _mod_parts(mod_ref)
    y = y0_ref[...] + y1_ref[...]
    inv_n = 1.0 / RWKV_HEAD
    mu = _seg_expand(_seg_sum(y, e_ref) * inv_n, et_ref)
    yc = y - mu
    var = _seg_sum(yc * yc, e_ref) * inv_n
    yn = (yc * _seg_expand(lax.rsqrt(var + GN_EPS), et_ref)) * lng_ref[...] + lnb_ref[...]
    v = v_ref[...]
    bonus = _seg_sum((r_ref[...] * rk_ref[...]) * (km_ref[0] + km_ref[1]), e_ref)
    yn = yn + _seg_expand(bonus, et_ref) * v
    out = _dotb(yn * gate_ref[...], wo_ref[...])
    _residual_and_route(x, out, mods, g2_ref, wrt_ref, brt_ref, xo_ref, h2_ref, route_ref)


def _rwkv_out(x, mods3, mod_spec, y0, y1, r, v, km, gate, i, j, p):
    t_tok, d = x.shape
    row = lambda a: a.reshape(a.shape[0], 1, a.shape[1])
    tok = pl.BlockSpec((TM, d), lambda t: (t, 0))
    dir_spec = pl.BlockSpec((2, TM, d), lambda t: (0, t, 0))
    args = (x, mods3, y0, y1, r, v, km, gate, row(p["rw_ln_g"]), row(p["rw_ln_b"]), row(p["rw_r_k"].reshape(-1, d)),
            p["rw_wo"], p["head_sel"], p["head_sel_t"], row(p["norm2_g"]), p["wrt"], p["brt"])
    in_specs = [tok, mod_spec, tok, tok, tok, tok, dir_spec, tok,
                _layer_spec(args[8].shape, j), _layer_spec(args[9].shape, j), _layer_spec(args[10].shape, j),
                _layer_spec(args[11].shape, j), _full_spec(args[12].shape), _full_spec(args[13].shape),
                _layer_spec(args[14].shape, i), _layer_spec(args[15].shape, i), _layer_spec(args[16].shape, i)]
    return pl.pallas_call(
        _rwkv_out_kernel,
        grid=(t_tok // TM,),
        in_specs=in_specs,
        out_specs=_mixer_out_specs(d),
        out_shape=_mixer_out_shapes(t_tok, d),
        compiler_params=pltpu.CompilerParams(dimension_semantics=("parallel",), vmem_limit_bytes=VMEM_LIMIT),
        name="rwkv_out",
    )(*args)


def _expert_kernel(be_ref, nu_ref, x_ref, wg_ref, wu_ref, wd_ref, o_ref, wg_s, wu_s, wd_s):
    b = pl.program_id(0)
    e = be_ref[b]
    prev = be_ref[jnp.maximum(b - 1, 0)]
    used = b < nu_ref[0]

    @pl.when(used & ((b == 0) | (e != prev)))
    def _():
        wg_s[...] = wg_ref[...].astype(BF16)
        wu_s[...] = wu_ref[...].astype(BF16)
        wd_s[...] = wd_ref[...].astype(BF16)

    @pl.when(used)
    def _():
        xb = x_ref[...]
        g = jnp.dot(xb, wg_s[...], preferred_element_type=F32)
        u = jnp.dot(xb, wu_s[...], preferred_element_type=F32)
        hmid = (g * jax.nn.sigmoid(g)) * u
        o_ref[...] = jnp.dot(hmid.astype(BF16), wd_s[...], preferred_element_type=F32).astype(o_ref.dtype)

    @pl.when(jnp.logical_not(used))
    def _():
        o_ref[...] = jnp.zeros_like(o_ref)


def _experts(buf, block_expert, n_used, w_gate, w_up, w_down, i):
    cap, d = buf.shape
    f = w_gate.shape[-1]
    n_blocks = cap // MOE_ROWS
    grid_spec = pltpu.PrefetchScalarGridSpec(
        num_scalar_prefetch=2,
        grid=(n_blocks,),
        in_specs=[pl.BlockSpec((MOE_ROWS, d), lambda b, be, nu: (b, 0)),
                  pl.BlockSpec((None, None, d, f), lambda b, be, nu: (i, be[b], 0, 0)),
                  pl.BlockSpec((None, None, d, f), lambda b, be, nu: (i, be[b], 0, 0)),
                  pl.BlockSpec((None, None, f, d), lambda b, be, nu: (i, be[b], 0, 0))],
        out_specs=pl.BlockSpec((MOE_ROWS, d), lambda b, be, nu: (b, 0)),
        scratch_shapes=[pltpu.VMEM((d, f), BF16), pltpu.VMEM((d, f), BF16), pltpu.VMEM((f, d), BF16)],
    )
    return pl.pallas_call(
        _expert_kernel,
        grid_spec=grid_spec,
        out_shape=jax.ShapeDtypeStruct((cap, d), BF16),
        compiler_params=pltpu.CompilerParams(dimension_semantics=("arbitrary",), vmem_limit_bytes=VMEM_LIMIT),
        name="experts",
    )(block_expert, n_used, buf, w_gate, w_up, w_down)


def _combine_kernel(x_ref, mod_ref, route_ref, y0_ref, y1_ref, fg_ref, xo_ref, *, final):
    gt2 = _mod_parts(mod_ref)[5]
    route = route_ref[...]
    xn = x_ref[...] + gt2 * (route[:, 2:3] * y0_ref[...].astype(F32) + route[:, 3:4] * y1_ref[...].astype(F32))
    if final:
        xn = (xn * lax.rsqrt(jnp.mean(xn * xn, axis=-1, keepdims=True) + NORM_EPS)) * fg_ref[...]
    xo_ref[...] = xn


def _combine(x, mods3, mod_spec, route, y0, y1, final_g, final, t_out):
    d = x.shape[1]
    tok = pl.BlockSpec((TM, d), lambda t: (t, 0))
    return pl.pallas_call(
        functools.partial(_combine_kernel, final=final),
        grid=(t_out // TM,),
        in_specs=[tok, mod_spec, pl.BlockSpec((TM, LANES), lambda t: (t, 0)), tok, tok, _full_spec((1, d))],
        out_specs=tok,
        out_shape=jax.ShapeDtypeStruct((t_out, d), F32),
        compiler_params=pltpu.CompilerParams(dimension_semantics=("parallel",), vmem_limit_bytes=VMEM_LIMIT),
        name="moe_combine",
    )(x, mods3, route, y0, y1, final_g.reshape(1, d))


def _dest_kernel(route_ref, d_ref, counts_ref, base_scr, start_scr):
    phase = pl.program_id(0)
    t = pl.program_id(1)
    route = route_ref[...]
    tm = route.shape[0]
    lane = lax.broadcasted_iota(jnp.int32, route.shape, 1)
    lane_f = lane.astype(F32)
    onehots = [jnp.where(lane_f == route[:, k:k + 1], 1.0, 0.0) for k in range(TOP_K)]

    @pl.when((phase == 0) & (t == 0))
    def _():
        base_scr[...] = jnp.zeros_like(base_scr)

    @pl.when((phase == 1) & (t == 0))
    def _():
        counts = base_scr[...]
        counts_ref[...] = counts
        padded = jnp.floor((counts + (MOE_ROWS - 1)) * (1.0 / MOE_ROWS)) * MOE_ROWS
        before = (lax.broadcasted_iota(jnp.int32, (LANES, LANES), 0)
                  < lax.broadcasted_iota(jnp.int32, (LANES, LANES), 1))
        start_scr[...] = _dot_x_sel(padded, jnp.where(before, 1.0, 0.0).astype(BF16))
        base_scr[...] = jnp.zeros_like(base_scr)

    @pl.when(phase == 0)
    def _():
        base_scr[...] = base_scr[...] + sum(jnp.sum(oh, axis=0, keepdims=True) for oh in onehots)

    @pl.when(phase == 1)
    def _():
        earlier = (lax.broadcasted_iota(jnp.int32, (tm, tm), 1)
                   < lax.broadcasted_iota(jnp.int32, (tm, tm), 0))
        earlier_b = jnp.where(earlier, 1.0, 0.0).astype(BF16)
        diag = (lax.broadcasted_iota(jnp.int32, (LANES, LANES), 0)
                == lax.broadcasted_iota(jnp.int32, (LANES, LANES), 1))
        base = base_scr[...] + start_scr[...]
        for k, oh in enumerate(onehots):
            before = jnp.dot(earlier_b, oh.astype(BF16), preferred_element_type=F32)
            dest = jnp.sum(oh * (base + before), axis=1, keepdims=True)
            rows = [jnp.sum(jnp.where(diag, jnp.broadcast_to(dest[i * LANES:(i + 1) * LANES], (LANES, LANES)), 0.0),
                            axis=0, keepdims=True) for i in range(tm // LANES)]
            d_ref[k, 0] = jnp.concatenate(rows, axis=0).astype(jnp.int32)
            base = base + jnp.sum(oh, axis=0, keepdims=True)
        base_scr[...] = base - start_scr[...]


def _dests(route):
    t_tok = route.shape[0]
    assert t_tok % DEST_TM == 0
    tiles = t_tok // DEST_TM
    return pl.pallas_call(
        _dest_kernel,
        grid=(2, tiles),
        in_specs=[pl.BlockSpec((DEST_TM, LANES), lambda ph, t: (t, 0))],
        out_specs=(pl.BlockSpec((TOP_K, 1, DEST_TM // LANES, LANES), lambda ph, t: (0, t * ph, 0, 0)),
                   pl.BlockSpec((1, LANES), lambda ph, t: (0, 0))),
        out_shape=(jax.ShapeDtypeStruct((TOP_K, tiles, DEST_TM // LANES, LANES), jnp.int32),
                   jax.ShapeDtypeStruct((1, LANES), F32)),
        scratch_shapes=[pltpu.VMEM((1, LANES), F32), pltpu.VMEM((1, LANES), F32)],
        compiler_params=pltpu.CompilerParams(dimension_semantics=("arbitrary", "arbitrary"),
                                             vmem_limit_bytes=VMEM_LIMIT),
        name="moe_dests",
    )(route)


def _moe(x, h2, route, mods3, mod_spec, i, p, final, t_out):
    t_tok, d = x.shape
    n_assign = t_tok * TOP_K
    dest4, counts_f = _dests(route)
    dest = dest4.reshape(TOP_K, t_tok)
    counts = counts_f[0, :N_EXPERTS].astype(jnp.int32)
    pad_end = jnp.cumsum(((counts + MOE_ROWS - 1) // MOE_ROWS) * MOE_ROWS)
    n_blocks = (n_assign + MOE_ROWS - 1) // MOE_ROWS + N_EXPERTS
    token = jnp.arange(t_tok, dtype=jnp.int32)
    slot_token = (jnp.arange(n_blocks * MOE_ROWS, dtype=jnp.int32) % t_tok).at[dest.reshape(-1)].set(
        jnp.tile(token, TOP_K), mode="promise_in_bounds", unique_indices=True)
    block_start = jnp.arange(n_blocks, dtype=jnp.int32) * MOE_ROWS
    block_expert = jnp.minimum(jnp.sum((pad_end[None, :] <= block_start[:, None]).astype(jnp.int32), axis=1),
                               N_EXPERTS - 1)
    n_used = (pad_end[-1:] // MOE_ROWS).astype(jnp.int32)
    buf = h2.at[slot_token].get(mode="promise_in_bounds")
    out = _experts(buf, block_expert, n_used, p["moe_w_gate"], p["moe_w_up"], p["moe_w_down"], i)
    y0 = out.at[dest[0]].get(mode="promise_in_bounds", unique_indices=True)
    y1 = out.at[dest[1]].get(mode="promise_in_bounds", unique_indices=True)
    return _combine(x, mods3, mod_spec, route, y0, y1, p["final_g"], final, t_out)


def kernel(x, c, ctx, c_ctx, ada_w, ada_b, norm1_g, norm2_g, final_g, ga_w_in, ga_b_in, ga_ln_g, ga_ln_b, ga_w_s,
           ga_b_s, ga_w_out, ga_b_out, rw_mu, rw_wr, rw_wk, rw_wv, rw_wo, rw_w0, rw_w1, rw_w2, rw_a0, rw_a1, rw_a2,
           rw_g1, rw_g2, rw_k_k, rw_k_a, rw_r_k, rw_ln_g, rw_ln_b, moe_w_grp, moe_b_grp, moe_w_exp, moe_b_exp,
           moe_w_gate, moe_w_up, moe_w_down):
    bsz, n_lat, d = x.shape
    n_ctx = ctx.shape[1]
    depth = ada_w.shape[0]
    n_mixers = 2
    assert n_lat % TM == 0 and n_ctx == TM and TM % GRID_W == 0
    assert n_lat % GMLP_TM == 0 and (bsz * n_ctx) % GMLP_TM == 0 and GMLP_TM % (GMLP_SPLIT * CHUNK) == 0
    assert n_lat % SCAN_BLOCK == 0 and n_ctx % SCAN_BLOCK == 0
    assert bsz + 1 <= 8 and d % (SCAN_LANES * SCAN_GROUPS) == 0
    assert MOE_ROWS & (MOE_ROWS - 1) == 0
    t_lat = bsz * n_lat
    lat_tiles = t_lat // TM
    tiles_per_batch = n_lat // TM
    heads = d // RWKV_HEAD

    s_rows = jnp.concatenate([c, c_ctx[None, :], jnp.zeros((8 - bsz - 1, d), F32)], axis=0)
    mods = _mods(s_rows, ada_w, ada_b)
    mods3 = mods.reshape(depth * 8, 1, N_MOD * d)

    def mod_spec_for(i, tm=TM):
        def imap(t):
            return (i * 8 + jnp.where(t < t_lat // tm, t // (n_lat // tm), bsz), 0, 0)
        return pl.BlockSpec((1, 1, N_MOD * d), imap)

    head_sel = (jnp.arange(d)[:, None] // RWKV_HEAD == jnp.arange(LANES)[None, :]).astype(BF16)
    zpad = lambda a, axis, n: jnp.pad(a, [(0, n - a.shape[k]) if k == axis else (0, 0) for k in range(a.ndim)])
    glora = ((rw_g1.shape[-1] + LANES - 1) // LANES) * LANES
    wrt = jnp.concatenate([moe_w_grp, moe_w_exp], axis=-1)
    brt = jnp.concatenate([moe_b_grp, moe_b_exp], axis=-1)
    p = dict(
        norm1_g=norm1_g, norm2_g=norm2_g, final_g=final_g,
        ga_w_in=ga_w_in.astype(BF16), ga_b_in=ga_b_in, ga_ln_g=ga_ln_g, ga_ln_b=ga_ln_b,
        ga_w_s=ga_w_s.astype(BF16), ga_b_s=ga_b_s[..., None], ga_w_out=ga_w_out.astype(BF16), ga_b_out=ga_b_out,
        rw_mu8=zpad(rw_mu, 1, 8),
        rw_wr=rw_wr.astype(BF16), rw_wk=rw_wk.astype(BF16), rw_wv=rw_wv.astype(BF16), rw_wo=rw_wo.astype(BF16),
        rw_w1c=jnp.concatenate([rw_w1[:, 0], rw_w1[:, 1]], axis=-1).astype(BF16),
        rw_w2c=jnp.concatenate([rw_w2[:, 0], rw_w2[:, 1]], axis=-2).astype(BF16),
        rw_a1c=jnp.concatenate([rw_a1[:, 0], rw_a1[:, 1]], axis=-1).astype(BF16),
        rw_a2c=jnp.concatenate([rw_a2[:, 0], rw_a2[:, 1]], axis=-2).astype(BF16),
        rw_w0=rw_w0, rw_a0=rw_a0,
        rw_g1p=zpad(rw_g1, 2, glora).astype(BF16), rw_g2p=zpad(rw_g2, 1, glora).astype(BF16),
        rw_k_k=rw_k_k, rw_k_a=rw_k_a, rw_r_k=rw_r_k, rw_ln_g=rw_ln_g, rw_ln_b=rw_ln_b,
        head_sel=head_sel, head_sel_t=head_sel.T,
        wrt=zpad(wrt, 2, LANES).astype(BF16), brt=zpad(brt, 1, LANES).reshape(depth, 1, LANES),
        moe_w_gate=moe_w_gate, moe_w_up=moe_w_up, moe_w_down=moe_w_down,
    )
    assert heads <= LANES and 2 * rw_w1.shape[-1] == LANES and 2 * rw_a1.shape[-1] == LANES

    xs = jnp.concatenate([x.reshape(t_lat, d), ctx.reshape(bsz * n_ctx, d)], axis=0)
    for i in range(depth):
        j = i // n_mixers
        mod_spec = mod_spec_for(i)
        if i % n_mixers == 0:
            xs, h2, route = _gmlp_layer(xs, mods3, mod_spec_for(i, GMLP_TM), i, j, p)
        else:
            r, v, kkn, gate, lw, km, bb = _rwkv_prep(xs, mods3, mod_spec, i, j, p, lat_tiles, tiles_per_batch)
            y0 = _wkv_scan(r, v, kkn, lw, km, bb, 0, bsz, n_lat, n_ctx)
            y1 = _wkv_scan(r, v, kkn, lw, km, bb, 1, bsz, n_lat, n_ctx)
            xs, h2, route = _rwkv_out(xs, mods3, mod_spec, y0, y1, r, v, km, gate, i, j, p)
        last = i == depth - 1
        xs = _moe(xs, h2, route, mods3, mod_spec, i, p, final=last, t_out=t_lat if last else xs.shape[0])
    return xs.reshape(bsz, n_lat, d)
```

```python
import functools
import math

import jax
import jax.numpy as jnp
from jax import lax
from jax.experimental import pallas as pl
from jax.experimental.pallas import tpu as pltpu

F32 = jnp.float32
BF16 = jnp.bfloat16

N_MOD = 6
NORM_EPS = 1e-6
GRID_W = 64
CHUNK = 128
GMLP_GROUP_CH = 128
LN_EPS = 1e-5
RWKV_HEAD = 64
GN_EPS = 64e-5
N_GROUPS = 4
EXPERTS_PER_GROUP = 8
N_EXPERTS = N_GROUPS * EXPERTS_PER_GROUP
TOP_K = 2

LANES = 128
TM = 256
GMLP_TM = 512
GMLP_SPLIT = 2
MOE_ROWS = 512
DEST_TM = 512
SCAN_BLOCK = 256
SCAN_CHUNK = 32
SCAN_LANES = 256
SCAN_GROUPS = 2
VMEM_LIMIT = 56 * 1024 * 1024
NEG_BIG = -3.0e38
SQRT_HALF = 0.7071067811865476
EXP_MINUS_HALF = 0.6065306597126334


def _dotb(a, b):
    return jnp.dot(a.astype(BF16), b.astype(BF16), preferred_element_type=F32)


def _dot_nt(a, b):
    return lax.dot_general(a.astype(BF16), b.astype(BF16), (((1,), (1,)), ((), ())), preferred_element_type=F32)


def _dot_tn(a, b):
    return lax.dot_general(a.astype(BF16), b.astype(BF16), (((0,), (0,)), ((), ())), preferred_element_type=F32)


def _split2(x):
    hi = x.astype(BF16)
    lo = (x - hi.astype(F32)).astype(BF16)
    return hi, lo


def _split3(x):
    hi = x.astype(BF16)
    r1 = x - hi.astype(F32)
    mid = r1.astype(BF16)
    lo = (r1 - mid.astype(F32)).astype(BF16)
    return hi, mid, lo


def _dot_x_sel(x, sel_bf16, terms=3):
    d = lambda p: jnp.dot(p, sel_bf16, preferred_element_type=F32)
    return sum(d(p) for p in (_split3(x) if terms == 3 else _split2(x)))


def _norm_mod(x, g, sc, sh):
    y = x * lax.rsqrt(jnp.mean(x * x, axis=-1, keepdims=True) + NORM_EPS)
    return (y * g) * (1.0 + sc) + sh


def _mod_parts(mod_ref):
    m = mod_ref[0]
    d = m.shape[-1] // N_MOD
    return [m[:, i * d:(i + 1) * d] for i in range(N_MOD)]


def _mods_kernel(s_ref, w_ref, b_ref, o_ref):
    s = s_ref[...]
    s = s * jax.nn.sigmoid(s)
    o_ref[0] = _dotb(s, w_ref[0]) + b_ref[0]


def _mods(s_rows, ada_w, ada_b):
    depth, d, nd = ada_w.shape
    tn = nd // 4
    return pl.pallas_call(
        _mods_kernel,
        grid=(depth, nd // tn),
        in_specs=[pl.BlockSpec(s_rows.shape, lambda i, j: (0, 0)),
                  pl.BlockSpec((1, d, tn), lambda i, j: (i, 0, j)),
                  pl.BlockSpec((1, 1, tn), lambda i, j: (i, 0, j))],
        out_specs=pl.BlockSpec((1, s_rows.shape[0], tn), lambda i, j: (i, 0, j)),
        out_shape=jax.ShapeDtypeStruct((depth, s_rows.shape[0], nd), F32),
        compiler_params=pltpu.CompilerParams(dimension_semantics=("parallel", "parallel"),
                                             vmem_limit_bytes=VMEM_LIMIT),
        name="mods",
    )(s_rows, ada_w, ada_b.reshape(depth, 1, nd))


def _route(logits):
    lane = lax.broadcasted_iota(jnp.int32, logits.shape, 1)
    lane_f = lane.astype(F32)
    big = jnp.float32(1e9)
    is_g = lane < N_GROUPS
    gl = jnp.where(is_g, logits, NEG_BIG)
    gm = jnp.max(gl, axis=1, keepdims=True)
    grp = jnp.min(jnp.where(is_g & (gl == gm), lane_f, big), axis=1, keepdims=True)
    p_grp = 1.0 / jnp.sum(jnp.where(is_g, jnp.exp(gl - gm), 0.0), axis=1, keepdims=True)
    lo = N_GROUPS + grp * EXPERTS_PER_GROUP
    in_grp = (lane_f >= lo) & (lane_f < lo + EXPERTS_PER_GROUP)
    el = jnp.where(in_grp, logits, NEG_BIG)
    m1 = jnp.max(el, axis=1, keepdims=True)
    i1 = jnp.min(jnp.where(in_grp & (el == m1), lane_f, big), axis=1, keepdims=True)
    rest = in_grp & (lane_f != i1)
    el2 = jnp.where(rest, logits, NEG_BIG)
    m2 = jnp.max(el2, axis=1, keepdims=True)
    i2 = jnp.min(jnp.where(rest & (el2 == m2), lane_f, big), axis=1, keepdims=True)
    e21 = jnp.exp(m2 - m1)
    s0 = 1.0 / (1.0 + e21)
    w0 = p_grp * s0
    w1 = p_grp * (e21 * s0)
    out = jnp.where(lane == 0, i1 - N_GROUPS,
                    jnp.where(lane == 1, i2 - N_GROUPS,
                              jnp.where(lane == 2, w0, jnp.where(lane == 3, w1, 0.0))))
    return out


def _residual_and_route(x, y, mods, g2_ref, wrt_ref, brt_ref, xo_ref, h2_ref, route_ref, rows=slice(None)):
    _, _, gt1, sh2, sc2, _ = mods
    xn = x + gt1 * y
    xo_ref[rows, :] = xn
    h2 = _norm_mod(xn, g2_ref[...], sc2, sh2).astype(BF16)
    h2_ref[rows, :] = h2
    logits = jnp.dot(h2, wrt_ref[...], preferred_element_type=F32) + brt_ref[...]
    route_ref[rows, :] = _route(logits)


def _round_robin(gens):
    while gens:
        gens = [g for g in gens if next(g, StopIteration) is not StopIteration]


def _gmlp_rows(rows, x_ref, mod_ref, g1_ref, g2_ref, win_ref, bin_ref, lng_ref, lnb_ref, ws_ref, bs_ref,
               wout_ref, bout_ref, wrt_ref, brt_ref, xo_ref, h2_ref, route_ref, gated_ref):
    x = x_ref[rows, :]
    mods = _mod_parts(mod_ref)
    sh1, sc1 = mods[0], mods[1]
    h = _norm_mod(x, g1_ref[...], sc1, sh1).astype(BF16)
    yield
    width = win_ref.shape[1] // 2
    gelu = lambda z: 0.5 * z * (1.0 + lax.erf(z * SQRT_HALF))
    u = gelu(jnp.dot(h, win_ref[:, :width], preferred_element_type=F32) + bin_ref[:, :width])
    yield
    v = gelu(jnp.dot(h, win_ref[:, width:], preferred_element_type=F32) + bin_ref[:, width:])
    yield
    mu = jnp.mean(v, axis=-1, keepdims=True)
    vc = v - mu
    var = jnp.mean(vc * vc, axis=-1, keepdims=True)
    vn = ((vc * lax.rsqrt(var + LN_EPS)) * lng_ref[...] + lnb_ref[...]).astype(BF16)
    yield
    n_chunks = x.shape[0] // CHUNK
    for g in range(width // GMLP_GROUP_CH):
        cs = slice(g * GMLP_GROUP_CH, (g + 1) * GMLP_GROUP_CH)
        rhs = jnp.concatenate([vn[c * CHUNK:(c + 1) * CHUNK, cs] for c in range(n_chunks)], axis=1)
        s = jnp.dot(ws_ref[g], rhs, preferred_element_type=F32) + bs_ref[g]
        for c in range(n_chunks):
            rsl = slice(c * CHUNK, (c + 1) * CHUNK)
            dst = slice(rows.start + c * CHUNK, rows.start + (c + 1) * CHUNK)
            gated_ref[dst, cs] = (u[rsl, cs] * s[:, c * GMLP_GROUP_CH:(c + 1) * GMLP_GROUP_CH]).astype(BF16)
        if g % 4 == 3:
            yield
    y = jnp.dot(gated_ref[rows, :], wout_ref[...], preferred_element_type=F32) + bout_ref[...]
    yield
    _residual_and_route(x, y, mods, g2_ref, wrt_ref, brt_ref, xo_ref, h2_ref, route_ref, rows)
    yield


def _gmlp_kernel(*refs):
    group = refs[0].shape[0] // GMLP_SPLIT
    _round_robin([_gmlp_rows(slice(q * group, (q + 1) * group), *refs) for q in range(GMLP_SPLIT)])


def _full_spec(shape):
    nd = len(shape)
    return pl.BlockSpec(shape, lambda *_: (0,) * nd)


def _layer_spec(shape, j, single_buffer=False):
    nd = len(shape)
    kwargs = dict(pipeline_mode=pl.Buffered(1)) if single_buffer else {}
    return pl.BlockSpec((None,) + tuple(shape[1:]), lambda *_: (j,) + (0,) * (nd - 1), **kwargs)


def _mixer_out_shapes(t_tok, d):
    return (jax.ShapeDtypeStruct((t_tok, d), F32), jax.ShapeDtypeStruct((t_tok, d), BF16),
            jax.ShapeDtypeStruct((t_tok, LANES), F32))


def _mixer_out_specs(d, tm=TM):
    return (pl.BlockSpec((tm, d), lambda t: (t, 0)), pl.BlockSpec((tm, d), lambda t: (t, 0)),
            pl.BlockSpec((tm, LANES), lambda t: (t, 0)))


def _gmlp_layer(x, mods3, mod_spec, i, j, p):
    t_tok, d = x.shape
    width = p["ga_w_out"].shape[1]
    row = lambda a: a.reshape(a.shape[0], 1, a.shape[1])
    args = (x, mods3, row(p["norm1_g"]), row(p["norm2_g"]), p["ga_w_in"], row(p["ga_b_in"]), row(p["ga_ln_g"]),
            row(p["ga_ln_b"]), p["ga_w_s"], p["ga_b_s"], p["ga_w_out"], row(p["ga_b_out"]), p["wrt"], p["brt"])
    in_specs = [pl.BlockSpec((GMLP_TM, d), lambda t: (t, 0)), mod_spec,
                _layer_spec(args[2].shape, i), _layer_spec(args[3].shape, i),
                _layer_spec(args[4].shape, j, True), _layer_spec(args[5].shape, j), _layer_spec(args[6].shape, j),
                _layer_spec(args[7].shape, j), _layer_spec(args[8].shape, j, True), _layer_spec(args[9].shape, j),
                _layer_spec(args[10].shape, j, True), _layer_spec(args[11].shape, j),
                _layer_spec(args[12].shape, i), _layer_spec(args[13].shape, i)]
    return pl.pallas_call(
        _gmlp_kernel,
        grid=(t_tok // GMLP_TM,),
        in_specs=in_specs,
        out_specs=_mixer_out_specs(d, GMLP_TM),
        out_shape=_mixer_out_shapes(t_tok, d),
        scratch_shapes=[pltpu.VMEM((GMLP_TM, width), BF16)],
        compiler_params=pltpu.CompilerParams(dimension_semantics=("parallel",), vmem_limit_bytes=VMEM_LIMIT),
        name="gmlp_layer",
    )(*args)


def _seg_sum(q, e_ref):
    return _dot_x_sel(q, e_ref[...], terms=2)


def _seg_expand(s, et_ref):
    return _dot_x_sel(s, et_ref[...], terms=2)


def _shifted(h, h_above, h_below, is_ctx, first_row_tile, last_row_tile):
    tm, d = h.shape
    q = d // 4
    row = lax.broadcasted_iota(jnp.int32, (tm, 1), 0)
    ctx_i = is_ctx.astype(jnp.int32)
    col = row & (jnp.where(is_ctx, tm, GRID_W) - 1)
    last_col = jnp.where(is_ctx, tm, GRID_W) - 1
    prev1 = jnp.where(col == 0, 0.0, pltpu.roll(h, 1, axis=0))
    next1 = jnp.where(col == last_col, 0.0, pltpu.roll(h, tm - 1, axis=0))
    up = jnp.concatenate([h_above, h[:tm - GRID_W]], axis=0)
    up = jnp.where(row < jnp.where(first_row_tile, GRID_W, 0), 0.0, up)
    down = jnp.concatenate([h[GRID_W:], h_below], axis=0)
    down = jnp.where(row >= jnp.where(last_row_tile, tm - GRID_W, tm), 0.0, down)
    lane_q = lax.broadcasted_iota(jnp.int32, (1, d), 1) >> int(math.log2(q))
    src = lane_q * (1 - ctx_i) + (lane_q >> 1) * ctx_i
    return jnp.where(src == 0, prev1, jnp.where(src == 1, next1, jnp.where(src == 2, up, down)))


def _rwkv_prep_kernel(x_ref, xa_ref, xb_ref, mod_ref, n1_ref, mu_ref, wr_ref, wk_ref, wv_ref, w1_ref, w2_ref, w0_ref,
                      a1_ref, a2_ref, a0_ref, g1_ref, g2_ref, kk_ref, ka_ref, e_ref, et_ref,
                      r_o, v_o, kkn_o, gate_o, lw_o, km_o, bb_o, *, lat_tiles, tiles_per_batch):
    t = pl.program_id(0)
    mods = _mod_parts(mod_ref)
    norm = lambda x: _norm_mod(x, n1_ref[...], mods[1], mods[0])
    h = norm(x_ref[...])
    tb = t % tiles_per_batch
    hs = _shifted(h, norm(xa_ref[...]), norm(xb_ref[...]), t >= lat_tiles, tb == 0, tb == tiles_per_batch - 1)
    xx = hs - h
    mu = mu_ref[...]
    xr, xw, xk, xv, xa, xg = [h + xx * mu[q:q + 1] for q in range(6)]
    r = _dotb(xr, wr_ref[...])
    k = _dotb(xk, wk_ref[...])
    v = _dotb(xv, wv_ref[...])
    r_o[...] = r
    v_o[...] = v
    kkr = k * kk_ref[...]
    rs = lax.rsqrt(_seg_sum(kkr * kkr, e_ref) + 1e-12)
    kkn = kkr * _seg_expand(rs, et_ref)
    kkn_o[...] = kkn
    gate_o[...] = _dotb(jax.nn.sigmoid(_dotb(xg, g1_ref[...])), g2_ref[...])
    tw = jnp.tanh(_dotb(xw, w1_ref[...]))
    ta = _dotb(xa, a1_ref[...])
    lane = lax.broadcasted_iota(jnp.int32, tw.shape, 1)
    half = tw.shape[1] // 2
    w0 = w0_ref[...]
    a0 = a0_ref[...]
    ka = ka_ref[...]
    for z in range(2):
        zm = (lane >= z * half) & (lane < (z + 1) * half)
        lwz = _dotb(jnp.where(zm, tw, 0.0), w2_ref[...])
        lw_o[z] = -EXP_MINUS_HALF * jax.nn.sigmoid(w0[z:z + 1] + lwz)
        la = _dotb(jnp.where(zm, ta, 0.0), a2_ref[...])
        a = jax.nn.sigmoid(a0[z:z + 1] + la)
        km_o[z] = k * (1.0 + (a - 1.0) * ka)
        bb_o[z] = kkn * a


def _rwkv_prep(x, mods3, mod_spec, i, j, p, lat_tiles, tiles_per_batch):
    t_tok, d = x.shape
    row = lambda a: a.reshape(a.shape[0], 1, a.shape[1])
    args = (x, x, x, mods3, row(p["norm1_g"]), p["rw_mu8"], p["rw_wr"], p["rw_wk"], p["rw_wv"], p["rw_w1c"],
            p["rw_w2c"], p["rw_w0"], p["rw_a1c"], p["rw_a2c"], p["rw_a0"], p["rw_g1p"], p["rw_g2p"],
            row(p["rw_k_k"]), row(p["rw_k_a"]), p["head_sel"], p["head_sel_t"])
    tok = pl.BlockSpec((TM, d), lambda t: (t, 0))
    rows_per_tile = TM // GRID_W
    last_row = t_tok // GRID_W - 1
    above = pl.BlockSpec((GRID_W, d), lambda t: (jnp.maximum(t * rows_per_tile - 1, 0), 0))
    below = pl.BlockSpec((GRID_W, d), lambda t: (jnp.minimum((t + 1) * rows_per_tile, last_row), 0))
    in_specs = ([tok, above, below, mod_spec, _layer_spec(args[4].shape, i)]
                + [_layer_spec(a.shape, j) for a in args[5:19]]
                + [_full_spec(args[19].shape), _full_spec(args[20].shape)])
    dir_spec = pl.BlockSpec((2, TM, d), lambda t: (0, t, 0))
    tok_shape = jax.ShapeDtypeStruct((t_tok, d), F32)
    dir_shape = jax.ShapeDtypeStruct((2, t_tok, d), F32)
    return pl.pallas_call(
        functools.partial(_rwkv_prep_kernel, lat_tiles=lat_tiles, tiles_per_batch=tiles_per_batch),
        grid=(t_tok // TM,),
        in_specs=in_specs,
        out_specs=(tok, tok, tok, tok, dir_spec, dir_spec, dir_spec),
        out_shape=(tok_shape, tok_shape, tok_shape, tok_shape, dir_shape, dir_shape, dir_shape),
        compiler_params=pltpu.CompilerParams(dimension_semantics=("parallel",), vmem_limit_bytes=VMEM_LIMIT),
        name="rwkv_prep",
    )(*args)


def _scan_prepare(r, v, kk, lw, km, bb, rev, out):
    c = SCAN_CHUNK
    nb = SCAN_BLOCK
    a = -kk
    row = lax.broadcasted_iota(jnp.int32, (nb, nb), 0)
    col = lax.broadcasted_iota(jnp.int32, (nb, nb), 1)
    shift_c = int(math.log2(c))
    same = (row >> shift_c) == (col >> shift_c)
    if rev:
        incl = same & (col >= row)
        strict = same & (col > row)
    else:
        incl = same & (col <= row)
        strict = same & (col < row)
    incl_b = jnp.where(incl, 1.0, 0.0).astype(BF16)
    eye = jnp.where(row == col, 1.0, 0.0)
    head_of_lane = col >> int(math.log2(RWKV_HEAD))

    lw_hi, lw_lo = _split2(lw)
    cum = (jnp.dot(incl_b, lw_hi, preferred_element_type=F32)
           + jnp.dot(incl_b, lw_lo, preferred_element_type=F32))
    yield
    last = [ci * c if rev else ci * c + c - 1 for ci in range(nb // c)]
    tot_rows = [cum[t:t + 1] for t in last]
    tot = jnp.concatenate([jnp.broadcast_to(t, (c, t.shape[1])) for t in tot_rows], axis=0)
    yield
    at = a * jnp.exp(cum - lw)
    rt = r * jnp.exp(cum)
    e_inv = jnp.exp(-cum)
    bt = bb * e_inv
    kt = km * e_inv
    e_rem = jnp.exp(tot - cum)
    bh = (bb * e_rem).astype(BF16)
    kh = (km * e_rem).astype(BF16)
    et = jnp.exp(jnp.concatenate(tot_rows, axis=0))
    bk = jnp.concatenate([bt, kt], axis=0).astype(BF16)
    yield
    n_heads = SCAN_LANES // RWKV_HEAD
    heads = range(n_heads)
    hms = [head_of_lane == hh for hh in heads]
    at_h = [jnp.where(hm, at, 0.0) for hm in hms]
    rt_h = [jnp.where(hm, rt, 0.0) for hm in hms]
    v_h = [jnp.where(hm, v, 0.0).astype(BF16) for hm in hms]
    v_roll = pltpu.roll(v, RWKV_HEAD, axis=1)
    v_s = [jnp.where(hms[(hh + 1) % n_heads], v_roll, 0.0).astype(BF16) for hh in heads]
    a_ab, a_ak, m_rb, m_rk = [], [], [], []
    for hh in heads:
        s = _dot_nt(jnp.concatenate([at_h[hh], rt_h[hh]], axis=0), bk)
        a_ab.append(jnp.where(strict, s[:nb, :nb], 0.0))
        a_ak.append(jnp.where(strict, s[:nb, nb:], 0.0).astype(BF16))
        m_rb.append(jnp.where(incl, s[nb:, :nb], 0.0).astype(BF16))
        m_rk.append(jnp.where(incl, s[nb:, nb:], 0.0).astype(BF16))
        yield
    pw = list(a_ab)
    tinv = [eye + m for m in a_ab]
    for _ in range(shift_c - 1):
        for hh in heads:
            pw[hh] = _dotb(pw[hh], pw[hh])
        yield
        for hh in heads:
            tinv[hh] = tinv[hh] + _dotb(tinv[hh], pw[hh])
        yield
    x1 = [jnp.dot(a_ak[hh], v_s[hh], preferred_element_type=F32) for hh in heads]
    yield
    tc = [_dotb(tinv[hh], at_h[hh] + x1[hh]) for hh in heads]
    yield
    mc = [_dotb(m_rb[hh], tc[hh]) for hh in heads]
    yield

    def pick(parts, shift):
        res = parts[(n_heads - 1 - shift) % n_heads]
        for lb in range(n_heads - 2, -1, -1):
            res = jnp.where(hms[lb], parts[(lb - shift) % n_heads], res)
        return res

    back = SCAN_LANES - RWKV_HEAD
    ah = pick(tc, 0)
    vh = pltpu.roll(pick(tc, 1), back, axis=1)
    rh = rt + pick(mc, 0)
    yh = (jnp.dot(jnp.concatenate(m_rk, axis=1), jnp.concatenate(v_h, axis=0), preferred_element_type=F32)
          + pltpu.roll(pick(mc, 1), back, axis=1))
    yield
    out.extend([ah.astype(BF16), rh.astype(BF16), vh, yh, bh, kh, v.astype(BF16), et])


def _scan_prepare_stages():
    return 7 + (SCAN_LANES // RWKV_HEAD) + 2 * (int(math.log2(SCAN_CHUNK)) - 1)


def _scan_chain(g, ops, y_ref, lanes, rev, out):
    c = SCAN_CHUNK
    ah, rh, vh, yh, bh, kh, vb, et = ops
    n = g.shape[0]
    shift_h = int(math.log2(RWKV_HEAD))
    blockdiag = ((lax.broadcasted_iota(jnp.int32, (n, n), 0) >> shift_h)
                 == (lax.broadcasted_iota(jnp.int32, (n, n), 1) >> shift_h))
    n_chunks = SCAN_BLOCK // c
    order = range(n_chunks - 1, -1, -1) if rev else range(n_chunks)
    for ci in order:
        rs = slice(ci * c, (ci + 1) * c)
        p = _dot_nt(jnp.concatenate([ah[rs], rh[rs]], axis=0), g)
        u = p[:c] + vh[rs]
        y_ref[rs, lanes] = p[c:] + yh[rs]
        yield
        dg = _dot_tn(jnp.concatenate([u.astype(BF16), vb[rs]], axis=0), jnp.concatenate([bh[rs], kh[rs]], axis=0))
        g = jnp.where(blockdiag, g * et[ci:ci + 1, :] + dg, 0.0)
        yield
    out.append(g)


def _scan_chain_stages():
    return 2 * (SCAN_BLOCK // SCAN_CHUNK)


def _interleave(gen_a, n_a, gen_b, n_b):
    done_b = 0
    for i in range(n_a):
        next(gen_a, None)
        want_b = ((i + 1) * n_b) // n_a
        while done_b < want_b:
            next(gen_b, None)
            done_b += 1
        yield
    for g in (gen_a, gen_b):
        for _ in g:
            pass


def _scan_kernel(r_ref, v_ref, kk_ref, lw_ref, km_ref, bb_ref, y_ref, g_scr, *ops_scr, rev, groups):
    @pl.when(pl.program_id(2) == 0)
    def _():
        g_scr[...] = jnp.zeros_like(g_scr)
        for ref in ops_scr:
            ref[...] = jnp.zeros_like(ref)

    slot = pl.program_id(2) % 2
    work, results = [], []
    for q in range(groups):
        lanes = slice(q * SCAN_LANES, (q + 1) * SCAN_LANES)
        prev = [ref.at[1 - slot, :, lanes] for ref in ops_scr]
        new, g_end = [], []
        prep = _scan_prepare(r_ref[:, lanes], v_ref[:, lanes], kk_ref[:, lanes], lw_ref[:, lanes], km_ref[:, lanes],
                             bb_ref[:, lanes], rev, new)
        chain = _scan_chain(g_scr[q], prev, y_ref, lanes, rev, g_end)
        work.append(_interleave(prep, _scan_prepare_stages(), chain, _scan_chain_stages()))
        results.append((lanes, new, g_end))
    while work:
        work = [w for w in work if next(w, StopIteration) is not StopIteration]
    for q, (lanes, new, g_end) in enumerate(results):
        g_scr[q] = g_end[0]
        for ref, val in zip(ops_scr, new):
            ref[slot, :, lanes] = val


def _wkv_scan(r, v, kkn, lw, km, bb, z, bsz, n_lat, n_ctx):
    t_tok, d = r.shape
    lat_blocks = n_lat // SCAN_BLOCK
    ctx_blocks = n_ctx // SCAN_BLOCK
    steps = ctx_blocks + lat_blocks
    ctx_base = bsz * lat_blocks
    rev = z == 1
    width = SCAN_LANES * SCAN_GROUPS

    def blk(b, s):
        if rev:
            cb = ctx_base + b * ctx_blocks + (ctx_blocks - 1 - s)
            lb = b * lat_blocks + (lat_blocks - 1 - (s - ctx_blocks))
        else:
            cb = ctx_base + b * ctx_blocks + s
            lb = b * lat_blocks + (s - ctx_blocks)
        return jnp.where(s < ctx_blocks, cb, lb)

    in_blk = lambda b, s: blk(b, jnp.minimum(s, steps - 1))
    out_blk = lambda b, s: blk(b, jnp.maximum(s - 1, 0))
    tok = pl.BlockSpec((SCAN_BLOCK, width), lambda b, g, s: (in_blk(b, s), g))
    dirs = pl.BlockSpec((None, SCAN_BLOCK, width), lambda b, g, s: (z, in_blk(b, s), g))
    n_chunks = SCAN_BLOCK // SCAN_CHUNK
    ops_scratch = [pltpu.VMEM((2, SCAN_BLOCK, width), BF16), pltpu.VMEM((2, SCAN_BLOCK, width), BF16),
                   pltpu.VMEM((2, SCAN_BLOCK, width), F32), pltpu.VMEM((2, SCAN_BLOCK, width), F32),
                   pltpu.VMEM((2, SCAN_BLOCK, width), BF16), pltpu.VMEM((2, SCAN_BLOCK, width), BF16),
                   pltpu.VMEM((2, SCAN_BLOCK, width), BF16), pltpu.VMEM((2, n_chunks, width), F32)]
    return pl.pallas_call(
        functools.partial(_scan_kernel, rev=rev, groups=SCAN_GROUPS),
        grid=(bsz, d // width, steps + 1),
        in_specs=[tok, tok, tok, dirs, dirs, dirs],
        out_specs=pl.BlockSpec((SCAN_BLOCK, width), lambda b, g, s: (out_blk(b, s), g)),
        out_shape=jax.ShapeDtypeStruct((t_tok, d), F32),
        scratch_shapes=[pltpu.VMEM((SCAN_GROUPS, SCAN_LANES, SCAN_LANES), F32)] + ops_scratch,
        compiler_params=pltpu.CompilerParams(dimension_semantics=("parallel", "parallel", "arbitrary"),
                                             vmem_limit_bytes=VMEM_LIMIT),
        name="wkv_scan_rev" if rev else "wkv_scan_fwd",
    )(r, v, kkn, lw, km, bb)


def _rwkv_out_kernel(x_ref, mod_ref, y0_ref, y1_ref, r_ref, v_ref, km_ref, gate_ref, lng_ref, lnb_ref, rk_ref,
                     wo_ref, e_ref, et_ref, g2_ref, wrt_ref, brt_ref, xo_ref, h2_ref, route_ref):
    x = x_ref[...]
    mods = _mod_parts(mod_ref)
    y = y0_ref[...] + y1_ref[...]
    inv_n = 1.0 / RWKV_HEAD
    mu = _seg_expand(_seg_sum(y, e_ref) * inv_n, et_ref)
    yc = y - mu
    var = _seg_sum(yc * yc, e_ref) * inv_n
    yn = (yc * _seg_expand(lax.rsqrt(var + GN_EPS), et_ref)) * lng_ref[...] + lnb_ref[...]
    v = v_ref[...]
    bonus = _seg_sum((r_ref[...] * rk_ref[...]) * (km_ref[0] + km_ref[1]), e_ref)
    yn = yn + _seg_expand(bonus, et_ref) * v
    out = _dotb(yn * gate_ref[...], wo_ref[...])
    _residual_and_route(x, out, mods, g2_ref, wrt_ref, brt_ref, xo_ref, h2_ref, route_ref)


def _rwkv_out(x, mods3, mod_spec, y0, y1, r, v, km, gate, i, j, p):
    t_tok, d = x.shape
    row = lambda a: a.reshape(a.shape[0], 1, a.shape[1])
    tok = pl.BlockSpec((TM, d), lambda t: (t, 0))
    dir_spec = pl.BlockSpec((2, TM, d), lambda t: (0, t, 0))
    args = (x, mods3, y0, y1, r, v, km, gate, row(p["rw_ln_g"]), row(p["rw_ln_b"]), row(p["rw_r_k"].reshape(-1, d)),
            p["rw_wo"], p["head_sel"], p["head_sel_t"], row(p["norm2_g"]), p["wrt"], p["brt"])
    in_specs = [tok, mod_spec, tok, tok, tok, tok, dir_spec, tok,
                _layer_spec(args[8].shape, j), _layer_spec(args[9].shape, j), _layer_spec(args[10].shape, j),
                _layer_spec(args[11].shape, j), _full_spec(args[12].shape), _full_spec(args[13].shape),
                _layer_spec(args[14].shape, i), _layer_spec(args[15].shape, i), _layer_spec(args[16].shape, i)]
    return pl.pallas_call(
        _rwkv_out_kernel,
        grid=(t_tok // TM,),
        in_specs=in_specs,
        out_specs=_mixer_out_specs(d),
        out_shape=_mixer_out_shapes(t_tok, d),
        compiler_params=pltpu.CompilerParams(dimension_semantics=("parallel",), vmem_limit_bytes=VMEM_LIMIT),
        name="rwkv_out",
    )(*args)


def _expert_kernel(be_ref, nu_ref, x_ref, wg_ref, wu_ref, wd_ref, *rest, block_lo):
    o_ref, wg_s, wu_s, wd_s = rest[-4:]
    step = pl.program_id(0)
    b = step + block_lo
    e = be_ref[b]
    prev = be_ref[jnp.maximum(b - 1, 0)]
    used = b < nu_ref[0]

    @pl.when(used & ((step == 0) | (e != prev)))
    def _():
        wg_s[...] = wg_ref[...].astype(BF16)
        wu_s[...] = wu_ref[...].astype(BF16)
        wd_s[...] = wd_ref[...].astype(BF16)

    @pl.when(used)
    def _():
        xb = x_ref[...]
        g = jnp.dot(xb, wg_s[...], preferred_element_type=F32)
        u = jnp.dot(xb, wu_s[...], preferred_element_type=F32)
        hmid = (g * jax.nn.sigmoid(g)) * u
        o_ref[...] = jnp.dot(hmid.astype(BF16), wd_s[...], preferred_element_type=F32)

    @pl.when(jnp.logical_not(used))
    def _():
        o_ref[...] = jnp.zeros_like(o_ref)


def _experts(buf, block_expert, n_used, w_gate, w_up, w_down, i, block_lo, n_blocks, earlier=None):
    rows, d = buf.shape
    f = w_gate.shape[-1]
    in_specs = [pl.BlockSpec((MOE_ROWS, d), lambda b, be, nu: (b, 0)),
                pl.BlockSpec((None, None, d, f), lambda b, be, nu: (i, be[b + block_lo], 0, 0)),
                pl.BlockSpec((None, None, d, f), lambda b, be, nu: (i, be[b + block_lo], 0, 0)),
                pl.BlockSpec((None, None, f, d), lambda b, be, nu: (i, be[b + block_lo], 0, 0))]
    args = [block_expert, n_used, buf, w_gate, w_up, w_down]
    aliases = {}
    if earlier is not None:
        in_specs.append(pl.BlockSpec(memory_space=pl.ANY))
        args.append(earlier)
        aliases = {len(args) - 1: 0}
    grid_spec = pltpu.PrefetchScalarGridSpec(
        num_scalar_prefetch=2,
        grid=(rows // MOE_ROWS,),
        in_specs=in_specs,
        out_specs=pl.BlockSpec((MOE_ROWS, d), lambda b, be, nu: (b + block_lo, 0)),
        scratch_shapes=[pltpu.VMEM((d, f), BF16), pltpu.VMEM((d, f), BF16), pltpu.VMEM((f, d), BF16)],
    )
    return pl.pallas_call(
        functools.partial(_expert_kernel, block_lo=block_lo),
        grid_spec=grid_spec,
        out_shape=jax.ShapeDtypeStruct((n_blocks * MOE_ROWS, d), F32),
        input_output_aliases=aliases,
        compiler_params=pltpu.CompilerParams(dimension_semantics=("arbitrary",), vmem_limit_bytes=VMEM_LIMIT),
        name="experts",
    )(*args)


def _combine_kernel(x_ref, mod_ref, route_ref, y0_ref, y1_ref, fg_ref, xo_ref, *, final):
    gt2 = _mod_parts(mod_ref)[5]
    route = route_ref[...]
    xn = x_ref[...] + gt2 * (route[:, 2:3] * y0_ref[...] + route[:, 3:4] * y1_ref[...])
    if final:
        xn = (xn * lax.rsqrt(jnp.mean(xn * xn, axis=-1, keepdims=True) + NORM_EPS)) * fg_ref[...]
    xo_ref[...] = xn


def _combine(x, mods3, mod_spec, route, y0, y1, final_g, final, t_out):
    d = x.shape[1]
    tok = pl.BlockSpec((TM, d), lambda t: (t, 0))
    return pl.pallas_call(
        functools.partial(_combine_kernel, final=final),
        grid=(t_out // TM,),
        in_specs=[tok, mod_spec, pl.BlockSpec((TM, LANES), lambda t: (t, 0)), tok, tok, _full_spec((1, d))],
        out_specs=tok,
        out_shape=jax.ShapeDtypeStruct((t_out, d), F32),
        compiler_params=pltpu.CompilerParams(dimension_semantics=("parallel",), vmem_limit_bytes=VMEM_LIMIT),
        name="moe_combine",
    )(x, mods3, route, y0, y1, final_g.reshape(1, d))


def _dest_kernel(route_ref, d_ref, counts_ref, base_scr, start_scr):
    phase = pl.program_id(0)
    t = pl.program_id(1)
    route = route_ref[...]
    tm = route.shape[0]
    lane = lax.broadcasted_iota(jnp.int32, route.shape, 1)
    lane_f = lane.astype(F32)
    onehots = [jnp.where(lane_f == route[:, k:k + 1], 1.0, 0.0) for k in range(TOP_K)]

    @pl.when((phase == 0) & (t == 0))
    def _():
        base_scr[...] = jnp.zeros_like(base_scr)

    @pl.when((phase == 1) & (t == 0))
    def _():
        counts = base_scr[...]
        counts_ref[...] = counts
        padded = jnp.floor((counts + (MOE_ROWS - 1)) * (1.0 / MOE_ROWS)) * MOE_ROWS
        before = (lax.broadcasted_iota(jnp.int32, (LANES, LANES), 0)
                  < lax.broadcasted_iota(jnp.int32, (LANES, LANES), 1))
        start_scr[...] = _dot_x_sel(padded, jnp.where(before, 1.0, 0.0).astype(BF16))
        base_scr[...] = jnp.zeros_like(base_scr)

    @pl.when(phase == 0)
    def _():
        base_scr[...] = base_scr[...] + sum(jnp.sum(oh, axis=0, keepdims=True) for oh in onehots)

    @pl.when(phase == 1)
    def _():
        earlier = (lax.broadcasted_iota(jnp.int32, (tm, tm), 1)
                   < lax.broadcasted_iota(jnp.int32, (tm, tm), 0))
        earlier_b = jnp.where(earlier, 1.0, 0.0).astype(BF16)
        diag = (lax.broadcasted_iota(jnp.int32, (LANES, LANES), 0)
                == lax.broadcasted_iota(jnp.int32, (LANES, LANES), 1))
        base = base_scr[...] + start_scr[...]
        for k, oh in enumerate(onehots):
            before = jnp.dot(earlier_b, oh.astype(BF16), preferred_element_type=F32)
            dest = jnp.sum(oh * (base + before), axis=1, keepdims=True)
            rows = [jnp.sum(jnp.where(diag, jnp.broadcast_to(dest[i * LANES:(i + 1) * LANES], (LANES, LANES)), 0.0),
                            axis=0, keepdims=True) for i in range(tm // LANES)]
            d_ref[k, 0] = jnp.concatenate(rows, axis=0).astype(jnp.int32)
            base = base + jnp.sum(oh, axis=0, keepdims=True)
        base_scr[...] = base - start_scr[...]


def _dests(route):
    t_tok = route.shape[0]
    assert t_tok % DEST_TM == 0
    tiles = t_tok // DEST_TM
    return pl.pallas_call(
        _dest_kernel,
        grid=(2, tiles),
        in_specs=[pl.BlockSpec((DEST_TM, LANES), lambda ph, t: (t, 0))],
        out_specs=(pl.BlockSpec((TOP_K, 1, DEST_TM // LANES, LANES), lambda ph, t: (0, t * ph, 0, 0)),
                   pl.BlockSpec((1, LANES), lambda ph, t: (0, 0))),
        out_shape=(jax.ShapeDtypeStruct((TOP_K, tiles, DEST_TM // LANES, LANES), jnp.int32),
                   jax.ShapeDtypeStruct((1, LANES), F32)),
        scratch_shapes=[pltpu.VMEM((1, LANES), F32), pltpu.VMEM((1, LANES), F32)],
        compiler_params=pltpu.CompilerParams(dimension_semantics=("arbitrary", "arbitrary"),
                                             vmem_limit_bytes=VMEM_LIMIT),
        name="moe_dests",
    )(route)


def _moe(x, h2, route, mods3, mod_spec, i, p, final, t_out):
    t_tok, d = x.shape
    n_assign = t_tok * TOP_K
    dest4, counts_f = _dests(route)
    dest = dest4.reshape(TOP_K, t_tok)
    counts = counts_f[0, :N_EXPERTS].astype(jnp.int32)
    pad_end = jnp.cumsum(((counts + MOE_ROWS - 1) // MOE_ROWS) * MOE_ROWS)
    n_blocks = (n_assign + MOE_ROWS - 1) // MOE_ROWS + N_EXPERTS
    token = jnp.arange(t_tok, dtype=jnp.int32)
    slot_token = (jnp.arange(n_blocks * MOE_ROWS, dtype=jnp.int32) % t_tok).at[dest.reshape(-1)].set(
        jnp.tile(token, TOP_K), mode="promise_in_bounds", unique_indices=True)
    block_start = jnp.arange(n_blocks, dtype=jnp.int32) * MOE_ROWS
    block_expert = jnp.minimum(jnp.sum((pad_end[None, :] <= block_start[:, None]).astype(jnp.int32), axis=1),
                               N_EXPERTS - 1)
    n_used = (pad_end[-1:] // MOE_ROWS).astype(jnp.int32)
    out, lo = None, 0
    for hi in (n_blocks // 2, n_blocks):
        buf = h2.at[slot_token[lo * MOE_ROWS:hi * MOE_ROWS]].get(mode="promise_in_bounds")
        out = _experts(buf, block_expert, n_used, p["moe_w_gate"], p["moe_w_up"], p["moe_w_down"], i, lo, n_blocks,
                       earlier=out)
        lo = hi
    y0 = out.at[dest[0]].get(mode="promise_in_bounds", unique_indices=True)
    y1 = out.at[dest[1]].get(mode="promise_in_bounds", unique_indices=True)
    return _combine(x, mods3, mod_spec, route, y0, y1, p["final_g"], final, t_out)


def kernel(x, c, ctx, c_ctx, ada_w, ada_b, norm1_g, norm2_g, final_g, ga_w_in, ga_b_in, ga_ln_g, ga_ln_b, ga_w_s,
           ga_b_s, ga_w_out, ga_b_out, rw_mu, rw_wr, rw_wk, rw_wv, rw_wo, rw_w0, rw_w1, rw_w2, rw_a0, rw_a1, rw_a2,
           rw_g1, rw_g2, rw_k_k, rw_k_a, rw_r_k, rw_ln_g, rw_ln_b, moe_w_grp, moe_b_grp, moe_w_exp, moe_b_exp,
           moe_w_gate, moe_w_up, moe_w_down):
    bsz, n_lat, d = x.shape
    n_ctx = ctx.shape[1]
    depth = ada_w.shape[0]
    n_mixers = 2
    assert n_lat % TM == 0 and n_ctx == TM and TM % GRID_W == 0
    assert n_lat % GMLP_TM == 0 and (bsz * n_ctx) % GMLP_TM == 0 and GMLP_TM % (GMLP_SPLIT * CHUNK) == 0
    assert n_lat % SCAN_BLOCK == 0 and n_ctx % SCAN_BLOCK == 0
    assert bsz + 1 <= 8 and d % (SCAN_LANES * SCAN_GROUPS) == 0
    assert MOE_ROWS & (MOE_ROWS - 1) == 0
    t_lat = bsz * n_lat
    lat_tiles = t_lat // TM
    tiles_per_batch = n_lat // TM
    heads = d // RWKV_HEAD

    s_rows = jnp.concatenate([c, c_ctx[None, :], jnp.zeros((8 - bsz - 1, d), F32)], axis=0)
    mods = _mods(s_rows, ada_w, ada_b)
    mods3 = mods.reshape(depth * 8, 1, N_MOD * d)

    def mod_spec_for(i, tm=TM):
        def imap(t):
            return (i * 8 + jnp.where(t < t_lat // tm, t // (n_lat // tm), bsz), 0, 0)
        return pl.BlockSpec((1, 1, N_MOD * d), imap)

    head_sel = (jnp.arange(d)[:, None] // RWKV_HEAD == jnp.arange(LANES)[None, :]).astype(BF16)
    zpad = lambda a, axis, n: jnp.pad(a, [(0, n - a.shape[k]) if k == axis else (0, 0) for k in range(a.ndim)])
    glora = ((rw_g1.shape[-1] + LANES - 1) // LANES) * LANES
    wrt = jnp.concatenate([moe_w_grp, moe_w_exp], axis=-1)
    brt = jnp.concatenate([moe_b_grp, moe_b_exp], axis=-1)
    p = dict(
        norm1_g=norm1_g, norm2_g=norm2_g, final_g=final_g,
        ga_w_in=ga_w_in.astype(BF16), ga_b_in=ga_b_in, ga_ln_g=ga_ln_g, ga_ln_b=ga_ln_b,
        ga_w_s=ga_w_s.astype(BF16), ga_b_s=ga_b_s[..., None], ga_w_out=ga_w_out.astype(BF16), ga_b_out=ga_b_out,
        rw_mu8=zpad(rw_mu, 1, 8),
        rw_wr=rw_wr.astype(BF16), rw_wk=rw_wk.astype(BF16), rw_wv=rw_wv.astype(BF16), rw_wo=rw_wo.astype(BF16),
        rw_w1c=jnp.concatenate([rw_w1[:, 0], rw_w1[:, 1]], axis=-1).astype(BF16),
        rw_w2c=jnp.concatenate([rw_w2[:, 0], rw_w2[:, 1]], axis=-2).astype(BF16),
        rw_a1c=jnp.concatenate([rw_a1[:, 0], rw_a1[:, 1]], axis=-1).astype(BF16),
        rw_a2c=jnp.concatenate([rw_a2[:, 0], rw_a2[:, 1]], axis=-2).astype(BF16),
        rw_w0=rw_w0, rw_a0=rw_a0,
        rw_g1p=zpad(rw_g1, 2, glora).astype(BF16), rw_g2p=zpad(rw_g2, 1, glora).astype(BF16),
        rw_k_k=rw_k_k, rw_k_a=rw_k_a, rw_r_k=rw_r_k, rw_ln_g=rw_ln_g, rw_ln_b=rw_ln_b,
        head_sel=head_sel, head_sel_t=head_sel.T,
        wrt=zpad(wrt, 2, LANES).astype(BF16), brt=zpad(brt, 1, LANES).reshape(depth, 1, LANES),
        moe_w_gate=moe_w_gate, moe_w_up=moe_w_up, moe_w_down=moe_w_down,
    )
    assert heads <= LANES and 2 * rw_w1.shape[-1] == LANES and 2 * rw_a1.shape[-1] == LANES

    xs = jnp.concatenate([x.reshape(t_lat, d), ctx.reshape(bsz * n_ctx, d)], axis=0)
    for i in range(depth):
        j = i // n_mixers
        mod_spec = mod_spec_for(i)
        if i % n_mixers == 0:
            xs, h2, route = _gmlp_layer(xs, mods3, mod_spec_for(i, GMLP_TM), i, j, p)
        else:
            r, v, kkn, gate, lw, km, bb = _rwkv_prep(xs, mods3, mod_spec, i, j, p, lat_tiles, tiles_per_batch)
            y0 = _wkv_scan(r, v, kkn, lw, km, bb, 0, bsz, n_lat, n_ctx)
            y1 = _wkv_scan(r, v, kkn, lw, km, bb, 1, bsz, n_lat, n_ctx)
            xs, h2, route = _rwkv_out(xs, mods3, mod_spec, y0, y1, r, v, km, gate, i, j, p)
        last = i == depth - 1
        xs = _moe(xs, h2, route, mods3, mod_spec, i, p, final=last, t_out=t_lat if last else xs.shape[0])
    return xs.reshape(bsz, n_lat, d)
```

```python
import functools
import math

import jax
import jax.numpy as jnp
from jax import lax
from jax.experimental import pallas as pl
from jax.experimental.pallas import tpu as pltpu

F32 = jnp.float32
BF16 = jnp.bfloat16

N_MOD = 6
NORM_EPS = 1e-6
GRID_W = 64
CHUNK = 128
GMLP_GROUP_CH = 128
LN_EPS = 1e-5
RWKV_HEAD = 64
GN_EPS = 64e-5
N_GROUPS = 4
EXPERTS_PER_GROUP = 8
N_EXPERTS = N_GROUPS * EXPERTS_PER_GROUP
TOP_K = 2

LANES = 128
TM = 256
GMLP_TM = 512
GMLP_SPLIT = 2
MOE_ROWS = 512
MOE_RANGES = 4
DEST_TM = 512
SCAN_BLOCK = 256
SCAN_CHUNK = 32
SCAN_LANES = 256
SCAN_GROUPS = 2
VMEM_LIMIT = 56 * 1024 * 1024
NEG_BIG = -3.0e38
SQRT_HALF = 0.7071067811865476
EXP_MINUS_HALF = 0.6065306597126334


def _dotb(a, b):
    return jnp.dot(a.astype(BF16), b.astype(BF16), preferred_element_type=F32)


def _dot_nt(a, b):
    return lax.dot_general(a.astype(BF16), b.astype(BF16), (((1,), (1,)), ((), ())), preferred_element_type=F32)


def _dot_tn(a, b):
    return lax.dot_general(a.astype(BF16), b.astype(BF16), (((0,), (0,)), ((), ())), preferred_element_type=F32)


def _split2(x):
    hi = x.astype(BF16)
    lo = (x - hi.astype(F32)).astype(BF16)
    return hi, lo


def _split3(x):
    hi = x.astype(BF16)
    r1 = x - hi.astype(F32)
    mid = r1.astype(BF16)
    lo = (r1 - mid.astype(F32)).astype(BF16)
    return hi, mid, lo


def _dot_x_sel(x, sel_bf16, terms=3):
    d = lambda p: jnp.dot(p, sel_bf16, preferred_element_type=F32)
    return sum(d(p) for p in (_split3(x) if terms == 3 else _split2(x)))


def _norm_mod(x, g, sc, sh):
    y = x * lax.rsqrt(jnp.mean(x * x, axis=-1, keepdims=True) + NORM_EPS)
    return (y * g) * (1.0 + sc) + sh


def _mod_parts(mod_ref):
    m = mod_ref[0]
    d = m.shape[-1] // N_MOD
    return [m[:, i * d:(i + 1) * d] for i in range(N_MOD)]


def _mods_kernel(s_ref, w_ref, b_ref, o_ref):
    s = s_ref[...]
    s = s * jax.nn.sigmoid(s)
    o_ref[0] = _dotb(s, w_ref[0]) + b_ref[0]


def _mods(s_rows, ada_w, ada_b):
    depth, d, nd = ada_w.shape
    tn = nd // 4
    return pl.pallas_call(
        _mods_kernel,
        grid=(depth, nd // tn),
        in_specs=[pl.BlockSpec(s_rows.shape, lambda i, j: (0, 0)),
                  pl.BlockSpec((1, d, tn), lambda i, j: (i, 0, j)),
                  pl.BlockSpec((1, 1, tn), lambda i, j: (i, 0, j))],
        out_specs=pl.BlockSpec((1, s_rows.shape[0], tn), lambda i, j: (i, 0, j)),
        out_shape=jax.ShapeDtypeStruct((depth, s_rows.shape[0], nd), F32),
        compiler_params=pltpu.CompilerParams(dimension_semantics=("parallel", "parallel"),
                                             vmem_limit_bytes=VMEM_LIMIT),
        name="mods",
    )(s_rows, ada_w, ada_b.reshape(depth, 1, nd))


def _route(logits):
    lane = lax.broadcasted_iota(jnp.int32, logits.shape, 1)
    lane_f = lane.astype(F32)
    big = jnp.float32(1e9)
    is_g = lane < N_GROUPS
    gl = jnp.where(is_g, logits, NEG_BIG)
    gm = jnp.max(gl, axis=1, keepdims=True)
    grp = jnp.min(jnp.where(is_g & (gl == gm), lane_f, big), axis=1, keepdims=True)
    p_grp = 1.0 / jnp.sum(jnp.where(is_g, jnp.exp(gl - gm), 0.0), axis=1, keepdims=True)
    lo = N_GROUPS + grp * EXPERTS_PER_GROUP
    in_grp = (lane_f >= lo) & (lane_f < lo + EXPERTS_PER_GROUP)
    el = jnp.where(in_grp, logits, NEG_BIG)
    m1 = jnp.max(el, axis=1, keepdims=True)
    i1 = jnp.min(jnp.where(in_grp & (el == m1), lane_f, big), axis=1, keepdims=True)
    rest = in_grp & (lane_f != i1)
    el2 = jnp.where(rest, logits, NEG_BIG)
    m2 = jnp.max(el2, axis=1, keepdims=True)
    i2 = jnp.min(jnp.where(rest & (el2 == m2), lane_f, big), axis=1, keepdims=True)
    e21 = jnp.exp(m2 - m1)
    s0 = 1.0 / (1.0 + e21)
    w0 = p_grp * s0
    w1 = p_grp * (e21 * s0)
    out = jnp.where(lane == 0, i1 - N_GROUPS,
                    jnp.where(lane == 1, i2 - N_GROUPS,
                              jnp.where(lane == 2, w0, jnp.where(lane == 3, w1, 0.0))))
    return out


def _residual_and_route(x, y, mods, g2_ref, wrt_ref, brt_ref, xo_ref, h2_ref, route_ref, rows=slice(None)):
    _, _, gt1, sh2, sc2, _ = mods
    xn = x + gt1 * y
    xo_ref[rows, :] = xn
    h2 = _norm_mod(xn, g2_ref[...], sc2, sh2).astype(BF16)
    h2_ref[rows, :] = h2
    logits = jnp.dot(h2, wrt_ref[...], preferred_element_type=F32) + brt_ref[...]
    route_ref[rows, :] = _route(logits)


def _round_robin(gens):
    while gens:
        gens = [g for g in gens if next(g, StopIteration) is not StopIteration]


def _gmlp_rows(rows, x_ref, mod_ref, g1_ref, g2_ref, win_ref, bin_ref, lng_ref, lnb_ref, ws_ref, bs_ref,
               wout_ref, bout_ref, wrt_ref, brt_ref, xo_ref, h2_ref, route_ref, gated_ref):
    x = x_ref[rows, :]
    mods = _mod_parts(mod_ref)
    sh1, sc1 = mods[0], mods[1]
    h = _norm_mod(x, g1_ref[...], sc1, sh1).astype(BF16)
    yield
    width = win_ref.shape[1] // 2
    gelu = lambda z: 0.5 * z * (1.0 + lax.erf(z * SQRT_HALF))
    u = gelu(jnp.dot(h, win_ref[:, :width], preferred_element_type=F32) + bin_ref[:, :width])
    yield
    v = gelu(jnp.dot(h, win_ref[:, width:], preferred_element_type=F32) + bin_ref[:, width:])
    yield
    mu = jnp.mean(v, axis=-1, keepdims=True)
    vc = v - mu
    var = jnp.mean(vc * vc, axis=-1, keepdims=True)
    vn = ((vc * lax.rsqrt(var + LN_EPS)) * lng_ref[...] + lnb_ref[...]).astype(BF16)
    yield
    n_chunks = x.shape[0] // CHUNK
    for g in range(width // GMLP_GROUP_CH):
        cs = slice(g * GMLP_GROUP_CH, (g + 1) * GMLP_GROUP_CH)
        rhs = jnp.concatenate([vn[c * CHUNK:(c + 1) * CHUNK, cs] for c in range(n_chunks)], axis=1)
        s = jnp.dot(ws_ref[g], rhs, preferred_element_type=F32) + bs_ref[g]
        for c in range(n_chunks):
            rsl = slice(c * CHUNK, (c + 1) * CHUNK)
            dst = slice(rows.start + c * CHUNK, rows.start + (c + 1) * CHUNK)
            gated_ref[dst, cs] = (u[rsl, cs] * s[:, c * GMLP_GROUP_CH:(c + 1) * GMLP_GROUP_CH]).astype(BF16)
        if g % 4 == 3:
            yield
    y = jnp.dot(gated_ref[rows, :], wout_ref[...], preferred_element_type=F32) + bout_ref[...]
    yield
    _residual_and_route(x, y, mods, g2_ref, wrt_ref, brt_ref, xo_ref, h2_ref, route_ref, rows)
    yield


def _gmlp_kernel(*refs):
    group = refs[0].shape[0] // GMLP_SPLIT
    _round_robin([_gmlp_rows(slice(q * group, (q + 1) * group), *refs) for q in range(GMLP_SPLIT)])


def _full_spec(shape):
    nd = len(shape)
    return pl.BlockSpec(shape, lambda *_: (0,) * nd)


def _layer_spec(shape, j, single_buffer=False):
    nd = len(shape)
    kwargs = dict(pipeline_mode=pl.Buffered(1)) if single_buffer else {}
    return pl.BlockSpec((None,) + tuple(shape[1:]), lambda *_: (j,) + (0,) * (nd - 1), **kwargs)


def _mixer_out_shapes(t_tok, d):
    return (jax.ShapeDtypeStruct((t_tok, d), F32), jax.ShapeDtypeStruct((t_tok, d), BF16),
            jax.ShapeDtypeStruct((t_tok, LANES), F32))


def _mixer_out_specs(d, tm=TM):
    return (pl.BlockSpec((tm, d), lambda t: (t, 0)), pl.BlockSpec((tm, d), lambda t: (t, 0)),
            pl.BlockSpec((tm, LANES), lambda t: (t, 0)))


def _gmlp_layer(x, mods3, mod_spec, i, j, p):
    t_tok, d = x.shape
    width = p["ga_w_out"].shape[1]
    row = lambda a: a.reshape(a.shape[0], 1, a.shape[1])
    args = (x, mods3, row(p["norm1_g"]), row(p["norm2_g"]), p["ga_w_in"], row(p["ga_b_in"]), row(p["ga_ln_g"]),
            row(p["ga_ln_b"]), p["ga_w_s"], p["ga_b_s"], p["ga_w_out"], row(p["ga_b_out"]), p["wrt"], p["brt"])
    in_specs = [pl.BlockSpec((GMLP_TM, d), lambda t: (t, 0)), mod_spec,
                _layer_spec(args[2].shape, i), _layer_spec(args[3].shape, i),
                _layer_spec(args[4].shape, j, True), _layer_spec(args[5].shape, j), _layer_spec(args[6].shape, j),
                _layer_spec(args[7].shape, j), _layer_spec(args[8].shape, j, True), _layer_spec(args[9].shape, j),
                _layer_spec(args[10].shape, j, True), _layer_spec(args[11].shape, j),
                _layer_spec(args[12].shape, i), _layer_spec(args[13].shape, i)]
    return pl.pallas_call(
        _gmlp_kernel,
        grid=(t_tok // GMLP_TM,),
        in_specs=in_specs,
        out_specs=_mixer_out_specs(d, GMLP_TM),
        out_shape=_mixer_out_shapes(t_tok, d),
        scratch_shapes=[pltpu.VMEM((GMLP_TM, width), BF16)],
        compiler_params=pltpu.CompilerParams(dimension_semantics=("parallel",), vmem_limit_bytes=VMEM_LIMIT),
        name="gmlp_layer",
    )(*args)


def _seg_sum(q, e_ref):
    return _dot_x_sel(q, e_ref[...], terms=2)


def _seg_expand(s, et_ref):
    return _dot_x_sel(s, et_ref[...], terms=2)


def _shifted(h, h_above, h_below, is_ctx, first_row_tile, last_row_tile):
    tm, d = h.shape
    q = d // 4
    row = lax.broadcasted_iota(jnp.int32, (tm, 1), 0)
    ctx_i = is_ctx.astype(jnp.int32)
    col = row & (jnp.where(is_ctx, tm, GRID_W) - 1)
    last_col = jnp.where(is_ctx, tm, GRID_W) - 1
    prev1 = jnp.where(col == 0, 0.0, pltpu.roll(h, 1, axis=0))
    next1 = jnp.where(col == last_col, 0.0, pltpu.roll(h, tm - 1, axis=0))
    up = jnp.concatenate([h_above, h[:tm - GRID_W]], axis=0)
    up = jnp.where(row < jnp.where(first_row_tile, GRID_W, 0), 0.0, up)
    down = jnp.concatenate([h[GRID_W:], h_below], axis=0)
    down = jnp.where(row >= jnp.where(last_row_tile, tm - GRID_W, tm), 0.0, down)
    lane_q = lax.broadcasted_iota(jnp.int32, (1, d), 1) >> int(math.log2(q))
    src = lane_q * (1 - ctx_i) + (lane_q >> 1) * ctx_i
    return jnp.where(src == 0, prev1, jnp.where(src == 1, next1, jnp.where(src == 2, up, down)))


def _rwkv_prep_kernel(x_ref, xa_ref, xb_ref, mod_ref, n1_ref, mu_ref, wr_ref, wk_ref, wv_ref, w1_ref, w2_ref, w0_ref,
                      a1_ref, a2_ref, a0_ref, g1_ref, g2_ref, kk_ref, ka_ref, e_ref, et_ref,
                      r_o, v_o, kkn_o, gate_o, lw_o, km_o, bb_o, *, lat_tiles, tiles_per_batch):
    t = pl.program_id(0)
    mods = _mod_parts(mod_ref)
    norm = lambda x: _norm_mod(x, n1_ref[...], mods[1], mods[0])
    h = norm(x_ref[...])
    tb = t % tiles_per_batch
    hs = _shifted(h, norm(xa_ref[...]), norm(xb_ref[...]), t >= lat_tiles, tb == 0, tb == tiles_per_batch - 1)
    xx = hs - h
    mu = mu_ref[...]
    xr, xw, xk, xv, xa, xg = [h + xx * mu[q:q + 1] for q in range(6)]
    r = _dotb(xr, wr_ref[...])
    k = _dotb(xk, wk_ref[...])
    v = _dotb(xv, wv_ref[...])
    r_o[...] = r
    v_o[...] = v
    kkr = k * kk_ref[...]
    rs = lax.rsqrt(_seg_sum(kkr * kkr, e_ref) + 1e-12)
    kkn = kkr * _seg_expand(rs, et_ref)
    kkn_o[...] = kkn
    gate_o[...] = _dotb(jax.nn.sigmoid(_dotb(xg, g1_ref[...])), g2_ref[...])
    tw = jnp.tanh(_dotb(xw, w1_ref[...]))
    ta = _dotb(xa, a1_ref[...])
    lane = lax.broadcasted_iota(jnp.int32, tw.shape, 1)
    half = tw.shape[1] // 2
    w0 = w0_ref[...]
    a0 = a0_ref[...]
    ka = ka_ref[...]
    for z in range(2):
        zm = (lane >= z * half) & (lane < (z + 1) * half)
        lwz = _dotb(jnp.where(zm, tw, 0.0), w2_ref[...])
        lw_o[z] = -EXP_MINUS_HALF * jax.nn.sigmoid(w0[z:z + 1] + lwz)
        la = _dotb(jnp.where(zm, ta, 0.0), a2_ref[...])
        a = jax.nn.sigmoid(a0[z:z + 1] + la)
        km_o[z] = k * (1.0 + (a - 1.0) * ka)
        bb_o[z] = kkn * a


def _rwkv_prep(x, mods3, mod_spec, i, j, p, lat_tiles, tiles_per_batch):
    t_tok, d = x.shape
    row = lambda a: a.reshape(a.shape[0], 1, a.shape[1])
    args = (x, x, x, mods3, row(p["norm1_g"]), p["rw_mu8"], p["rw_wr"], p["rw_wk"], p["rw_wv"], p["rw_w1c"],
            p["rw_w2c"], p["rw_w0"], p["rw_a1c"], p["rw_a2c"], p["rw_a0"], p["rw_g1p"], p["rw_g2p"],
            row(p["rw_k_k"]), row(p["rw_k_a"]), p["head_sel"], p["head_sel_t"])
    tok = pl.BlockSpec((TM, d), lambda t: (t, 0))
    rows_per_tile = TM // GRID_W
    last_row = t_tok // GRID_W - 1
    above = pl.BlockSpec((GRID_W, d), lambda t: (jnp.maximum(t * rows_per_tile - 1, 0), 0))
    below = pl.BlockSpec((GRID_W, d), lambda t: (jnp.minimum((t + 1) * rows_per_tile, last_row), 0))
    in_specs = ([tok, above, below, mod_spec, _layer_spec(args[4].shape, i)]
                + [_layer_spec(a.shape, j) for a in args[5:19]]
                + [_full_spec(args[19].shape), _full_spec(args[20].shape)])
    dir_spec = pl.BlockSpec((2, TM, d), lambda t: (0, t, 0))
    tok_shape = jax.ShapeDtypeStruct((t_tok, d), F32)
    dir_shape = jax.ShapeDtypeStruct((2, t_tok, d), F32)
    return pl.pallas_call(
        functools.partial(_rwkv_prep_kernel, lat_tiles=lat_tiles, tiles_per_batch=tiles_per_batch),
        grid=(t_tok // TM,),
        in_specs=in_specs,
        out_specs=(tok, tok, tok, tok, dir_spec, dir_spec, dir_spec),
        out_shape=(tok_shape, tok_shape, tok_shape, tok_shape, dir_shape, dir_shape, dir_shape),
        compiler_params=pltpu.CompilerParams(dimension_semantics=("parallel",), vmem_limit_bytes=VMEM_LIMIT),
        name="rwkv_prep",
    )(*args)


def _scan_prepare(r, v, kk, lw, km, bb, rev, out):
    c = SCAN_CHUNK
    nb = SCAN_BLOCK
    a = -kk
    row = lax.broadcasted_iota(jnp.int32, (nb, nb), 0)
    col = lax.broadcasted_iota(jnp.int32, (nb, nb), 1)
    shift_c = int(math.log2(c))
    same = (row >> shift_c) == (col >> shift_c)
    if rev:
        incl = same & (col >= row)
        strict = same & (col > row)
    else:
        incl = same & (col <= row)
        strict = same & (col < row)
    incl_b = jnp.where(incl, 1.0, 0.0).astype(BF16)
    eye = jnp.where(row == col, 1.0, 0.0)
    head_of_lane = col >> int(math.log2(RWKV_HEAD))

    lw_hi, lw_lo = _split2(lw)
    cum = (jnp.dot(incl_b, lw_hi, preferred_element_type=F32)
           + jnp.dot(incl_b, lw_lo, preferred_element_type=F32))
    yield
    last = [ci * c if rev else ci * c + c - 1 for ci in range(nb // c)]
    tot_rows = [cum[t:t + 1] for t in last]
    tot = jnp.concatenate([jnp.broadcast_to(t, (c, t.shape[1])) for t in tot_rows], axis=0)
    yield
    at = a * jnp.exp(cum - lw)
    rt = r * jnp.exp(cum)
    e_inv = jnp.exp(-cum)
    bt = bb * e_inv
    kt = km * e_inv
    e_rem = jnp.exp(tot - cum)
    bh = (bb * e_rem).astype(BF16)
    kh = (km * e_rem).astype(BF16)
    et = jnp.exp(jnp.concatenate(tot_rows, axis=0))
    bk = jnp.concatenate([bt, kt], axis=0).astype(BF16)
    yield
    n_heads = SCAN_LANES // RWKV_HEAD
    heads = range(n_heads)
    hms = [head_of_lane == hh for hh in heads]
    at_h = [jnp.where(hm, at, 0.0) for hm in hms]
    rt_h = [jnp.where(hm, rt, 0.0) for hm in hms]
    v_h = [jnp.where(hm, v, 0.0).astype(BF16) for hm in hms]
    v_roll = pltpu.roll(v, RWKV_HEAD, axis=1)
    v_s = [jnp.where(hms[(hh + 1) % n_heads], v_roll, 0.0).astype(BF16) for hh in heads]
    a_ab, a_ak, m_rb, m_rk = [], [], [], []
    for hh in heads:
        s = _dot_nt(jnp.concatenate([at_h[hh], rt_h[hh]], axis=0), bk)
        a_ab.append(jnp.where(strict, s[:nb, :nb], 0.0))
        a_ak.append(jnp.where(strict, s[:nb, nb:], 0.0).astype(BF16))
        m_rb.append(jnp.where(incl, s[nb:, :nb], 0.0).astype(BF16))
        m_rk.append(jnp.where(incl, s[nb:, nb:], 0.0).astype(BF16))
        yield
    pw = list(a_ab)
    tinv = [eye + m for m in a_ab]
    for _ in range(shift_c - 1):
        for hh in heads:
            pw[hh] = _dotb(pw[hh], pw[hh])
        yield
        for hh in heads:
            tinv[hh] = tinv[hh] + _dotb(tinv[hh], pw[hh])
        yield
    x1 = [jnp.dot(a_ak[hh], v_s[hh], preferred_element_type=F32) for hh in heads]
    yield
    tc = [_dotb(tinv[hh], at_h[hh] + x1[hh]) for hh in heads]
    yield
    mc = [_dotb(m_rb[hh], tc[hh]) for hh in heads]
    yield

    def pick(parts, shift):
        res = parts[(n_heads - 1 - shift) % n_heads]
        for lb in range(n_heads - 2, -1, -1):
            res = jnp.where(hms[lb], parts[(lb - shift) % n_heads], res)
        return res

    back = SCAN_LANES - RWKV_HEAD
    ah = pick(tc, 0)
    vh = pltpu.roll(pick(tc, 1), back, axis=1)
    rh = rt + pick(mc, 0)
    yh = (jnp.dot(jnp.concatenate(m_rk, axis=1), jnp.concatenate(v_h, axis=0), preferred_element_type=F32)
          + pltpu.roll(pick(mc, 1), back, axis=1))
    yield
    out.extend([ah.astype(BF16), rh.astype(BF16), vh, yh, bh, kh, v.astype(BF16), et])


def _scan_prepare_stages():
    return 7 + (SCAN_LANES // RWKV_HEAD) + 2 * (int(math.log2(SCAN_CHUNK)) - 1)


def _scan_chain(g, ops, y_ref, lanes, rev, out):
    c = SCAN_CHUNK
    ah, rh, vh, yh, bh, kh, vb, et = ops
    n = g.shape[0]
    shift_h = int(math.log2(RWKV_HEAD))
    blockdiag = ((lax.broadcasted_iota(jnp.int32, (n, n), 0) >> shift_h)
                 == (lax.broadcasted_iota(jnp.int32, (n, n), 1) >> shift_h))
    n_chunks = SCAN_BLOCK // c
    order = range(n_chunks - 1, -1, -1) if rev else range(n_chunks)
    for ci in order:
        rs = slice(ci * c, (ci + 1) * c)
        p = _dot_nt(jnp.concatenate([ah[rs], rh[rs]], axis=0), g)
        u = p[:c] + vh[rs]
        y_ref[rs, lanes] = p[c:] + yh[rs]
        yield
        dg = _dot_tn(jnp.concatenate([u.astype(BF16), vb[rs]], axis=0), jnp.concatenate([bh[rs], kh[rs]], axis=0))
        g = jnp.where(blockdiag, g * et[ci:ci + 1, :] + dg, 0.0)
        yield
    out.append(g)


def _scan_chain_stages():
    return 2 * (SCAN_BLOCK // SCAN_CHUNK)


def _interleave(gen_a, n_a, gen_b, n_b):
    done_b = 0
    for i in range(n_a):
        next(gen_a, None)
        want_b = ((i + 1) * n_b) // n_a
        while done_b < want_b:
            next(gen_b, None)
            done_b += 1
        yield
    for g in (gen_a, gen_b):
        for _ in g:
            pass


def _scan_kernel(r_ref, v_ref, kk_ref, lw_ref, km_ref, bb_ref, y_ref, g_scr, *ops_scr, rev, groups):
    @pl.when(pl.program_id(2) == 0)
    def _():
        g_scr[...] = jnp.zeros_like(g_scr)
        for ref in ops_scr:
            ref[...] = jnp.zeros_like(ref)

    slot = pl.program_id(2) % 2
    work, results = [], []
    for q in range(groups):
        lanes = slice(q * SCAN_LANES, (q + 1) * SCAN_LANES)
        prev = [ref.at[1 - slot, :, lanes] for ref in ops_scr]
        new, g_end = [], []
        prep = _scan_prepare(r_ref[:, lanes], v_ref[:, lanes], kk_ref[:, lanes], lw_ref[:, lanes], km_ref[:, lanes],
                             bb_ref[:, lanes], rev, new)
        chain = _scan_chain(g_scr[q], prev, y_ref, lanes, rev, g_end)
        work.append(_interleave(prep, _scan_prepare_stages(), chain, _scan_chain_stages()))
        results.append((lanes, new, g_end))
    while work:
        work = [w for w in work if next(w, StopIteration) is not StopIteration]
    for q, (lanes, new, g_end) in enumerate(results):
        g_scr[q] = g_end[0]
        for ref, val in zip(ops_scr, new):
            ref[slot, :, lanes] = val


def _wkv_scan(r, v, kkn, lw, km, bb, z, bsz, n_lat, n_ctx):
    t_tok, d = r.shape
    lat_blocks = n_lat // SCAN_BLOCK
    ctx_blocks = n_ctx // SCAN_BLOCK
    steps = ctx_blocks + lat_blocks
    ctx_base = bsz * lat_blocks
    rev = z == 1
    width = SCAN_LANES * SCAN_GROUPS

    def blk(b, s):
        if rev:
            cb = ctx_base + b * ctx_blocks + (ctx_blocks - 1 - s)
            lb = b * lat_blocks + (lat_blocks - 1 - (s - ctx_blocks))
        else:
            cb = ctx_base + b * ctx_blocks + s
            lb = b * lat_blocks + (s - ctx_blocks)
        return jnp.where(s < ctx_blocks, cb, lb)

    in_blk = lambda b, s: blk(b, jnp.minimum(s, steps - 1))
    out_blk = lambda b, s: blk(b, jnp.maximum(s - 1, 0))
    tok = pl.BlockSpec((SCAN_BLOCK, width), lambda b, g, s: (in_blk(b, s), g))
    dirs = pl.BlockSpec((None, SCAN_BLOCK, width), lambda b, g, s: (z, in_blk(b, s), g))
    n_chunks = SCAN_BLOCK // SCAN_CHUNK
    ops_scratch = [pltpu.VMEM((2, SCAN_BLOCK, width), BF16), pltpu.VMEM((2, SCAN_BLOCK, width), BF16),
                   pltpu.VMEM((2, SCAN_BLOCK, width), F32), pltpu.VMEM((2, SCAN_BLOCK, width), F32),
                   pltpu.VMEM((2, SCAN_BLOCK, width), BF16), pltpu.VMEM((2, SCAN_BLOCK, width), BF16),
                   pltpu.VMEM((2, SCAN_BLOCK, width), BF16), pltpu.VMEM((2, n_chunks, width), F32)]
    return pl.pallas_call(
        functools.partial(_scan_kernel, rev=rev, groups=SCAN_GROUPS),
        grid=(bsz, d // width, steps + 1),
        in_specs=[tok, tok, tok, dirs, dirs, dirs],
        out_specs=pl.BlockSpec((SCAN_BLOCK, width), lambda b, g, s: (out_blk(b, s), g)),
        out_shape=jax.ShapeDtypeStruct((t_tok, d), F32),
        scratch_shapes=[pltpu.VMEM((SCAN_GROUPS, SCAN_LANES, SCAN_LANES), F32)] + ops_scratch,
        compiler_params=pltpu.CompilerParams(dimension_semantics=("parallel", "parallel", "arbitrary"),
                                             vmem_limit_bytes=VMEM_LIMIT),
        name="wkv_scan_rev" if rev else "wkv_scan_fwd",
    )(r, v, kkn, lw, km, bb)


def _rwkv_out_kernel(x_ref, mod_ref, y0_ref, y1_ref, r_ref, v_ref, km_ref, gate_ref, lng_ref, lnb_ref, rk_ref,
                     wo_ref, e_ref, et_ref, g2_ref, wrt_ref, brt_ref, xo_ref, h2_ref, route_ref):
    x = x_ref[...]
    mods = _mod_parts(mod_ref)
    y = y0_ref[...] + y1_ref[...]
    inv_n = 1.0 / RWKV_HEAD
    mu = _seg_expand(_seg_sum(y, e_ref) * inv_n, et_ref)
    yc = y - mu
    var = _seg_sum(yc * yc, e_ref) * inv_n
    yn = (yc * _seg_expand(lax.rsqrt(var + GN_EPS), et_ref)) * lng_ref[...] + lnb_ref[...]
    v = v_ref[...]
    bonus = _seg_sum((r_ref[...] * rk_ref[...]) * (km_ref[0] + km_ref[1]), e_ref)
    yn = yn + _seg_expand(bonus, et_ref) * v
    out = _dotb(yn * gate_ref[...], wo_ref[...])
    _residual_and_route(x, out, mods, g2_ref, wrt_ref, brt_ref, xo_ref, h2_ref, route_ref)


def _rwkv_out(x, mods3, mod_spec, y0, y1, r, v, km, gate, i, j, p):
    t_tok, d = x.shape
    row = lambda a: a.reshape(a.shape[0], 1, a.shape[1])
    tok = pl.BlockSpec((TM, d), lambda t: (t, 0))
    dir_spec = pl.BlockSpec((2, TM, d), lambda t: (0, t, 0))
    args = (x, mods3, y0, y1, r, v, km, gate, row(p["rw_ln_g"]), row(p["rw_ln_b"]), row(p["rw_r_k"].reshape(-1, d)),
            p["rw_wo"], p["head_sel"], p["head_sel_t"], row(p["norm2_g"]), p["wrt"], p["brt"])
    in_specs = [tok, mod_spec, tok, tok, tok, tok, dir_spec, tok,
                _layer_spec(args[8].shape, j), _layer_spec(args[9].shape, j), _layer_spec(args[10].shape, j),
                _layer_spec(args[11].shape, j), _full_spec(args[12].shape), _full_spec(args[13].shape),
                _layer_spec(args[14].shape, i), _layer_spec(args[15].shape, i), _layer_spec(args[16].shape, i)]
    return pl.pallas_call(
        _rwkv_out_kernel,
        grid=(t_tok // TM,),
        in_specs=in_specs,
        out_specs=_mixer_out_specs(d),
        out_shape=_mixer_out_shapes(t_tok, d),
        compiler_params=pltpu.CompilerParams(dimension_semantics=("parallel",), vmem_limit_bytes=VMEM_LIMIT),
        name="rwkv_out",
    )(*args)


def _expert_kernel(be_ref, nu_ref, x_ref, wg_ref, wu_ref, wd_ref, *rest, block_lo):
    o_ref, wg_s, wu_s, wd_s = rest[-4:]
    step = pl.program_id(0)
    b = step + block_lo
    e = be_ref[b]
    prev = be_ref[jnp.maximum(b - 1, 0)]
    used = b < nu_ref[0]

    @pl.when(used & ((step == 0) | (e != prev)))
    def _():
        wg_s[...] = wg_ref[...].astype(BF16)
        wu_s[...] = wu_ref[...].astype(BF16)
        wd_s[...] = wd_ref[...].astype(BF16)

    @pl.when(used)
    def _():
        xb = x_ref[...]
        g = jnp.dot(xb, wg_s[...], preferred_element_type=F32)
        u = jnp.dot(xb, wu_s[...], preferred_element_type=F32)
        hmid = (g * jax.nn.sigmoid(g)) * u
        o_ref[...] = jnp.dot(hmid.astype(BF16), wd_s[...], preferred_element_type=F32)

    @pl.when(jnp.logical_not(used))
    def _():
        o_ref[...] = jnp.zeros_like(o_ref)


def _experts(buf, block_expert, n_used, w_gate, w_up, w_down, i, block_lo, n_blocks, earlier=None):
    rows, d = buf.shape
    f = w_gate.shape[-1]
    in_specs = [pl.BlockSpec((MOE_ROWS, d), lambda b, be, nu: (b, 0)),
                pl.BlockSpec((None, None, d, f), lambda b, be, nu: (i, be[b + block_lo], 0, 0)),
                pl.BlockSpec((None, None, d, f), lambda b, be, nu: (i, be[b + block_lo], 0, 0)),
                pl.BlockSpec((None, None, f, d), lambda b, be, nu: (i, be[b + block_lo], 0, 0))]
    args = [block_expert, n_used, buf, w_gate, w_up, w_down]
    aliases = {}
    if earlier is not None:
        in_specs.append(pl.BlockSpec(memory_space=pl.ANY))
        args.append(earlier)
        aliases = {len(args) - 1: 0}
    grid_spec = pltpu.PrefetchScalarGridSpec(
        num_scalar_prefetch=2,
        grid=(rows // MOE_ROWS,),
        in_specs=in_specs,
        out_specs=pl.BlockSpec((MOE_ROWS, d), lambda b, be, nu: (b + block_lo, 0)),
        scratch_shapes=[pltpu.VMEM((d, f), BF16), pltpu.VMEM((d, f), BF16), pltpu.VMEM((f, d), BF16)],
    )
    return pl.pallas_call(
        functools.partial(_expert_kernel, block_lo=block_lo),
        grid_spec=grid_spec,
        out_shape=jax.ShapeDtypeStruct((n_blocks * MOE_ROWS, d), F32),
        input_output_aliases=aliases,
        compiler_params=pltpu.CompilerParams(dimension_semantics=("arbitrary",), vmem_limit_bytes=VMEM_LIMIT),
        name="experts",
    )(*args)


def _combine_kernel(x_ref, mod_ref, route_ref, y0_ref, y1_ref, fg_ref, *rest, final):
    xo_ref = rest[-1]
    gt2 = _mod_parts(mod_ref)[5]
    route = route_ref[...]
    xn = x_ref[...] + gt2 * (route[:, 2:3] * y0_ref[...] + route[:, 3:4] * y1_ref[...])
    if final:
        xn = (xn * lax.rsqrt(jnp.mean(xn * xn, axis=-1, keepdims=True) + NORM_EPS)) * fg_ref[...]
    xo_ref[...] = xn


def _combine(x, mods3, mod_spec, route, y0, y1, final_g, final, t_out, tile_lo, earlier=None):
    d = x.shape[1]
    here = pl.BlockSpec((TM, d), lambda t: (t, 0))
    tok = pl.BlockSpec((TM, d), lambda t: (t + tile_lo, 0))
    in_specs = [tok, mod_spec, pl.BlockSpec((TM, LANES), lambda t: (t + tile_lo, 0)), here, here, _full_spec((1, d))]
    args = [x, mods3, route, y0, y1, final_g.reshape(1, d)]
    aliases = {}
    if earlier is not None:
        in_specs.append(pl.BlockSpec(memory_space=pl.ANY))
        args.append(earlier)
        aliases = {len(args) - 1: 0}
    return pl.pallas_call(
        functools.partial(_combine_kernel, final=final),
        grid=(y0.shape[0] // TM,),
        in_specs=in_specs,
        out_specs=tok,
        out_shape=jax.ShapeDtypeStruct((t_out, d), F32),
        input_output_aliases=aliases,
        compiler_params=pltpu.CompilerParams(dimension_semantics=("parallel",), vmem_limit_bytes=VMEM_LIMIT),
        name="moe_combine",
    )(*args)


def _dest_kernel(route_ref, d_ref, counts_ref, base_scr, start_scr):
    phase = pl.program_id(0)
    t = pl.program_id(1)
    route = route_ref[...]
    tm = route.shape[0]
    lane = lax.broadcasted_iota(jnp.int32, route.shape, 1)
    lane_f = lane.astype(F32)
    onehots = [jnp.where(lane_f == route[:, k:k + 1], 1.0, 0.0) for k in range(TOP_K)]

    @pl.when((phase == 0) & (t == 0))
    def _():
        base_scr[...] = jnp.zeros_like(base_scr)

    @pl.when((phase == 1) & (t == 0))
    def _():
        counts = base_scr[...]
        counts_ref[...] = counts
        padded = jnp.floor((counts + (MOE_ROWS - 1)) * (1.0 / MOE_ROWS)) * MOE_ROWS
        before = (lax.broadcasted_iota(jnp.int32, (LANES, LANES), 0)
                  < lax.broadcasted_iota(jnp.int32, (LANES, LANES), 1))
        start_scr[...] = _dot_x_sel(padded, jnp.where(before, 1.0, 0.0).astype(BF16))
        base_scr[...] = jnp.zeros_like(base_scr)

    @pl.when(phase == 0)
    def _():
        base_scr[...] = base_scr[...] + sum(jnp.sum(oh, axis=0, keepdims=True) for oh in onehots)

    @pl.when(phase == 1)
    def _():
        earlier = (lax.broadcasted_iota(jnp.int32, (tm, tm), 1)
                   < lax.broadcasted_iota(jnp.int32, (tm, tm), 0))
        earlier_b = jnp.where(earlier, 1.0, 0.0).astype(BF16)
        diag = (lax.broadcasted_iota(jnp.int32, (LANES, LANES), 0)
                == lax.broadcasted_iota(jnp.int32, (LANES, LANES), 1))
        base = base_scr[...] + start_scr[...]
        for k, oh in enumerate(onehots):
            before = jnp.dot(earlier_b, oh.astype(BF16), preferred_element_type=F32)
            dest = jnp.sum(oh * (base + before), axis=1, keepdims=True)
            rows = [jnp.sum(jnp.where(diag, jnp.broadcast_to(dest[i * LANES:(i + 1) * LANES], (LANES, LANES)), 0.0),
                            axis=0, keepdims=True) for i in range(tm // LANES)]
            d_ref[k, 0] = jnp.concatenate(rows, axis=0).astype(jnp.int32)
            base = base + jnp.sum(oh, axis=0, keepdims=True)
        base_scr[...] = base - start_scr[...]


def _dests(route):
    t_tok = route.shape[0]
    assert t_tok % DEST_TM == 0
    tiles = t_tok // DEST_TM
    return pl.pallas_call(
        _dest_kernel,
        grid=(2, tiles),
        in_specs=[pl.BlockSpec((DEST_TM, LANES), lambda ph, t: (t, 0))],
        out_specs=(pl.BlockSpec((TOP_K, 1, DEST_TM // LANES, LANES), lambda ph, t: (0, t * ph, 0, 0)),
                   pl.BlockSpec((1, LANES), lambda ph, t: (0, 0))),
        out_shape=(jax.ShapeDtypeStruct((TOP_K, tiles, DEST_TM // LANES, LANES), jnp.int32),
                   jax.ShapeDtypeStruct((1, LANES), F32)),
        scratch_shapes=[pltpu.VMEM((1, LANES), F32), pltpu.VMEM((1, LANES), F32)],
        compiler_params=pltpu.CompilerParams(dimension_semantics=("arbitrary", "arbitrary"),
                                             vmem_limit_bytes=VMEM_LIMIT),
        name="moe_dests",
    )(route)


def _moe(x, h2, route, mods3, mod_spec_at, i, p, final, t_out):
    t_tok, d = x.shape
    n_assign = t_tok * TOP_K
    dest4, counts_f = _dests(route)
    dest = dest4.reshape(TOP_K, t_tok)
    counts = counts_f[0, :N_EXPERTS].astype(jnp.int32)
    pad_end = jnp.cumsum(((counts + MOE_ROWS - 1) // MOE_ROWS) * MOE_ROWS)
    n_blocks = (n_assign + MOE_ROWS - 1) // MOE_ROWS + N_EXPERTS
    token = jnp.arange(t_tok, dtype=jnp.int32)
    slot_token = (jnp.arange(n_blocks * MOE_ROWS, dtype=jnp.int32) % t_tok).at[dest.reshape(-1)].set(
        jnp.tile(token, TOP_K), mode="promise_in_bounds", unique_indices=True)
    block_start = jnp.arange(n_blocks, dtype=jnp.int32) * MOE_ROWS
    block_expert = jnp.minimum(jnp.sum((pad_end[None, :] <= block_start[:, None]).astype(jnp.int32), axis=1),
                               N_EXPERTS - 1)
    n_used = (pad_end[-1:] // MOE_ROWS).astype(jnp.int32)
    out = None
    bounds = [n_blocks * q // MOE_RANGES for q in range(MOE_RANGES + 1)]
    for lo, hi in zip(bounds[:-1], bounds[1:]):
        buf = h2.at[slot_token[lo * MOE_ROWS:hi * MOE_ROWS]].get(mode="promise_in_bounds")
        out = _experts(buf, block_expert, n_used, p["moe_w_gate"], p["moe_w_up"], p["moe_w_down"], i, lo, n_blocks,
                       earlier=out)
    res = None
    tiles = t_out // TM
    bounds = [tiles * q // MOE_RANGES for q in range(MOE_RANGES + 1)]
    for lo, hi in zip(bounds[:-1], bounds[1:]):
        y0, y1 = [out.at[dest[k, lo * TM:hi * TM]].get(mode="promise_in_bounds", unique_indices=True)
                  for k in range(TOP_K)]
        res = _combine(x, mods3, mod_spec_at(lo), route, y0, y1, p["final_g"], final, t_out, lo, earlier=res)
    return res


def kernel(x, c, ctx, c_ctx, ada_w, ada_b, norm1_g, norm2_g, final_g, ga_w_in, ga_b_in, ga_ln_g, ga_ln_b, ga_w_s,
           ga_b_s, ga_w_out, ga_b_out, rw_mu, rw_wr, rw_wk, rw_wv, rw_wo, rw_w0, rw_w1, rw_w2, rw_a0, rw_a1, rw_a2,
           rw_g1, rw_g2, rw_k_k, rw_k_a, rw_r_k, rw_ln_g, rw_ln_b, moe_w_grp, moe_b_grp, moe_w_exp, moe_b_exp,
           moe_w_gate, moe_w_up, moe_w_down):
    bsz, n_lat, d = x.shape
    n_ctx = ctx.shape[1]
    depth = ada_w.shape[0]
    n_mixers = 2
    assert n_lat % TM == 0 and n_ctx == TM and TM % GRID_W == 0
    assert n_lat % GMLP_TM == 0 and (bsz * n_ctx) % GMLP_TM == 0 and GMLP_TM % (GMLP_SPLIT * CHUNK) == 0
    assert n_lat % SCAN_BLOCK == 0 and n_ctx % SCAN_BLOCK == 0
    assert bsz + 1 <= 8 and d % (SCAN_LANES * SCAN_GROUPS) == 0
    assert MOE_ROWS & (MOE_ROWS - 1) == 0
    t_lat = bsz * n_lat
    lat_tiles = t_lat // TM
    tiles_per_batch = n_lat // TM
    heads = d // RWKV_HEAD

    s_rows = jnp.concatenate([c, c_ctx[None, :], jnp.zeros((8 - bsz - 1, d), F32)], axis=0)
    mods = _mods(s_rows, ada_w, ada_b)
    mods3 = mods.reshape(depth * 8, 1, N_MOD * d)

    def mod_spec_for(i, tm=TM, tile_lo=0):
        def imap(t):
            t = t + tile_lo
            return (i * 8 + jnp.where(t < t_lat // tm, t // (n_lat // tm), bsz), 0, 0)
        return pl.BlockSpec((1, 1, N_MOD * d), imap)

    head_sel = (jnp.arange(d)[:, None] // RWKV_HEAD == jnp.arange(LANES)[None, :]).astype(BF16)
    zpad = lambda a, axis, n: jnp.pad(a, [(0, n - a.shape[k]) if k == axis else (0, 0) for k in range(a.ndim)])
    glora = ((rw_g1.shape[-1] + LANES - 1) // LANES) * LANES
    wrt = jnp.concatenate([moe_w_grp, moe_w_exp], axis=-1)
    brt = jnp.concatenate([moe_b_grp, moe_b_exp], axis=-1)
    p = dict(
        norm1_g=norm1_g, norm2_g=norm2_g, final_g=final_g,
        ga_w_in=ga_w_in.astype(BF16), ga_b_in=ga_b_in, ga_ln_g=ga_ln_g, ga_ln_b=ga_ln_b,
        ga_w_s=ga_w_s.astype(BF16), ga_b_s=ga_b_s[..., None], ga_w_out=ga_w_out.astype(BF16), ga_b_out=ga_b_out,
        rw_mu8=zpad(rw_mu, 1, 8),
        rw_wr=rw_wr.astype(BF16), rw_wk=rw_wk.astype(BF16), rw_wv=rw_wv.astype(BF16), rw_wo=rw_wo.astype(BF16),
        rw_w1c=jnp.concatenate([rw_w1[:, 0], rw_w1[:, 1]], axis=-1).astype(BF16),
        rw_w2c=jnp.concatenate([rw_w2[:, 0], rw_w2[:, 1]], axis=-2).astype(BF16),
        rw_a1c=jnp.concatenate([rw_a1[:, 0], rw_a1[:, 1]], axis=-1).astype(BF16),
        rw_a2c=jnp.concatenate([rw_a2[:, 0], rw_a2[:, 1]], axis=-2).astype(BF16),
        rw_w0=rw_w0, rw_a0=rw_a0,
        rw_g1p=zpad(rw_g1, 2, glora).astype(BF16), rw_g2p=zpad(rw_g2, 1, glora).astype(BF16),
        rw_k_k=rw_k_k, rw_k_a=rw_k_a, rw_r_k=rw_r_k, rw_ln_g=rw_ln_g, rw_ln_b=rw_ln_b,
        head_sel=head_sel, head_sel_t=head_sel.T,
        wrt=zpad(wrt, 2, LANES).astype(BF16), brt=zpad(brt, 1, LANES).reshape(depth, 1, LANES),
        moe_w_gate=moe_w_gate, moe_w_up=moe_w_up, moe_w_down=moe_w_down,
    )
    assert heads <= LANES and 2 * rw_w1.shape[-1] == LANES and 2 * rw_a1.shape[-1] == LANES

    xs = jnp.concatenate([x.reshape(t_lat, d), ctx.reshape(bsz * n_ctx, d)], axis=0)
    for i in range(depth):
        j = i // n_mixers
        mod_spec = mod_spec_for(i)
        if i % n_mixers == 0:
            xs, h2, route = _gmlp_layer(xs, mods3, mod_spec_for(i, GMLP_TM), i, j, p)
        else:
            r, v, kkn, gate, lw, km, bb = _rwkv_prep(xs, mods3, mod_spec, i, j, p, lat_tiles, tiles_per_batch)
            y0 = _wkv_scan(r, v, kkn, lw, km, bb, 0, bsz, n_lat, n_ctx)
            y1 = _wkv_scan(r, v, kkn, lw, km, bb, 1, bsz, n_lat, n_ctx)
            xs, h2, route = _rwkv_out(xs, mods3, mod_spec, y0, y1, r, v, km, gate, i, j, p)
        last = i == depth - 1
        xs = _moe(xs, h2, route, mods3, functools.partial(mod_spec_for, i, TM), i, p, final=last,
                  t_out=t_lat if last else xs.shape[0])
    return xs.reshape(bsz, n_lat, d)
```
